```python
import jax
import jax.numpy as jnp
from jax import lax
import numpy as np

D_MODEL = 1024
BATCH = 16
SEQ = 256
DEPTH = 4
DEC_BATCH = 4
DEC_SEQ = 2048
PAST_LEN = 512

GRID_W = 64
HEAD_DIM = 64
BRANCH_W = 256
N_BRANCH = 4
NA_HEADS = 4
NA_WIN_H = 8
NA_WIN_W = 16
NA_QBW = 16
NA_KBW = 32
HG_HEADS = 4
HG_DK = 64
HG_DV = 64
HG_CHUNK = 64
F_FLOOR = 1e-30
MLA_HEADS = 4
MLA_Q_RANK = 256
MLA_KV_RANK = 128
MLA_NOPE = 64
MLA_ROPE = 32
MLA_V = 64
WG_HEADS = 4
WG_KV_HEADS = 2
WG_WINDOW = 128
WG_BLOCK = 128

Q_BLOCK = 128
ROPE_BASE = 10000.0
EPS = 1e-6
NEG_INF = -1e30

IN_LAYOUT = (
    ('na_q', NA_HEADS * HEAD_DIM), ('na_k', NA_HEADS * HEAD_DIM), ('na_v', NA_HEADS * HEAD_DIM), ('na_z', BRANCH_W),
    ('hg_q', HG_HEADS * HG_DK), ('hg_ff', HG_HEADS * HG_DK), ('hg_fb', HG_HEADS * HG_DK),
    ('hg_i', HG_HEADS * HG_DV), ('hg_og', HG_HEADS * HG_DV), ('hg_z', BRANCH_W),
    ('mla_qa', MLA_Q_RANK), ('mla_kva', MLA_KV_RANK + MLA_ROPE), ('mla_z', BRANCH_W),
    ('wg_q', WG_HEADS * HEAD_DIM), ('wg_k', WG_KV_HEADS * HEAD_DIM), ('wg_v', WG_KV_HEADS * HEAD_DIM), ('wg_z', BRANCH_W),
    ('merge', N_BRANCH * D_MODEL),
)
IN_WIDTH = sum(w for _, w in IN_LAYOUT)

kernel_name = 'hybrid_diffusion_prefix_step'

F32 = jnp.float32


def rms_norm(x, g):
    xf = x.astype(F32)
    y = xf * lax.rsqrt(jnp.mean(xf * xf, axis=-1, keepdims=True) + EPS)
    return (y * g.astype(F32)).astype(x.dtype)


def heads(a, n):
    return a.reshape(a.shape[:-1] + (n, a.shape[-1] // n))


def split_cols(p):
    out = {}
    off = 0
    for name, w in IN_LAYOUT:
        out[name] = p[..., off:off + w]
        off += w
    return out


def axial_rope(n_tok, rot_dim):
    t = jnp.arange(n_tok)
    row = (t // GRID_W).astype(F32)
    col = (t % GRID_W).astype(F32)
    nf = rot_dim // 4
    inv = ROPE_BASE ** (-jnp.arange(nf, dtype=F32) / nf)
    ang = jnp.concatenate([row[:, None] * inv, col[:, None] * inv], axis=-1)
    return jnp.cos(ang), jnp.sin(ang)


def apply_rope(x, cos, sin):
    x1, x2 = jnp.split(x.astype(F32), 2, axis=-1)
    c = cos[None, :, None, :]
    s = sin[None, :, None, :]
    return jnp.concatenate([x1 * c - x2 * s, x1 * s + x2 * c], axis=-1).astype(x.dtype)


def sink_softmax(s, sink=None):
    if sink is None:
        return jax.nn.softmax(s, axis=-1)
    sk = sink.astype(F32)
    m = jnp.maximum(jnp.max(s, axis=-1, keepdims=True), sk)
    e = jnp.exp(s - m)
    return e / (jnp.sum(e, axis=-1, keepdims=True) + jnp.exp(sk - m))


def block_attention(q, k, v, sink=None):
    B, Lq, H, dq = q.shape
    Hk = k.shape[2]
    G = H // Hk
    dv = v.shape[-1]
    nb = Lq // Q_BLOCK
    scale = dq ** -0.5
    qb = q.reshape(B, nb, Q_BLOCK, Hk, G, dq).transpose(1, 0, 2, 3, 4, 5)
    sk = None if sink is None else sink.reshape(1, Hk, G, 1, 1)

    def one(qblk):
        s = jnp.einsum('bqkgd,bskd->bkgqs', qblk, k).astype(F32) * scale
        p = sink_softmax(s, sk)
        return jnp.einsum('bkgqs,bskd->bqkgd', p.astype(v.dtype), v)

    o = lax.map(one, qb)
    return o.transpose(1, 0, 2, 3, 4, 5).reshape(B, Lq, H, dv)


def neighborhood_attention(q, k, v, kc, vc, rpb):
    B, T, H, d = q.shape
    rows = T // GRID_W
    wh = min(NA_WIN_H, rows)
    ncb = GRID_W // NA_QBW
    scale = d ** -0.5
    r = jnp.arange(rows)
    key_rows = jnp.clip(r - wh // 2, 0, rows - wh)[:, None] + jnp.arange(wh)[None, :]
    qcol = jnp.arange(GRID_W).reshape(ncb, NA_QBW)
    kcol = jnp.clip(jnp.arange(ncb) * NA_QBW - NA_WIN_W // 2, 0, GRID_W - NA_KBW)[:, None] + jnp.arange(NA_KBW)[None, :]
    qstart = jnp.clip(qcol - NA_WIN_W // 2, 0, GRID_W - NA_WIN_W)
    kc3 = kcol[:, None, :]
    col_ok = (kc3 >= qstart[..., None]) & (kc3 < qstart[..., None] + NA_WIN_W)
    drow = key_rows - r[:, None]
    dcol = jnp.clip(kc3 - qcol[:, :, None] + NA_WIN_W - 1, 0, 2 * NA_WIN_W - 2)
    bias = rpb[:, (drow + NA_WIN_H - 1)[:, None, None, :, None], dcol[None, :, :, None, :]]
    kg = k.reshape(B, rows, GRID_W, H, d)
    vg = v.reshape(B, rows, GRID_W, H, d)
    ri = key_rows[:, :, None, None]
    ci = kcol[None, None, :, :]
    kw = kg[:, ri, ci]
    vw = vg[:, ri, ci]
    qg = q.reshape(B, rows, ncb, NA_QBW, H, d)
    s_loc = jnp.einsum('brcqhd,brwckhd->bhrcqwk', qg, kw).astype(F32) * scale + bias[None].astype(F32)
    s_loc = jnp.where(col_ok[None, None, None, :, :, None, :], s_loc, NEG_INF)
    nloc = wh * NA_KBW
    s_loc = s_loc.reshape(B, H, rows, ncb, NA_QBW, nloc)
    s_ctx = jnp.einsum('brcqhd,blhd->bhrcql', qg, kc).astype(F32) * scale
    p = jax.nn.softmax(jnp.concatenate([s_loc, s_ctx], axis=-1), axis=-1)
    p_loc = p[..., :nloc].reshape(B, H, rows, ncb, NA_QBW, wh, NA_KBW).astype(v.dtype)
    o = (jnp.einsum('bhrcqwk,brwckhd->brcqhd', p_loc, vw)
         + jnp.einsum('bhrcql,blhd->brcqhd', p[..., nloc:].astype(vc.dtype), vc))
    return o.reshape(B, T, H, d)


def window_attention(q, k, v, kc, vc, sink):
    B, T, H, d = q.shape
    Hk = k.shape[2]
    G = H // Hk
    nb = T // WG_BLOCK
    scale = d ** -0.5
    qb = q.reshape(B, nb, WG_BLOCK, Hk, G, d)
    pad = ((0, 0), (WG_BLOCK, WG_BLOCK), (0, 0), (0, 0))
    kp = jnp.pad(k, pad).reshape(B, nb + 2, WG_BLOCK, Hk, d)
    vp = jnp.pad(v, pad).reshape(B, nb + 2, WG_BLOCK, Hk, d)
    kw = jnp.concatenate([kp[:, :-2], kp[:, 1:-1], kp[:, 2:]], axis=2)
    vw = jnp.concatenate([vp[:, :-2], vp[:, 1:-1], vp[:, 2:]], axis=2)
    a = jnp.arange(WG_BLOCK)
    m = jnp.arange(3 * WG_BLOCK)
    rel = m[None, :] - WG_BLOCK - a[:, None]
    kpos = jnp.arange(nb)[:, None] * WG_BLOCK - WG_BLOCK + m[None, :]
    valid = (jnp.abs(rel) <= WG_WINDOW)[None] & ((kpos >= 0) & (kpos < T))[:, None, :]
    s_loc = jnp.einsum('bnqkgd,bnmkd->bkgnqm', qb, kw).astype(F32) * scale
    s_loc = jnp.where(valid[None, None, None], s_loc, NEG_INF)
    s_ctx = jnp.einsum('bnqkgd,blkd->bkgnql', qb, kc).astype(F32) * scale
    p = sink_softmax(jnp.concatenate([s_loc, s_ctx], axis=-1), sink.reshape(1, Hk, G, 1, 1, 1))
    nloc = 3 * WG_BLOCK
    o = (jnp.einsum('bkgnqm,bnmkd->bnqkgd', p[..., :nloc].astype(v.dtype), vw)
         + jnp.einsum('bkgnql,blkd->bnqkgd', p[..., nloc:].astype(vc.dtype), vc))
    return o.reshape(B, T, H, d)


def chunk_scan(q, k, v, logf, s0):
    B, T, H, dk = q.shape
    dv = v.shape[-1]
    C = HG_CHUNK
    n = T // C
    causal = jnp.tril(jnp.ones((C, C), dtype=bool))

    def to_chunks(a):
        return a.reshape(B, n, C, H, a.shape[-1]).transpose(1, 0, 3, 2, 4)

    def step(S, inp):
        qc, kc, vc, gc = inp
        b = jnp.cumsum(gc, axis=2)
        dec = jnp.exp(jnp.where(causal[:, :, None], b[:, :, :, None, :] - b[:, :, None, :, :], NEG_INF))
        att = jnp.einsum('bhtd,bhsd,bhtsd->bhts', qc, kc, dec)
        o = jnp.einsum('bhts,bhsv->bhtv', att, vc) + jnp.einsum('bhtd,bhdv->bhtv', qc * jnp.exp(b), S)
        b_last = b[:, :, -1:, :]
        S = jnp.exp(b_last[:, :, 0, :, None]) * S + jnp.einsum('bhsd,bhsv->bhdv', kc * jnp.exp(b_last - b), vc)
        return S, o

    S, o = lax.scan(step, s0, (to_chunks(q), to_chunks(k), to_chunks(v), to_chunks(logf)))
    return o.transpose(1, 0, 3, 2, 4).reshape(B, T, H, dv), S


def hgrn_gates(fz, lb):
    fz = fz.astype(F32)
    lb = lb.astype(F32)
    f = lb + (1.0 - lb) * jax.nn.sigmoid(fz)
    logf = jnp.log(jnp.maximum(f, F_FLOOR))
    k = (1.0 - lb) * jax.nn.sigmoid(-fz)
    return heads(logf, HG_HEADS), heads(k, HG_HEADS)


def hgrn_branch(cols, lb_pair, norm_g, s0_f, s0_b):
    q = heads(jax.nn.silu(cols['hg_q']).astype(F32), HG_HEADS)
    i = heads(cols['hg_i'].astype(F32), HG_HEADS)
    outs = []
    finals = []
    for fz, lb, s0, rev in ((cols['hg_ff'], lb_pair[0], s0_f, False), (cols['hg_fb'], lb_pair[1], s0_b, True)):
        logf, k = hgrn_gates(fz, lb)
        seq = (q, k, i, logf)
        if rev:
            seq = tuple(jnp.flip(a, axis=1) for a in seq)
        o, s = chunk_scan(seq[0], seq[1], seq[2], seq[3], s0)
        outs.append(jnp.flip(o, axis=1) if rev else o)
        finals.append(s)
    o = rms_norm(outs[0] + outs[1], norm_g) * jax.nn.sigmoid(heads(cols['hg_og'], HG_HEADS).astype(F32))
    return o.astype(cols['hg_q'].dtype), finals[0], finals[1]


def mla_query(qa, g_qa, w_qb, rope):
    q = heads(rms_norm(qa, g_qa) @ w_qb, MLA_HEADS)
    if rope is not None:
        q = jnp.concatenate([q[..., :MLA_NOPE], apply_rope(q[..., MLA_NOPE:], rope[0], rope[1])], axis=-1)
    return q


def mla_compress(kva, g_kv, rope):
    ckv = rms_norm(kva[..., :MLA_KV_RANK], g_kv)
    kr = kva[..., MLA_KV_RANK:]
    if rope is not None:
        kr = apply_rope(kr[:, :, None, :], rope[0], rope[1])[:, :, 0, :]
    return ckv, kr


def mla_expand(ckv, kr, w_kvb):
    kv = heads(ckv @ w_kvb, MLA_HEADS)
    B, T, H, _ = kv.shape
    k = jnp.concatenate([kv[..., :MLA_NOPE], jnp.broadcast_to(kr[:, :, None, :], (B, T, H, MLA_ROPE))], axis=-1)
    return k, kv[..., MLA_NOPE:]


def layer_pre(x, cvec, W, l):
    mod = jax.nn.silu(cvec) @ W['w_mod'][l] + W['b_mod'][l]
    shift, scale, gate = jnp.split(mod[:, None, :], 3, axis=-1)
    h = rms_norm(x, W['norm_g'][l]) * (1 + scale) + shift
    return split_cols(h @ W['w_in'][l]), gate


def layer_post(x, cols, gate, branch_outs, W, l):
    B, T, _ = x.shape
    zs = (cols['na_z'], cols['hg_z'], cols['mla_z'], cols['wg_z'])
    y = jnp.stack([o.reshape(B, T, BRANCH_W) * jax.nn.silu(z) for o, z in zip(branch_outs, zs)], axis=2)
    proj = jnp.einsum('btnw,nwd->btnd', y, W['w_branch'][l])
    mg = jax.nn.sigmoid(cols['merge'].reshape(B, T, N_BRANCH, D_MODEL))
    merged = jnp.sum(mg * proj, axis=2)
    return x + gate * (merged @ W['w_out'][l])


def context_layer(x, cvec, lb, W, l):
    cols, gate = layer_pre(x, cvec, W, l)
    B = x.shape[0]
    na_k = heads(cols['na_k'], NA_HEADS)
    na_v = heads(cols['na_v'], NA_HEADS)
    o_na = block_attention(heads(cols['na_q'], NA_HEADS), na_k, na_v)
    zero = jnp.zeros((B, HG_HEADS, HG_DK, HG_DV), F32)
    o_hg, s_f, s_b = hgrn_branch(cols, lb, W['hg_norm_g'][l], zero, zero)
    q = mla_query(cols['mla_qa'], W['mla_q_norm_g'][l], W['mla_w_qb'][l], None)
    ckv, kr = mla_compress(cols['mla_kva'], W['mla_kv_norm_g'][l], None)
    k, v = mla_expand(ckv, kr, W['mla_w_kvb'][l])
    o_mla = block_attention(q, k, v)
    wk = heads(cols['wg_k'], WG_KV_HEADS)
    wv = heads(cols['wg_v'], WG_KV_HEADS)
    o_wg = block_attention(heads(cols['wg_q'], WG_HEADS), wk, wv, W['wg_sink'][l])
    x = layer_post(x, cols, gate, (o_na, o_hg, o_mla, o_wg), W, l)
    return x, (na_k, na_v, jnp.stack([s_f, s_b], axis=1), ckv, kr, wk, wv)


def latent_layer(x, c, cache, lb, rope_mla, rope_wg, W, l):
    kc_na, vc_na, st, ckv_c, kr_c, kc_wg, vc_wg = cache
    cols, gate = layer_pre(x, c, W, l)
    o_na = neighborhood_attention(heads(cols['na_q'], NA_HEADS), heads(cols['na_k'], NA_HEADS),
                                  heads(cols['na_v'], NA_HEADS), kc_na, vc_na, W['na_rpb'][l])
    o_hg, _, _ = hgrn_branch(cols, lb, W['hg_norm_g'][l], st[:, 0].astype(F32), st[:, 1].astype(F32))
    q = mla_query(cols['mla_qa'], W['mla_q_norm_g'][l], W['mla_w_qb'][l], rope_mla)
    ckv, kr = mla_compress(cols['mla_kva'], W['mla_kv_norm_g'][l], rope_mla)
    k_lat, v_lat = mla_expand(ckv, kr, W['mla_w_kvb'][l])
    k_ctx, v_ctx = mla_expand(ckv_c, kr_c, W['mla_w_kvb'][l])
    o_mla = block_attention(q, jnp.concatenate([k_lat, k_ctx], axis=1), jnp.concatenate([v_lat, v_ctx], axis=1))
    wq = apply_rope(heads(cols['wg_q'], WG_HEADS), rope_wg[0], rope_wg[1])
    wk = apply_rope(heads(cols['wg_k'], WG_KV_HEADS), rope_wg[0], rope_wg[1])
    o_wg = window_attention(wq, wk, heads(cols['wg_v'], WG_KV_HEADS), kc_wg, vc_wg, W['wg_sink'][l])
    return layer_post(x, cols, gate, (o_na, o_hg, o_mla, o_wg), W, l)


def setup_inputs(seed: int = 0) -> dict:
    key = jax.random.key(seed)
    ks = jax.random.split(key, 32)

    def nrm(k, shape, s=1.0):
        return jax.random.normal(k, shape, F32) * s

    L = PAST_LEN
    Bd = DEC_BATCH
    return {
        'x_prompt': nrm(ks[0], (BATCH, SEQ, D_MODEL)),
        'x_sample': nrm(ks[1], (DEC_BATCH, DEC_SEQ, D_MODEL)),
        'cache_na_k': nrm(ks[2], (Bd, DEPTH, L, NA_HEADS, HEAD_DIM)),
        'cache_na_v': nrm(ks[3], (Bd, DEPTH, L, NA_HEADS, HEAD_DIM)),
        'state_hgrn': nrm(ks[4], (Bd, DEPTH, 2, HG_HEADS, HG_DK, HG_DV)),
        'cache_mla_ckv': nrm(ks[5], (Bd, DEPTH, L, MLA_KV_RANK)),
        'cache_mla_krope': nrm(ks[6], (Bd, DEPTH, L, MLA_ROPE)),
        'cache_wg_k': nrm(ks[7], (Bd, DEPTH, L, WG_KV_HEADS, HEAD_DIM)),
        'cache_wg_v': nrm(ks[8], (Bd, DEPTH, L, WG_KV_HEADS, HEAD_DIM)),
        'c': nrm(ks[9], (DEC_BATCH, D_MODEL)),
        'c_ctx': nrm(ks[10], (D_MODEL,)),
        'norm_g': 1.0 + nrm(ks[11], (DEPTH, D_MODEL), 0.02),
        'w_mod': nrm(ks[12], (DEPTH, D_MODEL, 3 * D_MODEL), 0.5 * D_MODEL ** -0.5),
        'b_mod': nrm(ks[13], (DEPTH, 3 * D_MODEL), 0.01),
        'w_in': nrm(ks[14], (DEPTH, D_MODEL, IN_WIDTH), D_MODEL ** -0.5),
        'na_rpb': nrm(ks[15], (DEPTH, NA_HEADS, 2 * NA_WIN_H - 1, 2 * NA_WIN_W - 1), 0.1),
        'hg_lb_logits': nrm(ks[16], (DEPTH, 2, HG_HEADS * HG_DK), 0.5),
        'hg_norm_g': 1.0 + nrm(ks[17], (DEPTH, HG_DV), 0.02),
        'mla_q_norm_g': 1.0 + nrm(ks[18], (DEPTH, MLA_Q_RANK), 0.02),
        'mla_kv_norm_g': 1.0 + nrm(ks[19], (DEPTH, MLA_KV_RANK), 0.02),
        'mla_w_qb': nrm(ks[20], (DEPTH, MLA_Q_RANK, MLA_HEADS * (MLA_NOPE + MLA_ROPE)), MLA_Q_RANK ** -0.5),
        'mla_w_kvb': nrm(ks[21], (DEPTH, MLA_KV_RANK, MLA_HEADS * (MLA_NOPE + MLA_V)), MLA_KV_RANK ** -0.5),
        'wg_sink': nrm(ks[22], (DEPTH, WG_HEADS), 0.5),
        'w_branch': nrm(ks[23], (DEPTH, N_BRANCH, BRANCH_W, D_MODEL), BRANCH_W ** -0.5),
        'w_out': nrm(ks[24], (DEPTH, D_MODEL, D_MODEL), D_MODEL ** -0.5),
        'final_g': 1.0 + nrm(ks[25], (D_MODEL,), 0.02),
    }


def reference(x_prompt, x_sample, cache_na_k, cache_na_v, state_hgrn, cache_mla_ckv, cache_mla_krope,
              cache_wg_k, cache_wg_v, c, c_ctx, norm_g, w_mod, b_mod, w_in, na_rpb, hg_lb_logits, hg_norm_g,
              mla_q_norm_g, mla_kv_norm_g, mla_w_qb, mla_w_kvb, wg_sink, w_branch, w_out, final_g):
    W = {'norm_g': norm_g, 'w_mod': w_mod, 'b_mod': b_mod, 'w_in': w_in, 'na_rpb': na_rpb,
         'hg_norm_g': hg_norm_g, 'mla_q_norm_g': mla_q_norm_g, 'mla_kv_norm_g': mla_kv_norm_g,
         'mla_w_qb': mla_w_qb, 'mla_w_kvb': mla_w_kvb, 'wg_sink': wg_sink, 'w_branch': w_branch, 'w_out': w_out}
    p_lb = jax.nn.softmax(hg_lb_logits.astype(F32), axis=0)
    lbs = jnp.cumsum(p_lb, axis=0) - p_lb[0]

    Bp = x_prompt.shape[0]
    cvec = jnp.broadcast_to(c_ctx[None, :], (Bp, D_MODEL))
    h = x_prompt
    per_layer = []
    for l in range(DEPTH):
        h, st = context_layer(h, cvec, lbs[l], W, l)
        per_layer.append(st)
    y_prompt = rms_norm(h, final_g)
    new_na_k = jnp.stack([s[0] for s in per_layer], axis=1)
    new_na_v = jnp.stack([s[1] for s in per_layer], axis=1)
    new_state_hgrn = jnp.stack([s[2] for s in per_layer], axis=1)
    new_mla_ckv = jnp.stack([s[3] for s in per_layer], axis=1)
    new_mla_krope = jnp.stack([s[4] for s in per_layer], axis=1)
    new_wg_k = jnp.stack([s[5] for s in per_layer], axis=1)
    new_wg_v = jnp.stack([s[6] for s in per_layer], axis=1)

    T = x_sample.shape[1]
    rope_mla = axial_rope(T, MLA_ROPE)
    rope_wg = axial_rope(T, HEAD_DIM)
    h = x_sample
    for l in range(DEPTH):
        cache_l = (cache_na_k[:, l], cache_na_v[:, l], state_hgrn[:, l], cache_mla_ckv[:, l],
                   cache_mla_krope[:, l], cache_wg_k[:, l], cache_wg_v[:, l])
        h = latent_layer(h, c, cache_l, lbs[l], rope_mla, rope_wg, W, l)
    y_sample = rms_norm(h, final_g)
    return (y_prompt, y_sample, new_na_k, new_na_v, new_state_hgrn, new_mla_ckv, new_mla_krope, new_wg_k, new_wg_v)
```

```python
import functools

import jax
import jax.numpy as jnp
import numpy as np
from jax import lax
from jax.experimental import pallas as pl
from jax.experimental.pallas import tpu as pltpu

F32 = jnp.float32
BF16 = jnp.bfloat16

D_MODEL = 1024
DEPTH = 4
GRID_W = 64
HEAD_DIM = 64
BRANCH_W = 256
N_BRANCH = 4
NA_HEADS = 4
NA_WIN_H = 8
NA_WIN_W = 16
HG_HEADS = 4
HG_DK = 64
HG_CHUNK = 64
F_FLOOR = 1e-30
MLA_HEADS = 4
MLA_Q_RANK = 256
MLA_KV_RANK = 128
MLA_NOPE = 64
MLA_ROPE = 32
WG_HEADS = 4
WG_KV_HEADS = 2
WG_WINDOW = 128
ROPE_BASE = 10000.0
EPS = 1e-6
NEG_INF = -1e30

_IN_LAYOUT = (
    ('na_q', 256), ('na_k', 256), ('na_v', 256), ('na_z', 256),
    ('hg_q', 256), ('hg_ff', 256), ('hg_fb', 256), ('hg_i', 256), ('hg_og', 256), ('hg_z', 256),
    ('mla_qa', 256), ('mla_kva', 160), ('mla_z', 256),
    ('wg_q', 256), ('wg_k', 128), ('wg_v', 128), ('wg_z', 256),
    ('merge', 4096),
)
_IN_OFF = {}
_o = 0
for _n, _w in _IN_LAYOUT:
    _IN_OFF[_n] = (_o, _o + _w)
    _o += _w

NA_W = 1024
HG_W = 1536
MLA_W = 896
WG_W = 1152
PROJ_W = NA_W + HG_W + MLA_W + WG_W

TOK_TILE = 256
NA_QROWS = 4
NA_KROWS = 12
WG_KWIN = 512
VMEM_LIMIT = 56 * 1024 * 1024


def _cparams(sem):
    return pltpu.CompilerParams(dimension_semantics=sem, vmem_limit_bytes=VMEM_LIMIT)


def _dot(a, b):
    return jnp.dot(a, b, preferred_element_type=F32)


def _dot_nt(a, b):
    return lax.dot_general(a, b, (((1,), (1,)), ((), ())), preferred_element_type=F32)


def _dot_tn(a, b):
    return lax.dot_general(a, b, (((0,), (0,)), ((), ())), preferred_element_type=F32)


def _sigmoid(x):
    return 1.0 / (1.0 + jnp.exp(-x))


def _silu(x):
    return x * _sigmoid(x)


def _lane_mask(width, lo, hi, dtype):
    lane = lax.broadcasted_iota(jnp.int32, (1, width), 1)
    return ((lane >= lo) & (lane < hi)).astype(dtype)


def _seg_matrix(width, seg):
    r = lax.broadcasted_iota(jnp.int32, (width, width), 0) // seg
    c = lax.broadcasted_iota(jnp.int32, (width, width), 1) // seg
    return r == c


def _seg_mean(x2, seg_bf16, seg):
    hi = x2.astype(BF16)
    lo = (x2 - hi.astype(F32)).astype(BF16)
    return (_dot(hi, seg_bf16) + _dot(lo, seg_bf16)) * (1.0 / seg)


def _mod_kernel(c_ref, w_ref, b_ref, o_ref):
    s = _silu(c_ref[...])
    o_ref[0] = _dot(s.astype(BF16), w_ref[0].astype(BF16)) + b_ref[0]


def _modulation(cond, w_mod, b_mod):
    nrow = cond.shape[0]
    tn = 1024
    return pl.pallas_call(
        _mod_kernel,
        grid=(DEPTH, 3 * D_MODEL // tn),
        in_specs=[
            pl.BlockSpec((nrow, D_MODEL), lambda l, j: (0, 0)),
            pl.BlockSpec((1, D_MODEL, tn), lambda l, j: (l, 0, j)),
            pl.BlockSpec((1, 1, tn), lambda l, j: (l, 0, j)),
        ],
        out_specs=pl.BlockSpec((1, nrow, tn), lambda l, j: (l, 0, j)),
        out_shape=jax.ShapeDtypeStruct((DEPTH, nrow, 3 * D_MODEL), F32),
        compiler_params=_cparams(("arbitrary", "arbitrary")),
        name="modulation",
    )(cond, w_mod, b_mod.reshape(DEPTH, 1, 3 * D_MODEL))


def _mod_norm(x, mod_ref, g_ref):
    ms = jnp.mean(x * x, axis=-1, keepdims=True)
    y = x * lax.rsqrt(ms + EPS) * g_ref[0]
    shift = mod_ref[0, 0, :, 0:D_MODEL]
    scale = mod_ref[0, 0, :, D_MODEL:2 * D_MODEL]
    gate = mod_ref[0, 0, :, 2 * D_MODEL:3 * D_MODEL]
    return y * (1.0 + scale) + shift, gate


def _in_kernel(l_ref, x_ref, mod_ref, g_ref, w_ref, na_ref, hg_ref, mla_ref, wg_ref):
    h, _ = _mod_norm(x_ref[0], mod_ref, g_ref)
    hb = h.astype(BF16)
    off = 0
    for ref, width in ((na_ref, NA_W), (hg_ref, HG_W), (mla_ref, MLA_W), (wg_ref, WG_W)):
        ref[0] = _dot(hb, w_ref[0, :, off:off + width])
        off += width


def _in_proj(lidx, x, mod4, mod_row0, per_batch, norm_g3, w_proj):
    B, T, _ = x.shape
    nt = T // TOK_TILE
    mrow = (lambda b: mod_row0 + b) if per_batch else (lambda b: mod_row0)
    grid_spec = pltpu.PrefetchScalarGridSpec(
        num_scalar_prefetch=1,
        grid=(B, nt),
        in_specs=[
            pl.BlockSpec((1, TOK_TILE, D_MODEL), lambda b, t, l: (b, t, 0)),
            pl.BlockSpec((1, 1, 1, 3 * D_MODEL), lambda b, t, l: (l[0], mrow(b), 0, 0)),
            pl.BlockSpec((1, 1, D_MODEL), lambda b, t, l: (l[0], 0, 0)),
            pl.BlockSpec((1, D_MODEL, PROJ_W), lambda b, t, l: (l[0], 0, 0)),
        ],
        out_specs=[
            pl.BlockSpec((1, TOK_TILE, NA_W), lambda b, t, l: (b, t, 0)),
            pl.BlockSpec((1, TOK_TILE, HG_W), lambda b, t, l: (b, t, 0)),
            pl.BlockSpec((1, TOK_TILE, MLA_W), lambda b, t, l: (b, t, 0)),
            pl.BlockSpec((1, TOK_TILE, WG_W), lambda b, t, l: (b, t, 0)),
        ],
    )
    return pl.pallas_call(
        _in_kernel,
        grid_spec=grid_spec,
        out_shape=[
            jax.ShapeDtypeStruct((B, T, NA_W), F32),
            jax.ShapeDtypeStruct((B, T, HG_W), F32),
            jax.ShapeDtypeStruct((B, T, MLA_W), F32),
            jax.ShapeDtypeStruct((B, T, WG_W), F32),
        ],
        compiler_params=_cparams(("arbitrary", "arbitrary")),
        name="in_proj",
    )(lidx, x, mod4, norm_g3, w_proj)


def _post_kernel(l_ref, x_ref, mod_ref, g_ref, y0_ref, y1_ref, y2_ref, y3_ref, wm_ref, wb_ref, wo_ref,
                 fg_ref, o_ref, *, final):
    x = x_ref[0]
    h, gate = _mod_norm(x, mod_ref, g_ref)
    hb = h.astype(BF16)
    acc = jnp.zeros((TOK_TILE, D_MODEL), F32)
    for n, y_ref in enumerate((y0_ref, y1_ref, y2_ref, y3_ref)):
        mg = _sigmoid(_dot(hb, wm_ref[0, :, n * D_MODEL:(n + 1) * D_MODEL]))
        acc = acc + mg * _dot(y_ref[0], wb_ref[0, n])
    out = x + gate * _dot(acc.astype(BF16), wo_ref[0])
    if final:
        ms = jnp.mean(out * out, axis=-1, keepdims=True)
        out = out * lax.rsqrt(ms + EPS) * fg_ref[...]
    o_ref[0] = out


def _post(lidx, x, mod4, mod_row0, per_batch, norm_g3, ys, w_merge, w_branch, w_out, final_g2, final):
    B, T, _ = x.shape
    nt = T // TOK_TILE
    mrow = (lambda b: mod_row0 + b) if per_batch else (lambda b: mod_row0)
    yspec = pl.BlockSpec((1, TOK_TILE, BRANCH_W), lambda b, t, l: (b, t, 0))
    grid_spec = pltpu.PrefetchScalarGridSpec(
        num_scalar_prefetch=1,
        grid=(B, nt),
        in_specs=[
            pl.BlockSpec((1, TOK_TILE, D_MODEL), lambda b, t, l: (b, t, 0)),
            pl.BlockSpec((1, 1, 1, 3 * D_MODEL), lambda b, t, l: (l[0], mrow(b), 0, 0)),
            pl.BlockSpec((1, 1, D_MODEL), lambda b, t, l: (l[0], 0, 0)),
            yspec, yspec, yspec, yspec,
            pl.BlockSpec((1, D_MODEL, N_BRANCH * D_MODEL), lambda b, t, l: (l[0], 0, 0)),
            pl.BlockSpec((1, N_BRANCH, BRANCH_W, D_MODEL), lambda b, t, l: (l[0], 0, 0, 0)),
            pl.BlockSpec((1, D_MODEL, D_MODEL), lambda b, t, l: (l[0], 0, 0)),
            pl.BlockSpec((1, D_MODEL), lambda b, t, l: (0, 0)),
        ],
        out_specs=pl.BlockSpec((1, TOK_TILE, D_MODEL), lambda b, t, l: (b, t, 0)),
    )
    return pl.pallas_call(
        functools.partial(_post_kernel, final=final),
        grid_spec=grid_spec,
        out_shape=jax.ShapeDtypeStruct((B, T, D_MODEL), F32),
        compiler_params=_cparams(("arbitrary", "arbitrary")),
        name="post_final" if final else "post",
    )(lidx, x, mod4, norm_g3, *ys, w_merge, w_branch, w_out, final_g2)


def _attend(q, kparts, vparts, biases, qmasks, omasks, sinks=None):
    out = None
    for h in range(len(qmasks)):
        qh = q * qmasks[h]
        ss = []
        for i, kp in enumerate(kparts):
            s = _dot_nt(qh, kp)
            if biases is not None and biases[h][i] is not None:
                s = s + biases[h][i]
            ss.append(s)
        m = ss[0].max(axis=-1, keepdims=True)
        for s in ss[1:]:
            m = jnp.maximum(m, s.max(axis=-1, keepdims=True))
        if sinks is not None:
            m = jnp.maximum(m, sinks[h])
        den = None
        oh = None
        for s, vp in zip(ss, vparts):
            p = jnp.exp(s - m)
            ps = p.sum(axis=-1, keepdims=True)
            den = ps if den is None else den + ps
            pv = _dot(p.astype(BF16), vp)
            oh = pv if oh is None else oh + pv
        if sinks is not None:
            den = den + jnp.exp(sinks[h] - m)
        oh = (oh / den) * omasks[h]
        out = oh if out is None else out + oh
    return out


def _head_masks(width, seg, nheads, dtype):
    return [_lane_mask(width, h * seg, (h + 1) * seg, dtype) for h in range(nheads)]


def _na_ctx_kernel(q_ref, k_ref, v_ref, z_ref, o_ref):
    q = (q_ref[0] * (HEAD_DIM ** -0.5)).astype(BF16)
    k = k_ref[0].astype(BF16)
    v = v_ref[0].astype(BF16)
    o = _attend(q, [k], [v], None, _head_masks(256, 64, NA_HEADS, BF16), _head_masks(256, 64, NA_HEADS, F32))
    o_ref[0] = (o * _silu(z_ref[0])).astype(BF16)


def _na_ctx(cols_na):
    B, T, _ = cols_na.shape

    def spec(j):
        return pl.BlockSpec((1, T, 256), lambda b, j=j: (b, 0, j))

    return pl.pallas_call(
        _na_ctx_kernel,
        grid=(B,),
        in_specs=[spec(0), spec(1), spec(2), spec(3)],
        out_specs=pl.BlockSpec((1, T, 256), lambda b: (b, 0, 0)),
        out_shape=jax.ShapeDtypeStruct((B, T, 256), BF16),
        compiler_params=_cparams(("arbitrary",)),
        name="na_ctx",
    )(cols_na, cols_na, cols_na, cols_na)


def _na_row_types(rows):
    wh = min(NA_WIN_H, rows)
    nblk = rows // NA_QROWS
    starts, types, table = [], [], []
    for blk in range(nblk):
        r0 = blk * NA_QROWS
        k0 = int(np.clip(r0 - wh // 2, 0, rows - NA_KROWS))
        desc = []
        for i in range(NA_QROWS):
            r = r0 + i
            ks = int(np.clip(r - wh // 2, 0, rows - wh))
            for j in range(NA_KROWS):
                kr = k0 + j
                desc.append((kr - r + NA_WIN_H - 1) if ks <= kr < ks + wh else None)
        desc = tuple(desc)
        if desc not in table:
            table.append(desc)
        starts.append(k0)
        types.append(table.index(desc))
    return starts, types, table


def _na_bias_kernel(rpb_ref, o_ref, t2_ref, *, table):
    l = pl.program_id(0)
    h = pl.program_id(1)
    nrow = 2 * NA_WIN_H - 1
    ncol = 2 * NA_WIN_W - 1
    c = lax.broadcasted_iota(jnp.int32, (GRID_W, 128), 0)
    lane = lax.broadcasted_iota(jnp.int32, (GRID_W, 128), 1)
    first = lane < GRID_W
    kc = jnp.where(first, lane, lane - GRID_W)
    qstart = jnp.clip(c - NA_WIN_W // 2, 0, GRID_W - NA_WIN_W)
    ok = (kc >= qstart) & (kc < qstart + NA_WIN_W)
    dcol = kc - c + NA_WIN_W - 1
    for a in range(-1, nrow):
        acc = jnp.full((GRID_W, 128), NEG_INF, F32)
        for bi in range(2 * NA_WIN_W - 1):
            base = (l * NA_HEADS + h) * nrow
            lo = rpb_ref[(base + a) * ncol + bi] if a >= 0 else jnp.float32(NEG_INF)
            hi = rpb_ref[(base + a + 1) * ncol + bi] if a + 1 < nrow else jnp.float32(NEG_INF)
            acc = jnp.where(dcol == bi, jnp.where(first, lo, hi), acc)
        t2_ref[a + 1] = jnp.where(ok, acc, NEG_INF)
    for ti, desc in enumerate(table):
        for i in range(NA_QROWS):
            for p in range(NA_KROWS // 2):
                a0 = desc[i * NA_KROWS + 2 * p]
                a1 = desc[i * NA_KROWS + 2 * p + 1]
                if a0 is None and a1 is None:
                    tile = jnp.full((GRID_W, 128), NEG_INF, F32)
                elif a1 is None:
                    tile = jnp.where(first, t2_ref[a0 + 1], NEG_INF)
                elif a0 is None:
                    tile = jnp.where(first, NEG_INF, t2_ref[a1])
                else:
                    tile = t2_ref[a0 + 1]
                o_ref[0, ti, 0, i * GRID_W:(i + 1) * GRID_W, p * 128:(p + 1) * 128] = tile


def _na_bias(na_rpb, table):
    nt = len(table)
    nq = NA_QROWS * GRID_W
    nk = NA_KROWS * GRID_W
    return pl.pallas_call(
        functools.partial(_na_bias_kernel, table=table),
        grid=(DEPTH, NA_HEADS),
        in_specs=[pl.BlockSpec(memory_space=pltpu.SMEM)],
        out_specs=pl.BlockSpec((1, nt, 1, nq, nk), lambda l, h: (l, 0, h, 0, 0)),
        out_shape=jax.ShapeDtypeStruct((DEPTH, nt, NA_HEADS, nq, nk), F32),
        scratch_shapes=[pltpu.VMEM((2 * NA_WIN_H, GRID_W, 128), F32)],
        compiler_params=_cparams(("arbitrary", "arbitrary")),
        name="na_bias",
    )(na_rpb.reshape(-1))


def _na_lat_kernel(l_ref, ks_ref, ty_ref, q_ref, k_ref, v_ref, z_ref, kc_ref, vc_ref, b_ref, o_ref):
    rb = pl.program_id(1)
    nk = NA_KROWS * GRID_W
    ks = pl.multiple_of(ks_ref[rb] * GRID_W, GRID_W)
    q = (q_ref[0] * (HEAD_DIM ** -0.5)).astype(BF16)
    kw = k_ref[0, pl.ds(ks, nk), :].astype(BF16)
    vw = v_ref[0, pl.ds(ks, nk), :].astype(BF16)
    kc = kc_ref[0, 0].astype(BF16)
    vc = vc_ref[0, 0].astype(BF16)
    biases = [[b_ref[0, 0, h], None] for h in range(NA_HEADS)]
    o = _attend(q, [kw, kc], [vw, vc], biases,
                _head_masks(256, 64, NA_HEADS, BF16), _head_masks(256, 64, NA_HEADS, F32))
    o_ref[0] = (o * _silu(z_ref[0])).astype(BF16)


def _na_lat(lidx, kstart, ktype, cols_na, cache_k, cache_v, bias):
    B, T, _ = cols_na.shape
    L = cache_k.shape[2]
    nq = NA_QROWS * GRID_W
    nk = NA_KROWS * GRID_W
    nblk = T // nq
    grid_spec = pltpu.PrefetchScalarGridSpec(
        num_scalar_prefetch=3,
        grid=(B, nblk),
        in_specs=[
            pl.BlockSpec((1, nq, 256), lambda b, r, l, ks, ty: (b, r, 0)),
            pl.BlockSpec((1, T, 256), lambda b, r, l, ks, ty: (b, 0, 1)),
            pl.BlockSpec((1, T, 256), lambda b, r, l, ks, ty: (b, 0, 2)),
            pl.BlockSpec((1, nq, 256), lambda b, r, l, ks, ty: (b, r, 3)),
            pl.BlockSpec((1, 1, L, 256), lambda b, r, l, ks, ty: (b, l[0], 0, 0)),
            pl.BlockSpec((1, 1, L, 256), lambda b, r, l, ks, ty: (b, l[0], 0, 0)),
            pl.BlockSpec((1, 1, NA_HEADS, nq, nk), lambda b, r, l, ks, ty: (l[0], ty[r], 0, 0, 0)),
        ],
        out_specs=pl.BlockSpec((1, nq, 256), lambda b, r, l, ks, ty: (b, r, 0)),
    )
    return pl.pallas_call(
        _na_lat_kernel,
        grid_spec=grid_spec,
        out_shape=jax.ShapeDtypeStruct((B, T, 256), BF16),
        compiler_params=_cparams(("arbitrary", "arbitrary")),
        name="na_lat",
    )(lidx, kstart, ktype, cols_na, cols_na, cols_na, cols_na, cache_k, cache_v, bias)


def _wg_heads(q, kparts, vparts, bias, sink_ref, l):
    qm = _head_masks(128, 64, 2, BF16)
    om = _head_masks(128, 64, 2, F32)
    outs = []
    for grp, heads in enumerate(((0, 2), (1, 3))):
        qg = q[:, grp * 128:(grp + 1) * 128]
        sinks = [sink_ref[l, hh] for hh in heads]
        biases = None if bias is None else [[bias] + [None] * (len(kparts) - 1)] * 2
        outs.append(_attend(qg, kparts, vparts, biases, qm, om, sinks))
    return jnp.concatenate(outs, axis=-1)


def _wg_ctx_kernel(l_ref, sink_ref, q_ref, z_ref, k_ref, v_ref, o_ref):
    q = (q_ref[0] * (HEAD_DIM ** -0.5)).astype(BF16)
    o = _wg_heads(q, [k_ref[0].astype(BF16)], [v_ref[0].astype(BF16)], None, sink_ref, l_ref[0])
    o_ref[0] = (o * _silu(z_ref[0])).astype(BF16)


def _wg_ctx(lidx, sink, cols_wg):
    B, T, _ = cols_wg.shape
    grid_spec = pltpu.PrefetchScalarGridSpec(
        num_scalar_prefetch=1,
        grid=(B,),
        in_specs=[
            pl.BlockSpec(memory_space=pltpu.SMEM),
            pl.BlockSpec((1, T, 256), lambda b, l: (b, 0, 0)),
            pl.BlockSpec((1, T, 256), lambda b, l: (b, 0, 2)),
            pl.BlockSpec((1, T, 128), lambda b, l: (b, 0, 6)),
            pl.BlockSpec((1, T, 128), lambda b, l: (b, 0, 8)),
        ],
        out_specs=pl.BlockSpec((1, T, 256), lambda b, l: (b, 0, 0)),
    )
    return pl.pallas_call(
        _wg_ctx_kernel,
        grid_spec=grid_spec,
        out_shape=jax.ShapeDtypeStruct((B, T, 256), BF16),
        compiler_params=_cparams(("arbitrary",)),
        name="wg_ctx",
    )(lidx, sink, cols_wg, cols_wg, cols_wg, cols_wg)


def _wg_lat_kernel(l_ref, sink_ref, q_ref, qr_ref, z_ref, k_ref, kr_ref, v_ref, kc_ref, vc_ref,
                   cq_ref, sq_ref, ck_ref, sk_ref, o_ref, *, seq):
    tb = pl.program_id(1)
    t0 = tb * TOK_TILE
    k0 = pl.multiple_of(jnp.clip(t0 - WG_WINDOW, 0, seq - WG_KWIN), 128)
    q = (q_ref[0] * cq_ref[...] + qr_ref[0] * sq_ref[...]) * (HEAD_DIM ** -0.5)
    kwin = k_ref[0, pl.ds(k0, WG_KWIN), :] * ck_ref[pl.ds(k0, WG_KWIN), :] \
        + kr_ref[0, pl.ds(k0, WG_KWIN), :] * sk_ref[pl.ds(k0, WG_KWIN), :]
    vwin = v_ref[0, pl.ds(k0, WG_KWIN), :]
    qi = t0 + lax.broadcasted_iota(jnp.int32, (TOK_TILE, WG_KWIN), 0)
    kj = k0 + lax.broadcasted_iota(jnp.int32, (TOK_TILE, WG_KWIN), 1)
    bias = jnp.where(jnp.abs(qi - kj) <= WG_WINDOW, 0.0, NEG_INF).astype(F32)
    o = _wg_heads(q.astype(BF16),
                  [kwin.astype(BF16), kc_ref[0, 0].astype(BF16)],
                  [vwin.astype(BF16), vc_ref[0, 0].astype(BF16)],
                  bias, sink_ref, l_ref[0])
    o_ref[0] = (o * _silu(z_ref[0])).astype(BF16)


def _wg_lat(lidx, sink, cols_wg, cache_k, cache_v, cq, sq, ck, sk):
    B, T, _ = cols_wg.shape
    L = cache_k.shape[2]
    nt = T // TOK_TILE
    grid_spec = pltpu.PrefetchScalarGridSpec(
        num_scalar_prefetch=1,
        grid=(B, nt),
        in_specs=[
            pl.BlockSpec(memory_space=pltpu.SMEM),
            pl.BlockSpec((1, TOK_TILE, 256), lambda b, t, l: (b, t, 0)),
            pl.BlockSpec((1, TOK_TILE, 256), lambda b, t, l: (b, t, 1)),
            pl.BlockSpec((1, TOK_TILE, 256), lambda b, t, l: (b, t, 2)),
            pl.BlockSpec((1, T, 128), lambda b, t, l: (b, 0, 6)),
            pl.BlockSpec((1, T, 128), lambda b, t, l: (b, 0, 7)),
            pl.BlockSpec((1, T, 128), lambda b, t, l: (b, 0, 8)),
            pl.BlockSpec((1, 1, L, 128), lambda b, t, l: (b, l[0], 0, 0)),
            pl.BlockSpec((1, 1, L, 128), lambda b, t, l: (b, l[0], 0, 0)),
            pl.BlockSpec((TOK_TILE, 256), lambda b, t, l: (t, 0)),
            pl.BlockSpec((TOK_TILE, 256), lambda b, t, l: (t, 0)),
            pl.BlockSpec((T, 128), lambda b, t, l: (0, 0)),
            pl.BlockSpec((T, 128), lambda b, t, l: (0, 0)),
        ],
        out_specs=pl.BlockSpec((1, TOK_TILE, 256), lambda b, t, l: (b, t, 0)),
    )
    return pl.pallas_call(
        functools.partial(_wg_lat_kernel, seq=T),
        grid_spec=grid_spec,
        out_shape=jax.ShapeDtypeStruct((B, T, 256), BF16),
        compiler_params=_cparams(("arbitrary", "arbitrary")),
        name="wg_lat",
    )(lidx, sink, cols_wg, cols_wg, cols_wg, cols_wg, cols_wg, cols_wg, cache_k, cache_v, cq, sq, ck, sk)


def _mla_kv_kernel(l_ref, ckv_ref, kr_ref, krr_ref, g_ref, w_ref, c_ref, s_ref, k_out, v_out, *rest,
                   norm, rope):
    ckv = ckv_ref[0]
    if norm:
        ms = jnp.mean(ckv * ckv, axis=-1, keepdims=True)
        ckv = ckv * lax.rsqrt(ms + EPS) * g_ref[0]
        rest[0][0] = ckv
    kv = _dot(ckv.astype(BF16), w_ref[0])
    kr = kr_ref[0]
    if rope:
        kr = kr * c_ref[...] + krr_ref[0] * s_ref[...]
    k_out[0, :, 0:256] = kv[:, 0:256].astype(BF16)
    k_out[0, :, 256:384] = kr.astype(BF16)
    v_out[0] = kv[:, 256:512].astype(BF16)


def _mla_kv(lidx, ckv_arr, ckv_blk, kr_arr, kr_blk, krr_arr, krr_blk, g_kv3, w_kvb, c4, s4, norm, rope,
            layered):
    B = ckv_arr.shape[0]
    T = ckv_arr.shape[-2]
    nt = T // TOK_TILE
    if layered:
        def src(blk):
            return pl.BlockSpec((None, 1, TOK_TILE, 128), lambda b, t, l: (b, l[0], t, 0))
    else:
        def src(blk):
            return pl.BlockSpec((1, TOK_TILE, 128), lambda b, t, l, blk=blk: (b, t, blk))
    out_specs = [
        pl.BlockSpec((1, TOK_TILE, 384), lambda b, t, l: (b, t, 0)),
        pl.BlockSpec((1, TOK_TILE, 256), lambda b, t, l: (b, t, 0)),
    ]
    out_shape = [
        jax.ShapeDtypeStruct((B, T, 384), BF16),
        jax.ShapeDtypeStruct((B, T, 256), BF16),
    ]
    if norm:
        out_specs.append(pl.BlockSpec((1, TOK_TILE, 128), lambda b, t, l: (b, t, 0)))
        out_shape.append(jax.ShapeDtypeStruct((B, T, 128), F32))
    grid_spec = pltpu.PrefetchScalarGridSpec(
        num_scalar_prefetch=1,
        grid=(B, nt),
        in_specs=[
            src(ckv_blk), src(kr_blk), src(krr_blk),
            pl.BlockSpec((1, 1, 128), lambda b, t, l: (l[0], 0, 0)),
            pl.BlockSpec((1, 128, 512), lambda b, t, l: (l[0], 0, 0)),
            pl.BlockSpec((TOK_TILE, 128), lambda b, t, l: (t if rope else 0, 0)),
            pl.BlockSpec((TOK_TILE, 128), lambda b, t, l: (t if rope else 0, 0)),
        ],
        out_specs=out_specs,
    )
    return pl.pallas_call(
        functools.partial(_mla_kv_kernel, norm=norm, rope=rope),
        grid_spec=grid_spec,
        out_shape=out_shape,
        compiler_params=_cparams(("arbitrary", "arbitrary")),
        name="mla_kv" + ("_norm" if norm else "") + ("_rope" if rope else ""),
    )(lidx, ckv_arr, kr_arr, krr_arr, g_kv3, w_kvb, c4, s4)


def _mla_attn_kernel(l_ref, qa_ref, z_ref, g_ref, w_ref, c_ref, s_ref, *refs, nparts, rope):
    kparts = [refs[2 * i][0] for i in range(nparts)]
    vparts = [refs[2 * i + 1][0] for i in range(nparts)]
    o_ref = refs[2 * nparts]
    qa = qa_ref[0]
    ms = jnp.mean(qa * qa, axis=-1, keepdims=True)
    qn = qa * lax.rsqrt(ms + EPS) * g_ref[0]
    qq = _dot(qn.astype(BF16), w_ref[0])
    qr = qq[:, 256:384]
    if rope:
        qr = qr * c_ref[...] + qq[:, 384:512] * s_ref[...]
    scale = (MLA_NOPE + MLA_ROPE) ** -0.5
    q = jnp.concatenate([qq[:, 0:256], qr], axis=-1) * scale
    lane = lax.broadcasted_iota(jnp.int32, (1, 384), 1)
    qmasks = []
    for h in range(MLA_HEADS):
        m = ((lane >= 64 * h) & (lane < 64 * h + 64)) | ((lane >= 256 + 32 * h) & (lane < 256 + 32 * h + 32))
        qmasks.append(m.astype(BF16))
    o = _attend(q.astype(BF16), kparts, vparts, None, qmasks, _head_masks(256, 64, MLA_HEADS, F32))
    o_ref[0] = (o * _silu(z_ref[0])).astype(BF16)


def _mla_attn(lidx, cols_mla, g_q3, w_qb, c4, s4, kv_parts, rope):
    B, T, _ = cols_mla.shape
    nt = T // TOK_TILE
    in_specs = [
        pl.BlockSpec((1, TOK_TILE, 256), lambda b, t, l: (b, t, 0)),
        pl.BlockSpec((1, TOK_TILE, 256), lambda b, t, l: (b, t, 1)),
        pl.BlockSpec((1, 1, 256), lambda b, t, l: (l[0], 0, 0)),
        pl.BlockSpec((1, 256, 512), lambda b, t, l: (l[0], 0, 0)),
        pl.BlockSpec((TOK_TILE, 128), lambda b, t, l: (t if rope else 0, 0)),
        pl.BlockSpec((TOK_TILE, 128), lambda b, t, l: (t if rope else 0, 0)),
    ]
    args = [lidx, cols_mla, cols_mla, g_q3, w_qb, c4, s4]
    for kk, vv in kv_parts:
        in_specs.append(pl.BlockSpec((1, kk.shape[1], 384), lambda b, t, l: (b, 0, 0)))
        in_specs.append(pl.BlockSpec((1, vv.shape[1], 256), lambda b, t, l: (b, 0, 0)))
        args += [kk, vv]
    grid_spec = pltpu.PrefetchScalarGridSpec(
        num_scalar_prefetch=1,
        grid=(B, nt),
        in_specs=in_specs,
        out_specs=pl.BlockSpec((1, TOK_TILE, 256), lambda b, t, l: (b, t, 0)),
    )
    return pl.pallas_call(
        functools.partial(_mla_attn_kernel, nparts=len(kv_parts), rope=rope),
        grid_spec=grid_spec,
        out_shape=jax.ShapeDtypeStruct((B, T, 256), BF16),
        compiler_params=_cparams(("arbitrary", "arbitrary")),
        name="mla_attn_lat" if rope else "mla_attn_ctx",
    )(*args)


def _hg_chunk(c, rev, lb, q_ref, f_ref, i_ref, st_ref, b_sc, k_sc, v_sc, o_sc, tri, seg, bdmask):
    C = HG_CHUNK
    r0 = pl.multiple_of(c * C, C)
    qv = q_ref[0, pl.ds(r0, C), :]
    qs = _silu(qv)
    fz = f_ref[0, pl.ds(r0, C), :]
    f = lb + (1.0 - lb) * _sigmoid(fz)
    logf = jnp.log(jnp.maximum(f, F_FLOOR))
    kk = (1.0 - lb) * _sigmoid(-fz)
    v = i_ref[0, pl.ds(r0, C), :]
    hi = logf.astype(BF16)
    r1 = logf - hi.astype(F32)
    mid = r1.astype(BF16)
    lo = (r1 - mid.astype(F32)).astype(BF16)
    b = _dot(tri, hi) + _dot(tri, mid) + _dot(tri, lo)
    b_sc[...] = b
    k_sc[...] = kk
    v_sc[...] = v
    total = b[0:1, :] if rev else b[C - 1:C, :]
    st = st_ref[...]
    o_sc[...] = _dot_nt((qs * jnp.exp(b)).astype(BF16), st.astype(BF16))
    sub = lax.broadcasted_iota(jnp.int32, (8, 256), 0)
    for sb in range(C // 8):
        d0 = sb * 8
        lo_r, hi_r = (0, d0 + 8) if rev else (d0, C)
        nrow = hi_r - lo_r
        rest_lo, rest_hi = (0, d0) if rev else (d0 + 8, C)
        parts = []
        for ds in range(8):
            s = d0 + ds
            brow = b_sc[s:s + 1, :]
            krow = k_sc[s:s + 1, :]
            keep = (sub <= ds) if rev else (sub >= ds)
            ddiag = jnp.where(keep, b[d0:d0 + 8, :] - brow, NEG_INF)
            rdiag = qs[d0:d0 + 8, :] * jnp.exp(ddiag) * krow
            if rest_hi > rest_lo:
                rrest = qs[rest_lo:rest_hi, :] * jnp.exp(b[rest_lo:rest_hi, :] - brow) * krow
                parts += [rrest, rdiag] if rev else [rdiag, rrest]
            else:
                parts.append(rdiag)
        rg = jnp.concatenate(parts, axis=0).astype(BF16)
        res = _dot(rg, seg)
        acc = o_sc[lo_r:hi_r, :]
        for ds in range(8):
            s = d0 + ds
            acc = acc + res[ds * nrow:(ds + 1) * nrow, :] * v_sc[s:s + 1, :]
        o_sc[lo_r:hi_r, :] = acc
    kd = (kk * jnp.exp(total - b)).astype(BF16)
    st_ref[...] = jnp.exp(total) * st + bdmask * _dot_tn(v.astype(BF16), kd)


def _hg_kernel(l_ref, q_ref, f_ref, i_ref, og_ref, z_ref, lbl_ref, g_ref, s0_ref, y_ref, sto_ref,
               st_ref, oacc_ref, b_sc, k_sc, v_sc, o_sc, *, nb):
    j = pl.program_id(1)
    l = l_ref[0]
    C = HG_CHUNK
    ncb = TOK_TILE // C

    @pl.when((j == 0) | (j == nb))
    def _():
        st_ref[...] = s0_ref[0, 0]

    lg = lbl_ref[...]
    mx = lg[0]
    for m in range(1, DEPTH):
        mx = jnp.maximum(mx, lg[m])
    ex = [jnp.exp(lg[m] - mx) for m in range(DEPTH)]
    tot = ex[0]
    for m in range(1, DEPTH):
        tot = tot + ex[m]
    lbs = jnp.zeros_like(tot)
    for m in range(1, DEPTH):
        lbs = lbs + jnp.where(m <= l, ex[m] / tot, 0.0)

    rr = lax.broadcasted_iota(jnp.int32, (C, C), 0)
    cc = lax.broadcasted_iota(jnp.int32, (C, C), 1)
    seg = _seg_matrix(256, HG_DK)
    seg_b = seg.astype(BF16)
    bdmask = seg.astype(F32)

    @pl.when(j < nb)
    def _():
        tri = (cc <= rr).astype(BF16)
        lb = lbs[0:1, :]

        def body(c, carry):
            _hg_chunk(c, False, lb, q_ref, f_ref, i_ref, st_ref, b_sc, k_sc, v_sc, o_sc, tri, seg_b, bdmask)
            row = pl.multiple_of(j * TOK_TILE + c * C, C)
            oacc_ref[pl.ds(row, C), :] = o_sc[...]
            return carry

        lax.fori_loop(0, ncb, body, 0)

    @pl.when(j >= nb)
    def _():
        tri = (cc >= rr).astype(BF16)
        lb = lbs[1:2, :]
        blk = 2 * nb - 1 - j

        def body(ci, carry):
            c = ncb - 1 - ci
            _hg_chunk(c, True, lb, q_ref, f_ref, i_ref, st_ref, b_sc, k_sc, v_sc, o_sc, tri, seg_b, bdmask)
            r0 = pl.multiple_of(c * C, C)
            row = pl.multiple_of(blk * TOK_TILE + c * C, C)
            o = oacc_ref[pl.ds(row, C), :] + o_sc[...]
            ms = _seg_mean(o * o, seg_b, HG_DK)
            on = o * lax.rsqrt(ms + EPS) * g_ref[0]
            on = on * _sigmoid(og_ref[0, pl.ds(r0, C), :])
            y_ref[0, pl.ds(r0, C), :] = (on * _silu(z_ref[0, pl.ds(r0, C), :])).astype(BF16)
            return carry

        lax.fori_loop(0, ncb, body, 0)

    sto_ref[0, 0] = st_ref[...]


def _hgrn(lidx, cols_hg, lb_logits, g3, s0, s0_per_batch):
    B, T, _ = cols_hg.shape
    nb = T // TOK_TILE

    def blk(j):
        return jnp.where(j < nb, j, 2 * nb - 1 - j)

    def dirn(j):
        return jnp.where(j < nb, 0, 1)

    def col(jcol):
        return pl.BlockSpec((1, TOK_TILE, 256), lambda b, j, l, jcol=jcol: (b, blk(j), jcol))

    grid_spec = pltpu.PrefetchScalarGridSpec(
        num_scalar_prefetch=1,
        grid=(B, 2 * nb),
        in_specs=[
            col(0),
            pl.BlockSpec((1, TOK_TILE, 256), lambda b, j, l: (b, blk(j), 1 + dirn(j))),
            col(3), col(4), col(5),
            pl.BlockSpec((DEPTH, 2, 256), lambda b, j, l: (0, 0, 0)),
            pl.BlockSpec((1, 1, 256), lambda b, j, l: (l[0], 0, 0)),
            pl.BlockSpec((1, 1, 256, 256),
                         (lambda b, j, l: (b, dirn(j), 0, 0)) if s0_per_batch else (lambda b, j, l: (0, 0, 0, 0))),
        ],
        out_specs=[
            pl.BlockSpec((1, TOK_TILE, 256), lambda b, j, l: (b, jnp.where(j < nb, nb - 1, 2 * nb - 1 - j), 0)),
            pl.BlockSpec((1, 1, 256, 256), lambda b, j, l: (b, dirn(j), 0, 0)),
        ],
        scratch_shapes=[
            pltpu.VMEM((256, 256), F32),
            pltpu.VMEM((T, 256), F32),
            pltpu.VMEM((HG_CHUNK, 256), F32),
            pltpu.VMEM((HG_CHUNK, 256), F32),
            pltpu.VMEM((HG_CHUNK, 256), F32),
            pltpu.VMEM((HG_CHUNK, 256), F32),
        ],
    )
    return pl.pallas_call(
        functools.partial(_hg_kernel, nb=nb),
        grid_spec=grid_spec,
        out_shape=[
            jax.ShapeDtypeStruct((B, T, 256), BF16),
            jax.ShapeDtypeStruct((B, 2, 256, 256), F32),
        ],
        compiler_params=_cparams(("arbitrary", "arbitrary")),
        name="hgrn",
    )(lidx, cols_hg, cols_hg, cols_hg, cols_hg, cols_hg, lb_logits, g3, s0)


def _rot_cols(w, nheads):
    shp = w.shape
    r = shp[-1] // nheads
    w4 = w.reshape(shp[:-1] + (nheads, 2, r // 2))
    return jnp.concatenate([-w4[..., 1, :], w4[..., 0, :]], axis=-1).reshape(shp)


def _perm_heads(w, nheads, order):
    shp = w.shape
    w3 = w.reshape(shp[:-1] + (nheads, shp[-1] // nheads))
    return w3[..., list(order), :].reshape(shp)


_WG_ORDER = (0, 2, 1, 3)


def _prep_w_in(w_in):
    def c(name):
        lo, hi = _IN_OFF[name]
        return w_in[:, :, lo:hi]

    kva = c('mla_kva')
    kr = kva[:, :, MLA_KV_RANK:]
    kr4 = jnp.tile(kr, (1, 1, 4))
    kr4r = jnp.tile(_rot_cols(kr, 1), (1, 1, 4))
    wq = _perm_heads(c('wg_q'), WG_HEADS, _WG_ORDER)
    wk = c('wg_k')
    parts = [
        c('na_q'), c('na_k'), c('na_v'), c('na_z'),
        c('hg_q'), c('hg_ff'), c('hg_fb'), c('hg_i'), c('hg_og'), c('hg_z'),
        c('mla_qa'), c('mla_z'), kva[:, :, :MLA_KV_RANK], kr4, kr4r,
        wq, _rot_cols(wq, WG_HEADS), _perm_heads(c('wg_z'), WG_HEADS, _WG_ORDER),
        wk, _rot_cols(wk, WG_KV_HEADS), c('wg_v'),
    ]
    return jnp.concatenate(parts, axis=-1).astype(BF16), c('merge').astype(BF16)


def _rope_tables(n_tok, rot_dim, reps):
    t = np.arange(n_tok)
    row = (t // GRID_W).astype(np.float32)
    col = (t % GRID_W).astype(np.float32)
    nf = rot_dim // 4
    inv = jnp.asarray(ROPE_BASE, F32) ** (-jnp.arange(nf, dtype=F32) / nf)
    ang = jnp.concatenate([jnp.asarray(row)[:, None] * inv, jnp.asarray(col)[:, None] * inv], axis=-1)
    cos, sin = jnp.cos(ang), jnp.sin(ang)
    return (jnp.tile(jnp.concatenate([cos, cos], axis=-1), (1, reps)),
            jnp.tile(jnp.concatenate([sin, sin], axis=-1), (1, reps)))


def _state_to_blockdiag(st):
    B = st.shape[0]
    stt = jnp.swapaxes(st, -1, -2)
    eye = jnp.eye(HG_HEADS, dtype=st.dtype)
    bd = stt[:, :, :, :, None, :] * eye[None, None, :, None, :, None]
    return bd.reshape(B, 2, HG_HEADS * HG_DK, HG_HEADS * HG_DK)


def _blockdiag_to_state(bd):
    B = bd.shape[0]
    b6 = bd.reshape(B, 2, HG_HEADS, HG_DK, HG_HEADS, HG_DK)
    diag = jnp.stack([b6[:, :, h, :, h, :] for h in range(HG_HEADS)], axis=2)
    return jnp.swapaxes(diag, -1, -2)


def kernel(x_prompt, x_sample, cache_na_k, cache_na_v, state_hgrn, cache_mla_ckv, cache_mla_krope, cache_wg_k, cache_wg_v, c, c_ctx, norm_g, w_mod, b_mod, w_in, na_rpb, hg_lb_logits, hg_norm_g, mla_q_norm_g, mla_kv_norm_g, mla_w_qb, mla_w_kvb, wg_sink, w_branch, w_out, final_g):
    Bp, Tp, _ = x_prompt.shape
    Bd, Td, _ = x_sample.shape
    L = cache_na_k.shape[2]
    rows = Td // GRID_W
    assert Tp % TOK_TILE == 0 and Td % TOK_TILE == 0 and L % TOK_TILE == 0
    assert rows % NA_QROWS == 0 and rows >= NA_KROWS and Td >= WG_KWIN

    w_proj, w_merge = _prep_w_in(w_in)
    wb = w_branch.astype(BF16)
    wb = wb.at[:, 3].set(_perm_heads(jnp.swapaxes(wb[:, 3], -1, -2), WG_HEADS, _WG_ORDER).swapaxes(-1, -2))
    wo = w_out.astype(BF16)
    qb = mla_w_qb.reshape(DEPTH, MLA_Q_RANK, MLA_HEADS, MLA_NOPE + MLA_ROPE)
    qb_n = qb[..., :MLA_NOPE].reshape(DEPTH, MLA_Q_RANK, MLA_HEADS * MLA_NOPE)
    qb_r = qb[..., MLA_NOPE:].reshape(DEPTH, MLA_Q_RANK, MLA_HEADS * MLA_ROPE)
    w_qb = jnp.concatenate([qb_n, qb_r, _rot_cols(qb_r, MLA_HEADS)], axis=-1).astype(BF16)
    kvb = mla_w_kvb.reshape(DEPTH, MLA_KV_RANK, MLA_HEADS, 2 * MLA_NOPE)
    w_kvb = jnp.concatenate([kvb[..., :MLA_NOPE].reshape(DEPTH, MLA_KV_RANK, -1),
                             kvb[..., MLA_NOPE:].reshape(DEPTH, MLA_KV_RANK, -1)], axis=-1).astype(BF16)
    norm_g3 = norm_g.reshape(DEPTH, 1, D_MODEL)
    g_q3 = mla_q_norm_g.reshape(DEPTH, 1, MLA_Q_RANK)
    g_kv3 = mla_kv_norm_g.reshape(DEPTH, 1, MLA_KV_RANK)
    hg_g3 = jnp.tile(hg_norm_g, (1, HG_HEADS)).reshape(DEPTH, 1, HG_HEADS * HG_DK)
    final_g2 = final_g.reshape(1, D_MODEL)
    cq, sq = _rope_tables(Td, HEAD_DIM, WG_HEADS)
    ck, sk = cq[:, :128], sq[:, :128]
    c4, s4 = _rope_tables(Td, MLA_ROPE, MLA_HEADS)

    nrow = 8 * ((1 + Bd + 7) // 8)
    cond = jnp.zeros((nrow, D_MODEL), F32).at[0].set(c_ctx).at[1:1 + Bd].set(c)
    mod4 = _modulation(cond, w_mod, b_mod).reshape(DEPTH, nrow, 1, 3 * D_MODEL)

    kstart, ktype, table = _na_row_types(rows)
    na_bias = _na_bias(na_rpb, table)
    kstart = jnp.asarray(kstart, jnp.int32)
    ktype = jnp.asarray(ktype, jnp.int32)

    cna_k = cache_na_k.reshape(Bd, DEPTH, L, NA_HEADS * HEAD_DIM)
    cna_v = cache_na_v.reshape(Bd, DEPTH, L, NA_HEADS * HEAD_DIM)
    cwg_k = cache_wg_k.reshape(Bd, DEPTH, L, WG_KV_HEADS * HEAD_DIM)
    cwg_v = cache_wg_v.reshape(Bd, DEPTH, L, WG_KV_HEADS * HEAD_DIM)
    ckr4 = jnp.tile(cache_mla_krope, (1, 1, 1, 4))
    st_bd = _state_to_blockdiag(state_hgrn.reshape(Bd * DEPTH, 2, HG_HEADS, HG_DK, HG_DK)).reshape(
        Bd, DEPTH, 2, 256, 256)
    st_zero = jnp.zeros((1, 2, 256, 256), F32)

    hp, hs = x_prompt, x_sample
    caches = [[] for _ in range(7)]
    for layer in range(DEPTH):
        lidx = jnp.full((1,), layer, jnp.int32)
        final = layer == DEPTH - 1

        na, hg, mla, wg = _in_proj(lidx, hp, mod4, 0, False, norm_g3, w_proj)
        y_na = _na_ctx(na)
        y_hg, st_out = _hgrn(lidx, hg, hg_lb_logits, hg_g3, st_zero, False)
        kc_, vc_, ckv_n = _mla_kv(lidx, mla, 4, mla, 5, mla, 6, g_kv3, w_kvb, c4, s4, True, False, False)
        y_mla = _mla_attn(lidx, mla, g_q3, w_qb, c4, s4, [(kc_, vc_)], False)
        y_wg = _wg_ctx(lidx, wg_sink, wg)
        hp = _post(lidx, hp, mod4, 0, False, norm_g3, (y_na, y_hg, y_mla, y_wg), w_merge, wb, wo, final_g2, final)
        caches[0].append(na[:, :, 256:512].reshape(Bp, Tp, NA_HEADS, HEAD_DIM))
        caches[1].append(na[:, :, 512:768].reshape(Bp, Tp, NA_HEADS, HEAD_DIM))
        caches[2].append(_blockdiag_to_state(st_out))
        caches[3].append(ckv_n)
        caches[4].append(mla[:, :, 640:640 + MLA_ROPE])
        caches[5].append(wg[:, :, 768:896].reshape(Bp, Tp, WG_KV_HEADS, HEAD_DIM))
        caches[6].append(wg[:, :, 1024:1152].reshape(Bp, Tp, WG_KV_HEADS, HEAD_DIM))

        na, hg, mla, wg = _in_proj(lidx, hs, mod4, 1, True, norm_g3, w_proj)
        y_na = _na_lat(lidx, kstart, ktype, na, cna_k, cna_v, na_bias)
        y_hg, _ = _hgrn(lidx, hg, hg_lb_logits, hg_g3, st_bd[:, layer], True)
        kl, vl, _ = _mla_kv(lidx, mla, 4, mla, 5, mla, 6, g_kv3, w_kvb, c4, s4, True, True, False)
        kx, vx = _mla_kv(lidx, cache_mla_ckv, 0, ckr4, 0, ckr4, 0, g_kv3, w_kvb, c4, s4, False, False, True)
        y_mla = _mla_attn(lidx, mla, g_q3, w_qb, c4, s4, [(kl, vl), (kx, vx)], True)
        y_wg = _wg_lat(lidx, wg_sink, wg, cwg_k, cwg_v, cq, sq, ck, sk)
        hs = _post(lidx, hs, mod4, 1, True, norm_g3, (y_na, y_hg, y_mla, y_wg), w_merge, wb, wo, final_g2, final)

    outs = [jnp.stack(cs, axis=1) for cs in caches]
    return (hp, hs, outs[0], outs[1], outs[2], outs[3], outs[4], outs[5], outs[6])
```

```python
import functools

import jax
import jax.numpy as jnp
import numpy as np
from jax import lax
from jax.experimental import pallas as pl
from jax.experimental.pallas import tpu as pltpu

F32 = jnp.float32
BF16 = jnp.bfloat16

D_MODEL = 1024
DEPTH = 4
GRID_W = 64
HEAD_DIM = 64
BRANCH_W = 256
N_BRANCH = 4
NA_HEADS = 4
NA_WIN_H = 8
NA_WIN_W = 16
HG_HEADS = 4
HG_DK = 64
HG_CHUNK = 64
HG_SUB = 16
LOG2E = 1.4426950408889634
F_FLOOR = 1e-30
MLA_HEADS = 4
MLA_Q_RANK = 256
MLA_KV_RANK = 128
MLA_NOPE = 64
MLA_ROPE = 32
WG_HEADS = 4
WG_KV_HEADS = 2
WG_WINDOW = 128
ROPE_BASE = 10000.0
EPS = 1e-6
NEG_INF = -1e30

_IN_LAYOUT = (
    ('na_q', 256), ('na_k', 256), ('na_v', 256), ('na_z', 256),
    ('hg_q', 256), ('hg_ff', 256), ('hg_fb', 256), ('hg_i', 256), ('hg_og', 256), ('hg_z', 256),
    ('mla_qa', 256), ('mla_kva', 160), ('mla_z', 256),
    ('wg_q', 256), ('wg_k', 128), ('wg_v', 128), ('wg_z', 256),
    ('merge', 4096),
)
_IN_OFF = {}
_o = 0
for _n, _w in _IN_LAYOUT:
    _IN_OFF[_n] = (_o, _o + _w)
    _o += _w

NA_W = 1024
HG_W = 1536
MLA_W = 896
WG_W = 1152
PROJ_W = NA_W + HG_W + MLA_W + WG_W

TOK_TILE = 256
NA_QROWS = 4
NA_KROWS = 12
WG_KWIN = 512
VMEM_LIMIT = 56 * 1024 * 1024


def _cparams(sem):
    return pltpu.CompilerParams(dimension_semantics=sem, vmem_limit_bytes=VMEM_LIMIT)


def _dot(a, b):
    return jnp.dot(a, b, preferred_element_type=F32)


def _dot_nt(a, b):
    return lax.dot_general(a, b, (((1,), (1,)), ((), ())), preferred_element_type=F32)


def _dot_tn(a, b):
    return lax.dot_general(a, b, (((0,), (0,)), ((), ())), preferred_element_type=F32)


def _sigmoid(x):
    return 1.0 / (1.0 + jnp.exp(-x))


def _silu(x):
    return x * _sigmoid(x)


def _lane_mask(width, lo, hi, dtype):
    lane = lax.broadcasted_iota(jnp.int32, (1, width), 1)
    return ((lane >= lo) & (lane < hi)).astype(dtype)


def _seg_matrix(width, seg):
    r = lax.broadcasted_iota(jnp.int32, (width, width), 0) // seg
    c = lax.broadcasted_iota(jnp.int32, (width, width), 1) // seg
    return r == c


def _seg_mean(x2, seg_bf16, seg):
    hi = x2.astype(BF16)
    lo = (x2 - hi.astype(F32)).astype(BF16)
    return (_dot(hi, seg_bf16) + _dot(lo, seg_bf16)) * (1.0 / seg)


def _mod_kernel(c_ref, w_ref, b_ref, o_ref):
    s = _silu(c_ref[...])
    o_ref[0] = _dot(s.astype(BF16), w_ref[0].astype(BF16)) + b_ref[0]


def _modulation(cond, w_mod, b_mod):
    nrow = cond.shape[0]
    tn = 1024
    return pl.pallas_call(
        _mod_kernel,
        grid=(DEPTH, 3 * D_MODEL // tn),
        in_specs=[
            pl.BlockSpec((nrow, D_MODEL), lambda l, j: (0, 0)),
            pl.BlockSpec((1, D_MODEL, tn), lambda l, j: (l, 0, j)),
            pl.BlockSpec((1, 1, tn), lambda l, j: (l, 0, j)),
        ],
        out_specs=pl.BlockSpec((1, nrow, tn), lambda l, j: (l, 0, j)),
        out_shape=jax.ShapeDtypeStruct((DEPTH, nrow, 3 * D_MODEL), F32),
        compiler_params=_cparams(("arbitrary", "arbitrary")),
        name="modulation",
    )(cond, w_mod, b_mod.reshape(DEPTH, 1, 3 * D_MODEL))


def _mod_norm(x, mod_ref, g_ref):
    ms = jnp.mean(x * x, axis=-1, keepdims=True)
    y = x * lax.rsqrt(ms + EPS) * g_ref[0]
    shift = mod_ref[0, 0, :, 0:D_MODEL]
    scale = mod_ref[0, 0, :, D_MODEL:2 * D_MODEL]
    gate = mod_ref[0, 0, :, 2 * D_MODEL:3 * D_MODEL]
    return y * (1.0 + scale) + shift, gate


def _in_kernel(l_ref, x_ref, mod_ref, g_ref, w_ref, na_ref, hg_ref, mla_ref, wg_ref):
    h, _ = _mod_norm(x_ref[0], mod_ref, g_ref)
    hb = h.astype(BF16)
    off = 0
    for ref, width in ((na_ref, NA_W), (hg_ref, HG_W), (mla_ref, MLA_W), (wg_ref, WG_W)):
        ref[0] = _dot(hb, w_ref[0, :, off:off + width])
        off += width


def _in_proj(lidx, x, mod4, mod_row0, per_batch, norm_g3, w_proj):
    B, T, _ = x.shape
    nt = T // TOK_TILE
    mrow = (lambda b: mod_row0 + b) if per_batch else (lambda b: mod_row0)
    grid_spec = pltpu.PrefetchScalarGridSpec(
        num_scalar_prefetch=1,
        grid=(B, nt),
        in_specs=[
            pl.BlockSpec((1, TOK_TILE, D_MODEL), lambda b, t, l: (b, t, 0)),
            pl.BlockSpec((1, 1, 1, 3 * D_MODEL), lambda b, t, l: (l[0], mrow(b), 0, 0)),
            pl.BlockSpec((1, 1, D_MODEL), lambda b, t, l: (l[0], 0, 0)),
            pl.BlockSpec((1, D_MODEL, PROJ_W), lambda b, t, l: (l[0], 0, 0)),
        ],
        out_specs=[
            pl.BlockSpec((1, TOK_TILE, NA_W), lambda b, t, l: (b, t, 0)),
            pl.BlockSpec((1, TOK_TILE, HG_W), lambda b, t, l: (b, t, 0)),
            pl.BlockSpec((1, TOK_TILE, MLA_W), lambda b, t, l: (b, t, 0)),
            pl.BlockSpec((1, TOK_TILE, WG_W), lambda b, t, l: (b, t, 0)),
        ],
    )
    return pl.pallas_call(
        _in_kernel,
        grid_spec=grid_spec,
        out_shape=[
            jax.ShapeDtypeStruct((B, T, NA_W), F32),
            jax.ShapeDtypeStruct((B, T, HG_W), F32),
            jax.ShapeDtypeStruct((B, T, MLA_W), F32),
            jax.ShapeDtypeStruct((B, T, WG_W), F32),
        ],
        compiler_params=_cparams(("arbitrary", "arbitrary")),
        name="in_proj",
    )(lidx, x, mod4, norm_g3, w_proj)


def _post_kernel(l_ref, x_ref, mod_ref, g_ref, y0_ref, y1_ref, y2_ref, y3_ref, wm_ref, wb_ref, wo_ref,
                 fg_ref, o_ref, *, final):
    x = x_ref[0]
    h, gate = _mod_norm(x, mod_ref, g_ref)
    hb = h.astype(BF16)
    acc = jnp.zeros((TOK_TILE, D_MODEL), F32)
    for n, y_ref in enumerate((y0_ref, y1_ref, y2_ref, y3_ref)):
        mg = _sigmoid(_dot(hb, wm_ref[0, :, n * D_MODEL:(n + 1) * D_MODEL]))
        acc = acc + mg * _dot(y_ref[0], wb_ref[0, n])
    out = x + gate * _dot(acc.astype(BF16), wo_ref[0])
    if final:
        ms = jnp.mean(out * out, axis=-1, keepdims=True)
        out = out * lax.rsqrt(ms + EPS) * fg_ref[...]
    o_ref[0] = out


def _post(lidx, x, mod4, mod_row0, per_batch, norm_g3, ys, w_merge, w_branch, w_out, final_g2, final):
    B, T, _ = x.shape
    nt = T // TOK_TILE
    mrow = (lambda b: mod_row0 + b) if per_batch else (lambda b: mod_row0)
    yspec = pl.BlockSpec((1, TOK_TILE, BRANCH_W), lambda b, t, l: (b, t, 0))
    grid_spec = pltpu.PrefetchScalarGridSpec(
        num_scalar_prefetch=1,
        grid=(B, nt),
        in_specs=[
            pl.BlockSpec((1, TOK_TILE, D_MODEL), lambda b, t, l: (b, t, 0)),
            pl.BlockSpec((1, 1, 1, 3 * D_MODEL), lambda b, t, l: (l[0], mrow(b), 0, 0)),
            pl.BlockSpec((1, 1, D_MODEL), lambda b, t, l: (l[0], 0, 0)),
            yspec, yspec, yspec, yspec,
            pl.BlockSpec((1, D_MODEL, N_BRANCH * D_MODEL), lambda b, t, l: (l[0], 0, 0)),
            pl.BlockSpec((1, N_BRANCH, BRANCH_W, D_MODEL), lambda b, t, l: (l[0], 0, 0, 0)),
            pl.BlockSpec((1, D_MODEL, D_MODEL), lambda b, t, l: (l[0], 0, 0)),
            pl.BlockSpec((1, D_MODEL), lambda b, t, l: (0, 0)),
        ],
        out_specs=pl.BlockSpec((1, TOK_TILE, D_MODEL), lambda b, t, l: (b, t, 0)),
    )
    return pl.pallas_call(
        functools.partial(_post_kernel, final=final),
        grid_spec=grid_spec,
        out_shape=jax.ShapeDtypeStruct((B, T, D_MODEL), F32),
        compiler_params=_cparams(("arbitrary", "arbitrary")),
        name="post_final" if final else "post",
    )(lidx, x, mod4, norm_g3, *ys, w_merge, w_branch, w_out, final_g2)


def _attend(q, kparts, vparts, biases, qmasks, omasks, sinks=None):
    out = None
    for h in range(len(omasks)):
        qh = q[h] if qmasks is None else q * qmasks[h]
        ss = []
        for i, kp in enumerate(kparts):
            s = _dot_nt(qh, kp[h] if isinstance(kp, (list, tuple)) else kp)
            if biases is not None and biases[h][i] is not None:
                s = s + biases[h][i]
            ss.append(s)
        m = ss[0].max(axis=-1, keepdims=True)
        for s in ss[1:]:
            m = jnp.maximum(m, s.max(axis=-1, keepdims=True))
        if sinks is not None:
            m = jnp.maximum(m, sinks[h])
        den = None
        oh = None
        for s, vp in zip(ss, vparts):
            p = jnp.exp2(s - m)
            ps = p.sum(axis=-1, keepdims=True)
            den = ps if den is None else den + ps
            pv = _dot(p.astype(BF16), vp)
            oh = pv if oh is None else oh + pv
        if sinks is not None:
            den = den + jnp.exp2(sinks[h] - m)
        oh = (oh / den) * omasks[h]
        out = oh if out is None else out + oh
    return out


def _head_masks(width, seg, nheads, dtype):
    return [_lane_mask(width, h * seg, (h + 1) * seg, dtype) for h in range(nheads)]


def _na_ctx_kernel(q_ref, k_ref, v_ref, z_ref, o_ref):
    q = (q_ref[0] * (HEAD_DIM ** -0.5 * LOG2E)).astype(BF16)
    k = k_ref[0].astype(BF16)
    v = v_ref[0].astype(BF16)
    o = _attend(q, [k], [v], None, _head_masks(256, 64, NA_HEADS, BF16), _head_masks(256, 64, NA_HEADS, F32))
    o_ref[0] = (o * _silu(z_ref[0])).astype(BF16)


def _na_ctx(cols_na):
    B, T, _ = cols_na.shape

    def spec(j):
        return pl.BlockSpec((1, T, 256), lambda b, j=j: (b, 0, j))

    return pl.pallas_call(
        _na_ctx_kernel,
        grid=(B,),
        in_specs=[spec(0), spec(1), spec(2), spec(3)],
        out_specs=pl.BlockSpec((1, T, 256), lambda b: (b, 0, 0)),
        out_shape=jax.ShapeDtypeStruct((B, T, 256), BF16),
        compiler_params=_cparams(("arbitrary",)),
        name="na_ctx",
    )(cols_na, cols_na, cols_na, cols_na)


def _na_row_types(rows):
    wh = min(NA_WIN_H, rows)
    nblk = rows // NA_QROWS
    starts, types, table = [], [], []
    for blk in range(nblk):
        r0 = blk * NA_QROWS
        k0 = int(np.clip(r0 - wh // 2, 0, rows - NA_KROWS))
        desc = []
        for i in range(NA_QROWS):
            r = r0 + i
            ks = int(np.clip(r - wh // 2, 0, rows - wh))
            for j in range(NA_KROWS):
                kr = k0 + j
                desc.append((kr - r + NA_WIN_H - 1) if ks <= kr < ks + wh else None)
        desc = tuple(desc)
        if desc not in table:
            table.append(desc)
        starts.append(k0)
        types.append(table.index(desc))
    return starts, types, table


def _na_bias_kernel(rpb_ref, o_ref, t2_ref, *, table):
    l = pl.program_id(0)
    h = pl.program_id(1)
    nrow = 2 * NA_WIN_H - 1
    ncol = 2 * NA_WIN_W - 1
    c = lax.broadcasted_iota(jnp.int32, (GRID_W, 128), 0)
    lane = lax.broadcasted_iota(jnp.int32, (GRID_W, 128), 1)
    first = lane < GRID_W
    kc = jnp.where(first, lane, lane - GRID_W)
    qstart = jnp.clip(c - NA_WIN_W // 2, 0, GRID_W - NA_WIN_W)
    ok = (kc >= qstart) & (kc < qstart + NA_WIN_W)
    dcol = kc - c + NA_WIN_W - 1
    for a in range(-1, nrow):
        acc = jnp.full((GRID_W, 128), NEG_INF, F32)
        for bi in range(2 * NA_WIN_W - 1):
            base = (l * NA_HEADS + h) * nrow
            lo = rpb_ref[(base + a) * ncol + bi] if a >= 0 else jnp.float32(NEG_INF)
            hi = rpb_ref[(base + a + 1) * ncol + bi] if a + 1 < nrow else jnp.float32(NEG_INF)
            acc = jnp.where(dcol == bi, jnp.where(first, lo, hi), acc)
        t2_ref[a + 1] = jnp.where(ok, acc * LOG2E, NEG_INF)
    for ti, desc in enumerate(table):
        for i in range(NA_QROWS):
            for p in range(NA_KROWS // 2):
                a0 = desc[i * NA_KROWS + 2 * p]
                a1 = desc[i * NA_KROWS + 2 * p + 1]
                if a0 is None and a1 is None:
                    tile = jnp.full((GRID_W, 128), NEG_INF, F32)
                elif a1 is None:
                    tile = jnp.where(first, t2_ref[a0 + 1], NEG_INF)
                elif a0 is None:
                    tile = jnp.where(first, NEG_INF, t2_ref[a1])
                else:
                    tile = t2_ref[a0 + 1]
                o_ref[0, ti, 0, i * GRID_W:(i + 1) * GRID_W, p * 128:(p + 1) * 128] = tile


def _na_bias(na_rpb, table):
    nt = len(table)
    nq = NA_QROWS * GRID_W
    nk = NA_KROWS * GRID_W
    return pl.pallas_call(
        functools.partial(_na_bias_kernel, table=table),
        grid=(DEPTH, NA_HEADS),
        in_specs=[pl.BlockSpec(memory_space=pltpu.SMEM)],
        out_specs=pl.BlockSpec((1, nt, 1, nq, nk), lambda l, h: (l, 0, h, 0, 0)),
        out_shape=jax.ShapeDtypeStruct((DEPTH, nt, NA_HEADS, nq, nk), F32),
        scratch_shapes=[pltpu.VMEM((2 * NA_WIN_H, GRID_W, 128), F32)],
        compiler_params=_cparams(("arbitrary", "arbitrary")),
        name="na_bias",
    )(na_rpb.reshape(-1))


def _na_lat_kernel(l_ref, ks_ref, ty_ref, q_ref, k_ref, v_ref, z_ref, kc_ref, vc_ref, b_ref, o_ref):
    rb = pl.program_id(1)
    nk = NA_KROWS * GRID_W
    ks = pl.multiple_of(ks_ref[rb] * GRID_W, GRID_W)
    q = (q_ref[0] * (HEAD_DIM ** -0.5 * LOG2E)).astype(BF16)
    kw = k_ref[0, pl.ds(ks, nk), :].astype(BF16)
    vw = v_ref[0, pl.ds(ks, nk), :].astype(BF16)
    kc = kc_ref[0, 0].astype(BF16)
    vc = vc_ref[0, 0].astype(BF16)
    biases = [[b_ref[0, 0, h], None] for h in range(NA_HEADS)]
    o = _attend(q, [kw, kc], [vw, vc], biases,
                _head_masks(256, 64, NA_HEADS, BF16), _head_masks(256, 64, NA_HEADS, F32))
    o_ref[0] = (o * _silu(z_ref[0])).astype(BF16)


def _na_lat(lidx, kstart, ktype, cols_na, cache_k, cache_v, bias):
    B, T, _ = cols_na.shape
    L = cache_k.shape[2]
    nq = NA_QROWS * GRID_W
    nk = NA_KROWS * GRID_W
    nblk = T // nq
    grid_spec = pltpu.PrefetchScalarGridSpec(
        num_scalar_prefetch=3,
        grid=(B, nblk),
        in_specs=[
            pl.BlockSpec((1, nq, 256), lambda b, r, l, ks, ty: (b, r, 0)),
            pl.BlockSpec((1, T, 256), lambda b, r, l, ks, ty: (b, 0, 1)),
            pl.BlockSpec((1, T, 256), lambda b, r, l, ks, ty: (b, 0, 2)),
            pl.BlockSpec((1, nq, 256), lambda b, r, l, ks, ty: (b, r, 3)),
            pl.BlockSpec((1, 1, L, 256), lambda b, r, l, ks, ty: (b, l[0], 0, 0)),
            pl.BlockSpec((1, 1, L, 256), lambda b, r, l, ks, ty: (b, l[0], 0, 0)),
            pl.BlockSpec((1, 1, NA_HEADS, nq, nk), lambda b, r, l, ks, ty: (l[0], ty[r], 0, 0, 0)),
        ],
        out_specs=pl.BlockSpec((1, nq, 256), lambda b, r, l, ks, ty: (b, r, 0)),
    )
    return pl.pallas_call(
        _na_lat_kernel,
        grid_spec=grid_spec,
        out_shape=jax.ShapeDtypeStruct((B, T, 256), BF16),
        compiler_params=_cparams(("arbitrary", "arbitrary")),
        name="na_lat",
    )(lidx, kstart, ktype, cols_na, cols_na, cols_na, cols_na, cache_k, cache_v, bias)


def _wg_heads(q, kparts, vparts, bias, sink_ref, l):
    qm = _head_masks(128, 64, 2, BF16)
    om = _head_masks(128, 64, 2, F32)
    outs = []
    for grp, heads in enumerate(((0, 2), (1, 3))):
        qg = q[:, grp * 128:(grp + 1) * 128]
        sinks = [sink_ref[l, hh] * LOG2E for hh in heads]
        biases = None if bias is None else [[bias] + [None] * (len(kparts) - 1)] * 2
        outs.append(_attend(qg, kparts, vparts, biases, qm, om, sinks))
    return jnp.concatenate(outs, axis=-1)


def _wg_ctx_kernel(l_ref, sink_ref, q_ref, z_ref, k_ref, v_ref, o_ref):
    q = (q_ref[0] * (HEAD_DIM ** -0.5 * LOG2E)).astype(BF16)
    o = _wg_heads(q, [k_ref[0].astype(BF16)], [v_ref[0].astype(BF16)], None, sink_ref, l_ref[0])
    o_ref[0] = (o * _silu(z_ref[0])).astype(BF16)


def _wg_ctx(lidx, sink, cols_wg):
    B, T, _ = cols_wg.shape
    grid_spec = pltpu.PrefetchScalarGridSpec(
        num_scalar_prefetch=1,
        grid=(B,),
        in_specs=[
            pl.BlockSpec(memory_space=pltpu.SMEM),
            pl.BlockSpec((1, T, 256), lambda b, l: (b, 0, 0)),
            pl.BlockSpec((1, T, 256), lambda b, l: (b, 0, 2)),
            pl.BlockSpec((1, T, 128), lambda b, l: (b, 0, 6)),
            pl.BlockSpec((1, T, 128), lambda b, l: (b, 0, 8)),
        ],
        out_specs=pl.BlockSpec((1, T, 256), lambda b, l: (b, 0, 0)),
    )
    return pl.pallas_call(
        _wg_ctx_kernel,
        grid_spec=grid_spec,
        out_shape=jax.ShapeDtypeStruct((B, T, 256), BF16),
        compiler_params=_cparams(("arbitrary",)),
        name="wg_ctx",
    )(lidx, sink, cols_wg, cols_wg, cols_wg, cols_wg)


def _wg_lat_kernel(l_ref, sink_ref, q_ref, qr_ref, z_ref, k_ref, kr_ref, v_ref, kc_ref, vc_ref,
                   cq_ref, sq_ref, ck_ref, sk_ref, o_ref, *, seq):
    tb = pl.program_id(1)
    t0 = tb * TOK_TILE
    k0 = pl.multiple_of(jnp.clip(t0 - WG_WINDOW, 0, seq - WG_KWIN), 128)
    q = (q_ref[0] * cq_ref[...] + qr_ref[0] * sq_ref[...]) * (HEAD_DIM ** -0.5 * LOG2E)
    kwin = k_ref[0, pl.ds(k0, WG_KWIN), :] * ck_ref[pl.ds(k0, WG_KWIN), :] \
        + kr_ref[0, pl.ds(k0, WG_KWIN), :] * sk_ref[pl.ds(k0, WG_KWIN), :]
    vwin = v_ref[0, pl.ds(k0, WG_KWIN), :]
    qi = t0 + lax.broadcasted_iota(jnp.int32, (TOK_TILE, WG_KWIN), 0)
    kj = k0 + lax.broadcasted_iota(jnp.int32, (TOK_TILE, WG_KWIN), 1)
    bias = jnp.where(jnp.abs(qi - kj) <= WG_WINDOW, 0.0, NEG_INF).astype(F32)
    o = _wg_heads(q.astype(BF16),
                  [kwin.astype(BF16), kc_ref[0, 0].astype(BF16)],
                  [vwin.astype(BF16), vc_ref[0, 0].astype(BF16)],
                  bias, sink_ref, l_ref[0])
    o_ref[0] = (o * _silu(z_ref[0])).astype(BF16)


def _wg_lat(lidx, sink, cols_wg, cache_k, cache_v, cq, sq, ck, sk):
    B, T, _ = cols_wg.shape
    L = cache_k.shape[2]
    nt = T // TOK_TILE
    grid_spec = pltpu.PrefetchScalarGridSpec(
        num_scalar_prefetch=1,
        grid=(B, nt),
        in_specs=[
            pl.BlockSpec(memory_space=pltpu.SMEM),
            pl.BlockSpec((1, TOK_TILE, 256), lambda b, t, l: (b, t, 0)),
            pl.BlockSpec((1, TOK_TILE, 256), lambda b, t, l: (b, t, 1)),
            pl.BlockSpec((1, TOK_TILE, 256), lambda b, t, l: (b, t, 2)),
            pl.BlockSpec((1, T, 128), lambda b, t, l: (b, 0, 6)),
            pl.BlockSpec((1, T, 128), lambda b, t, l: (b, 0, 7)),
            pl.BlockSpec((1, T, 128), lambda b, t, l: (b, 0, 8)),
            pl.BlockSpec((1, 1, L, 128), lambda b, t, l: (b, l[0], 0, 0)),
            pl.BlockSpec((1, 1, L, 128), lambda b, t, l: (b, l[0], 0, 0)),
            pl.BlockSpec((TOK_TILE, 256), lambda b, t, l: (t, 0)),
            pl.BlockSpec((TOK_TILE, 256), lambda b, t, l: (t, 0)),
            pl.BlockSpec((T, 128), lambda b, t, l: (0, 0)),
            pl.BlockSpec((T, 128), lambda b, t, l: (0, 0)),
        ],
        out_specs=pl.BlockSpec((1, TOK_TILE, 256), lambda b, t, l: (b, t, 0)),
    )
    return pl.pallas_call(
        functools.partial(_wg_lat_kernel, seq=T),
        grid_spec=grid_spec,
        out_shape=jax.ShapeDtypeStruct((B, T, 256), BF16),
        compiler_params=_cparams(("arbitrary", "arbitrary")),
        name="wg_lat",
    )(lidx, sink, cols_wg, cols_wg, cols_wg, cols_wg, cols_wg, cols_wg, cache_k, cache_v, cq, sq, ck, sk)


def _mla_kv_kernel(l_ref, ckv_ref, kr_ref, krr_ref, g_ref, w_ref, c_ref, s_ref, k_out, v_out, *rest,
                   norm, rope):
    ckv = ckv_ref[0]
    if norm:
        ms = jnp.mean(ckv * ckv, axis=-1, keepdims=True)
        ckv = ckv * lax.rsqrt(ms + EPS) * g_ref[0]
        rest[0][0] = ckv
    kv = _dot(ckv.astype(BF16), w_ref[0])
    kr = kr_ref[0]
    if rope:
        kr = kr * c_ref[...] + krr_ref[0] * s_ref[...]
    for h in range(MLA_HEADS):
        k_out[0, :, 128 * h:128 * h + 128] = (kv[:, 128 * h:128 * h + 128] + kr).astype(BF16)
    v_out[0] = kv[:, 512:768].astype(BF16)


def _mla_kv(lidx, ckv_arr, ckv_blk, kr_arr, kr_blk, krr_arr, krr_blk, g_kv3, w_kvb, c4, s4, norm, rope,
            layered):
    B = ckv_arr.shape[0]
    T = ckv_arr.shape[-2]
    nt = T // TOK_TILE
    if layered:
        def src(blk):
            return pl.BlockSpec((None, 1, TOK_TILE, 128), lambda b, t, l: (b, l[0], t, 0))
    else:
        def src(blk):
            return pl.BlockSpec((1, TOK_TILE, 128), lambda b, t, l, blk=blk: (b, t, blk))
    out_specs = [
        pl.BlockSpec((1, TOK_TILE, 512), lambda b, t, l: (b, t, 0)),
        pl.BlockSpec((1, TOK_TILE, 256), lambda b, t, l: (b, t, 0)),
    ]
    out_shape = [
        jax.ShapeDtypeStruct((B, T, 512), BF16),
        jax.ShapeDtypeStruct((B, T, 256), BF16),
    ]
    if norm:
        out_specs.append(pl.BlockSpec((1, TOK_TILE, 128), lambda b, t, l: (b, t, 0)))
        out_shape.append(jax.ShapeDtypeStruct((B, T, 128), F32))
    grid_spec = pltpu.PrefetchScalarGridSpec(
        num_scalar_prefetch=1,
        grid=(B, nt),
        in_specs=[
            src(ckv_blk), src(kr_blk), src(krr_blk),
            pl.BlockSpec((1, 1, 128), lambda b, t, l: (l[0], 0, 0)),
            pl.BlockSpec((1, 128, 768), lambda b, t, l: (l[0], 0, 0)),
            pl.BlockSpec((TOK_TILE, 128), lambda b, t, l: (t if rope else 0, 0)),
            pl.BlockSpec((TOK_TILE, 128), lambda b, t, l: (t if rope else 0, 0)),
        ],
        out_specs=out_specs,
    )
    return pl.pallas_call(
        functools.partial(_mla_kv_kernel, norm=norm, rope=rope),
        grid_spec=grid_spec,
        out_shape=out_shape,
        compiler_params=_cparams(("arbitrary", "arbitrary")),
        name="mla_kv" + ("_norm" if norm else "") + ("_rope" if rope else ""),
    )(lidx, ckv_arr, kr_arr, krr_arr, g_kv3, w_kvb, c4, s4)


def _mla_attn_kernel(l_ref, qa_ref, z_ref, g_ref, w_ref, c_ref, s_ref, *refs, nparts, rope):
    kparts = [[refs[2 * i][0, :, 128 * h:128 * h + 128] for h in range(MLA_HEADS)] for i in range(nparts)]
    vparts = [refs[2 * i + 1][0] for i in range(nparts)]
    o_ref = refs[2 * nparts]
    qa = qa_ref[0]
    ms = jnp.mean(qa * qa, axis=-1, keepdims=True)
    qn = qa * lax.rsqrt(ms + EPS) * g_ref[0]
    qq = _dot(qn.astype(BF16), w_ref[0])
    scale = (MLA_NOPE + MLA_ROPE) ** -0.5 * LOG2E
    q = []
    for h in range(MLA_HEADS):
        qh = qq[:, 128 * h:128 * h + 128]
        if rope:
            qh = qh * c_ref[...] + qq[:, 512 + 128 * h:512 + 128 * h + 128] * s_ref[...]
        q.append((qh * scale).astype(BF16))
    o = _attend(q, kparts, vparts, None, None, _head_masks(256, 64, MLA_HEADS, F32))
    o_ref[0] = (o * _silu(z_ref[0])).astype(BF16)


def _mla_attn(lidx, cols_mla, g_q3, w_qb, c4, s4, kv_parts, rope):
    B, T, _ = cols_mla.shape
    nt = T // TOK_TILE
    in_specs = [
        pl.BlockSpec((1, TOK_TILE, 256), lambda b, t, l: (b, t, 0)),
        pl.BlockSpec((1, TOK_TILE, 256), lambda b, t, l: (b, t, 1)),
        pl.BlockSpec((1, 1, 256), lambda b, t, l: (l[0], 0, 0)),
        pl.BlockSpec((1, 256, 1024), lambda b, t, l: (l[0], 0, 0)),
        pl.BlockSpec((TOK_TILE, 128), lambda b, t, l: (t if rope else 0, 0)),
        pl.BlockSpec((TOK_TILE, 128), lambda b, t, l: (t if rope else 0, 0)),
    ]
    args = [lidx, cols_mla, cols_mla, g_q3, w_qb, c4, s4]
    for kk, vv in kv_parts:
        in_specs.append(pl.BlockSpec((1, kk.shape[1], 512), lambda b, t, l: (b, 0, 0)))
        in_specs.append(pl.BlockSpec((1, vv.shape[1], 256), lambda b, t, l: (b, 0, 0)))
        args += [kk, vv]
    grid_spec = pltpu.PrefetchScalarGridSpec(
        num_scalar_prefetch=1,
        grid=(B, nt),
        in_specs=in_specs,
        out_specs=pl.BlockSpec((1, TOK_TILE, 256), lambda b, t, l: (b, t, 0)),
    )
    return pl.pallas_call(
        functools.partial(_mla_attn_kernel, nparts=len(kv_parts), rope=rope),
        grid_spec=grid_spec,
        out_shape=jax.ShapeDtypeStruct((B, T, 256), BF16),
        compiler_params=_cparams(("arbitrary", "arbitrary")),
        name="mla_attn_lat" if rope else "mla_attn_ctx",
    )(*args)


def _hg_chunk(c, rev, lb, q_ref, f_ref, i_ref, st_ref, a_sc, v_sc, o_sc, tri, seg, bdmask):
    C = HG_CHUNK
    r0 = pl.multiple_of(c * C, C)
    qv = q_ref[0, pl.ds(r0, C), :]
    qs = _silu(qv)
    fz = f_ref[0, pl.ds(r0, C), :]
    f = lb + (1.0 - lb) * _sigmoid(fz)
    logf = jnp.log(jnp.maximum(f, F_FLOOR))
    kk = (1.0 - lb) * _sigmoid(-fz)
    v = i_ref[0, pl.ds(r0, C), :]
    hi = logf.astype(BF16)
    r1 = logf - hi.astype(F32)
    mid = r1.astype(BF16)
    lo = (r1 - mid.astype(F32)).astype(BF16)
    b = _dot(tri, hi) + _dot(tri, mid) + _dot(tri, lo)
    b2 = b * LOG2E
    a_sc[...] = jnp.log2(kk) - b2
    v_sc[...] = v
    total = b[0:1, :] if rev else b[C - 1:C, :]
    st = st_ref[...]
    o_sc[...] = _dot_nt((qs * jnp.exp2(b2)).astype(BF16), st.astype(BF16))
    S = HG_SUB
    nsub = C // S
    hm = _head_masks(256, HG_DK, HG_HEADS, F32)
    sub = lax.broadcasted_iota(jnp.int32, (8, 256), 0)

    qx, kd_rows, v_rows, blk_of_col = [], [], [], []
    for g in range(nsub):
        g0 = g * S
        src_lo, src_hi = (g0 + S, C) if rev else (0, g0)
        if src_hi <= src_lo:
            continue
        ct = b2[g0 + S:g0 + S + 1, :] if rev else b2[g0 - 1:g0, :]
        qd = qs[g0:g0 + S, :] * jnp.exp2(b2[g0:g0 + S, :] - ct)
        qx += [qd * m for m in hm]
        kd_rows.append(kk[src_lo:src_hi, :] * jnp.exp2(ct - b2[src_lo:src_hi, :]))
        v_rows.append(v[src_lo:src_hi, :])
        blk_of_col += [len(kd_rows) - 1] * (src_hi - src_lo)
    ncol = 128
    pad = ncol - len(blk_of_col)
    kd_all = jnp.concatenate(kd_rows + [jnp.zeros((pad, 256), F32)], axis=0).astype(BF16)
    v_all = jnp.concatenate(v_rows + [jnp.zeros((pad, 256), F32)], axis=0).astype(BF16)
    nblk = len(kd_rows)
    rowblk = lax.broadcasted_iota(jnp.int32, (nblk * HG_HEADS * S, ncol), 0) // (HG_HEADS * S)
    colidx = lax.broadcasted_iota(jnp.int32, (nblk * HG_HEADS * S, ncol), 1)
    colblk = jnp.full_like(colidx, nblk)
    start = 0
    for k_, rows_ in enumerate(kd_rows):
        n_ = rows_.shape[0]
        colblk = jnp.where((colidx >= start) & (colidx < start + n_), k_, colblk)
        start += n_
    att = _dot_nt(jnp.concatenate(qx, axis=0).astype(BF16), kd_all)
    att = jnp.where(rowblk == colblk, att, 0.0)
    off = _dot(att.astype(BF16), v_all)

    k_ = 0
    for g in range(nsub):
        g0 = g * S
        parts, spans = [], []
        for s in range(g0, g0 + S):
            t0 = (s // 8) * 8
            arow = a_sc[s:s + 1, :]
            keep = (sub <= s - t0) if rev else (sub >= s - t0)
            rdiag = qs[t0:t0 + 8, :] * jnp.exp2(jnp.where(keep, b2[t0:t0 + 8, :] + arow, NEG_INF))
            rest_lo, rest_hi = (g0, t0) if rev else (t0 + 8, g0 + S)
            if rest_hi > rest_lo:
                rrest = qs[rest_lo:rest_hi, :] * jnp.exp2(b2[rest_lo:rest_hi, :] + arow)
                parts += [rrest, rdiag] if rev else [rdiag, rrest]
            else:
                parts.append(rdiag)
            spans.append((min(rest_lo, t0), max(rest_hi, t0 + 8)) if rest_hi > rest_lo else (t0, t0 + 8))
        res = _dot(jnp.concatenate(parts, axis=0).astype(BF16), seg)
        acc = [o_sc[g0 + 8 * i:g0 + 8 * i + 8, :] for i in range(S // 8)]
        has_off = (g < nsub - 1) if rev else (g > 0)
        if has_off:
            base = k_ * HG_HEADS * S
            k_ += 1
            for h in range(HG_HEADS):
                for i in range(S // 8):
                    acc[i] = acc[i] + off[base + h * S + 8 * i:base + h * S + 8 * i + 8, :] * hm[h]
        pos = 0
        for s, (lo_r, hi_r) in zip(range(g0, g0 + S), spans):
            vrow = v_sc[s:s + 1, :]
            for i in range((hi_r - lo_r) // 8):
                ti = (lo_r - g0) // 8 + i
                acc[ti] = acc[ti] + res[pos + 8 * i:pos + 8 * i + 8, :] * vrow
            pos += hi_r - lo_r
        for i in range(S // 8):
            o_sc[g0 + 8 * i:g0 + 8 * i + 8, :] = acc[i]
    kd = (kk * jnp.exp(total - b)).astype(BF16)
    st_ref[...] = jnp.exp(total) * st + bdmask * _dot_tn(v.astype(BF16), kd)


def _hg_kernel(l_ref, q_ref, f_ref, i_ref, og_ref, z_ref, lbl_ref, g_ref, s0_ref, y_ref, sto_ref,
               st_ref, oacc_ref, a_sc, v_sc, o_sc, *, nb):
    j = pl.program_id(1)
    l = l_ref[0]
    C = HG_CHUNK
    ncb = TOK_TILE // C

    @pl.when((j == 0) | (j == nb))
    def _():
        st_ref[...] = s0_ref[0, 0]

    lg = lbl_ref[...]
    mx = lg[0]
    for m in range(1, DEPTH):
        mx = jnp.maximum(mx, lg[m])
    ex = [jnp.exp(lg[m] - mx) for m in range(DEPTH)]
    tot = ex[0]
    for m in range(1, DEPTH):
        tot = tot + ex[m]
    lbs = jnp.zeros_like(tot)
    for m in range(1, DEPTH):
        lbs = lbs + jnp.where(m <= l, ex[m] / tot, 0.0)

    rr = lax.broadcasted_iota(jnp.int32, (C, C), 0)
    cc = lax.broadcasted_iota(jnp.int32, (C, C), 1)
    seg = _seg_matrix(256, HG_DK)
    seg_b = seg.astype(BF16)
    bdmask = seg.astype(F32)

    @pl.when(j < nb)
    def _():
        tri = (cc <= rr).astype(BF16)
        lb = lbs[0:1, :]

        def body(c, carry):
            _hg_chunk(c, False, lb, q_ref, f_ref, i_ref, st_ref, a_sc, v_sc, o_sc, tri, seg_b, bdmask)
            row = pl.multiple_of(j * TOK_TILE + c * C, C)
            oacc_ref[pl.ds(row, C), :] = o_sc[...]
            return carry

        lax.fori_loop(0, ncb, body, 0)

    @pl.when(j >= nb)
    def _():
        tri = (cc >= rr).astype(BF16)
        lb = lbs[1:2, :]
        blk = 2 * nb - 1 - j

        def body(ci, carry):
            c = ncb - 1 - ci
            _hg_chunk(c, True, lb, q_ref, f_ref, i_ref, st_ref, a_sc, v_sc, o_sc, tri, seg_b, bdmask)
            r0 = pl.multiple_of(c * C, C)
            row = pl.multiple_of(blk * TOK_TILE + c * C, C)
            o = oacc_ref[pl.ds(row, C), :] + o_sc[...]
            ms = _seg_mean(o * o, seg_b, HG_DK)
            on = o * lax.rsqrt(ms + EPS) * g_ref[0]
            on = on * _sigmoid(og_ref[0, pl.ds(r0, C), :])
            y_ref[0, pl.ds(r0, C), :] = (on * _silu(z_ref[0, pl.ds(r0, C), :])).astype(BF16)
            return carry

        lax.fori_loop(0, ncb, body, 0)

    sto_ref[0, 0] = st_ref[...]


def _hgrn(lidx, cols_hg, lb_logits, g3, s0, s0_per_batch):
    B, T, _ = cols_hg.shape
    nb = T // TOK_TILE

    def blk(j):
        return jnp.where(j < nb, j, 2 * nb - 1 - j)

    def dirn(j):
        return jnp.where(j < nb, 0, 1)

    def col(jcol):
        return pl.BlockSpec((1, TOK_TILE, 256), lambda b, j, l, jcol=jcol: (b, blk(j), jcol))

    grid_spec = pltpu.PrefetchScalarGridSpec(
        num_scalar_prefetch=1,
        grid=(B, 2 * nb),
        in_specs=[
            col(0),
            pl.BlockSpec((1, TOK_TILE, 256), lambda b, j, l: (b, blk(j), 1 + dirn(j))),
            col(3), col(4), col(5),
            pl.BlockSpec((DEPTH, 2, 256), lambda b, j, l: (0, 0, 0)),
            pl.BlockSpec((1, 1, 256), lambda b, j, l: (l[0], 0, 0)),
            pl.BlockSpec((1, 1, 256, 256),
                         (lambda b, j, l: (b, dirn(j), 0, 0)) if s0_per_batch else (lambda b, j, l: (0, 0, 0, 0))),
        ],
        out_specs=[
            pl.BlockSpec((1, TOK_TILE, 256), lambda b, j, l: (b, jnp.where(j < nb, nb - 1, 2 * nb - 1 - j), 0)),
            pl.BlockSpec((1, 1, 256, 256), lambda b, j, l: (b, dirn(j), 0, 0)),
        ],
        scratch_shapes=[
            pltpu.VMEM((256, 256), F32),
            pltpu.VMEM((T, 256), F32),
            pltpu.VMEM((HG_CHUNK, 256), F32),
            pltpu.VMEM((HG_CHUNK, 256), F32),
            pltpu.VMEM((HG_CHUNK, 256), F32),
        ],
    )
    return pl.pallas_call(
        functools.partial(_hg_kernel, nb=nb),
        grid_spec=grid_spec,
        out_shape=[
            jax.ShapeDtypeStruct((B, T, 256), BF16),
            jax.ShapeDtypeStruct((B, 2, 256, 256), F32),
        ],
        compiler_params=_cparams(("arbitrary", "arbitrary")),
        name="hgrn",
    )(lidx, cols_hg, cols_hg, cols_hg, cols_hg, cols_hg, lb_logits, g3, s0)


def _rot_cols(w, nheads):
    shp = w.shape
    r = shp[-1] // nheads
    w4 = w.reshape(shp[:-1] + (nheads, 2, r // 2))
    return jnp.concatenate([-w4[..., 1, :], w4[..., 0, :]], axis=-1).reshape(shp)


def _perm_heads(w, nheads, order):
    shp = w.shape
    w3 = w.reshape(shp[:-1] + (nheads, shp[-1] // nheads))
    return w3[..., list(order), :].reshape(shp)


_WG_ORDER = (0, 2, 1, 3)


def _mla_rope_slot(a):
    pad = [(0, 0)] * (a.ndim - 1) + [(MLA_NOPE, 128 - MLA_NOPE - MLA_ROPE)]
    return jnp.pad(a, pad)


def _prep_w_in(w_in):
    def c(name):
        lo, hi = _IN_OFF[name]
        return w_in[:, :, lo:hi]

    kva = c('mla_kva')
    kr = kva[:, :, MLA_KV_RANK:]
    kr4 = _mla_rope_slot(kr)
    kr4r = _mla_rope_slot(_rot_cols(kr, 1))
    wq = _perm_heads(c('wg_q'), WG_HEADS, _WG_ORDER)
    wk = c('wg_k')
    parts = [
        c('na_q'), c('na_k'), c('na_v'), c('na_z'),
        c('hg_q'), c('hg_ff'), c('hg_fb'), c('hg_i'), c('hg_og'), c('hg_z'),
        c('mla_qa'), c('mla_z'), kva[:, :, :MLA_KV_RANK], kr4, kr4r,
        wq, _rot_cols(wq, WG_HEADS), _perm_heads(c('wg_z'), WG_HEADS, _WG_ORDER),
        wk, _rot_cols(wk, WG_KV_HEADS), c('wg_v'),
    ]
    return jnp.concatenate(parts, axis=-1).astype(BF16), c('merge').astype(BF16)


def _rope_tables(n_tok, rot_dim, reps):
    t = np.arange(n_tok)
    row = (t // GRID_W).astype(np.float32)
    col = (t % GRID_W).astype(np.float32)
    nf = rot_dim // 4
    inv = jnp.asarray(ROPE_BASE, F32) ** (-jnp.arange(nf, dtype=F32) / nf)
    ang = jnp.concatenate([jnp.asarray(row)[:, None] * inv, jnp.asarray(col)[:, None] * inv], axis=-1)
    cos, sin = jnp.cos(ang), jnp.sin(ang)
    return (jnp.tile(jnp.concatenate([cos, cos], axis=-1), (1, reps)),
            jnp.tile(jnp.concatenate([sin, sin], axis=-1), (1, reps)))


def _state_to_blockdiag(st):
    B = st.shape[0]
    stt = jnp.swapaxes(st, -1, -2)
    eye = jnp.eye(HG_HEADS, dtype=st.dtype)
    bd = stt[:, :, :, :, None, :] * eye[None, None, :, None, :, None]
    return bd.reshape(B, 2, HG_HEADS * HG_DK, HG_HEADS * HG_DK)


def _blockdiag_to_state(bd):
    B = bd.shape[0]
    b6 = bd.reshape(B, 2, HG_HEADS, HG_DK, HG_HEADS, HG_DK)
    diag = jnp.stack([b6[:, :, h, :, h, :] for h in range(HG_HEADS)], axis=2)
    return jnp.swapaxes(diag, -1, -2)


def kernel(x_prompt, x_sample, cache_na_k, cache_na_v, state_hgrn, cache_mla_ckv, cache_mla_krope, cache_wg_k, cache_wg_v, c, c_ctx, norm_g, w_mod, b_mod, w_in, na_rpb, hg_lb_logits, hg_norm_g, mla_q_norm_g, mla_kv_norm_g, mla_w_qb, mla_w_kvb, wg_sink, w_branch, w_out, final_g):
    Bp, Tp, _ = x_prompt.shape
    Bd, Td, _ = x_sample.shape
    L = cache_na_k.shape[2]
    rows = Td // GRID_W
    assert Tp % TOK_TILE == 0 and Td % TOK_TILE == 0 and L % TOK_TILE == 0
    assert rows % NA_QROWS == 0 and rows >= NA_KROWS and Td >= WG_KWIN

    w_proj, w_merge = _prep_w_in(w_in)
    wb = w_branch.astype(BF16)
    wb = wb.at[:, 3].set(_perm_heads(jnp.swapaxes(wb[:, 3], -1, -2), WG_HEADS, _WG_ORDER).swapaxes(-1, -2))
    wo = w_out.astype(BF16)
    qb = mla_w_qb.reshape(DEPTH, MLA_Q_RANK, MLA_HEADS, MLA_NOPE + MLA_ROPE)
    qb_g1 = jnp.pad(qb, ((0, 0), (0, 0), (0, 0), (0, 128 - MLA_NOPE - MLA_ROPE)))
    qb_g2 = _mla_rope_slot(_rot_cols(qb[..., MLA_NOPE:], 1))
    w_qb = jnp.concatenate([qb_g1.reshape(DEPTH, MLA_Q_RANK, -1), qb_g2.reshape(DEPTH, MLA_Q_RANK, -1)],
                           axis=-1).astype(BF16)
    kvb = mla_w_kvb.reshape(DEPTH, MLA_KV_RANK, MLA_HEADS, 2 * MLA_NOPE)
    kvb_k = jnp.pad(kvb[..., :MLA_NOPE], ((0, 0), (0, 0), (0, 0), (0, 128 - MLA_NOPE)))
    w_kvb = jnp.concatenate([kvb_k.reshape(DEPTH, MLA_KV_RANK, -1),
                             kvb[..., MLA_NOPE:].reshape(DEPTH, MLA_KV_RANK, -1)], axis=-1).astype(BF16)
    norm_g3 = norm_g.reshape(DEPTH, 1, D_MODEL)
    g_q3 = mla_q_norm_g.reshape(DEPTH, 1, MLA_Q_RANK)
    g_kv3 = mla_kv_norm_g.reshape(DEPTH, 1, MLA_KV_RANK)
    hg_g3 = jnp.tile(hg_norm_g, (1, HG_HEADS)).reshape(DEPTH, 1, HG_HEADS * HG_DK)
    final_g2 = final_g.reshape(1, D_MODEL)
    cq, sq = _rope_tables(Td, HEAD_DIM, WG_HEADS)
    ck, sk = cq[:, :128], sq[:, :128]
    c1, s1 = _rope_tables(Td, MLA_ROPE, 1)
    c4 = jnp.concatenate([jnp.ones((Td, MLA_NOPE), F32), c1, jnp.zeros((Td, 32), F32)], axis=-1)
    s4 = _mla_rope_slot(s1)

    nrow = 8 * ((1 + Bd + 7) // 8)
    cond = jnp.zeros((nrow, D_MODEL), F32).at[0].set(c_ctx).at[1:1 + Bd].set(c)
    mod4 = _modulation(cond, w_mod, b_mod).reshape(DEPTH, nrow, 1, 3 * D_MODEL)

    kstart, ktype, table = _na_row_types(rows)
    na_bias = _na_bias(na_rpb, table)
    kstart = jnp.asarray(kstart, jnp.int32)
    ktype = jnp.asarray(ktype, jnp.int32)

    cna_k = cache_na_k.reshape(Bd, DEPTH, L, NA_HEADS * HEAD_DIM)
    cna_v = cache_na_v.reshape(Bd, DEPTH, L, NA_HEADS * HEAD_DIM)
    cwg_k = cache_wg_k.reshape(Bd, DEPTH, L, WG_KV_HEADS * HEAD_DIM)
    cwg_v = cache_wg_v.reshape(Bd, DEPTH, L, WG_KV_HEADS * HEAD_DIM)
    ckr4 = _mla_rope_slot(cache_mla_krope)
    st_bd = _state_to_blockdiag(state_hgrn.reshape(Bd * DEPTH, 2, HG_HEADS, HG_DK, HG_DK)).reshape(
        Bd, DEPTH, 2, 256, 256)
    st_zero = jnp.zeros((1, 2, 256, 256), F32)

    hp, hs = x_prompt, x_sample
    caches = [[] for _ in range(7)]
    for layer in range(DEPTH):
        lidx = jnp.full((1,), layer, jnp.int32)
        final = layer == DEPTH - 1

        na, hg, mla, wg = _in_proj(lidx, hp, mod4, 0, False, norm_g3, w_proj)
        y_na = _na_ctx(na)
        y_hg, st_out = _hgrn(lidx, hg, hg_lb_logits, hg_g3, st_zero, False)
        kc_, vc_, ckv_n = _mla_kv(lidx, mla, 4, mla, 5, mla, 6, g_kv3, w_kvb, c4, s4, True, False, False)
        y_mla = _mla_attn(lidx, mla, g_q3, w_qb, c4, s4, [(kc_, vc_)], False)
        y_wg = _wg_ctx(lidx, wg_sink, wg)
        hp = _post(lidx, hp, mod4, 0, False, norm_g3, (y_na, y_hg, y_mla, y_wg), w_merge, wb, wo, final_g2, final)
        caches[0].append(na[:, :, 256:512].reshape(Bp, Tp, NA_HEADS, HEAD_DIM))
        caches[1].append(na[:, :, 512:768].reshape(Bp, Tp, NA_HEADS, HEAD_DIM))
        caches[2].append(_blockdiag_to_state(st_out))
        caches[3].append(ckv_n)
        caches[4].append(mla[:, :, 640 + MLA_NOPE:640 + MLA_NOPE + MLA_ROPE])
        caches[5].append(wg[:, :, 768:896].reshape(Bp, Tp, WG_KV_HEADS, HEAD_DIM))
        caches[6].append(wg[:, :, 1024:1152].reshape(Bp, Tp, WG_KV_HEADS, HEAD_DIM))

        na, hg, mla, wg = _in_proj(lidx, hs, mod4, 1, True, norm_g3, w_proj)
        y_na = _na_lat(lidx, kstart, ktype, na, cna_k, cna_v, na_bias)
        y_hg, _ = _hgrn(lidx, hg, hg_lb_logits, hg_g3, st_bd[:, layer], True)
        kl, vl, _ = _mla_kv(lidx, mla, 4, mla, 5, mla, 6, g_kv3, w_kvb, c4, s4, True, True, False)
        kx, vx = _mla_kv(lidx, cache_mla_ckv, 0, ckr4, 0, ckr4, 0, g_kv3, w_kvb, c4, s4, False, False, True)
        y_mla = _mla_attn(lidx, mla, g_q3, w_qb, c4, s4, [(kl, vl), (kx, vx)], True)
        y_wg = _wg_lat(lidx, wg_sink, wg, cwg_k, cwg_v, cq, sq, ck, sk)
        hs = _post(lidx, hs, mod4, 1, True, norm_g3, (y_na, y_hg, y_mla, y_wg), w_merge, wb, wo, final_g2, final)

    outs = [jnp.stack(cs, axis=1) for cs in caches]
    return (hp, hs, outs[0], outs[1], outs[2], outs[3], outs[4], outs[5], outs[6])
```

```python
import functools

import jax
import jax.numpy as jnp
import numpy as np
from jax import lax
from jax.experimental import pallas as pl
from jax.experimental.pallas import tpu as pltpu

F32 = jnp.float32
BF16 = jnp.bfloat16

D_MODEL = 1024
DEPTH = 4
GRID_W = 64
HEAD_DIM = 64
BRANCH_W = 256
N_BRANCH = 4
NA_HEADS = 4
NA_WIN_H = 8
NA_WIN_W = 16
HG_HEADS = 4
HG_DK = 64
HG_CHUNK = 64
HG_SUB = 16
LOG2E = 1.4426950408889634
F_FLOOR = 1e-30
MLA_HEADS = 4
MLA_Q_RANK = 256
MLA_KV_RANK = 128
MLA_NOPE = 64
MLA_ROPE = 32
WG_HEADS = 4
WG_KV_HEADS = 2
WG_WINDOW = 128
ROPE_BASE = 10000.0
EPS = 1e-6
NEG_INF = -1e30

_IN_LAYOUT = (
    ('na_q', 256), ('na_k', 256), ('na_v', 256), ('na_z', 256),
    ('hg_q', 256), ('hg_ff', 256), ('hg_fb', 256), ('hg_i', 256), ('hg_og', 256), ('hg_z', 256),
    ('mla_qa', 256), ('mla_kva', 160), ('mla_z', 256),
    ('wg_q', 256), ('wg_k', 128), ('wg_v', 128), ('wg_z', 256),
    ('merge', 4096),
)
_IN_OFF = {}
_o = 0
for _n, _w in _IN_LAYOUT:
    _IN_OFF[_n] = (_o, _o + _w)
    _o += _w

NA_W = 1024
HG_W = 1536
MLA_W = 768
WG_W = 768
PROJ_W = NA_W + HG_W + MLA_W + WG_W

TOK_TILE = 256
PROJ_TILE = 512
NA_QROWS = 4
NA_KROWS = 12
WG_KWIN = 512
VMEM_LIMIT = 56 * 1024 * 1024


def _cparams(sem):
    return pltpu.CompilerParams(dimension_semantics=sem, vmem_limit_bytes=VMEM_LIMIT)


def _dot(a, b):
    return jnp.dot(a, b, preferred_element_type=F32)


def _dot_nt(a, b):
    return lax.dot_general(a, b, (((1,), (1,)), ((), ())), preferred_element_type=F32)


def _dot_tn(a, b):
    return lax.dot_general(a, b, (((0,), (0,)), ((), ())), preferred_element_type=F32)


def _sigmoid(x):
    return 1.0 / (1.0 + jnp.exp(-x))


def _silu(x):
    return x * _sigmoid(x)


def _lane_mask(width, lo, hi, dtype):
    lane = lax.broadcasted_iota(jnp.int32, (1, width), 1)
    return ((lane >= lo) & (lane < hi)).astype(dtype)


def _seg_matrix(width, seg):
    r = lax.broadcasted_iota(jnp.int32, (width, width), 0) // seg
    c = lax.broadcasted_iota(jnp.int32, (width, width), 1) // seg
    return r == c


def _seg_mean(x2, seg_bf16, seg):
    hi = x2.astype(BF16)
    lo = (x2 - hi.astype(F32)).astype(BF16)
    return (_dot(hi, seg_bf16) + _dot(lo, seg_bf16)) * (1.0 / seg)


def _mod_kernel(c_ref, w_ref, b_ref, o_ref):
    s = _silu(c_ref[...])
    o_ref[0] = _dot(s.astype(BF16), w_ref[0].astype(BF16)) + b_ref[0]


def _modulation(cond, w_mod, b_mod):
    nrow = cond.shape[0]
    tn = 1024
    return pl.pallas_call(
        _mod_kernel,
        grid=(DEPTH, 3 * D_MODEL // tn),
        in_specs=[
            pl.BlockSpec((nrow, D_MODEL), lambda l, j: (0, 0)),
            pl.BlockSpec((1, D_MODEL, tn), lambda l, j: (l, 0, j)),
            pl.BlockSpec((1, 1, tn), lambda l, j: (l, 0, j)),
        ],
        out_specs=pl.BlockSpec((1, nrow, tn), lambda l, j: (l, 0, j)),
        out_shape=jax.ShapeDtypeStruct((DEPTH, nrow, 3 * D_MODEL), F32),
        compiler_params=_cparams(("arbitrary", "arbitrary")),
        name="modulation",
    )(cond, w_mod, b_mod.reshape(DEPTH, 1, 3 * D_MODEL))


def _mod_norm(x, mod_ref, g_ref):
    ms = jnp.mean(x * x, axis=-1, keepdims=True)
    y = x * lax.rsqrt(ms + EPS) * g_ref[0]
    shift = mod_ref[0, 0, :, 0:D_MODEL]
    scale = mod_ref[0, 0, :, D_MODEL:2 * D_MODEL]
    gate = mod_ref[0, 0, :, 2 * D_MODEL:3 * D_MODEL]
    return y * (1.0 + scale) + shift, gate


def _in_kernel(l_ref, x_ref, mod_ref, g_ref, w_ref, gkv_ref, wkvb_ref, rm_ref, c_ref, s_ref,
               na_ref, hg_ref, mla_ref, wg_ref, kall_ref, vall_ref, *rest, rope, emit_ckv):
    h, _ = _mod_norm(x_ref[0], mod_ref, g_ref)
    hb = h.astype(BF16)
    na_ref[0] = _dot(hb, w_ref[0, :, 0:NA_W])
    hg_ref[0] = _dot(hb, w_ref[0, :, NA_W:NA_W + HG_W])
    wg_ref[0] = _dot(hb, w_ref[0, :, NA_W + HG_W + MLA_W:PROJ_W])
    mla = _dot(hb, w_ref[0, :, NA_W + HG_W:NA_W + HG_W + MLA_W])
    mla_ref[0] = mla[:, 0:512]
    ckv = mla[:, 512:640]
    ms = jnp.mean(ckv * ckv, axis=-1, keepdims=True)
    ckv = ckv * lax.rsqrt(ms + EPS) * gkv_ref[0]
    kr = mla[:, 640:768]
    if emit_ckv:
        rest[0][0] = ckv
        rest[1][0] = kr
    if rope:
        kr = kr * c_ref[...] + _dot(kr.astype(BF16), rm_ref[...]) * s_ref[...]
    kv = _dot(ckv.astype(BF16), wkvb_ref[0])
    for hd in range(MLA_HEADS):
        kall_ref[0, :, 128 * hd:128 * hd + 128] = (kv[:, 128 * hd:128 * hd + 128] + kr).astype(BF16)
    vall_ref[0] = kv[:, 512:768].astype(BF16)


def _in_proj(lidx, x, mod4, mod_row0, per_batch, norm_g3, w_proj, g_kv3, w_kvb, rm_mla, c4, s4, rope):
    B, T, _ = x.shape
    tile = min(PROJ_TILE, T)
    nt = T // tile
    mrow = (lambda b: mod_row0 + b) if per_batch else (lambda b: mod_row0)
    emit_ckv = not rope

    def tok(width):
        return pl.BlockSpec((1, tile, width), lambda b, t, l: (b, t, 0))

    out_specs = [tok(NA_W), tok(HG_W), tok(512), tok(WG_W), tok(512), tok(256)]
    out_shape = [
        jax.ShapeDtypeStruct((B, T, NA_W), F32),
        jax.ShapeDtypeStruct((B, T, HG_W), F32),
        jax.ShapeDtypeStruct((B, T, 512), F32),
        jax.ShapeDtypeStruct((B, T, WG_W), F32),
        jax.ShapeDtypeStruct((B, T, 512), BF16),
        jax.ShapeDtypeStruct((B, T, 256), BF16),
    ]
    if emit_ckv:
        out_specs += [tok(128), tok(128)]
        out_shape += [jax.ShapeDtypeStruct((B, T, 128), F32), jax.ShapeDtypeStruct((B, T, 128), F32)]
    grid_spec = pltpu.PrefetchScalarGridSpec(
        num_scalar_prefetch=1,
        grid=(B, nt),
        in_specs=[
            pl.BlockSpec((1, tile, D_MODEL), lambda b, t, l: (b, t, 0)),
            pl.BlockSpec((1, 1, 1, 3 * D_MODEL), lambda b, t, l: (l[0], mrow(b), 0, 0)),
            pl.BlockSpec((1, 1, D_MODEL), lambda b, t, l: (l[0], 0, 0)),
            pl.BlockSpec((1, D_MODEL, PROJ_W), lambda b, t, l: (l[0], 0, 0)),
            pl.BlockSpec((1, 1, 128), lambda b, t, l: (l[0], 0, 0)),
            pl.BlockSpec((1, 128, 768), lambda b, t, l: (l[0], 0, 0)),
            pl.BlockSpec((128, 128), lambda b, t, l: (0, 0)),
            pl.BlockSpec((tile, 128), lambda b, t, l: (t if rope else 0, 0)),
            pl.BlockSpec((tile, 128), lambda b, t, l: (t if rope else 0, 0)),
        ],
        out_specs=out_specs,
    )
    return pl.pallas_call(
        functools.partial(_in_kernel, rope=rope, emit_ckv=emit_ckv),
        grid_spec=grid_spec,
        out_shape=out_shape,
        compiler_params=_cparams(("arbitrary", "arbitrary")),
        name="in_proj_lat" if rope else "in_proj_ctx",
    )(lidx, x, mod4, norm_g3, w_proj, g_kv3, w_kvb, rm_mla, c4, s4)


def _post_kernel(l_ref, x_ref, mod_ref, g_ref, y0_ref, y1_ref, y2_ref, y3_ref, wm_ref, wb_ref, wo_ref,
                 fg_ref, o_ref, *, final):
    x = x_ref[0]
    h, gate = _mod_norm(x, mod_ref, g_ref)
    hb = h.astype(BF16)
    acc = jnp.zeros(x.shape, F32)
    for n, y_ref in enumerate((y0_ref, y1_ref, y2_ref, y3_ref)):
        mg = _sigmoid(_dot(hb, wm_ref[0, :, n * D_MODEL:(n + 1) * D_MODEL]))
        acc = acc + mg * _dot(y_ref[0], wb_ref[0, n])
    out = x + gate * _dot(acc.astype(BF16), wo_ref[0])
    if final:
        ms = jnp.mean(out * out, axis=-1, keepdims=True)
        out = out * lax.rsqrt(ms + EPS) * fg_ref[...]
    o_ref[0] = out


def _post(lidx, x, mod4, mod_row0, per_batch, norm_g3, ys, w_merge, w_branch, w_out, final_g2, final):
    B, T, _ = x.shape
    tile = min(PROJ_TILE, T)
    nt = T // tile
    mrow = (lambda b: mod_row0 + b) if per_batch else (lambda b: mod_row0)
    yspec = pl.BlockSpec((1, tile, BRANCH_W), lambda b, t, l: (b, t, 0))
    grid_spec = pltpu.PrefetchScalarGridSpec(
        num_scalar_prefetch=1,
        grid=(B, nt),
        in_specs=[
            pl.BlockSpec((1, tile, D_MODEL), lambda b, t, l: (b, t, 0)),
            pl.BlockSpec((1, 1, 1, 3 * D_MODEL), lambda b, t, l: (l[0], mrow(b), 0, 0)),
            pl.BlockSpec((1, 1, D_MODEL), lambda b, t, l: (l[0], 0, 0)),
            yspec, yspec, yspec, yspec,
            pl.BlockSpec((1, D_MODEL, N_BRANCH * D_MODEL), lambda b, t, l: (l[0], 0, 0)),
            pl.BlockSpec((1, N_BRANCH, BRANCH_W, D_MODEL), lambda b, t, l: (l[0], 0, 0, 0)),
            pl.BlockSpec((1, D_MODEL, D_MODEL), lambda b, t, l: (l[0], 0, 0)),
            pl.BlockSpec((1, D_MODEL), lambda b, t, l: (0, 0)),
        ],
        out_specs=pl.BlockSpec((1, tile, D_MODEL), lambda b, t, l: (b, t, 0)),
    )
    return pl.pallas_call(
        functools.partial(_post_kernel, final=final),
        grid_spec=grid_spec,
        out_shape=jax.ShapeDtypeStruct((B, T, D_MODEL), F32),
        compiler_params=_cparams(("arbitrary", "arbitrary")),
        name="post_final" if final else "post",
    )(lidx, x, mod4, norm_g3, *ys, w_merge, w_branch, w_out, final_g2)


def _attend(q, kparts, vparts, biases, qmasks, omasks, sinks=None):
    out = None
    for h in range(len(omasks)):
        qh = q[h] if qmasks is None else q * qmasks[h]
        ss = []
        for i, kp in enumerate(kparts):
            s = _dot_nt(qh, kp[h] if isinstance(kp, (list, tuple)) else kp)
            if biases is not None and biases[h][i] is not None:
                s = s + biases[h][i]
            ss.append(s)
        m = ss[0].max(axis=-1, keepdims=True)
        for s in ss[1:]:
            m = jnp.maximum(m, s.max(axis=-1, keepdims=True))
        if sinks is not None:
            m = jnp.maximum(m, sinks[h])
        den = None
        oh = None
        for s, vp in zip(ss, vparts):
            p = jnp.exp2(s - m)
            ps = p.sum(axis=-1, keepdims=True)
            den = ps if den is None else den + ps
            pv = _dot(p.astype(BF16), vp)
            oh = pv if oh is None else oh + pv
        if sinks is not None:
            den = den + jnp.exp2(sinks[h] - m)
        oh = (oh / den) * omasks[h]
        out = oh if out is None else out + oh
    return out


def _head_masks(width, seg, nheads, dtype):
    return [_lane_mask(width, h * seg, (h + 1) * seg, dtype) for h in range(nheads)]


def _na_ctx_kernel(q_ref, k_ref, v_ref, z_ref, o_ref):
    q = (q_ref[0] * (HEAD_DIM ** -0.5 * LOG2E)).astype(BF16)
    k = k_ref[0].astype(BF16)
    v = v_ref[0].astype(BF16)
    o = _attend(q, [k], [v], None, _head_masks(256, 64, NA_HEADS, BF16), _head_masks(256, 64, NA_HEADS, F32))
    o_ref[0] = (o * _silu(z_ref[0])).astype(BF16)


def _na_ctx(cols_na):
    B, T, _ = cols_na.shape

    def spec(j):
        return pl.BlockSpec((1, T, 256), lambda b, j=j: (b, 0, j))

    return pl.pallas_call(
        _na_ctx_kernel,
        grid=(B,),
        in_specs=[spec(0), spec(1), spec(2), spec(3)],
        out_specs=pl.BlockSpec((1, T, 256), lambda b: (b, 0, 0)),
        out_shape=jax.ShapeDtypeStruct((B, T, 256), BF16),
        compiler_params=_cparams(("arbitrary",)),
        name="na_ctx",
    )(cols_na, cols_na, cols_na, cols_na)


def _na_row_types(rows):
    wh = min(NA_WIN_H, rows)
    nblk = rows // NA_QROWS
    starts, types, table = [], [], []
    for blk in range(nblk):
        r0 = blk * NA_QROWS
        k0 = int(np.clip(r0 - wh // 2, 0, rows - NA_KROWS))
        desc = []
        for i in range(NA_QROWS):
            r = r0 + i
            ks = int(np.clip(r - wh // 2, 0, rows - wh))
            for j in range(NA_KROWS):
                kr = k0 + j
                desc.append((kr - r + NA_WIN_H - 1) if ks <= kr < ks + wh else None)
        desc = tuple(desc)
        if desc not in table:
            table.append(desc)
        starts.append(k0)
        types.append(table.index(desc))
    return starts, types, table


def _na_bias_kernel(rpb_ref, o_ref, t2_ref, *, table):
    l = pl.program_id(0)
    h = pl.program_id(1)
    nrow = 2 * NA_WIN_H - 1
    ncol = 2 * NA_WIN_W - 1
    c = lax.broadcasted_iota(jnp.int32, (GRID_W, 128), 0)
    lane = lax.broadcasted_iota(jnp.int32, (GRID_W, 128), 1)
    first = lane < GRID_W
    kc = jnp.where(first, lane, lane - GRID_W)
    qstart = jnp.clip(c - NA_WIN_W // 2, 0, GRID_W - NA_WIN_W)
    ok = (kc >= qstart) & (kc < qstart + NA_WIN_W)
    dcol = kc - c + NA_WIN_W - 1
    for a in range(-1, nrow):
        acc = jnp.full((GRID_W, 128), NEG_INF, F32)
        for bi in range(2 * NA_WIN_W - 1):
            base = (l * NA_HEADS + h) * nrow
            lo = rpb_ref[(base + a) * ncol + bi] if a >= 0 else jnp.float32(NEG_INF)
            hi = rpb_ref[(base + a + 1) * ncol + bi] if a + 1 < nrow else jnp.float32(NEG_INF)
            acc = jnp.where(dcol == bi, jnp.where(first, lo, hi), acc)
        t2_ref[a + 1] = jnp.where(ok, acc * LOG2E, NEG_INF)
    for ti, desc in enumerate(table):
        for i in range(NA_QROWS):
            for p in range(NA_KROWS // 2):
                a0 = desc[i * NA_KROWS + 2 * p]
                a1 = desc[i * NA_KROWS + 2 * p + 1]
                if a0 is None and a1 is None:
                    tile = jnp.full((GRID_W, 128), NEG_INF, F32)
                elif a1 is None:
                    tile = jnp.where(first, t2_ref[a0 + 1], NEG_INF)
                elif a0 is None:
                    tile = jnp.where(first, NEG_INF, t2_ref[a1])
                else:
                    tile = t2_ref[a0 + 1]
                o_ref[0, ti, 0, i * GRID_W:(i + 1) * GRID_W, p * 128:(p + 1) * 128] = tile


def _na_bias(na_rpb, table):
    nt = len(table)
    nq = NA_QROWS * GRID_W
    nk = NA_KROWS * GRID_W
    return pl.pallas_call(
        functools.partial(_na_bias_kernel, table=table),
        grid=(DEPTH, NA_HEADS),
        in_specs=[pl.BlockSpec(memory_space=pltpu.SMEM)],
        out_specs=pl.BlockSpec((1, nt, 1, nq, nk), lambda l, h: (l, 0, h, 0, 0)),
        out_shape=jax.ShapeDtypeStruct((DEPTH, nt, NA_HEADS, nq, nk), F32),
        scratch_shapes=[pltpu.VMEM((2 * NA_WIN_H, GRID_W, 128), F32)],
        compiler_params=_cparams(("arbitrary", "arbitrary")),
        name="na_bias",
    )(na_rpb.reshape(-1))


def _na_lat_kernel(l_ref, ks_ref, ty_ref, q_ref, k_ref, v_ref, z_ref, kc_ref, vc_ref, b_ref, o_ref):
    rb = pl.program_id(1)
    nk = NA_KROWS * GRID_W
    ks = pl.multiple_of(ks_ref[rb] * GRID_W, GRID_W)
    q = (q_ref[0] * (HEAD_DIM ** -0.5 * LOG2E)).astype(BF16)
    kw = k_ref[0, pl.ds(ks, nk), :].astype(BF16)
    vw = v_ref[0, pl.ds(ks, nk), :].astype(BF16)
    kc = kc_ref[0, 0].astype(BF16)
    vc = vc_ref[0, 0].astype(BF16)
    biases = [[b_ref[0, 0, h], None] for h in range(NA_HEADS)]
    o = _attend(q, [kw, kc], [vw, vc], biases,
                _head_masks(256, 64, NA_HEADS, BF16), _head_masks(256, 64, NA_HEADS, F32))
    o_ref[0] = (o * _silu(z_ref[0])).astype(BF16)


def _na_lat(lidx, kstart, ktype, cols_na, cache_k, cache_v, bias):
    B, T, _ = cols_na.shape
    L = cache_k.shape[2]
    nq = NA_QROWS * GRID_W
    nk = NA_KROWS * GRID_W
    nblk = T // nq
    grid_spec = pltpu.PrefetchScalarGridSpec(
        num_scalar_prefetch=3,
        grid=(B, nblk),
        in_specs=[
            pl.BlockSpec((1, nq, 256), lambda b, r, l, ks, ty: (b, r, 0)),
            pl.BlockSpec((1, T, 256), lambda b, r, l, ks, ty: (b, 0, 1)),
            pl.BlockSpec((1, T, 256), lambda b, r, l, ks, ty: (b, 0, 2)),
            pl.BlockSpec((1, nq, 256), lambda b, r, l, ks, ty: (b, r, 3)),
            pl.BlockSpec((1, 1, L, 256), lambda b, r, l, ks, ty: (b, l[0], 0, 0)),
            pl.BlockSpec((1, 1, L, 256), lambda b, r, l, ks, ty: (b, l[0], 0, 0)),
            pl.BlockSpec((1, 1, NA_HEADS, nq, nk), lambda b, r, l, ks, ty: (l[0], ty[r], 0, 0, 0)),
        ],
        out_specs=pl.BlockSpec((1, nq, 256), lambda b, r, l, ks, ty: (b, r, 0)),
    )
    return pl.pallas_call(
        _na_lat_kernel,
        grid_spec=grid_spec,
        out_shape=jax.ShapeDtypeStruct((B, T, 256), BF16),
        compiler_params=_cparams(("arbitrary", "arbitrary")),
        name="na_lat",
    )(lidx, kstart, ktype, cols_na, cols_na, cols_na, cols_na, cache_k, cache_v, bias)


def _wg_heads(q, kparts, vparts, bias, sink_ref, l):
    qm = _head_masks(128, 64, 2, BF16)
    om = _head_masks(128, 64, 2, F32)
    outs = []
    for grp, heads in enumerate(((0, 2), (1, 3))):
        qg = q[:, grp * 128:(grp + 1) * 128]
        sinks = [sink_ref[l, hh] * LOG2E for hh in heads]
        biases = None if bias is None else [[bias] + [None] * (len(kparts) - 1)] * 2
        outs.append(_attend(qg, kparts, vparts, biases, qm, om, sinks))
    return jnp.concatenate(outs, axis=-1)


def _wg_ctx_kernel(l_ref, sink_ref, q_ref, z_ref, k_ref, v_ref, o_ref):
    q = (q_ref[0] * (HEAD_DIM ** -0.5 * LOG2E)).astype(BF16)
    o = _wg_heads(q, [k_ref[0].astype(BF16)], [v_ref[0].astype(BF16)], None, sink_ref, l_ref[0])
    o_ref[0] = (o * _silu(z_ref[0])).astype(BF16)


def _wg_ctx(lidx, sink, cols_wg):
    B, T, _ = cols_wg.shape
    grid_spec = pltpu.PrefetchScalarGridSpec(
        num_scalar_prefetch=1,
        grid=(B,),
        in_specs=[
            pl.BlockSpec(memory_space=pltpu.SMEM),
            pl.BlockSpec((1, T, 256), lambda b, l: (b, 0, 0)),
            pl.BlockSpec((1, T, 256), lambda b, l: (b, 0, 1)),
            pl.BlockSpec((1, T, 128), lambda b, l: (b, 0, 4)),
            pl.BlockSpec((1, T, 128), lambda b, l: (b, 0, 5)),
        ],
        out_specs=pl.BlockSpec((1, T, 256), lambda b, l: (b, 0, 0)),
    )
    return pl.pallas_call(
        _wg_ctx_kernel,
        grid_spec=grid_spec,
        out_shape=jax.ShapeDtypeStruct((B, T, 256), BF16),
        compiler_params=_cparams(("arbitrary",)),
        name="wg_ctx",
    )(lidx, sink, cols_wg, cols_wg, cols_wg, cols_wg)


def _wg_lat_kernel(l_ref, sink_ref, q_ref, z_ref, k_ref, v_ref, kc_ref, vc_ref,
                   cq_ref, sq_ref, ck_ref, sk_ref, rm_ref, o_ref, *, seq):
    tb = pl.program_id(1)
    t0 = tb * TOK_TILE
    k0 = pl.multiple_of(jnp.clip(t0 - WG_WINDOW, 0, seq - WG_KWIN), 128)
    q = q_ref[0]
    q = (q * cq_ref[...] + _dot(q.astype(BF16), rm_ref[...]) * sq_ref[...]) * (HEAD_DIM ** -0.5 * LOG2E)
    kwin = k_ref[0, pl.ds(k0, WG_KWIN), :]
    kwin = kwin * ck_ref[pl.ds(k0, WG_KWIN), :] \
        + _dot(kwin.astype(BF16), rm_ref[0:128, 0:128]) * sk_ref[pl.ds(k0, WG_KWIN), :]
    vwin = v_ref[0, pl.ds(k0, WG_KWIN), :]
    qi = t0 + lax.broadcasted_iota(jnp.int32, (TOK_TILE, WG_KWIN), 0)
    kj = k0 + lax.broadcasted_iota(jnp.int32, (TOK_TILE, WG_KWIN), 1)
    bias = jnp.where(jnp.abs(qi - kj) <= WG_WINDOW, 0.0, NEG_INF).astype(F32)
    o = _wg_heads(q.astype(BF16),
                  [kwin.astype(BF16), kc_ref[0, 0].astype(BF16)],
                  [vwin.astype(BF16), vc_ref[0, 0].astype(BF16)],
                  bias, sink_ref, l_ref[0])
    o_ref[0] = (o * _silu(z_ref[0])).astype(BF16)


def _wg_lat(lidx, sink, cols_wg, cache_k, cache_v, cq, sq, ck, sk, rm):
    B, T, _ = cols_wg.shape
    L = cache_k.shape[2]
    nt = T // TOK_TILE
    grid_spec = pltpu.PrefetchScalarGridSpec(
        num_scalar_prefetch=1,
        grid=(B, nt),
        in_specs=[
            pl.BlockSpec(memory_space=pltpu.SMEM),
            pl.BlockSpec((1, TOK_TILE, 256), lambda b, t, l: (b, t, 0)),
            pl.BlockSpec((1, TOK_TILE, 256), lambda b, t, l: (b, t, 1)),
            pl.BlockSpec((1, T, 128), lambda b, t, l: (b, 0, 4)),
            pl.BlockSpec((1, T, 128), lambda b, t, l: (b, 0, 5)),
            pl.BlockSpec((1, 1, L, 128), lambda b, t, l: (b, l[0], 0, 0)),
            pl.BlockSpec((1, 1, L, 128), lambda b, t, l: (b, l[0], 0, 0)),
            pl.BlockSpec((TOK_TILE, 256), lambda b, t, l: (t, 0)),
            pl.BlockSpec((TOK_TILE, 256), lambda b, t, l: (t, 0)),
            pl.BlockSpec((T, 128), lambda b, t, l: (0, 0)),
            pl.BlockSpec((T, 128), lambda b, t, l: (0, 0)),
            pl.BlockSpec((256, 256), lambda b, t, l: (0, 0)),
        ],
        out_specs=pl.BlockSpec((1, TOK_TILE, 256), lambda b, t, l: (b, t, 0)),
    )
    return pl.pallas_call(
        functools.partial(_wg_lat_kernel, seq=T),
        grid_spec=grid_spec,
        out_shape=jax.ShapeDtypeStruct((B, T, 256), BF16),
        compiler_params=_cparams(("arbitrary", "arbitrary")),
        name="wg_lat",
    )(lidx, sink, cols_wg, cols_wg, cols_wg, cols_wg, cache_k, cache_v, cq, sq, ck, sk, rm)


def _mla_kv_kernel(l_ref, ckv_ref, kr_ref, krr_ref, g_ref, w_ref, c_ref, s_ref, k_out, v_out, *rest,
                   norm, rope):
    ckv = ckv_ref[0]
    if norm:
        ms = jnp.mean(ckv * ckv, axis=-1, keepdims=True)
        ckv = ckv * lax.rsqrt(ms + EPS) * g_ref[0]
        rest[0][0] = ckv
    kv = _dot(ckv.astype(BF16), w_ref[0])
    kr = kr_ref[0]
    if rope:
        kr = kr * c_ref[...] + krr_ref[0] * s_ref[...]
    for h in range(MLA_HEADS):
        k_out[0, :, 128 * h:128 * h + 128] = (kv[:, 128 * h:128 * h + 128] + kr).astype(BF16)
    v_out[0] = kv[:, 512:768].astype(BF16)


def _mla_kv(lidx, ckv_arr, ckv_blk, kr_arr, kr_blk, krr_arr, krr_blk, g_kv3, w_kvb, c4, s4, norm, rope,
            layered):
    B = ckv_arr.shape[0]
    T = ckv_arr.shape[-2]
    nt = T // TOK_TILE
    if layered:
        def src(blk):
            return pl.BlockSpec((None, 1, TOK_TILE, 128), lambda b, t, l: (b, l[0], t, 0))
    else:
        def src(blk):
            return pl.BlockSpec((1, TOK_TILE, 128), lambda b, t, l, blk=blk: (b, t, blk))
    out_specs = [
        pl.BlockSpec((1, TOK_TILE, 512), lambda b, t, l: (b, t, 0)),
        pl.BlockSpec((1, TOK_TILE, 256), lambda b, t, l: (b, t, 0)),
    ]
    out_shape = [
        jax.ShapeDtypeStruct((B, T, 512), BF16),
        jax.ShapeDtypeStruct((B, T, 256), BF16),
    ]
    if norm:
        out_specs.append(pl.BlockSpec((1, TOK_TILE, 128), lambda b, t, l: (b, t, 0)))
        out_shape.append(jax.ShapeDtypeStruct((B, T, 128), F32))
    grid_spec = pltpu.PrefetchScalarGridSpec(
        num_scalar_prefetch=1,
        grid=(B, nt),
        in_specs=[
            src(ckv_blk), src(kr_blk), src(krr_blk),
            pl.BlockSpec((1, 1, 128), lambda b, t, l: (l[0], 0, 0)),
            pl.BlockSpec((1, 128, 768), lambda b, t, l: (l[0], 0, 0)),
            pl.BlockSpec((TOK_TILE, 128), lambda b, t, l: (t if rope else 0, 0)),
            pl.BlockSpec((TOK_TILE, 128), lambda b, t, l: (t if rope else 0, 0)),
        ],
        out_specs=out_specs,
    )
    return pl.pallas_call(
        functools.partial(_mla_kv_kernel, norm=norm, rope=rope),
        grid_spec=grid_spec,
        out_shape=out_shape,
        compiler_params=_cparams(("arbitrary", "arbitrary")),
        name="mla_kv" + ("_norm" if norm else "") + ("_rope" if rope else ""),
    )(lidx, ckv_arr, kr_arr, krr_arr, g_kv3, w_kvb, c4, s4)


def _mla_attn_kernel(l_ref, qa_ref, z_ref, g_ref, w_ref, c_ref, s_ref, *refs, nparts, rope):
    kparts = [[refs[2 * i][0, :, 128 * h:128 * h + 128] for h in range(MLA_HEADS)] for i in range(nparts)]
    vparts = [refs[2 * i + 1][0] for i in range(nparts)]
    o_ref = refs[2 * nparts]
    qa = qa_ref[0]
    ms = jnp.mean(qa * qa, axis=-1, keepdims=True)
    qn = qa * lax.rsqrt(ms + EPS) * g_ref[0]
    qq = _dot(qn.astype(BF16), w_ref[0])
    scale = (MLA_NOPE + MLA_ROPE) ** -0.5 * LOG2E
    q = []
    for h in range(MLA_HEADS):
        qh = qq[:, 128 * h:128 * h + 128]
        if rope:
            qh = qh * c_ref[...] + qq[:, 512 + 128 * h:512 + 128 * h + 128] * s_ref[...]
        q.append((qh * scale).astype(BF16))
    o = _attend(q, kparts, vparts, None, None, _head_masks(256, 64, MLA_HEADS, F32))
    o_ref[0] = (o * _silu(z_ref[0])).astype(BF16)


def _mla_attn(lidx, cols_mla, g_q3, w_qb, c4, s4, kv_parts, rope):
    B, T, _ = cols_mla.shape
    nt = T // TOK_TILE
    in_specs = [
        pl.BlockSpec((1, TOK_TILE, 256), lambda b, t, l: (b, t, 0)),
        pl.BlockSpec((1, TOK_TILE, 256), lambda b, t, l: (b, t, 1)),
        pl.BlockSpec((1, 1, 256), lambda b, t, l: (l[0], 0, 0)),
        pl.BlockSpec((1, 256, 1024), lambda b, t, l: (l[0], 0, 0)),
        pl.BlockSpec((TOK_TILE, 128), lambda b, t, l: (t if rope else 0, 0)),
        pl.BlockSpec((TOK_TILE, 128), lambda b, t, l: (t if rope else 0, 0)),
    ]
    args = [lidx, cols_mla, cols_mla, g_q3, w_qb, c4, s4]
    for kk, vv in kv_parts:
        in_specs.append(pl.BlockSpec((1, kk.shape[1], 512), lambda b, t, l: (b, 0, 0)))
        in_specs.append(pl.BlockSpec((1, vv.shape[1], 256), lambda b, t, l: (b, 0, 0)))
        args += [kk, vv]
    grid_spec = pltpu.PrefetchScalarGridSpec(
        num_scalar_prefetch=1,
        grid=(B, nt),
        in_specs=in_specs,
        out_specs=pl.BlockSpec((1, TOK_TILE, 256), lambda b, t, l: (b, t, 0)),
    )
    return pl.pallas_call(
        functools.partial(_mla_attn_kernel, nparts=len(kv_parts), rope=rope),
        grid_spec=grid_spec,
        out_shape=jax.ShapeDtypeStruct((B, T, 256), BF16),
        compiler_params=_cparams(("arbitrary", "arbitrary")),
        name="mla_attn_lat" if rope else "mla_attn_ctx",
    )(*args)


def _hg_chunk(c, rev, lb, q_ref, f_ref, i_ref, st_ref, a_sc, v_sc, o_sc, tri, seg, bdmask):
    C = HG_CHUNK
    r0 = pl.multiple_of(c * C, C)
    qv = q_ref[0, pl.ds(r0, C), :]
    qs = _silu(qv)
    fz = f_ref[0, pl.ds(r0, C), :]
    f = lb + (1.0 - lb) * _sigmoid(fz)
    logf = jnp.log(jnp.maximum(f, F_FLOOR))
    kk = (1.0 - lb) * _sigmoid(-fz)
    v = i_ref[0, pl.ds(r0, C), :]
    hi = logf.astype(BF16)
    r1 = logf - hi.astype(F32)
    mid = r1.astype(BF16)
    lo = (r1 - mid.astype(F32)).astype(BF16)
    b = _dot(tri, hi) + _dot(tri, mid) + _dot(tri, lo)
    b2 = b * LOG2E
    a_sc[...] = jnp.log2(kk) - b2
    v_sc[...] = v
    total = b[0:1, :] if rev else b[C - 1:C, :]
    st = st_ref[...]
    o_sc[...] = _dot_nt((qs * jnp.exp2(b2)).astype(BF16), st.astype(BF16))
    S = HG_SUB
    nsub = C // S
    hm = _head_masks(256, HG_DK, HG_HEADS, F32)
    sub = lax.broadcasted_iota(jnp.int32, (8, 256), 0)

    qx, kd_rows, v_rows, blk_of_col = [], [], [], []
    for g in range(nsub):
        g0 = g * S
        src_lo, src_hi = (g0 + S, C) if rev else (0, g0)
        if src_hi <= src_lo:
            continue
        ct = b2[g0 + S:g0 + S + 1, :] if rev else b2[g0 - 1:g0, :]
        qd = qs[g0:g0 + S, :] * jnp.exp2(b2[g0:g0 + S, :] - ct)
        qx += [qd * m for m in hm]
        kd_rows.append(kk[src_lo:src_hi, :] * jnp.exp2(ct - b2[src_lo:src_hi, :]))
        v_rows.append(v[src_lo:src_hi, :])
        blk_of_col += [len(kd_rows) - 1] * (src_hi - src_lo)
    ncol = 128
    pad = ncol - len(blk_of_col)
    kd_all = jnp.concatenate(kd_rows + [jnp.zeros((pad, 256), F32)], axis=0).astype(BF16)
    v_all = jnp.concatenate(v_rows + [jnp.zeros((pad, 256), F32)], axis=0).astype(BF16)
    nblk = len(kd_rows)
    rowblk = lax.broadcasted_iota(jnp.int32, (nblk * HG_HEADS * S, ncol), 0) // (HG_HEADS * S)
    colidx = lax.broadcasted_iota(jnp.int32, (nblk * HG_HEADS * S, ncol), 1)
    colblk = jnp.full_like(colidx, nblk)
    start = 0
    for k_, rows_ in enumerate(kd_rows):
        n_ = rows_.shape[0]
        colblk = jnp.where((colidx >= start) & (colidx < start + n_), k_, colblk)
        start += n_
    att = _dot_nt(jnp.concatenate(qx, axis=0).astype(BF16), kd_all)
    att = jnp.where(rowblk == colblk, att, 0.0)
    off = _dot(att.astype(BF16), v_all)

    k_ = 0
    for g in range(nsub):
        g0 = g * S
        parts, spans = [], []
        for s in range(g0, g0 + S):
            t0 = (s // 8) * 8
            arow = a_sc[s:s + 1, :]
            keep = (sub <= s - t0) if rev else (sub >= s - t0)
            rdiag = qs[t0:t0 + 8, :] * jnp.exp2(jnp.where(keep, b2[t0:t0 + 8, :] + arow, NEG_INF))
            rest_lo, rest_hi = (g0, t0) if rev else (t0 + 8, g0 + S)
            if rest_hi > rest_lo:
                rrest = qs[rest_lo:rest_hi, :] * jnp.exp2(b2[rest_lo:rest_hi, :] + arow)
                parts += [rrest, rdiag] if rev else [rdiag, rrest]
            else:
                parts.append(rdiag)
            spans.append((min(rest_lo, t0), max(rest_hi, t0 + 8)) if rest_hi > rest_lo else (t0, t0 + 8))
        res = _dot(jnp.concatenate(parts, axis=0).astype(BF16), seg)
        acc = [o_sc[g0 + 8 * i:g0 + 8 * i + 8, :] for i in range(S // 8)]
        has_off = (g < nsub - 1) if rev else (g > 0)
        if has_off:
            base = k_ * HG_HEADS * S
            k_ += 1
            for h in range(HG_HEADS):
                for i in range(S // 8):
                    acc[i] = acc[i] + off[base + h * S + 8 * i:base + h * S + 8 * i + 8, :] * hm[h]
        pos = 0
        for s, (lo_r, hi_r) in zip(range(g0, g0 + S), spans):
            vrow = v_sc[s:s + 1, :]
            for i in range((hi_r - lo_r) // 8):
                ti = (lo_r - g0) // 8 + i
                acc[ti] = acc[ti] + res[pos + 8 * i:pos + 8 * i + 8, :] * vrow
            pos += hi_r - lo_r
        for i in range(S // 8):
            o_sc[g0 + 8 * i:g0 + 8 * i + 8, :] = acc[i]
    kd = (kk * jnp.exp(total - b)).astype(BF16)
    st_ref[...] = jnp.exp(total) * st + bdmask * _dot_tn(v.astype(BF16), kd)


def _hg_kernel(l_ref, q_ref, f_ref, i_ref, og_ref, z_ref, lbl_ref, g_ref, s0_ref, y_ref, sto_ref,
               st_ref, oacc_ref, a_sc, v_sc, o_sc, *, nb):
    j = pl.program_id(1)
    l = l_ref[0]
    C = HG_CHUNK
    ncb = TOK_TILE // C

    @pl.when((j == 0) | (j == nb))
    def _():
        st_ref[...] = s0_ref[0, 0]

    lg = lbl_ref[...]
    mx = lg[0]
    for m in range(1, DEPTH):
        mx = jnp.maximum(mx, lg[m])
    ex = [jnp.exp(lg[m] - mx) for m in range(DEPTH)]
    tot = ex[0]
    for m in range(1, DEPTH):
        tot = tot + ex[m]
    lbs = jnp.zeros_like(tot)
    for m in range(1, DEPTH):
        lbs = lbs + jnp.where(m <= l, ex[m] / tot, 0.0)

    rr = lax.broadcasted_iota(jnp.int32, (C, C), 0)
    cc = lax.broadcasted_iota(jnp.int32, (C, C), 1)
    seg = _seg_matrix(256, HG_DK)
    seg_b = seg.astype(BF16)
    bdmask = seg.astype(F32)

    @pl.when(j < nb)
    def _():
        tri = (cc <= rr).astype(BF16)
        lb = lbs[0:1, :]

        def body(c, carry):
            _hg_chunk(c, False, lb, q_ref, f_ref, i_ref, st_ref, a_sc, v_sc, o_sc, tri, seg_b, bdmask)
            row = pl.multiple_of(j * TOK_TILE + c * C, C)
            oacc_ref[pl.ds(row, C), :] = o_sc[...]
            return carry

        lax.fori_loop(0, ncb, body, 0)

    @pl.when(j >= nb)
    def _():
        tri = (cc >= rr).astype(BF16)
        lb = lbs[1:2, :]
        blk = 2 * nb - 1 - j

        def body(ci, carry):
            c = ncb - 1 - ci
            _hg_chunk(c, True, lb, q_ref, f_ref, i_ref, st_ref, a_sc, v_sc, o_sc, tri, seg_b, bdmask)
            r0 = pl.multiple_of(c * C, C)
            row = pl.multiple_of(blk * TOK_TILE + c * C, C)
            o = oacc_ref[pl.ds(row, C), :] + o_sc[...]
            ms = _seg_mean(o * o, seg_b, HG_DK)
            on = o * lax.rsqrt(ms + EPS) * g_ref[0]
            on = on * _sigmoid(og_ref[0, pl.ds(r0, C), :])
            y_ref[0, pl.ds(r0, C), :] = (on * _silu(z_ref[0, pl.ds(r0, C), :])).astype(BF16)
            return carry

        lax.fori_loop(0, ncb, body, 0)

    sto_ref[0, 0] = st_ref[...]


def _hgrn(lidx, cols_hg, lb_logits, g3, s0, s0_per_batch):
    B, T, _ = cols_hg.shape
    nb = T // TOK_TILE

    def blk(j):
        return jnp.where(j < nb, j, 2 * nb - 1 - j)

    def dirn(j):
        return jnp.where(j < nb, 0, 1)

    def col(jcol):
        return pl.BlockSpec((1, TOK_TILE, 256), lambda b, j, l, jcol=jcol: (b, blk(j), jcol))

    grid_spec = pltpu.PrefetchScalarGridSpec(
        num_scalar_prefetch=1,
        grid=(B, 2 * nb),
        in_specs=[
            col(0),
            pl.BlockSpec((1, TOK_TILE, 256), lambda b, j, l: (b, blk(j), 1 + dirn(j))),
            col(3), col(4), col(5),
            pl.BlockSpec((DEPTH, 2, 256), lambda b, j, l: (0, 0, 0)),
            pl.BlockSpec((1, 1, 256), lambda b, j, l: (l[0], 0, 0)),
            pl.BlockSpec((1, 1, 256, 256),
                         (lambda b, j, l: (b, dirn(j), 0, 0)) if s0_per_batch else (lambda b, j, l: (0, 0, 0, 0))),
        ],
        out_specs=[
            pl.BlockSpec((1, TOK_TILE, 256), lambda b, j, l: (b, jnp.where(j < nb, nb - 1, 2 * nb - 1 - j), 0)),
            pl.BlockSpec((1, 1, 256, 256), lambda b, j, l: (b, dirn(j), 0, 0)),
        ],
        scratch_shapes=[
            pltpu.VMEM((256, 256), F32),
            pltpu.VMEM((T, 256), F32),
            pltpu.VMEM((HG_CHUNK, 256), F32),
            pltpu.VMEM((HG_CHUNK, 256), F32),
            pltpu.VMEM((HG_CHUNK, 256), F32),
        ],
    )
    return pl.pallas_call(
        functools.partial(_hg_kernel, nb=nb),
        grid_spec=grid_spec,
        out_shape=[
            jax.ShapeDtypeStruct((B, T, 256), BF16),
            jax.ShapeDtypeStruct((B, 2, 256, 256), F32),
        ],
        compiler_params=_cparams(("arbitrary", "arbitrary")),
        name="hgrn",
    )(lidx, cols_hg, cols_hg, cols_hg, cols_hg, cols_hg, lb_logits, g3, s0)


def _rot_cols(w, nheads):
    shp = w.shape
    r = shp[-1] // nheads
    w4 = w.reshape(shp[:-1] + (nheads, 2, r // 2))
    return jnp.concatenate([-w4[..., 1, :], w4[..., 0, :]], axis=-1).reshape(shp)


def _perm_heads(w, nheads, order):
    shp = w.shape
    w3 = w.reshape(shp[:-1] + (nheads, shp[-1] // nheads))
    return w3[..., list(order), :].reshape(shp)


_WG_ORDER = (0, 2, 1, 3)


def _mla_rope_slot(a):
    pad = [(0, 0)] * (a.ndim - 1) + [(MLA_NOPE, 128 - MLA_NOPE - MLA_ROPE)]
    return jnp.pad(a, pad)


def _prep_w_in(w_in):
    def c(name):
        lo, hi = _IN_OFF[name]
        return w_in[:, :, lo:hi]

    kva = c('mla_kva')
    lo = _IN_OFF['na_q'][0]
    hi = _IN_OFF['hg_z'][1]
    parts = [
        w_in[:, :, lo:hi],
        c('mla_qa'), c('mla_z'), kva[:, :, :MLA_KV_RANK], _mla_rope_slot(kva[:, :, MLA_KV_RANK:]),
        _perm_heads(c('wg_q'), WG_HEADS, _WG_ORDER), _perm_heads(c('wg_z'), WG_HEADS, _WG_ORDER),
        c('wg_k'), c('wg_v'),
    ]
    return jnp.concatenate([p.astype(BF16) for p in parts], axis=-1), c('merge').astype(BF16)


def _rot_matrix(width, head):
    i = np.arange(width)[:, None]
    j = np.arange(width)[None, :]
    half = head // 2
    first = (j % head) < half
    m = np.where(first & (i == j + half), -1.0, 0.0) + np.where(~first & (i == j - half), 1.0, 0.0)
    return jnp.asarray(m, BF16)


def _rope_tables(n_tok, rot_dim, reps):
    t = np.arange(n_tok)
    row = (t // GRID_W).astype(np.float32)
    col = (t % GRID_W).astype(np.float32)
    nf = rot_dim // 4
    inv = jnp.asarray(ROPE_BASE, F32) ** (-jnp.arange(nf, dtype=F32) / nf)
    ang = jnp.concatenate([jnp.asarray(row)[:, None] * inv, jnp.asarray(col)[:, None] * inv], axis=-1)
    cos, sin = jnp.cos(ang), jnp.sin(ang)
    return (jnp.tile(jnp.concatenate([cos, cos], axis=-1), (1, reps)),
            jnp.tile(jnp.concatenate([sin, sin], axis=-1), (1, reps)))


def _state_to_blockdiag(st):
    B = st.shape[0]
    stt = jnp.swapaxes(st, -1, -2)
    eye = jnp.eye(HG_HEADS, dtype=st.dtype)
    bd = stt[:, :, :, :, None, :] * eye[None, None, :, None, :, None]
    return bd.reshape(B, 2, HG_HEADS * HG_DK, HG_HEADS * HG_DK)


def _blockdiag_to_state(bd):
    B = bd.shape[0]
    b6 = bd.reshape(B, 2, HG_HEADS, HG_DK, HG_HEADS, HG_DK)
    diag = jnp.stack([b6[:, :, h, :, h, :] for h in range(HG_HEADS)], axis=2)
    return jnp.swapaxes(diag, -1, -2)


def kernel(x_prompt, x_sample, cache_na_k, cache_na_v, state_hgrn, cache_mla_ckv, cache_mla_krope, cache_wg_k, cache_wg_v, c, c_ctx, norm_g, w_mod, b_mod, w_in, na_rpb, hg_lb_logits, hg_norm_g, mla_q_norm_g, mla_kv_norm_g, mla_w_qb, mla_w_kvb, wg_sink, w_branch, w_out, final_g):
    Bp, Tp, _ = x_prompt.shape
    Bd, Td, _ = x_sample.shape
    L = cache_na_k.shape[2]
    rows = Td // GRID_W
    assert Tp % TOK_TILE == 0 and Td % TOK_TILE == 0 and L % TOK_TILE == 0
    assert rows % NA_QROWS == 0 and rows >= NA_KROWS and Td >= WG_KWIN

    w_proj, w_merge = _prep_w_in(w_in)
    wb = w_branch.astype(BF16)
    wb = wb.at[:, 3].set(_perm_heads(jnp.swapaxes(wb[:, 3], -1, -2), WG_HEADS, _WG_ORDER).swapaxes(-1, -2))
    wo = w_out.astype(BF16)
    qb = mla_w_qb.reshape(DEPTH, MLA_Q_RANK, MLA_HEADS, MLA_NOPE + MLA_ROPE)
    qb_g1 = jnp.pad(qb, ((0, 0), (0, 0), (0, 0), (0, 128 - MLA_NOPE - MLA_ROPE)))
    qb_g2 = _mla_rope_slot(_rot_cols(qb[..., MLA_NOPE:], 1))
    w_qb = jnp.concatenate([qb_g1.reshape(DEPTH, MLA_Q_RANK, -1), qb_g2.reshape(DEPTH, MLA_Q_RANK, -1)],
                           axis=-1).astype(BF16)
    kvb = mla_w_kvb.reshape(DEPTH, MLA_KV_RANK, MLA_HEADS, 2 * MLA_NOPE)
    kvb_k = jnp.pad(kvb[..., :MLA_NOPE], ((0, 0), (0, 0), (0, 0), (0, 128 - MLA_NOPE)))
    w_kvb = jnp.concatenate([kvb_k.reshape(DEPTH, MLA_KV_RANK, -1),
                             kvb[..., MLA_NOPE:].reshape(DEPTH, MLA_KV_RANK, -1)], axis=-1).astype(BF16)
    norm_g3 = norm_g.reshape(DEPTH, 1, D_MODEL)
    g_q3 = mla_q_norm_g.reshape(DEPTH, 1, MLA_Q_RANK)
    g_kv3 = mla_kv_norm_g.reshape(DEPTH, 1, MLA_KV_RANK)
    hg_g3 = jnp.tile(hg_norm_g, (1, HG_HEADS)).reshape(DEPTH, 1, HG_HEADS * HG_DK)
    final_g2 = final_g.reshape(1, D_MODEL)
    cq, sq = _rope_tables(Td, HEAD_DIM, WG_HEADS)
    ck, sk = cq[:, :128], sq[:, :128]
    c1, s1 = _rope_tables(Td, MLA_ROPE, 1)
    c4 = jnp.concatenate([jnp.ones((Td, MLA_NOPE), F32), c1, jnp.zeros((Td, 32), F32)], axis=-1)
    s4 = _mla_rope_slot(s1)
    rm_wg = _rot_matrix(WG_HEADS * HEAD_DIM, HEAD_DIM)
    rm_mla = jnp.zeros((128, 128), BF16).at[MLA_NOPE:MLA_NOPE + MLA_ROPE, MLA_NOPE:MLA_NOPE + MLA_ROPE].set(
        _rot_matrix(MLA_ROPE, MLA_ROPE))

    nrow = 8 * ((1 + Bd + 7) // 8)
    cond = jnp.zeros((nrow, D_MODEL), F32).at[0].set(c_ctx).at[1:1 + Bd].set(c)
    mod4 = _modulation(cond, w_mod, b_mod).reshape(DEPTH, nrow, 1, 3 * D_MODEL)

    kstart, ktype, table = _na_row_types(rows)
    na_bias = _na_bias(na_rpb, table)
    kstart = jnp.asarray(kstart, jnp.int32)
    ktype = jnp.asarray(ktype, jnp.int32)

    cna_k = cache_na_k.reshape(Bd, DEPTH, L, NA_HEADS * HEAD_DIM)
    cna_v = cache_na_v.reshape(Bd, DEPTH, L, NA_HEADS * HEAD_DIM)
    cwg_k = cache_wg_k.reshape(Bd, DEPTH, L, WG_KV_HEADS * HEAD_DIM)
    cwg_v = cache_wg_v.reshape(Bd, DEPTH, L, WG_KV_HEADS * HEAD_DIM)
    ckr4 = _mla_rope_slot(cache_mla_krope)
    st_bd = _state_to_blockdiag(state_hgrn.reshape(Bd * DEPTH, 2, HG_HEADS, HG_DK, HG_DK)).reshape(
        Bd, DEPTH, 2, 256, 256)
    st_zero = jnp.zeros((1, 2, 256, 256), F32)

    hp, hs = x_prompt, x_sample
    caches = [[] for _ in range(7)]
    for layer in range(DEPTH):
        lidx = jnp.full((1,), layer, jnp.int32)
        final = layer == DEPTH - 1

        flat = (1, Bp * Tp)
        na, hg, mla, wg, kc_, vc_, ckv_n, krs = _in_proj(
            lidx, hp.reshape(flat + (D_MODEL,)), mod4, 0, False, norm_g3, w_proj, g_kv3, w_kvb, rm_mla, c4, s4,
            False)
        na, hg, mla, wg, kc_, vc_ = (a.reshape(Bp, Tp, a.shape[-1]) for a in (na, hg, mla, wg, kc_, vc_))
        y_na = _na_ctx(na)
        y_hg, st_out = _hgrn(lidx, hg, hg_lb_logits, hg_g3, st_zero, False)
        y_mla = _mla_attn(lidx, mla, g_q3, w_qb, c4, s4, [(kc_, vc_)], False)
        y_wg = _wg_ctx(lidx, wg_sink, wg)
        ys = tuple(y.reshape(flat + (BRANCH_W,)) for y in (y_na, y_hg, y_mla, y_wg))
        hp = _post(lidx, hp.reshape(flat + (D_MODEL,)), mod4, 0, False, norm_g3, ys, w_merge, wb, wo, final_g2,
                   final).reshape(Bp, Tp, D_MODEL)
        caches[0].append(na[:, :, 256:512].reshape(Bp, Tp, NA_HEADS, HEAD_DIM))
        caches[1].append(na[:, :, 512:768].reshape(Bp, Tp, NA_HEADS, HEAD_DIM))
        caches[2].append(_blockdiag_to_state(st_out))
        caches[3].append(ckv_n.reshape(Bp, Tp, MLA_KV_RANK))
        caches[4].append(krs.reshape(Bp, Tp, 128)[:, :, MLA_NOPE:MLA_NOPE + MLA_ROPE])
        caches[5].append(wg[:, :, 512:640].reshape(Bp, Tp, WG_KV_HEADS, HEAD_DIM))
        caches[6].append(wg[:, :, 640:768].reshape(Bp, Tp, WG_KV_HEADS, HEAD_DIM))

        na, hg, mla, wg, kl, vl = _in_proj(lidx, hs, mod4, 1, True, norm_g3, w_proj, g_kv3, w_kvb, rm_mla, c4, s4,
                                           True)
        y_na = _na_lat(lidx, kstart, ktype, na, cna_k, cna_v, na_bias)
        y_hg, _ = _hgrn(lidx, hg, hg_lb_logits, hg_g3, st_bd[:, layer], True)
        kx, vx = _mla_kv(lidx, cache_mla_ckv, 0, ckr4, 0, ckr4, 0, g_kv3, w_kvb, c4, s4, False, False, True)
        y_mla = _mla_attn(lidx, mla, g_q3, w_qb, c4, s4, [(kl, vl), (kx, vx)], True)
        y_wg = _wg_lat(lidx, wg_sink, wg, cwg_k, cwg_v, cq, sq, ck, sk, rm_wg)
        hs = _post(lidx, hs, mod4, 1, True, norm_g3, (y_na, y_hg, y_mla, y_wg), w_merge, wb, wo, final_g2, final)

    outs = [jnp.stack(cs, axis=1) for cs in caches]
    return (hp, hs, outs[0], outs[1], outs[2], outs[3], outs[4], outs[5], outs[6])
```

```python
import functools

import jax
import jax.numpy as jnp
import numpy as np
from jax import lax
from jax.experimental import pallas as pl
from jax.experimental.pallas import tpu as pltpu

F32 = jnp.float32
BF16 = jnp.bfloat16

D_MODEL = 1024
DEPTH = 4
GRID_W = 64
HEAD_DIM = 64
BRANCH_W = 256
N_BRANCH = 4
NA_HEADS = 4
NA_WIN_H = 8
NA_WIN_W = 16
HG_HEADS = 4
HG_DK = 64
HG_CHUNK = 64
HG_SUB = 16
HG_FAST_SPAN = 100.0
LOG2E = 1.4426950408889634
F_FLOOR = 1e-30
MLA_HEADS = 4
MLA_Q_RANK = 256
MLA_KV_RANK = 128
MLA_NOPE = 64
MLA_ROPE = 32
WG_HEADS = 4
WG_KV_HEADS = 2
WG_WINDOW = 128
ROPE_BASE = 10000.0
EPS = 1e-6
NEG_INF = -1e30

_IN_LAYOUT = (
    ('na_q', 256), ('na_k', 256), ('na_v', 256), ('na_z', 256),
    ('hg_q', 256), ('hg_ff', 256), ('hg_fb', 256), ('hg_i', 256), ('hg_og', 256), ('hg_z', 256),
    ('mla_qa', 256), ('mla_kva', 160), ('mla_z', 256),
    ('wg_q', 256), ('wg_k', 128), ('wg_v', 128), ('wg_z', 256),
    ('merge', 4096),
)
_IN_OFF = {}
_o = 0
for _n, _w in _IN_LAYOUT:
    _IN_OFF[_n] = (_o, _o + _w)
    _o += _w

NA_W = 1024
HG_W = 1536
MLA_W = 768
WG_W = 768
PROJ_W = NA_W + HG_W + MLA_W + WG_W

TOK_TILE = 256
PROJ_TILE = 512
NA_QROWS = 4
NA_KROWS = 12
WG_KWIN = 512
VMEM_LIMIT = 56 * 1024 * 1024


def _cparams(sem):
    return pltpu.CompilerParams(dimension_semantics=sem, vmem_limit_bytes=VMEM_LIMIT)


def _dot(a, b):
    return jnp.dot(a, b, preferred_element_type=F32)


def _dot_nt(a, b):
    return lax.dot_general(a, b, (((1,), (1,)), ((), ())), preferred_element_type=F32)


def _dot_tn(a, b):
    return lax.dot_general(a, b, (((0,), (0,)), ((), ())), preferred_element_type=F32)


def _sigmoid(x):
    return 1.0 / (1.0 + jnp.exp(-x))


def _silu(x):
    return x * _sigmoid(x)


def _lane_mask(width, lo, hi, dtype):
    lane = lax.broadcasted_iota(jnp.int32, (1, width), 1)
    return ((lane >= lo) & (lane < hi)).astype(dtype)


def _seg_matrix(width, seg):
    r = lax.broadcasted_iota(jnp.int32, (width, width), 0) // seg
    c = lax.broadcasted_iota(jnp.int32, (width, width), 1) // seg
    return r == c


def _seg_mean(x2, seg_bf16, seg):
    hi = x2.astype(BF16)
    lo = (x2 - hi.astype(F32)).astype(BF16)
    return (_dot(hi, seg_bf16) + _dot(lo, seg_bf16)) * (1.0 / seg)


def _mod_kernel(c_ref, w_ref, b_ref, o_ref):
    s = _silu(c_ref[...])
    o_ref[0] = _dot(s.astype(BF16), w_ref[0].astype(BF16)) + b_ref[0]


def _modulation(cond, w_mod, b_mod):
    nrow = cond.shape[0]
    tn = 1024
    return pl.pallas_call(
        _mod_kernel,
        grid=(DEPTH, 3 * D_MODEL // tn),
        in_specs=[
            pl.BlockSpec((nrow, D_MODEL), lambda l, j: (0, 0)),
            pl.BlockSpec((1, D_MODEL, tn), lambda l, j: (l, 0, j)),
            pl.BlockSpec((1, 1, tn), lambda l, j: (l, 0, j)),
        ],
        out_specs=pl.BlockSpec((1, nrow, tn), lambda l, j: (l, 0, j)),
        out_shape=jax.ShapeDtypeStruct((DEPTH, nrow, 3 * D_MODEL), F32),
        compiler_params=_cparams(("arbitrary", "arbitrary")),
        name="modulation",
    )(cond, w_mod, b_mod.reshape(DEPTH, 1, 3 * D_MODEL))


def _mod_norm(x, mod_ref, g_ref):
    ms = jnp.mean(x * x, axis=-1, keepdims=True)
    y = x * lax.rsqrt(ms + EPS) * g_ref[0]
    shift = mod_ref[0, 0, :, 0:D_MODEL]
    scale = mod_ref[0, 0, :, D_MODEL:2 * D_MODEL]
    gate = mod_ref[0, 0, :, 2 * D_MODEL:3 * D_MODEL]
    return y * (1.0 + scale) + shift, gate


def _in_kernel(l_ref, x_ref, mod_ref, g_ref, w_ref, gkv_ref, wkvb_ref, rm_ref, c_ref, s_ref,
               na_ref, hg_ref, mla_ref, wg_ref, kall_ref, vall_ref, *rest, rope, emit_ckv):
    h, _ = _mod_norm(x_ref[0], mod_ref, g_ref)
    hb = h.astype(BF16)
    na_ref[0] = _dot(hb, w_ref[0, :, 0:NA_W])
    hg_ref[0] = _dot(hb, w_ref[0, :, NA_W:NA_W + HG_W])
    wg_ref[0] = _dot(hb, w_ref[0, :, NA_W + HG_W + MLA_W:PROJ_W])
    mla = _dot(hb, w_ref[0, :, NA_W + HG_W:NA_W + HG_W + MLA_W])
    mla_ref[0] = mla[:, 0:512]
    ckv = mla[:, 512:640]
    ms = jnp.mean(ckv * ckv, axis=-1, keepdims=True)
    ckv = ckv * lax.rsqrt(ms + EPS) * gkv_ref[0]
    kr = mla[:, 640:768]
    if emit_ckv:
        rest[0][0] = ckv
        rest[1][0] = kr
    if rope:
        kr = kr * c_ref[...] + _dot(kr.astype(BF16), rm_ref[...]) * s_ref[...]
    kv = _dot(ckv.astype(BF16), wkvb_ref[0])
    for hd in range(MLA_HEADS):
        kall_ref[0, :, 128 * hd:128 * hd + 128] = (kv[:, 128 * hd:128 * hd + 128] + kr).astype(BF16)
    vall_ref[0] = kv[:, 512:768].astype(BF16)


def _in_proj(lidx, x, mod4, mod_row0, per_batch, norm_g3, w_proj, g_kv3, w_kvb, rm_mla, c4, s4, rope):
    B, T, _ = x.shape
    tile = min(PROJ_TILE, T)
    nt = T // tile
    mrow = (lambda b: mod_row0 + b) if per_batch else (lambda b: mod_row0)
    emit_ckv = not rope

    def tok(width):
        return pl.BlockSpec((1, tile, width), lambda b, t, l: (b, t, 0))

    out_specs = [tok(NA_W), tok(HG_W), tok(512), tok(WG_W), tok(512), tok(256)]
    out_shape = [
        jax.ShapeDtypeStruct((B, T, NA_W), F32),
        jax.ShapeDtypeStruct((B, T, HG_W), F32),
        jax.ShapeDtypeStruct((B, T, 512), F32),
        jax.ShapeDtypeStruct((B, T, WG_W), F32),
        jax.ShapeDtypeStruct((B, T, 512), BF16),
        jax.ShapeDtypeStruct((B, T, 256), BF16),
    ]
    if emit_ckv:
        out_specs += [tok(128), tok(128)]
        out_shape += [jax.ShapeDtypeStruct((B, T, 128), F32), jax.ShapeDtypeStruct((B, T, 128), F32)]
    grid_spec = pltpu.PrefetchScalarGridSpec(
        num_scalar_prefetch=1,
        grid=(B, nt),
        in_specs=[
            pl.BlockSpec((1, tile, D_MODEL), lambda b, t, l: (b, t, 0)),
            pl.BlockSpec((1, 1, 1, 3 * D_MODEL), lambda b, t, l: (l[0], mrow(b), 0, 0)),
            pl.BlockSpec((1, 1, D_MODEL), lambda b, t, l: (l[0], 0, 0)),
            pl.BlockSpec((1, D_MODEL, PROJ_W), lambda b, t, l: (l[0], 0, 0)),
            pl.BlockSpec((1, 1, 128), lambda b, t, l: (l[0], 0, 0)),
            pl.BlockSpec((1, 128, 768), lambda b, t, l: (l[0], 0, 0)),
            pl.BlockSpec((128, 128), lambda b, t, l: (0, 0)),
            pl.BlockSpec((tile, 128), lambda b, t, l: (t if rope else 0, 0)),
            pl.BlockSpec((tile, 128), lambda b, t, l: (t if rope else 0, 0)),
        ],
        out_specs=out_specs,
    )
    return pl.pallas_call(
        functools.partial(_in_kernel, rope=rope, emit_ckv=emit_ckv),
        grid_spec=grid_spec,
        out_shape=out_shape,
        compiler_params=_cparams(("arbitrary", "arbitrary")),
        name="in_proj_lat" if rope else "in_proj_ctx",
    )(lidx, x, mod4, norm_g3, w_proj, g_kv3, w_kvb, rm_mla, c4, s4)


def _post_kernel(l_ref, x_ref, mod_ref, g_ref, y0_ref, y1_ref, y2_ref, y3_ref, wm_ref, wb_ref, wo_ref,
                 fg_ref, o_ref, *, final):
    x = x_ref[0]
    h, gate = _mod_norm(x, mod_ref, g_ref)
    hb = h.astype(BF16)
    acc = jnp.zeros(x.shape, F32)
    for n, y_ref in enumerate((y0_ref, y1_ref, y2_ref, y3_ref)):
        mg = _sigmoid(_dot(hb, wm_ref[0, :, n * D_MODEL:(n + 1) * D_MODEL]))
        acc = acc + mg * _dot(y_ref[0], wb_ref[0, n])
    out = x + gate * _dot(acc.astype(BF16), wo_ref[0])
    if final:
        ms = jnp.mean(out * out, axis=-1, keepdims=True)
        out = out * lax.rsqrt(ms + EPS) * fg_ref[...]
    o_ref[0] = out


def _post(lidx, x, mod4, mod_row0, per_batch, norm_g3, ys, w_merge, w_branch, w_out, final_g2, final):
    B, T, _ = x.shape
    tile = min(PROJ_TILE, T)
    nt = T // tile
    mrow = (lambda b: mod_row0 + b) if per_batch else (lambda b: mod_row0)
    yspec = pl.BlockSpec((1, tile, BRANCH_W), lambda b, t, l: (b, t, 0))
    grid_spec = pltpu.PrefetchScalarGridSpec(
        num_scalar_prefetch=1,
        grid=(B, nt),
        in_specs=[
            pl.BlockSpec((1, tile, D_MODEL), lambda b, t, l: (b, t, 0)),
            pl.BlockSpec((1, 1, 1, 3 * D_MODEL), lambda b, t, l: (l[0], mrow(b), 0, 0)),
            pl.BlockSpec((1, 1, D_MODEL), lambda b, t, l: (l[0], 0, 0)),
            yspec, yspec, yspec, yspec,
            pl.BlockSpec((1, D_MODEL, N_BRANCH * D_MODEL), lambda b, t, l: (l[0], 0, 0)),
            pl.BlockSpec((1, N_BRANCH, BRANCH_W, D_MODEL), lambda b, t, l: (l[0], 0, 0, 0)),
            pl.BlockSpec((1, D_MODEL, D_MODEL), lambda b, t, l: (l[0], 0, 0)),
            pl.BlockSpec((1, D_MODEL), lambda b, t, l: (0, 0)),
        ],
        out_specs=pl.BlockSpec((1, tile, D_MODEL), lambda b, t, l: (b, t, 0)),
    )
    return pl.pallas_call(
        functools.partial(_post_kernel, final=final),
        grid_spec=grid_spec,
        out_shape=jax.ShapeDtypeStruct((B, T, D_MODEL), F32),
        compiler_params=_cparams(("arbitrary", "arbitrary")),
        name="post_final" if final else "post",
    )(lidx, x, mod4, norm_g3, *ys, w_merge, w_branch, w_out, final_g2)


def _attend(q, kparts, vparts, biases, qmasks, omasks, sinks=None):
    out = None
    for h in range(len(omasks)):
        qh = q[h] if qmasks is None else q * qmasks[h]
        ss = []
        for i, kp in enumerate(kparts):
            s = _dot_nt(qh, kp[h] if isinstance(kp, (list, tuple)) else kp)
            if biases is not None and biases[h][i] is not None:
                s = s + biases[h][i]
            ss.append(s)
        m = ss[0].max(axis=-1, keepdims=True)
        for s in ss[1:]:
            m = jnp.maximum(m, s.max(axis=-1, keepdims=True))
        if sinks is not None:
            m = jnp.maximum(m, sinks[h])
        den = None
        oh = None
        for s, vp in zip(ss, vparts):
            p = jnp.exp2(s - m)
            ps = p.sum(axis=-1, keepdims=True)
            den = ps if den is None else den + ps
            pv = _dot(p.astype(BF16), vp)
            oh = pv if oh is None else oh + pv
        if sinks is not None:
            den = den + jnp.exp2(sinks[h] - m)
        oh = (oh / den) * omasks[h]
        out = oh if out is None else out + oh
    return out


def _head_masks(width, seg, nheads, dtype):
    return [_lane_mask(width, h * seg, (h + 1) * seg, dtype) for h in range(nheads)]


def _na_ctx_kernel(q_ref, k_ref, v_ref, z_ref, o_ref):
    q = (q_ref[0] * (HEAD_DIM ** -0.5 * LOG2E)).astype(BF16)
    k = k_ref[0].astype(BF16)
    v = v_ref[0].astype(BF16)
    o = _attend(q, [k], [v], None, _head_masks(256, 64, NA_HEADS, BF16), _head_masks(256, 64, NA_HEADS, F32))
    o_ref[0] = (o * _silu(z_ref[0])).astype(BF16)


def _na_ctx(cols_na):
    B, T, _ = cols_na.shape

    def spec(j):
        return pl.BlockSpec((1, T, 256), lambda b, j=j: (b, 0, j))

    return pl.pallas_call(
        _na_ctx_kernel,
        grid=(B,),
        in_specs=[spec(0), spec(1), spec(2), spec(3)],
        out_specs=pl.BlockSpec((1, T, 256), lambda b: (b, 0, 0)),
        out_shape=jax.ShapeDtypeStruct((B, T, 256), BF16),
        compiler_params=_cparams(("arbitrary",)),
        name="na_ctx",
    )(cols_na, cols_na, cols_na, cols_na)


def _na_row_types(rows):
    wh = min(NA_WIN_H, rows)
    nblk = rows // NA_QROWS
    starts, types, table = [], [], []
    for blk in range(nblk):
        r0 = blk * NA_QROWS
        k0 = int(np.clip(r0 - wh // 2, 0, rows - NA_KROWS))
        desc = []
        for i in range(NA_QROWS):
            r = r0 + i
            ks = int(np.clip(r - wh // 2, 0, rows - wh))
            for j in range(NA_KROWS):
                kr = k0 + j
                desc.append((kr - r + NA_WIN_H - 1) if ks <= kr < ks + wh else None)
        desc = tuple(desc)
        if desc not in table:
            table.append(desc)
        starts.append(k0)
        types.append(table.index(desc))
    return starts, types, table


def _na_bias_kernel(rpb_ref, o_ref, t2_ref, *, table):
    l = pl.program_id(0)
    h = pl.program_id(1)
    nrow = 2 * NA_WIN_H - 1
    ncol = 2 * NA_WIN_W - 1
    c = lax.broadcasted_iota(jnp.int32, (GRID_W, 128), 0)
    lane = lax.broadcasted_iota(jnp.int32, (GRID_W, 128), 1)
    first = lane < GRID_W
    kc = jnp.where(first, lane, lane - GRID_W)
    qstart = jnp.clip(c - NA_WIN_W // 2, 0, GRID_W - NA_WIN_W)
    ok = (kc >= qstart) & (kc < qstart + NA_WIN_W)
    dcol = kc - c + NA_WIN_W - 1
    for a in range(-1, nrow):
        acc = jnp.full((GRID_W, 128), NEG_INF, F32)
        for bi in range(2 * NA_WIN_W - 1):
            base = (l * NA_HEADS + h) * nrow
            lo = rpb_ref[(base + a) * ncol + bi] if a >= 0 else jnp.float32(NEG_INF)
            hi = rpb_ref[(base + a + 1) * ncol + bi] if a + 1 < nrow else jnp.float32(NEG_INF)
            acc = jnp.where(dcol == bi, jnp.where(first, lo, hi), acc)
        t2_ref[a + 1] = jnp.where(ok, acc * LOG2E, NEG_INF)
    for ti, desc in enumerate(table):
        for i in range(NA_QROWS):
            for p in range(NA_KROWS // 2):
                a0 = desc[i * NA_KROWS + 2 * p]
                a1 = desc[i * NA_KROWS + 2 * p + 1]
                if a0 is None and a1 is None:
                    tile = jnp.full((GRID_W, 128), NEG_INF, F32)
                elif a1 is None:
                    tile = jnp.where(first, t2_ref[a0 + 1], NEG_INF)
                elif a0 is None:
                    tile = jnp.where(first, NEG_INF, t2_ref[a1])
                else:
                    tile = t2_ref[a0 + 1]
                o_ref[0, ti, 0, i * GRID_W:(i + 1) * GRID_W, p * 128:(p + 1) * 128] = tile


def _na_bias(na_rpb, table):
    nt = len(table)
    nq = NA_QROWS * GRID_W
    nk = NA_KROWS * GRID_W
    return pl.pallas_call(
        functools.partial(_na_bias_kernel, table=table),
        grid=(DEPTH, NA_HEADS),
        in_specs=[pl.BlockSpec(memory_space=pltpu.SMEM)],
        out_specs=pl.BlockSpec((1, nt, 1, nq, nk), lambda l, h: (l, 0, h, 0, 0)),
        out_shape=jax.ShapeDtypeStruct((DEPTH, nt, NA_HEADS, nq, nk), F32),
        scratch_shapes=[pltpu.VMEM((2 * NA_WIN_H, GRID_W, 128), F32)],
        compiler_params=_cparams(("arbitrary", "arbitrary")),
        name="na_bias",
    )(na_rpb.reshape(-1))


def _na_lat_kernel(l_ref, ks_ref, ty_ref, q_ref, k_ref, v_ref, z_ref, kc_ref, vc_ref, b_ref, o_ref):
    rb = pl.program_id(1)
    nk = NA_KROWS * GRID_W
    ks = pl.multiple_of(ks_ref[rb] * GRID_W, GRID_W)
    q = (q_ref[0] * (HEAD_DIM ** -0.5 * LOG2E)).astype(BF16)
    kw = k_ref[0, pl.ds(ks, nk), :].astype(BF16)
    vw = v_ref[0, pl.ds(ks, nk), :].astype(BF16)
    kc = kc_ref[0, 0].astype(BF16)
    vc = vc_ref[0, 0].astype(BF16)
    biases = [[b_ref[0, 0, h], None] for h in range(NA_HEADS)]
    o = _attend(q, [kw, kc], [vw, vc], biases,
                _head_masks(256, 64, NA_HEADS, BF16), _head_masks(256, 64, NA_HEADS, F32))
    o_ref[0] = (o * _silu(z_ref[0])).astype(BF16)


def _na_lat(lidx, kstart, ktype, cols_na, cache_k, cache_v, bias):
    B, T, _ = cols_na.shape
    L = cache_k.shape[2]
    nq = NA_QROWS * GRID_W
    nk = NA_KROWS * GRID_W
    nblk = T // nq
    grid_spec = pltpu.PrefetchScalarGridSpec(
        num_scalar_prefetch=3,
        grid=(B, nblk),
        in_specs=[
            pl.BlockSpec((1, nq, 256), lambda b, r, l, ks, ty: (b, r, 0)),
            pl.BlockSpec((1, T, 256), lambda b, r, l, ks, ty: (b, 0, 1)),
            pl.BlockSpec((1, T, 256), lambda b, r, l, ks, ty: (b, 0, 2)),
            pl.BlockSpec((1, nq, 256), lambda b, r, l, ks, ty: (b, r, 3)),
            pl.BlockSpec((1, 1, L, 256), lambda b, r, l, ks, ty: (b, l[0], 0, 0)),
            pl.BlockSpec((1, 1, L, 256), lambda b, r, l, ks, ty: (b, l[0], 0, 0)),
            pl.BlockSpec((1, 1, NA_HEADS, nq, nk), lambda b, r, l, ks, ty: (l[0], ty[r], 0, 0, 0)),
        ],
        out_specs=pl.BlockSpec((1, nq, 256), lambda b, r, l, ks, ty: (b, r, 0)),
    )
    return pl.pallas_call(
        _na_lat_kernel,
        grid_spec=grid_spec,
        out_shape=jax.ShapeDtypeStruct((B, T, 256), BF16),
        compiler_params=_cparams(("arbitrary", "arbitrary")),
        name="na_lat",
    )(lidx, kstart, ktype, cols_na, cols_na, cols_na, cols_na, cache_k, cache_v, bias)


def _wg_heads(q, kparts, vparts, bias, sink_ref, l):
    qm = _head_masks(128, 64, 2, BF16)
    om = _head_masks(128, 64, 2, F32)
    outs = []
    for grp, heads in enumerate(((0, 2), (1, 3))):
        qg = q[:, grp * 128:(grp + 1) * 128]
        sinks = [sink_ref[l, hh] * LOG2E for hh in heads]
        biases = None if bias is None else [[bias] + [None] * (len(kparts) - 1)] * 2
        outs.append(_attend(qg, kparts, vparts, biases, qm, om, sinks))
    return jnp.concatenate(outs, axis=-1)


def _wg_ctx_kernel(l_ref, sink_ref, q_ref, z_ref, k_ref, v_ref, o_ref):
    q = (q_ref[0] * (HEAD_DIM ** -0.5 * LOG2E)).astype(BF16)
    o = _wg_heads(q, [k_ref[0].astype(BF16)], [v_ref[0].astype(BF16)], None, sink_ref, l_ref[0])
    o_ref[0] = (o * _silu(z_ref[0])).astype(BF16)


def _wg_ctx(lidx, sink, cols_wg):
    B, T, _ = cols_wg.shape
    grid_spec = pltpu.PrefetchScalarGridSpec(
        num_scalar_prefetch=1,
        grid=(B,),
        in_specs=[
            pl.BlockSpec(memory_space=pltpu.SMEM),
            pl.BlockSpec((1, T, 256), lambda b, l: (b, 0, 0)),
            pl.BlockSpec((1, T, 256), lambda b, l: (b, 0, 1)),
            pl.BlockSpec((1, T, 128), lambda b, l: (b, 0, 4)),
            pl.BlockSpec((1, T, 128), lambda b, l: (b, 0, 5)),
        ],
        out_specs=pl.BlockSpec((1, T, 256), lambda b, l: (b, 0, 0)),
    )
    return pl.pallas_call(
        _wg_ctx_kernel,
        grid_spec=grid_spec,
        out_shape=jax.ShapeDtypeStruct((B, T, 256), BF16),
        compiler_params=_cparams(("arbitrary",)),
        name="wg_ctx",
    )(lidx, sink, cols_wg, cols_wg, cols_wg, cols_wg)


def _wg_lat_kernel(l_ref, sink_ref, q_ref, z_ref, k_ref, v_ref, kc_ref, vc_ref,
                   cq_ref, sq_ref, ck_ref, sk_ref, rm_ref, o_ref, *, seq):
    tb = pl.program_id(1)
    t0 = tb * TOK_TILE
    k0 = pl.multiple_of(jnp.clip(t0 - WG_WINDOW, 0, seq - WG_KWIN), 128)
    q = q_ref[0]
    q = (q * cq_ref[...] + _dot(q.astype(BF16), rm_ref[...]) * sq_ref[...]) * (HEAD_DIM ** -0.5 * LOG2E)
    kwin = k_ref[0, pl.ds(k0, WG_KWIN), :]
    kwin = kwin * ck_ref[pl.ds(k0, WG_KWIN), :] \
        + _dot(kwin.astype(BF16), rm_ref[0:128, 0:128]) * sk_ref[pl.ds(k0, WG_KWIN), :]
    vwin = v_ref[0, pl.ds(k0, WG_KWIN), :]
    qi = t0 + lax.broadcasted_iota(jnp.int32, (TOK_TILE, WG_KWIN), 0)
    kj = k0 + lax.broadcasted_iota(jnp.int32, (TOK_TILE, WG_KWIN), 1)
    bias = jnp.where(jnp.abs(qi - kj) <= WG_WINDOW, 0.0, NEG_INF).astype(F32)
    o = _wg_heads(q.astype(BF16),
                  [kwin.astype(BF16), kc_ref[0, 0].astype(BF16)],
                  [vwin.astype(BF16), vc_ref[0, 0].astype(BF16)],
                  bias, sink_ref, l_ref[0])
    o_ref[0] = (o * _silu(z_ref[0])).astype(BF16)


def _wg_lat(lidx, sink, cols_wg, cache_k, cache_v, cq, sq, ck, sk, rm):
    B, T, _ = cols_wg.shape
    L = cache_k.shape[2]
    nt = T // TOK_TILE
    grid_spec = pltpu.PrefetchScalarGridSpec(
        num_scalar_prefetch=1,
        grid=(B, nt),
        in_specs=[
            pl.BlockSpec(memory_space=pltpu.SMEM),
            pl.BlockSpec((1, TOK_TILE, 256), lambda b, t, l: (b, t, 0)),
            pl.BlockSpec((1, TOK_TILE, 256), lambda b, t, l: (b, t, 1)),
            pl.BlockSpec((1, T, 128), lambda b, t, l: (b, 0, 4)),
            pl.BlockSpec((1, T, 128), lambda b, t, l: (b, 0, 5)),
            pl.BlockSpec((1, 1, L, 128), lambda b, t, l: (b, l[0], 0, 0)),
            pl.BlockSpec((1, 1, L, 128), lambda b, t, l: (b, l[0], 0, 0)),
            pl.BlockSpec((TOK_TILE, 256), lambda b, t, l: (t, 0)),
            pl.BlockSpec((TOK_TILE, 256), lambda b, t, l: (t, 0)),
            pl.BlockSpec((T, 128), lambda b, t, l: (0, 0)),
            pl.BlockSpec((T, 128), lambda b, t, l: (0, 0)),
            pl.BlockSpec((256, 256), lambda b, t, l: (0, 0)),
        ],
        out_specs=pl.BlockSpec((1, TOK_TILE, 256), lambda b, t, l: (b, t, 0)),
    )
    return pl.pallas_call(
        functools.partial(_wg_lat_kernel, seq=T),
        grid_spec=grid_spec,
        out_shape=jax.ShapeDtypeStruct((B, T, 256), BF16),
        compiler_params=_cparams(("arbitrary", "arbitrary")),
        name="wg_lat",
    )(lidx, sink, cols_wg, cols_wg, cols_wg, cols_wg, cache_k, cache_v, cq, sq, ck, sk, rm)


def _mla_kv_kernel(l_ref, ckv_ref, kr_ref, krr_ref, g_ref, w_ref, c_ref, s_ref, k_out, v_out, *rest,
                   norm, rope):
    ckv = ckv_ref[0]
    if norm:
        ms = jnp.mean(ckv * ckv, axis=-1, keepdims=True)
        ckv = ckv * lax.rsqrt(ms + EPS) * g_ref[0]
        rest[0][0] = ckv
    kv = _dot(ckv.astype(BF16), w_ref[0])
    kr = kr_ref[0]
    if rope:
        kr = kr * c_ref[...] + krr_ref[0] * s_ref[...]
    for h in range(MLA_HEADS):
        k_out[0, :, 128 * h:128 * h + 128] = (kv[:, 128 * h:128 * h + 128] + kr).astype(BF16)
    v_out[0] = kv[:, 512:768].astype(BF16)


def _mla_kv(lidx, ckv_arr, ckv_blk, kr_arr, kr_blk, krr_arr, krr_blk, g_kv3, w_kvb, c4, s4, norm, rope,
            layered):
    B = ckv_arr.shape[0]
    T = ckv_arr.shape[-2]
    nt = T // TOK_TILE
    if layered:
        def src(blk):
            return pl.BlockSpec((None, 1, TOK_TILE, 128), lambda b, t, l: (b, l[0], t, 0))
    else:
        def src(blk):
            return pl.BlockSpec((1, TOK_TILE, 128), lambda b, t, l, blk=blk: (b, t, blk))
    out_specs = [
        pl.BlockSpec((1, TOK_TILE, 512), lambda b, t, l: (b, t, 0)),
        pl.BlockSpec((1, TOK_TILE, 256), lambda b, t, l: (b, t, 0)),
    ]
    out_shape = [
        jax.ShapeDtypeStruct((B, T, 512), BF16),
        jax.ShapeDtypeStruct((B, T, 256), BF16),
    ]
    if norm:
        out_specs.append(pl.BlockSpec((1, TOK_TILE, 128), lambda b, t, l: (b, t, 0)))
        out_shape.append(jax.ShapeDtypeStruct((B, T, 128), F32))
    grid_spec = pltpu.PrefetchScalarGridSpec(
        num_scalar_prefetch=1,
        grid=(B, nt),
        in_specs=[
            src(ckv_blk), src(kr_blk), src(krr_blk),
            pl.BlockSpec((1, 1, 128), lambda b, t, l: (l[0], 0, 0)),
            pl.BlockSpec((1, 128, 768), lambda b, t, l: (l[0], 0, 0)),
            pl.BlockSpec((TOK_TILE, 128), lambda b, t, l: (t if rope else 0, 0)),
            pl.BlockSpec((TOK_TILE, 128), lambda b, t, l: (t if rope else 0, 0)),
        ],
        out_specs=out_specs,
    )
    return pl.pallas_call(
        functools.partial(_mla_kv_kernel, norm=norm, rope=rope),
        grid_spec=grid_spec,
        out_shape=out_shape,
        compiler_params=_cparams(("arbitrary", "arbitrary")),
        name="mla_kv" + ("_norm" if norm else "") + ("_rope" if rope else ""),
    )(lidx, ckv_arr, kr_arr, krr_arr, g_kv3, w_kvb, c4, s4)


def _mla_attn_kernel(l_ref, qa_ref, z_ref, g_ref, w_ref, c_ref, s_ref, *refs, nparts, rope):
    kparts = [[refs[2 * i][0, :, 128 * h:128 * h + 128] for h in range(MLA_HEADS)] for i in range(nparts)]
    vparts = [refs[2 * i + 1][0] for i in range(nparts)]
    o_ref = refs[2 * nparts]
    qa = qa_ref[0]
    ms = jnp.mean(qa * qa, axis=-1, keepdims=True)
    qn = qa * lax.rsqrt(ms + EPS) * g_ref[0]
    qq = _dot(qn.astype(BF16), w_ref[0])
    scale = (MLA_NOPE + MLA_ROPE) ** -0.5 * LOG2E
    q = []
    for h in range(MLA_HEADS):
        qh = qq[:, 128 * h:128 * h + 128]
        if rope:
            qh = qh * c_ref[...] + qq[:, 512 + 128 * h:512 + 128 * h + 128] * s_ref[...]
        q.append((qh * scale).astype(BF16))
    o = _attend(q, kparts, vparts, None, None, _head_masks(256, 64, MLA_HEADS, F32))
    o_ref[0] = (o * _silu(z_ref[0])).astype(BF16)


def _mla_attn(lidx, cols_mla, g_q3, w_qb, c4, s4, kv_parts, rope):
    B, T, _ = cols_mla.shape
    nt = T // TOK_TILE
    in_specs = [
        pl.BlockSpec((1, TOK_TILE, 256), lambda b, t, l: (b, t, 0)),
        pl.BlockSpec((1, TOK_TILE, 256), lambda b, t, l: (b, t, 1)),
        pl.BlockSpec((1, 1, 256), lambda b, t, l: (l[0], 0, 0)),
        pl.BlockSpec((1, 256, 1024), lambda b, t, l: (l[0], 0, 0)),
        pl.BlockSpec((TOK_TILE, 128), lambda b, t, l: (t if rope else 0, 0)),
        pl.BlockSpec((TOK_TILE, 128), lambda b, t, l: (t if rope else 0, 0)),
    ]
    args = [lidx, cols_mla, cols_mla, g_q3, w_qb, c4, s4]
    for kk, vv in kv_parts:
        in_specs.append(pl.BlockSpec((1, kk.shape[1], 512), lambda b, t, l: (b, 0, 0)))
        in_specs.append(pl.BlockSpec((1, vv.shape[1], 256), lambda b, t, l: (b, 0, 0)))
        args += [kk, vv]
    grid_spec = pltpu.PrefetchScalarGridSpec(
        num_scalar_prefetch=1,
        grid=(B, nt),
        in_specs=in_specs,
        out_specs=pl.BlockSpec((1, TOK_TILE, 256), lambda b, t, l: (b, t, 0)),
    )
    return pl.pallas_call(
        functools.partial(_mla_attn_kernel, nparts=len(kv_parts), rope=rope),
        grid_spec=grid_spec,
        out_shape=jax.ShapeDtypeStruct((B, T, 256), BF16),
        compiler_params=_cparams(("arbitrary", "arbitrary")),
        name="mla_attn_lat" if rope else "mla_attn_ctx",
    )(*args)


def _hg_blocks(rev, include_diag):
    C, S = HG_CHUNK, HG_SUB
    out = []
    for g0 in range(0, C, S):
        if rev:
            lo, hi = (g0 if include_diag else g0 + S), C
        else:
            lo, hi = 0, (g0 + S if include_diag else g0)
        if hi > lo:
            out.append((g0, lo, hi))
    return out


def _hg_mask(rev, include_diag):
    blocks = _hg_blocks(rev, include_diag)
    S = HG_SUB
    nrow = len(blocks) * HG_HEADS * S
    ncol = 128 * -(-sum(hi - lo for _, lo, hi in blocks) // 128)
    r = lax.broadcasted_iota(jnp.int32, (nrow, ncol), 0)
    cidx = lax.broadcasted_iota(jnp.int32, (nrow, ncol), 1)
    rblk = r // (HG_HEADS * S)
    cblk = jnp.full_like(cidx, len(blocks))
    s = jnp.zeros_like(cidx)
    t = r % S
    start = 0
    for k_, (g0, lo, hi) in enumerate(blocks):
        inb = (cidx >= start) & (cidx < start + hi - lo)
        cblk = jnp.where(inb, k_, cblk)
        s = jnp.where(inb, cidx - start + lo, s)
        t = jnp.where(rblk == k_, t + g0, t)
        start += hi - lo
    ok = rblk == cblk
    if include_diag:
        ok = ok & ((s >= t) if rev else (s <= t))
    return ok.astype(F32)


def _hg_stacked(qs, kk, v, b2, rev, include_diag, mask, hm):
    C, S = HG_CHUNK, HG_SUB
    blocks = _hg_blocks(rev, include_diag)
    qx, kd_rows, v_rows = [], [], []
    for g0, lo, hi in blocks:
        has_ref = (g0 + S < C) if rev else (g0 > 0)
        eq = b2[g0:g0 + S, :]
        es = -b2[lo:hi, :]
        if has_ref:
            ct = b2[g0 + S:g0 + S + 1, :] if rev else b2[g0 - 1:g0, :]
            eq = eq - ct
            es = es + ct
        qd = qs[g0:g0 + S, :] * jnp.exp2(eq)
        qx += [qd * m for m in hm]
        kd_rows.append(kk[lo:hi, :] * jnp.exp2(es))
        v_rows.append(v[lo:hi, :])
    pad = mask.shape[1] - sum(hi - lo for _, lo, hi in blocks)
    if pad:
        kd_rows.append(jnp.zeros((pad, 256), F32))
        v_rows.append(jnp.zeros((pad, 256), F32))
    kd_all = jnp.concatenate(kd_rows, axis=0).astype(BF16)
    v_all = jnp.concatenate(v_rows, axis=0).astype(BF16)
    att = _dot_nt(jnp.concatenate(qx, axis=0).astype(BF16), kd_all)
    att = jnp.where(mask > 0.5, att, 0.0)
    return _dot(att.astype(BF16), v_all), blocks


def _hg_gates(lb, rev, q_ref, f_ref, qs_sc, lf_sc, kk_sc):
    qs_sc[...] = _silu(q_ref[0])
    fz = f_ref[0]
    f = lb + (1.0 - lb) * _sigmoid(fz)
    lf = jnp.log(jnp.maximum(f, F_FLOOR))
    kk_sc[...] = (1.0 - lb) * _sigmoid(-fz)
    rr = lax.broadcasted_iota(jnp.int32, (TOK_TILE, TOK_TILE), 0)
    cc = lax.broadcasted_iota(jnp.int32, (TOK_TILE, TOK_TILE), 1)
    tri = ((rr // HG_CHUNK == cc // HG_CHUNK) & ((cc >= rr) if rev else (cc <= rr))).astype(BF16)
    hi = lf.astype(BF16)
    r1 = lf - hi.astype(F32)
    mid = r1.astype(BF16)
    lo = (r1 - mid.astype(F32)).astype(BF16)
    lf_sc[...] = _dot(tri, hi) + _dot(tri, mid) + _dot(tri, lo)
    nblk = TOK_TILE // HG_SUB
    r = lax.broadcasted_iota(jnp.int32, (nblk, TOK_TILE), 0)
    cidx = lax.broadcasted_iota(jnp.int32, (nblk, TOK_TILE), 1)
    span = _dot((cidx // HG_SUB == r).astype(BF16), (lf * (-LOG2E)).astype(BF16))
    return jnp.max(span) <= HG_FAST_SPAN


def _hg_chunk(r0, rev, fast, st, qs_sc, lf_sc, kk_sc, i_ref, a_sc, v_sc, o_sc, seg, bdmask, mask):
    C = HG_CHUNK
    S = HG_SUB
    qs = qs_sc[pl.ds(r0, C), :]
    b = lf_sc[pl.ds(r0, C), :]
    kk = kk_sc[pl.ds(r0, C), :]
    v = i_ref[0, pl.ds(r0, C), :]
    b2 = b * LOG2E
    total = b[0:1, :] if rev else b[C - 1:C, :]
    inter = _dot_nt((qs * jnp.exp2(b2)).astype(BF16), st.astype(BF16))
    hm = _head_masks(256, HG_DK, HG_HEADS, F32)
    off, blocks = _hg_stacked(qs, kk, v, b2, rev, fast, mask, hm)

    def head_sum(k_, i):
        acc = None
        for h in range(HG_HEADS):
            r_ = k_ * HG_HEADS * S + h * S + 8 * i
            t_ = off[r_:r_ + 8, :] * hm[h]
            acc = t_ if acc is None else acc + t_
        return acc

    if fast:
        tiles = []
        for k_, (g0, _, _) in enumerate(blocks):
            for i in range(S // 8):
                tiles.append(inter[g0 + 8 * i:g0 + 8 * i + 8, :] + head_sum(k_, i))
        o = jnp.concatenate(tiles, axis=0)
    else:
        a_sc[...] = jnp.log2(kk) - b2
        v_sc[...] = v
        o_sc[...] = inter
        for k_, (g0, _, _) in enumerate(blocks):
            for i in range(S // 8):
                o_sc[g0 + 8 * i:g0 + 8 * i + 8, :] = o_sc[g0 + 8 * i:g0 + 8 * i + 8, :] + head_sum(k_, i)
        sub = lax.broadcasted_iota(jnp.int32, (8, 256), 0)
        for g0 in range(0, C, S):
            parts, spans = [], []
            for s in range(g0, g0 + S):
                t0 = (s // 8) * 8
                arow = a_sc[s:s + 1, :]
                keep = (sub <= s - t0) if rev else (sub >= s - t0)
                rdiag = qs[t0:t0 + 8, :] * jnp.exp2(jnp.where(keep, b2[t0:t0 + 8, :] + arow, NEG_INF))
                rest_lo, rest_hi = (g0, t0) if rev else (t0 + 8, g0 + S)
                if rest_hi > rest_lo:
                    rrest = qs[rest_lo:rest_hi, :] * jnp.exp2(b2[rest_lo:rest_hi, :] + arow)
                    parts += [rrest, rdiag] if rev else [rdiag, rrest]
                else:
                    parts.append(rdiag)
                spans.append((min(rest_lo, t0), max(rest_hi, t0 + 8)) if rest_hi > rest_lo else (t0, t0 + 8))
            res = _dot(jnp.concatenate(parts, axis=0).astype(BF16), seg)
            acc = [o_sc[g0 + 8 * i:g0 + 8 * i + 8, :] for i in range(S // 8)]
            pos = 0
            for s, (lo_r, hi_r) in zip(range(g0, g0 + S), spans):
                vrow = v_sc[s:s + 1, :]
                for i in range((hi_r - lo_r) // 8):
                    ti = (lo_r - g0) // 8 + i
                    acc[ti] = acc[ti] + res[pos + 8 * i:pos + 8 * i + 8, :] * vrow
                pos += hi_r - lo_r
            for i in range(S // 8):
                o_sc[g0 + 8 * i:g0 + 8 * i + 8, :] = acc[i]
        o = o_sc[...]
    kd = (kk * jnp.exp(total - b)).astype(BF16)
    return o, jnp.exp(total) * st + bdmask * _dot_tn(v.astype(BF16), kd)


def _hg_kernel(l_ref, q_ref, f_ref, i_ref, og_ref, z_ref, lbl_ref, g_ref, s0_ref, y_ref, sto_ref,
               st_ref, oacc_ref, qs_sc, lf_sc, kk_sc, a_sc, v_sc, o_sc, *, nb):
    j = pl.program_id(1)
    l = l_ref[0]
    C = HG_CHUNK
    ncb = TOK_TILE // C

    @pl.when((j == 0) | (j == nb))
    def _():
        st_ref[...] = s0_ref[0, 0]

    lg = lbl_ref[...]
    mx = lg[0]
    for m in range(1, DEPTH):
        mx = jnp.maximum(mx, lg[m])
    ex = [jnp.exp(lg[m] - mx) for m in range(DEPTH)]
    tot = ex[0]
    for m in range(1, DEPTH):
        tot = tot + ex[m]
    lbs = jnp.zeros_like(tot)
    for m in range(1, DEPTH):
        lbs = lbs + jnp.where(m <= l, ex[m] / tot, 0.0)

    seg = _seg_matrix(256, HG_DK)
    seg_b = seg.astype(BF16)
    bdmask = seg.astype(F32)

    def run(rev, emit):
        lb = lbs[1:2, :] if rev else lbs[0:1, :]
        small = _hg_gates(lb, rev, q_ref, f_ref, qs_sc, lf_sc, kk_sc)
        common = (qs_sc, lf_sc, kk_sc, i_ref, a_sc, v_sc, o_sc, seg_b, bdmask)

        @pl.when(small)
        def _():
            mask = _hg_mask(rev, True)
            st = st_ref[...]
            for ci in range(ncb):
                r0 = (ncb - 1 - ci if rev else ci) * C
                o, st = _hg_chunk(r0, rev, True, st, *common, mask)
                emit(r0, o)
            st_ref[...] = st

        @pl.when(jnp.logical_not(small))
        def _():
            mask = _hg_mask(rev, False)

            def body(ci, carry):
                r0 = pl.multiple_of((ncb - 1 - ci if rev else ci) * C, C)
                o, st = _hg_chunk(r0, rev, False, st_ref[...], *common, mask)
                st_ref[...] = st
                emit(r0, o)
                return carry

            lax.fori_loop(0, ncb, body, 0)

    @pl.when(j < nb)
    def _():
        def emit(r0, o):
            oacc_ref[pl.ds(pl.multiple_of(j * TOK_TILE + r0, C), C), :] = o

        run(False, emit)

    @pl.when(j >= nb)
    def _():
        blk = 2 * nb - 1 - j

        def emit(r0, o):
            o = oacc_ref[pl.ds(pl.multiple_of(blk * TOK_TILE + r0, C), C), :] + o
            ms = _seg_mean(o * o, seg_b, HG_DK)
            on = o * lax.rsqrt(ms + EPS) * g_ref[0]
            on = on * _sigmoid(og_ref[0, pl.ds(r0, C), :])
            y_ref[0, pl.ds(r0, C), :] = (on * _silu(z_ref[0, pl.ds(r0, C), :])).astype(BF16)

        run(True, emit)

    sto_ref[0, 0] = st_ref[...]


def _hgrn(lidx, cols_hg, lb_logits, g3, s0, s0_per_batch):
    B, T, _ = cols_hg.shape
    nb = T // TOK_TILE

    def blk(j):
        return jnp.where(j < nb, j, 2 * nb - 1 - j)

    def dirn(j):
        return jnp.where(j < nb, 0, 1)

    def col(jcol):
        return pl.BlockSpec((1, TOK_TILE, 256), lambda b, j, l, jcol=jcol: (b, blk(j), jcol))

    grid_spec = pltpu.PrefetchScalarGridSpec(
        num_scalar_prefetch=1,
        grid=(B, 2 * nb),
        in_specs=[
            col(0),
            pl.BlockSpec((1, TOK_TILE, 256), lambda b, j, l: (b, blk(j), 1 + dirn(j))),
            col(3), col(4), col(5),
            pl.BlockSpec((DEPTH, 2, 256), lambda b, j, l: (0, 0, 0)),
            pl.BlockSpec((1, 1, 256), lambda b, j, l: (l[0], 0, 0)),
            pl.BlockSpec((1, 1, 256, 256),
                         (lambda b, j, l: (b, dirn(j), 0, 0)) if s0_per_batch else (lambda b, j, l: (0, 0, 0, 0))),
        ],
        out_specs=[
            pl.BlockSpec((1, TOK_TILE, 256), lambda b, j, l: (b, jnp.where(j < nb, nb - 1, 2 * nb - 1 - j), 0)),
            pl.BlockSpec((1, 1, 256, 256), lambda b, j, l: (b, dirn(j), 0, 0)),
        ],
        scratch_shapes=[
            pltpu.VMEM((256, 256), F32),
            pltpu.VMEM((T, 256), F32),
            pltpu.VMEM((TOK_TILE, 256), F32),
            pltpu.VMEM((TOK_TILE, 256), F32),
            pltpu.VMEM((TOK_TILE, 256), F32),
            pltpu.VMEM((HG_CHUNK, 256), F32),
            pltpu.VMEM((HG_CHUNK, 256), F32),
            pltpu.VMEM((HG_CHUNK, 256), F32),
        ],
    )
    return pl.pallas_call(
        functools.partial(_hg_kernel, nb=nb),
        grid_spec=grid_spec,
        out_shape=[
            jax.ShapeDtypeStruct((B, T, 256), BF16),
            jax.ShapeDtypeStruct((B, 2, 256, 256), F32),
        ],
        compiler_params=_cparams(("arbitrary", "arbitrary")),
        name="hgrn",
    )(lidx, cols_hg, cols_hg, cols_hg, cols_hg, cols_hg, lb_logits, g3, s0)


def _rot_cols(w, nheads):
    shp = w.shape
    r = shp[-1] // nheads
    w4 = w.reshape(shp[:-1] + (nheads, 2, r // 2))
    return jnp.concatenate([-w4[..., 1, :], w4[..., 0, :]], axis=-1).reshape(shp)


def _perm_heads(w, nheads, order):
    shp = w.shape
    w3 = w.reshape(shp[:-1] + (nheads, shp[-1] // nheads))
    return w3[..., list(order), :].reshape(shp)


_WG_ORDER = (0, 2, 1, 3)


def _mla_rope_slot(a):
    pad = [(0, 0)] * (a.ndim - 1) + [(MLA_NOPE, 128 - MLA_NOPE - MLA_ROPE)]
    return jnp.pad(a, pad)


def _prep_w_in(w_in):
    def c(name):
        lo, hi = _IN_OFF[name]
        return w_in[:, :, lo:hi]

    kva = c('mla_kva')
    lo = _IN_OFF['na_q'][0]
    hi = _IN_OFF['hg_z'][1]
    parts = [
        w_in[:, :, lo:hi],
        c('mla_qa'), c('mla_z'), kva[:, :, :MLA_KV_RANK], _mla_rope_slot(kva[:, :, MLA_KV_RANK:]),
        _perm_heads(c('wg_q'), WG_HEADS, _WG_ORDER), _perm_heads(c('wg_z'), WG_HEADS, _WG_ORDER),
        c('wg_k'), c('wg_v'),
    ]
    return jnp.concatenate([p.astype(BF16) for p in parts], axis=-1), c('merge').astype(BF16)


def _rot_matrix(width, head):
    i = np.arange(width)[:, None]
    j = np.arange(width)[None, :]
    half = head // 2
    first = (j % head) < half
    m = np.where(first & (i == j + half), -1.0, 0.0) + np.where(~first & (i == j - half), 1.0, 0.0)
    return jnp.asarray(m, BF16)


def _rope_tables(n_tok, rot_dim, reps):
    t = np.arange(n_tok)
    row = (t // GRID_W).astype(np.float32)
    col = (t % GRID_W).astype(np.float32)
    nf = rot_dim // 4
    inv = jnp.asarray(ROPE_BASE, F32) ** (-jnp.arange(nf, dtype=F32) / nf)
    ang = jnp.concatenate([jnp.asarray(row)[:, None] * inv, jnp.asarray(col)[:, None] * inv], axis=-1)
    cos, sin = jnp.cos(ang), jnp.sin(ang)
    return (jnp.tile(jnp.concatenate([cos, cos], axis=-1), (1, reps)),
            jnp.tile(jnp.concatenate([sin, sin], axis=-1), (1, reps)))


def _state_to_blockdiag(st):
    B = st.shape[0]
    stt = jnp.swapaxes(st, -1, -2)
    eye = jnp.eye(HG_HEADS, dtype=st.dtype)
    bd = stt[:, :, :, :, None, :] * eye[None, None, :, None, :, None]
    return bd.reshape(B, 2, HG_HEADS * HG_DK, HG_HEADS * HG_DK)


def _blockdiag_to_state(bd):
    B = bd.shape[0]
    b6 = bd.reshape(B, 2, HG_HEADS, HG_DK, HG_HEADS, HG_DK)
    diag = jnp.stack([b6[:, :, h, :, h, :] for h in range(HG_HEADS)], axis=2)
    return jnp.swapaxes(diag, -1, -2)


def kernel(x_prompt, x_sample, cache_na_k, cache_na_v, state_hgrn, cache_mla_ckv, cache_mla_krope, cache_wg_k, cache_wg_v, c, c_ctx, norm_g, w_mod, b_mod, w_in, na_rpb, hg_lb_logits, hg_norm_g, mla_q_norm_g, mla_kv_norm_g, mla_w_qb, mla_w_kvb, wg_sink, w_branch, w_out, final_g):
    Bp, Tp, _ = x_prompt.shape
    Bd, Td, _ = x_sample.shape
    L = cache_na_k.shape[2]
    rows = Td // GRID_W
    assert Tp % TOK_TILE == 0 and Td % TOK_TILE == 0 and L % TOK_TILE == 0
    assert rows % NA_QROWS == 0 and rows >= NA_KROWS and Td >= WG_KWIN

    w_proj, w_merge = _prep_w_in(w_in)
    wb = w_branch.astype(BF16)
    wb = wb.at[:, 3].set(_perm_heads(jnp.swapaxes(wb[:, 3], -1, -2), WG_HEADS, _WG_ORDER).swapaxes(-1, -2))
    wo = w_out.astype(BF16)
    qb = mla_w_qb.reshape(DEPTH, MLA_Q_RANK, MLA_HEADS, MLA_NOPE + MLA_ROPE)
    qb_g1 = jnp.pad(qb, ((0, 0), (0, 0), (0, 0), (0, 128 - MLA_NOPE - MLA_ROPE)))
    qb_g2 = _mla_rope_slot(_rot_cols(qb[..., MLA_NOPE:], 1))
    w_qb = jnp.concatenate([qb_g1.reshape(DEPTH, MLA_Q_RANK, -1), qb_g2.reshape(DEPTH, MLA_Q_RANK, -1)],
                           axis=-1).astype(BF16)
    kvb = mla_w_kvb.reshape(DEPTH, MLA_KV_RANK, MLA_HEADS, 2 * MLA_NOPE)
    kvb_k = jnp.pad(kvb[..., :MLA_NOPE], ((0, 0), (0, 0), (0, 0), (0, 128 - MLA_NOPE)))
    w_kvb = jnp.concatenate([kvb_k.reshape(DEPTH, MLA_KV_RANK, -1),
                             kvb[..., MLA_NOPE:].reshape(DEPTH, MLA_KV_RANK, -1)], axis=-1).astype(BF16)
    norm_g3 = norm_g.reshape(DEPTH, 1, D_MODEL)
    g_q3 = mla_q_norm_g.reshape(DEPTH, 1, MLA_Q_RANK)
    g_kv3 = mla_kv_norm_g.reshape(DEPTH, 1, MLA_KV_RANK)
    hg_g3 = jnp.tile(hg_norm_g, (1, HG_HEADS)).reshape(DEPTH, 1, HG_HEADS * HG_DK)
    final_g2 = final_g.reshape(1, D_MODEL)
    cq, sq = _rope_tables(Td, HEAD_DIM, WG_HEADS)
    ck, sk = cq[:, :128], sq[:, :128]
    c1, s1 = _rope_tables(Td, MLA_ROPE, 1)
    c4 = jnp.concatenate([jnp.ones((Td, MLA_NOPE), F32), c1, jnp.zeros((Td, 32), F32)], axis=-1)
    s4 = _mla_rope_slot(s1)
    rm_wg = _rot_matrix(WG_HEADS * HEAD_DIM, HEAD_DIM)
    rm_mla = jnp.zeros((128, 128), BF16).at[MLA_NOPE:MLA_NOPE + MLA_ROPE, MLA_NOPE:MLA_NOPE + MLA_ROPE].set(
        _rot_matrix(MLA_ROPE, MLA_ROPE))

    nrow = 8 * ((1 + Bd + 7) // 8)
    cond = jnp.zeros((nrow, D_MODEL), F32).at[0].set(c_ctx).at[1:1 + Bd].set(c)
    mod4 = _modulation(cond, w_mod, b_mod).reshape(DEPTH, nrow, 1, 3 * D_MODEL)

    kstart, ktype, table = _na_row_types(rows)
    na_bias = _na_bias(na_rpb, table)
    kstart = jnp.asarray(kstart, jnp.int32)
    ktype = jnp.asarray(ktype, jnp.int32)

    cna_k = cache_na_k.reshape(Bd, DEPTH, L, NA_HEADS * HEAD_DIM)
    cna_v = cache_na_v.reshape(Bd, DEPTH, L, NA_HEADS * HEAD_DIM)
    cwg_k = cache_wg_k.reshape(Bd, DEPTH, L, WG_KV_HEADS * HEAD_DIM)
    cwg_v = cache_wg_v.reshape(Bd, DEPTH, L, WG_KV_HEADS * HEAD_DIM)
    ckr4 = _mla_rope_slot(cache_mla_krope)
    st_bd = _state_to_blockdiag(state_hgrn.reshape(Bd * DEPTH, 2, HG_HEADS, HG_DK, HG_DK)).reshape(
        Bd, DEPTH, 2, 256, 256)
    st_zero = jnp.zeros((1, 2, 256, 256), F32)

    hp, hs = x_prompt, x_sample
    caches = [[] for _ in range(7)]
    for layer in range(DEPTH):
        lidx = jnp.full((1,), layer, jnp.int32)
        final = layer == DEPTH - 1

        flat = (1, Bp * Tp)
        na, hg, mla, wg, kc_, vc_, ckv_n, krs = _in_proj(
            lidx, hp.reshape(flat + (D_MODEL,)), mod4, 0, False, norm_g3, w_proj, g_kv3, w_kvb, rm_mla, c4, s4,
            False)
        na, hg, mla, wg, kc_, vc_ = (a.reshape(Bp, Tp, a.shape[-1]) for a in (na, hg, mla, wg, kc_, vc_))
        y_na = _na_ctx(na)
        y_hg, st_out = _hgrn(lidx, hg, hg_lb_logits, hg_g3, st_zero, False)
        y_mla = _mla_attn(lidx, mla, g_q3, w_qb, c4, s4, [(kc_, vc_)], False)
        y_wg = _wg_ctx(lidx, wg_sink, wg)
        ys = tuple(y.reshape(flat + (BRANCH_W,)) for y in (y_na, y_hg, y_mla, y_wg))
        hp = _post(lidx, hp.reshape(flat + (D_MODEL,)), mod4, 0, False, norm_g3, ys, w_merge, wb, wo, final_g2,
                   final).reshape(Bp, Tp, D_MODEL)
        caches[0].append(na[:, :, 256:512])
        caches[1].append(na[:, :, 512:768])
        caches[2].append(st_out)
        caches[3].append(ckv_n.reshape(Bp, Tp, MLA_KV_RANK))
        caches[4].append(krs.reshape(Bp, Tp, 128)[:, :, MLA_NOPE:MLA_NOPE + MLA_ROPE])
        caches[5].append(wg[:, :, 512:640])
        caches[6].append(wg[:, :, 640:768])

        na, hg, mla, wg, kl, vl = _in_proj(lidx, hs, mod4, 1, True, norm_g3, w_proj, g_kv3, w_kvb, rm_mla, c4, s4,
                                           True)
        y_na = _na_lat(lidx, kstart, ktype, na, cna_k, cna_v, na_bias)
        y_hg, _ = _hgrn(lidx, hg, hg_lb_logits, hg_g3, st_bd[:, layer], True)
        kx, vx = _mla_kv(lidx, cache_mla_ckv, 0, ckr4, 0, ckr4, 0, g_kv3, w_kvb, c4, s4, False, False, True)
        y_mla = _mla_attn(lidx, mla, g_q3, w_qb, c4, s4, [(kl, vl), (kx, vx)], True)
        y_wg = _wg_lat(lidx, wg_sink, wg, cwg_k, cwg_v, cq, sq, ck, sk, rm_wg)
        hs = _post(lidx, hs, mod4, 1, True, norm_g3, (y_na, y_hg, y_mla, y_wg), w_merge, wb, wo, final_g2, final)

    outs = [jnp.stack(cs, axis=1) for cs in caches]
    new_state = _blockdiag_to_state(outs[2].reshape(Bp * DEPTH, 2, 256, 256)).reshape(
        Bp, DEPTH, 2, HG_HEADS, HG_DK, HG_DK)
    return (hp, hs,
            outs[0].reshape(Bp, DEPTH, Tp, NA_HEADS, HEAD_DIM), outs[1].reshape(Bp, DEPTH, Tp, NA_HEADS, HEAD_DIM),
            new_state, outs[3], outs[4],
            outs[5].reshape(Bp, DEPTH, Tp, WG_KV_HEADS, HEAD_DIM), outs[6].reshape(Bp, DEPTH, Tp, WG_KV_HEADS, HEAD_DIM))
```

```python
import functools

import jax
import jax.numpy as jnp
import numpy as np
from jax import lax
from jax.experimental import pallas as pl
from jax.experimental.pallas import tpu as pltpu

F32 = jnp.float32
BF16 = jnp.bfloat16

D_MODEL = 1024
DEPTH = 4
GRID_W = 64
HEAD_DIM = 64
BRANCH_W = 256
N_BRANCH = 4
NA_HEADS = 4
NA_WIN_H = 8
NA_WIN_W = 16
HG_HEADS = 4
HG_DK = 64
HG_CHUNK = 64
HG_SUB = 16
HG_FAST_SPAN = 100.0
LOG2E = 1.4426950408889634
F_FLOOR = 1e-30
MLA_HEADS = 4
MLA_Q_RANK = 256
MLA_KV_RANK = 128
MLA_NOPE = 64
MLA_ROPE = 32
WG_HEADS = 4
WG_KV_HEADS = 2
WG_WINDOW = 128
ROPE_BASE = 10000.0
EPS = 1e-6
NEG_INF = -1e30

_IN_LAYOUT = (
    ('na_q', 256), ('na_k', 256), ('na_v', 256), ('na_z', 256),
    ('hg_q', 256), ('hg_ff', 256), ('hg_fb', 256), ('hg_i', 256), ('hg_og', 256), ('hg_z', 256),
    ('mla_qa', 256), ('mla_kva', 160), ('mla_z', 256),
    ('wg_q', 256), ('wg_k', 128), ('wg_v', 128), ('wg_z', 256),
    ('merge', 4096),
)
_IN_OFF = {}
_o = 0
for _n, _w in _IN_LAYOUT:
    _IN_OFF[_n] = (_o, _o + _w)
    _o += _w

NA_W = 1024
HG_W = 1536
MLA_W = 768
WG_W = 768
PROJ_W = NA_W + HG_W + MLA_W + WG_W

TOK_TILE = 256
PROJ_TILE = 512
NA_QROWS = 4
NA_KROWS = 12
WG_KWIN = 512
VMEM_LIMIT = 56 * 1024 * 1024


def _cparams(sem):
    return pltpu.CompilerParams(dimension_semantics=sem, vmem_limit_bytes=VMEM_LIMIT)


def _dot(a, b):
    return jnp.dot(a, b, preferred_element_type=F32)


def _dot_nt(a, b):
    return lax.dot_general(a, b, (((1,), (1,)), ((), ())), preferred_element_type=F32)


def _dot_tn(a, b):
    return lax.dot_general(a, b, (((0,), (0,)), ((), ())), preferred_element_type=F32)


def _sigmoid(x):
    return 1.0 / (1.0 + jnp.exp(-x))


def _silu(x):
    return x * _sigmoid(x)


def _lane_mask(width, lo, hi, dtype):
    lane = lax.broadcasted_iota(jnp.int32, (1, width), 1)
    return ((lane >= lo) & (lane < hi)).astype(dtype)


def _seg_matrix(width, seg):
    r = lax.broadcasted_iota(jnp.int32, (width, width), 0) // seg
    c = lax.broadcasted_iota(jnp.int32, (width, width), 1) // seg
    return r == c


def _seg_mean(x2, seg_bf16, seg):
    hi = x2.astype(BF16)
    lo = (x2 - hi.astype(F32)).astype(BF16)
    return (_dot(hi, seg_bf16) + _dot(lo, seg_bf16)) * (1.0 / seg)


def _mod_kernel(c_ref, w_ref, b_ref, o_ref):
    s = _silu(c_ref[...])
    o_ref[0] = _dot(s.astype(BF16), w_ref[0].astype(BF16)) + b_ref[0]


def _modulation(cond, w_mod, b_mod):
    nrow = cond.shape[0]
    tn = 1024
    return pl.pallas_call(
        _mod_kernel,
        grid=(DEPTH, 3 * D_MODEL // tn),
        in_specs=[
            pl.BlockSpec((nrow, D_MODEL), lambda l, j: (0, 0)),
            pl.BlockSpec((1, D_MODEL, tn), lambda l, j: (l, 0, j)),
            pl.BlockSpec((1, 1, tn), lambda l, j: (l, 0, j)),
        ],
        out_specs=pl.BlockSpec((1, nrow, tn), lambda l, j: (l, 0, j)),
        out_shape=jax.ShapeDtypeStruct((DEPTH, nrow, 3 * D_MODEL), F32),
        compiler_params=_cparams(("arbitrary", "arbitrary")),
        name="modulation",
    )(cond, w_mod, b_mod.reshape(DEPTH, 1, 3 * D_MODEL))


def _mod_norm(x, mod_ref, g_ref):
    ms = jnp.mean(x * x, axis=-1, keepdims=True)
    y = x * lax.rsqrt(ms + EPS) * g_ref[0]
    shift = mod_ref[0, 0, :, 0:D_MODEL]
    scale = mod_ref[0, 0, :, D_MODEL:2 * D_MODEL]
    gate = mod_ref[0, 0, :, 2 * D_MODEL:3 * D_MODEL]
    return y * (1.0 + scale) + shift, gate


_CACHE_WIDTHS = (NA_HEADS * HEAD_DIM, NA_HEADS * HEAD_DIM, MLA_KV_RANK, MLA_ROPE,
                 WG_KV_HEADS * HEAD_DIM, WG_KV_HEADS * HEAD_DIM)


def _in_kernel(l_ref, x_ref, mod_ref, g_ref, w_ref, gkv_ref, wkvb_ref, rm_ref, c_ref, s_ref, *rest,
               rope, n_alias, seq):
    rest = rest[n_alias:]
    na_ref, hg_ref, mla_ref, wg_ref, kall_ref, vall_ref = rest[:6]
    h, _ = _mod_norm(x_ref[0], mod_ref, g_ref)
    hb = h.astype(BF16)
    na = _dot(hb, w_ref[0, :, 0:NA_W])
    na_ref[0] = na.astype(na_ref.dtype)
    hg_ref[0] = _dot(hb, w_ref[0, :, NA_W:NA_W + HG_W])
    wg = _dot(hb, w_ref[0, :, NA_W + HG_W + MLA_W:PROJ_W])
    wg_ref[0] = wg.astype(wg_ref.dtype)
    mla = _dot(hb, w_ref[0, :, NA_W + HG_W:NA_W + HG_W + MLA_W])
    mla_ref[0] = mla[:, 0:512].astype(mla_ref.dtype)
    ckv = mla[:, 512:640]
    ms = jnp.mean(ckv * ckv, axis=-1, keepdims=True)
    ckv = ckv * lax.rsqrt(ms + EPS) * gkv_ref[0]
    kr = mla[:, 640:768]
    if not rope:
        pieces = (na[:, 256:512], na[:, 512:768], ckv, kr[:, MLA_NOPE:MLA_NOPE + MLA_ROPE],
                  wg[:, 512:640], wg[:, 640:768])
        for ref, val in zip(rest[6:], pieces):
            for i in range(val.shape[0] // seq):
                ref[i, 0] = val[i * seq:(i + 1) * seq, :]
    if rope:
        kr = kr * c_ref[...] + _dot(kr.astype(BF16), rm_ref[...]) * s_ref[...]
    kv = _dot(ckv.astype(BF16), wkvb_ref[0])
    for hd in range(MLA_HEADS):
        kall_ref[0, :, 128 * hd:128 * hd + 128] = (kv[:, 128 * hd:128 * hd + 128] + kr).astype(BF16)
    vall_ref[0] = kv[:, 512:768].astype(BF16)


def _in_proj(lidx, x, mod4, mod_row0, per_batch, norm_g3, w_proj, g_kv3, w_kvb, rm_mla, c4, s4, rope,
             caches=None, cache_shape=None):
    B, T, _ = x.shape
    tile = min(PROJ_TILE, T)
    nt = T // tile
    mrow = (lambda b: mod_row0 + b) if per_batch else (lambda b: mod_row0)
    col_dtype = BF16 if rope else F32

    def tok(width):
        return pl.BlockSpec((1, tile, width), lambda b, t, l: (b, t, 0))

    out_specs = [tok(NA_W), tok(HG_W), tok(512), tok(WG_W), tok(512), tok(256)]
    out_shape = [
        jax.ShapeDtypeStruct((B, T, NA_W), col_dtype),
        jax.ShapeDtypeStruct((B, T, HG_W), F32),
        jax.ShapeDtypeStruct((B, T, 512), col_dtype),
        jax.ShapeDtypeStruct((B, T, WG_W), col_dtype),
        jax.ShapeDtypeStruct((B, T, 512), BF16),
        jax.ShapeDtypeStruct((B, T, 256), BF16),
    ]
    in_specs = [
        pl.BlockSpec((1, tile, D_MODEL), lambda b, t, l: (b, t, 0)),
        pl.BlockSpec((1, 1, 1, 3 * D_MODEL), lambda b, t, l: (l[0], mrow(b), 0, 0)),
        pl.BlockSpec((1, 1, D_MODEL), lambda b, t, l: (l[0], 0, 0)),
        pl.BlockSpec((1, D_MODEL, PROJ_W), lambda b, t, l: (l[0], 0, 0)),
        pl.BlockSpec((1, 1, 128), lambda b, t, l: (l[0], 0, 0)),
        pl.BlockSpec((1, 128, 768), lambda b, t, l: (l[0], 0, 0)),
        pl.BlockSpec((128, 128), lambda b, t, l: (0, 0)),
        pl.BlockSpec((tile, 128), lambda b, t, l: (t if rope else 0, 0)),
        pl.BlockSpec((tile, 128), lambda b, t, l: (t if rope else 0, 0)),
    ]
    args = [lidx, x, mod4, norm_g3, w_proj, g_kv3, w_kvb, rm_mla, c4, s4]
    aliases = {}
    seq = tile
    if not rope:
        nbatch, _, seq = cache_shape
        assert B == 1 and tile % seq == 0
        for width in _CACHE_WIDTHS:
            out_specs.append(pl.BlockSpec((tile // seq, 1, seq, width), lambda b, t, l: (t, l[0], 0, 0)))
            out_shape.append(jax.ShapeDtypeStruct((nbatch, DEPTH, seq, width), F32))
        if caches is not None:
            for i, arr in enumerate(caches):
                aliases[len(args)] = 6 + i
                in_specs.append(pl.BlockSpec(memory_space=pl.ANY))
                args.append(arr)
    grid_spec = pltpu.PrefetchScalarGridSpec(
        num_scalar_prefetch=1,
        grid=(B, nt),
        in_specs=in_specs,
        out_specs=out_specs,
    )
    return pl.pallas_call(
        functools.partial(_in_kernel, rope=rope, n_alias=len(aliases), seq=seq),
        grid_spec=grid_spec,
        out_shape=out_shape,
        input_output_aliases=aliases,
        compiler_params=_cparams(("arbitrary", "arbitrary")),
        name="in_proj_lat" if rope else ("in_proj_ctx" if caches is None else "in_proj_ctx_inplace"),
    )(*args)


def _post_kernel(l_ref, x_ref, mod_ref, g_ref, y0_ref, y1_ref, y2_ref, y3_ref, wm_ref, wb_ref, wo_ref,
                 fg_ref, o_ref, *, final):
    x = x_ref[0]
    h, gate = _mod_norm(x, mod_ref, g_ref)
    hb = h.astype(BF16)
    acc = jnp.zeros(x.shape, F32)
    for n, y_ref in enumerate((y0_ref, y1_ref, y2_ref, y3_ref)):
        mg = _sigmoid(_dot(hb, wm_ref[0, :, n * D_MODEL:(n + 1) * D_MODEL]))
        acc = acc + mg * _dot(y_ref[0], wb_ref[0, n])
    out = x + gate * _dot(acc.astype(BF16), wo_ref[0])
    if final:
        ms = jnp.mean(out * out, axis=-1, keepdims=True)
        out = out * lax.rsqrt(ms + EPS) * fg_ref[...]
    o_ref[0] = out


def _post(lidx, x, mod4, mod_row0, per_batch, norm_g3, ys, w_merge, w_branch, w_out, final_g2, final):
    B, T, _ = x.shape
    tile = min(PROJ_TILE, T)
    nt = T // tile
    mrow = (lambda b: mod_row0 + b) if per_batch else (lambda b: mod_row0)
    yspec = pl.BlockSpec((1, tile, BRANCH_W), lambda b, t, l: (b, t, 0))
    grid_spec = pltpu.PrefetchScalarGridSpec(
        num_scalar_prefetch=1,
        grid=(B, nt),
        in_specs=[
            pl.BlockSpec((1, tile, D_MODEL), lambda b, t, l: (b, t, 0)),
            pl.BlockSpec((1, 1, 1, 3 * D_MODEL), lambda b, t, l: (l[0], mrow(b), 0, 0)),
            pl.BlockSpec((1, 1, D_MODEL), lambda b, t, l: (l[0], 0, 0)),
            yspec, yspec, yspec, yspec,
            pl.BlockSpec((1, D_MODEL, N_BRANCH * D_MODEL), lambda b, t, l: (l[0], 0, 0)),
            pl.BlockSpec((1, N_BRANCH, BRANCH_W, D_MODEL), lambda b, t, l: (l[0], 0, 0, 0)),
            pl.BlockSpec((1, D_MODEL, D_MODEL), lambda b, t, l: (l[0], 0, 0)),
            pl.BlockSpec((1, D_MODEL), lambda b, t, l: (0, 0)),
        ],
        out_specs=pl.BlockSpec((1, tile, D_MODEL), lambda b, t, l: (b, t, 0)),
    )
    return pl.pallas_call(
        functools.partial(_post_kernel, final=final),
        grid_spec=grid_spec,
        out_shape=jax.ShapeDtypeStruct((B, T, D_MODEL), F32),
        compiler_params=_cparams(("arbitrary", "arbitrary")),
        name="post_final" if final else "post",
    )(lidx, x, mod4, norm_g3, *ys, w_merge, w_branch, w_out, final_g2)


def _attend(q, kparts, vparts, biases, qmasks, omasks, sinks=None):
    out = None
    for h in range(len(omasks)):
        qh = q[h] if qmasks is None else q * qmasks[h]
        ss = []
        for i, kp in enumerate(kparts):
            s = _dot_nt(qh, kp[h] if isinstance(kp, (list, tuple)) else kp)
            if biases is not None and biases[h][i] is not None:
                s = s + biases[h][i]
            ss.append(s)
        m = ss[0].max(axis=-1, keepdims=True)
        for s in ss[1:]:
            m = jnp.maximum(m, s.max(axis=-1, keepdims=True))
        if sinks is not None:
            m = jnp.maximum(m, sinks[h])
        den = None
        oh = None
        for s, vp in zip(ss, vparts):
            p = jnp.exp2(s - m)
            ps = p.sum(axis=-1, keepdims=True)
            den = ps if den is None else den + ps
            pv = _dot(p.astype(BF16), vp)
            oh = pv if oh is None else oh + pv
        if sinks is not None:
            den = den + jnp.exp2(sinks[h] - m)
        oh = (oh / den) * omasks[h]
        out = oh if out is None else out + oh
    return out


def _head_masks(width, seg, nheads, dtype):
    return [_lane_mask(width, h * seg, (h + 1) * seg, dtype) for h in range(nheads)]


def _na_ctx_kernel(q_ref, k_ref, v_ref, z_ref, o_ref):
    q = q_ref[0].astype(BF16)
    k = k_ref[0].astype(BF16)
    v = v_ref[0].astype(BF16)
    o = _attend(q, [k], [v], None, _head_masks(256, 64, NA_HEADS, BF16), _head_masks(256, 64, NA_HEADS, F32))
    o_ref[0] = (o * _silu(z_ref[0].astype(F32))).astype(BF16)


def _na_ctx(cols_na):
    B, T, _ = cols_na.shape

    def spec(j):
        return pl.BlockSpec((1, T, 256), lambda b, j=j: (b, 0, j))

    return pl.pallas_call(
        _na_ctx_kernel,
        grid=(B,),
        in_specs=[spec(0), spec(1), spec(2), spec(3)],
        out_specs=pl.BlockSpec((1, T, 256), lambda b: (b, 0, 0)),
        out_shape=jax.ShapeDtypeStruct((B, T, 256), BF16),
        compiler_params=_cparams(("arbitrary",)),
        name="na_ctx",
    )(cols_na, cols_na, cols_na, cols_na)


def _na_row_types(rows):
    wh = min(NA_WIN_H, rows)
    nblk = rows // NA_QROWS
    starts, types, table = [], [], []
    for blk in range(nblk):
        r0 = blk * NA_QROWS
        k0 = int(np.clip(r0 - wh // 2, 0, rows - NA_KROWS))
        desc = []
        for i in range(NA_QROWS):
            r = r0 + i
            ks = int(np.clip(r - wh // 2, 0, rows - wh))
            for j in range(NA_KROWS):
                kr = k0 + j
                desc.append((kr - r + NA_WIN_H - 1) if ks <= kr < ks + wh else None)
        desc = tuple(desc)
        if desc not in table:
            table.append(desc)
        starts.append(k0)
        types.append(table.index(desc))
    return starts, types, table


def _na_bias_kernel(rpb_ref, o_ref, t2_ref, *, table):
    l = pl.program_id(0)
    h = pl.program_id(1)
    nrow = 2 * NA_WIN_H - 1
    ncol = 2 * NA_WIN_W - 1
    c = lax.broadcasted_iota(jnp.int32, (GRID_W, 128), 0)
    lane = lax.broadcasted_iota(jnp.int32, (GRID_W, 128), 1)
    first = lane < GRID_W
    kc = jnp.where(first, lane, lane - GRID_W)
    qstart = jnp.clip(c - NA_WIN_W // 2, 0, GRID_W - NA_WIN_W)
    ok = (kc >= qstart) & (kc < qstart + NA_WIN_W)
    dcol = kc - c + NA_WIN_W - 1
    for a in range(-1, nrow):
        acc = jnp.full((GRID_W, 128), NEG_INF, F32)
        for bi in range(2 * NA_WIN_W - 1):
            base = (l * NA_HEADS + h) * nrow
            lo = rpb_ref[(base + a) * ncol + bi] if a >= 0 else jnp.float32(NEG_INF)
            hi = rpb_ref[(base + a + 1) * ncol + bi] if a + 1 < nrow else jnp.float32(NEG_INF)
            acc = jnp.where(dcol == bi, jnp.where(first, lo, hi), acc)
        t2_ref[a + 1] = jnp.where(ok, acc * LOG2E, NEG_INF)
    for ti, desc in enumerate(table):
        for i in range(NA_QROWS):
            for p in range(NA_KROWS // 2):
                a0 = desc[i * NA_KROWS + 2 * p]
                a1 = desc[i * NA_KROWS + 2 * p + 1]
                if a0 is None and a1 is None:
                    tile = jnp.full((GRID_W, 128), NEG_INF, F32)
                elif a1 is None:
                    tile = jnp.where(first, t2_ref[a0 + 1], NEG_INF)
                elif a0 is None:
                    tile = jnp.where(first, NEG_INF, t2_ref[a1])
                else:
                    tile = t2_ref[a0 + 1]
                o_ref[0, ti, 0, i * GRID_W:(i + 1) * GRID_W, p * 128:(p + 1) * 128] = tile


def _na_bias(na_rpb, table):
    nt = len(table)
    nq = NA_QROWS * GRID_W
    nk = NA_KROWS * GRID_W
    return pl.pallas_call(
        functools.partial(_na_bias_kernel, table=table),
        grid=(DEPTH, NA_HEADS),
        in_specs=[pl.BlockSpec(memory_space=pltpu.SMEM)],
        out_specs=pl.BlockSpec((1, nt, 1, nq, nk), lambda l, h: (l, 0, h, 0, 0)),
        out_shape=jax.ShapeDtypeStruct((DEPTH, nt, NA_HEADS, nq, nk), F32),
        scratch_shapes=[pltpu.VMEM((2 * NA_WIN_H, GRID_W, 128), F32)],
        compiler_params=_cparams(("arbitrary", "arbitrary")),
        name="na_bias",
    )(na_rpb.reshape(-1))


def _na_lat_kernel(l_ref, ks_ref, ty_ref, q_ref, k_ref, v_ref, z_ref, kc_ref, vc_ref, b_ref, o_ref):
    rb = pl.program_id(1)
    nk = NA_KROWS * GRID_W
    ks = pl.multiple_of(ks_ref[rb] * GRID_W, GRID_W)
    q = q_ref[0].astype(BF16)
    kw = k_ref[0, pl.ds(ks, nk), :].astype(BF16)
    vw = v_ref[0, pl.ds(ks, nk), :].astype(BF16)
    kc = kc_ref[0, 0].astype(BF16)
    vc = vc_ref[0, 0].astype(BF16)
    biases = [[b_ref[0, 0, h], None] for h in range(NA_HEADS)]
    o = _attend(q, [kw, kc], [vw, vc], biases,
                _head_masks(256, 64, NA_HEADS, BF16), _head_masks(256, 64, NA_HEADS, F32))
    o_ref[0] = (o * _silu(z_ref[0].astype(F32))).astype(BF16)


def _na_lat(lidx, kstart, ktype, cols_na, cache_k, cache_v, bias):
    B, T, _ = cols_na.shape
    L = cache_k.shape[2]
    nq = NA_QROWS * GRID_W
    nk = NA_KROWS * GRID_W
    nblk = T // nq
    grid_spec = pltpu.PrefetchScalarGridSpec(
        num_scalar_prefetch=3,
        grid=(B, nblk),
        in_specs=[
            pl.BlockSpec((1, nq, 256), lambda b, r, l, ks, ty: (b, r, 0)),
            pl.BlockSpec((1, T, 256), lambda b, r, l, ks, ty: (b, 0, 1)),
            pl.BlockSpec((1, T, 256), lambda b, r, l, ks, ty: (b, 0, 2)),
            pl.BlockSpec((1, nq, 256), lambda b, r, l, ks, ty: (b, r, 3)),
            pl.BlockSpec((1, 1, L, 256), lambda b, r, l, ks, ty: (b, l[0], 0, 0)),
            pl.BlockSpec((1, 1, L, 256), lambda b, r, l, ks, ty: (b, l[0], 0, 0)),
            pl.BlockSpec((1, 1, NA_HEADS, nq, nk), lambda b, r, l, ks, ty: (l[0], ty[r], 0, 0, 0)),
        ],
        out_specs=pl.BlockSpec((1, nq, 256), lambda b, r, l, ks, ty: (b, r, 0)),
    )
    return pl.pallas_call(
        _na_lat_kernel,
        grid_spec=grid_spec,
        out_shape=jax.ShapeDtypeStruct((B, T, 256), BF16),
        compiler_params=_cparams(("arbitrary", "arbitrary")),
        name="na_lat",
    )(lidx, kstart, ktype, cols_na, cols_na, cols_na, cols_na, cache_k, cache_v, bias)


def _wg_heads(q, kparts, vparts, bias, sink_ref, l):
    qm = _head_masks(128, 64, 2, BF16)
    om = _head_masks(128, 64, 2, F32)
    outs = []
    for grp, heads in enumerate(((0, 2), (1, 3))):
        qg = q[:, grp * 128:(grp + 1) * 128]
        sinks = [sink_ref[l, hh] * LOG2E for hh in heads]
        biases = None if bias is None else [[bias] + [None] * (len(kparts) - 1)] * 2
        outs.append(_attend(qg, kparts, vparts, biases, qm, om, sinks))
    return jnp.concatenate(outs, axis=-1)


def _wg_ctx_kernel(l_ref, sink_ref, q_ref, z_ref, k_ref, v_ref, o_ref):
    q = q_ref[0].astype(BF16)
    o = _wg_heads(q, [k_ref[0].astype(BF16)], [v_ref[0].astype(BF16)], None, sink_ref, l_ref[0])
    o_ref[0] = (o * _silu(z_ref[0].astype(F32))).astype(BF16)


def _wg_ctx(lidx, sink, cols_wg):
    B, T, _ = cols_wg.shape
    grid_spec = pltpu.PrefetchScalarGridSpec(
        num_scalar_prefetch=1,
        grid=(B,),
        in_specs=[
            pl.BlockSpec(memory_space=pltpu.SMEM),
            pl.BlockSpec((1, T, 256), lambda b, l: (b, 0, 0)),
            pl.BlockSpec((1, T, 256), lambda b, l: (b, 0, 1)),
            pl.BlockSpec((1, T, 128), lambda b, l: (b, 0, 4)),
            pl.BlockSpec((1, T, 128), lambda b, l: (b, 0, 5)),
        ],
        out_specs=pl.BlockSpec((1, T, 256), lambda b, l: (b, 0, 0)),
    )
    return pl.pallas_call(
        _wg_ctx_kernel,
        grid_spec=grid_spec,
        out_shape=jax.ShapeDtypeStruct((B, T, 256), BF16),
        compiler_params=_cparams(("arbitrary",)),
        name="wg_ctx",
    )(lidx, sink, cols_wg, cols_wg, cols_wg, cols_wg)


def _wg_lat_kernel(l_ref, sink_ref, q_ref, z_ref, k_ref, v_ref, kc_ref, vc_ref,
                   cq_ref, sq_ref, ck_ref, sk_ref, rm_ref, o_ref, *, seq):
    tb = pl.program_id(1)
    t0 = tb * TOK_TILE
    k0 = pl.multiple_of(jnp.clip(t0 - WG_WINDOW, 0, seq - WG_KWIN), 128)
    q = q_ref[0]
    q = q.astype(F32) * cq_ref[...] + _dot(q.astype(BF16), rm_ref[...]) * sq_ref[...]
    kwin = k_ref[0, pl.ds(k0, WG_KWIN), :]
    kwin = kwin.astype(F32) * ck_ref[pl.ds(k0, WG_KWIN), :] \
        + _dot(kwin.astype(BF16), rm_ref[0:128, 0:128]) * sk_ref[pl.ds(k0, WG_KWIN), :]
    vwin = v_ref[0, pl.ds(k0, WG_KWIN), :]
    qi = t0 + lax.broadcasted_iota(jnp.int32, (TOK_TILE, WG_KWIN), 0)
    kj = k0 + lax.broadcasted_iota(jnp.int32, (TOK_TILE, WG_KWIN), 1)
    bias = jnp.where(jnp.abs(qi - kj) <= WG_WINDOW, 0.0, NEG_INF).astype(F32)
    o = _wg_heads(q.astype(BF16),
                  [kwin.astype(BF16), kc_ref[0, 0].astype(BF16)],
                  [vwin.astype(BF16), vc_ref[0, 0].astype(BF16)],
                  bias, sink_ref, l_ref[0])
    o_ref[0] = (o * _silu(z_ref[0].astype(F32))).astype(BF16)


def _wg_lat(lidx, sink, cols_wg, cache_k, cache_v, cq, sq, ck, sk, rm):
    B, T, _ = cols_wg.shape
    L = cache_k.shape[2]
    nt = T // TOK_TILE
    grid_spec = pltpu.PrefetchScalarGridSpec(
        num_scalar_prefetch=1,
        grid=(B, nt),
        in_specs=[
            pl.BlockSpec(memory_space=pltpu.SMEM),
            pl.BlockSpec((1, TOK_TILE, 256), lambda b, t, l: (b, t, 0)),
            pl.BlockSpec((1, TOK_TILE, 256), lambda b, t, l: (b, t, 1)),
            pl.BlockSpec((1, T, 128), lambda b, t, l: (b, 0, 4)),
            pl.BlockSpec((1, T, 128), lambda b, t, l: (b, 0, 5)),
            pl.BlockSpec((1, 1, L, 128), lambda b, t, l: (b, l[0], 0, 0)),
            pl.BlockSpec((1, 1, L, 128), lambda b, t, l: (b, l[0], 0, 0)),
            pl.BlockSpec((TOK_TILE, 256), lambda b, t, l: (t, 0)),
            pl.BlockSpec((TOK_TILE, 256), lambda b, t, l: (t, 0)),
            pl.BlockSpec((T, 128), lambda b, t, l: (0, 0)),
            pl.BlockSpec((T, 128), lambda b, t, l: (0, 0)),
            pl.BlockSpec((256, 256), lambda b, t, l: (0, 0)),
        ],
        out_specs=pl.BlockSpec((1, TOK_TILE, 256), lambda b, t, l: (b, t, 0)),
    )
    return pl.pallas_call(
        functools.partial(_wg_lat_kernel, seq=T),
        grid_spec=grid_spec,
        out_shape=jax.ShapeDtypeStruct((B, T, 256), BF16),
        compiler_params=_cparams(("arbitrary", "arbitrary")),
        name="wg_lat",
    )(lidx, sink, cols_wg, cols_wg, cols_wg, cols_wg, cache_k, cache_v, cq, sq, ck, sk, rm)


def _mla_kv_kernel(l_ref, ckv_ref, kr_ref, krr_ref, g_ref, w_ref, c_ref, s_ref, k_out, v_out, *rest,
                   norm, rope):
    ckv = ckv_ref[0]
    if norm:
        ms = jnp.mean(ckv * ckv, axis=-1, keepdims=True)
        ckv = ckv * lax.rsqrt(ms + EPS) * g_ref[0]
        rest[0][0] = ckv
    kv = _dot(ckv.astype(BF16), w_ref[0])
    kr = kr_ref[0]
    if rope:
        kr = kr * c_ref[...] + krr_ref[0] * s_ref[...]
    for h in range(MLA_HEADS):
        k_out[0, :, 128 * h:128 * h + 128] = (kv[:, 128 * h:128 * h + 128] + kr).astype(BF16)
    v_out[0] = kv[:, 512:768].astype(BF16)


def _mla_kv(lidx, ckv_arr, ckv_blk, kr_arr, kr_blk, krr_arr, krr_blk, g_kv3, w_kvb, c4, s4, norm, rope,
            layered):
    B = ckv_arr.shape[0]
    T = ckv_arr.shape[-2]
    nt = T // TOK_TILE
    if layered:
        def src(blk):
            return pl.BlockSpec((None, 1, TOK_TILE, 128), lambda b, t, l: (b, l[0], t, 0))
    else:
        def src(blk):
            return pl.BlockSpec((1, TOK_TILE, 128), lambda b, t, l, blk=blk: (b, t, blk))
    out_specs = [
        pl.BlockSpec((1, TOK_TILE, 512), lambda b, t, l: (b, t, 0)),
        pl.BlockSpec((1, TOK_TILE, 256), lambda b, t, l: (b, t, 0)),
    ]
    out_shape = [
        jax.ShapeDtypeStruct((B, T, 512), BF16),
        jax.ShapeDtypeStruct((B, T, 256), BF16),
    ]
    if norm:
        out_specs.append(pl.BlockSpec((1, TOK_TILE, 128), lambda b, t, l: (b, t, 0)))
        out_shape.append(jax.ShapeDtypeStruct((B, T, 128), F32))
    grid_spec = pltpu.PrefetchScalarGridSpec(
        num_scalar_prefetch=1,
        grid=(B, nt),
        in_specs=[
            src(ckv_blk), src(kr_blk), src(krr_blk),
            pl.BlockSpec((1, 1, 128), lambda b, t, l: (l[0], 0, 0)),
            pl.BlockSpec((1, 128, 768), lambda b, t, l: (l[0], 0, 0)),
            pl.BlockSpec((TOK_TILE, 128), lambda b, t, l: (t if rope else 0, 0)),
            pl.BlockSpec((TOK_TILE, 128), lambda b, t, l: (t if rope else 0, 0)),
        ],
        out_specs=out_specs,
    )
    return pl.pallas_call(
        functools.partial(_mla_kv_kernel, norm=norm, rope=rope),
        grid_spec=grid_spec,
        out_shape=out_shape,
        compiler_params=_cparams(("arbitrary", "arbitrary")),
        name="mla_kv" + ("_norm" if norm else "") + ("_rope" if rope else ""),
    )(lidx, ckv_arr, kr_arr, krr_arr, g_kv3, w_kvb, c4, s4)


def _mla_attn_kernel(l_ref, qa_ref, z_ref, g_ref, w_ref, c_ref, s_ref, *refs, nparts, rope):
    kparts = [[refs[2 * i][0, :, 128 * h:128 * h + 128] for h in range(MLA_HEADS)] for i in range(nparts)]
    vparts = [refs[2 * i + 1][0] for i in range(nparts)]
    o_ref = refs[2 * nparts]
    qa = qa_ref[0].astype(F32)
    ms = jnp.mean(qa * qa, axis=-1, keepdims=True)
    qn = qa * lax.rsqrt(ms + EPS) * g_ref[0]
    qq = _dot(qn.astype(BF16), w_ref[0])
    q = []
    for h in range(MLA_HEADS):
        qh = qq[:, 128 * h:128 * h + 128]
        if rope:
            qh = qh * c_ref[...] + qq[:, 512 + 128 * h:512 + 128 * h + 128] * s_ref[...]
        q.append(qh.astype(BF16))
    o = _attend(q, kparts, vparts, None, None, _head_masks(256, 64, MLA_HEADS, F32))
    o_ref[0] = (o * _silu(z_ref[0].astype(F32))).astype(BF16)


def _mla_attn(lidx, cols_mla, g_q3, w_qb, c4, s4, kv_parts, rope):
    B, T, _ = cols_mla.shape
    nt = T // TOK_TILE
    in_specs = [
        pl.BlockSpec((1, TOK_TILE, 256), lambda b, t, l: (b, t, 0)),
        pl.BlockSpec((1, TOK_TILE, 256), lambda b, t, l: (b, t, 1)),
        pl.BlockSpec((1, 1, 256), lambda b, t, l: (l[0], 0, 0)),
        pl.BlockSpec((1, 256, 1024), lambda b, t, l: (l[0], 0, 0)),
        pl.BlockSpec((TOK_TILE, 128), lambda b, t, l: (t if rope else 0, 0)),
        pl.BlockSpec((TOK_TILE, 128), lambda b, t, l: (t if rope else 0, 0)),
    ]
    args = [lidx, cols_mla, cols_mla, g_q3, w_qb, c4, s4]
    for kk, vv in kv_parts:
        in_specs.append(pl.BlockSpec((1, kk.shape[1], 512), lambda b, t, l: (b, 0, 0)))
        in_specs.append(pl.BlockSpec((1, vv.shape[1], 256), lambda b, t, l: (b, 0, 0)))
        args += [kk, vv]
    grid_spec = pltpu.PrefetchScalarGridSpec(
        num_scalar_prefetch=1,
        grid=(B, nt),
        in_specs=in_specs,
        out_specs=pl.BlockSpec((1, TOK_TILE, 256), lambda b, t, l: (b, t, 0)),
    )
    return pl.pallas_call(
        functools.partial(_mla_attn_kernel, nparts=len(kv_parts), rope=rope),
        grid_spec=grid_spec,
        out_shape=jax.ShapeDtypeStruct((B, T, 256), BF16),
        compiler_params=_cparams(("arbitrary", "arbitrary")),
        name="mla_attn_lat" if rope else "mla_attn_ctx",
    )(*args)


def _hg_blocks(rev, include_diag):
    C, S = HG_CHUNK, HG_SUB
    out = []
    for g0 in range(0, C, S):
        if rev:
            lo, hi = (g0 if include_diag else g0 + S), C
        else:
            lo, hi = 0, (g0 + S if include_diag else g0)
        if hi > lo:
            out.append((g0, lo, hi))
    return out


def _hg_mask(rev, include_diag):
    blocks = _hg_blocks(rev, include_diag)
    S = HG_SUB
    nrow = len(blocks) * HG_HEADS * S
    ncol = 128 * -(-sum(hi - lo for _, lo, hi in blocks) // 128)
    r = lax.broadcasted_iota(jnp.int32, (nrow, ncol), 0)
    cidx = lax.broadcasted_iota(jnp.int32, (nrow, ncol), 1)
    rblk = r // (HG_HEADS * S)
    cblk = jnp.full_like(cidx, len(blocks))
    s = jnp.zeros_like(cidx)
    t = r % S
    start = 0
    for k_, (g0, lo, hi) in enumerate(blocks):
        inb = (cidx >= start) & (cidx < start + hi - lo)
        cblk = jnp.where(inb, k_, cblk)
        s = jnp.where(inb, cidx - start + lo, s)
        t = jnp.where(rblk == k_, t + g0, t)
        start += hi - lo
    ok = rblk == cblk
    if include_diag:
        ok = ok & ((s >= t) if rev else (s <= t))
    return ok.astype(F32)


def _hg_stacked(qs, kk, v, b2, rev, include_diag, mask, hm):
    C, S = HG_CHUNK, HG_SUB
    blocks = _hg_blocks(rev, include_diag)
    qx, kd_rows, v_rows = [], [], []
    for g0, lo, hi in blocks:
        has_ref = (g0 + S < C) if rev else (g0 > 0)
        eq = b2[g0:g0 + S, :]
        es = -b2[lo:hi, :]
        if has_ref:
            ct = b2[g0 + S:g0 + S + 1, :] if rev else b2[g0 - 1:g0, :]
            eq = eq - ct
            es = es + ct
        qd = qs[g0:g0 + S, :] * jnp.exp2(eq)
        qx += [qd * m for m in hm]
        kd_rows.append(kk[lo:hi, :] * jnp.exp2(es))
        v_rows.append(v[lo:hi, :])
    pad = mask.shape[1] - sum(hi - lo for _, lo, hi in blocks)
    if pad:
        kd_rows.append(jnp.zeros((pad, 256), F32))
        v_rows.append(jnp.zeros((pad, 256), F32))
    kd_all = jnp.concatenate(kd_rows, axis=0).astype(BF16)
    v_all = jnp.concatenate(v_rows, axis=0).astype(BF16)
    att = _dot_nt(jnp.concatenate(qx, axis=0).astype(BF16), kd_all)
    att = jnp.where(mask > 0.5, att, 0.0)
    return _dot(att.astype(BF16), v_all), blocks


def _hg_gates(lb, rev, q_ref, f_ref, qs_sc, lf_sc, kk_sc):
    qs_sc[...] = _silu(q_ref[0])
    fz = f_ref[0]
    f = lb + (1.0 - lb) * _sigmoid(fz)
    lf = jnp.log(jnp.maximum(f, F_FLOOR))
    kk_sc[...] = (1.0 - lb) * _sigmoid(-fz)
    rr = lax.broadcasted_iota(jnp.int32, (TOK_TILE, TOK_TILE), 0)
    cc = lax.broadcasted_iota(jnp.int32, (TOK_TILE, TOK_TILE), 1)
    tri = ((rr // HG_CHUNK == cc // HG_CHUNK) & ((cc >= rr) if rev else (cc <= rr))).astype(BF16)
    hi = lf.astype(BF16)
    r1 = lf - hi.astype(F32)
    mid = r1.astype(BF16)
    lo = (r1 - mid.astype(F32)).astype(BF16)
    lf_sc[...] = _dot(tri, hi) + _dot(tri, mid) + _dot(tri, lo)
    nblk = TOK_TILE // HG_SUB
    r = lax.broadcasted_iota(jnp.int32, (nblk, TOK_TILE), 0)
    cidx = lax.broadcasted_iota(jnp.int32, (nblk, TOK_TILE), 1)
    span = _dot((cidx // HG_SUB == r).astype(BF16), (lf * (-LOG2E)).astype(BF16))
    return jnp.max(span) <= HG_FAST_SPAN


def _hg_chunk(r0, rev, fast, st, qs_sc, lf_sc, kk_sc, i_ref, a_sc, v_sc, o_sc, seg, bdmask, mask):
    C = HG_CHUNK
    S = HG_SUB
    qs = qs_sc[pl.ds(r0, C), :]
    b = lf_sc[pl.ds(r0, C), :]
    kk = kk_sc[pl.ds(r0, C), :]
    v = i_ref[0, pl.ds(r0, C), :]
    b2 = b * LOG2E
    total = b[0:1, :] if rev else b[C - 1:C, :]
    inter = _dot_nt((qs * jnp.exp2(b2)).astype(BF16), st.astype(BF16))
    hm = _head_masks(256, HG_DK, HG_HEADS, F32)
    off, blocks = _hg_stacked(qs, kk, v, b2, rev, fast, mask, hm)

    def head_sum(k_, i):
        acc = None
        for h in range(HG_HEADS):
            r_ = k_ * HG_HEADS * S + h * S + 8 * i
            t_ = off[r_:r_ + 8, :] * hm[h]
            acc = t_ if acc is None else acc + t_
        return acc

    if fast:
        tiles = []
        for k_, (g0, _, _) in enumerate(blocks):
            for i in range(S // 8):
                tiles.append(inter[g0 + 8 * i:g0 + 8 * i + 8, :] + head_sum(k_, i))
        o = jnp.concatenate(tiles, axis=0)
    else:
        a_sc[...] = jnp.log2(kk) - b2
        v_sc[...] = v
        o_sc[...] = inter
        for k_, (g0, _, _) in enumerate(blocks):
            for i in range(S // 8):
                o_sc[g0 + 8 * i:g0 + 8 * i + 8, :] = o_sc[g0 + 8 * i:g0 + 8 * i + 8, :] + head_sum(k_, i)
        sub = lax.broadcasted_iota(jnp.int32, (8, 256), 0)
        for g0 in range(0, C, S):
            parts, spans = [], []
            for s in range(g0, g0 + S):
                t0 = (s // 8) * 8
                arow = a_sc[s:s + 1, :]
                keep = (sub <= s - t0) if rev else (sub >= s - t0)
                rdiag = qs[t0:t0 + 8, :] * jnp.exp2(jnp.where(keep, b2[t0:t0 + 8, :] + arow, NEG_INF))
                rest_lo, rest_hi = (g0, t0) if rev else (t0 + 8, g0 + S)
                if rest_hi > rest_lo:
                    rrest = qs[rest_lo:rest_hi, :] * jnp.exp2(b2[rest_lo:rest_hi, :] + arow)
                    parts += [rrest, rdiag] if rev else [rdiag, rrest]
                else:
                    parts.append(rdiag)
                spans.append((min(rest_lo, t0), max(rest_hi, t0 + 8)) if rest_hi > rest_lo else (t0, t0 + 8))
            res = _dot(jnp.concatenate(parts, axis=0).astype(BF16), seg)
            acc = [o_sc[g0 + 8 * i:g0 + 8 * i + 8, :] for i in range(S // 8)]
            pos = 0
            for s, (lo_r, hi_r) in zip(range(g0, g0 + S), spans):
                vrow = v_sc[s:s + 1, :]
                for i in range((hi_r - lo_r) // 8):
                    ti = (lo_r - g0) // 8 + i
                    acc[ti] = acc[ti] + res[pos + 8 * i:pos + 8 * i + 8, :] * vrow
                pos += hi_r - lo_r
            for i in range(S // 8):
                o_sc[g0 + 8 * i:g0 + 8 * i + 8, :] = acc[i]
        o = o_sc[...]
    kd = (kk * jnp.exp(total - b)).astype(BF16)
    return o, jnp.exp(total) * st + bdmask * _dot_tn(v.astype(BF16), kd)


def _hg_kernel(l_ref, q_ref, f_ref, i_ref, og_ref, z_ref, lbl_ref, g_ref, s0_ref, y_ref, sto_ref,
               st_ref, oacc_ref, qs_sc, lf_sc, kk_sc, a_sc, v_sc, o_sc, *, nb):
    j = pl.program_id(1)
    l = l_ref[0]
    C = HG_CHUNK
    ncb = TOK_TILE // C

    @pl.when((j == 0) | (j == nb))
    def _():
        st_ref[...] = s0_ref[0, 0]

    lg = lbl_ref[...]
    mx = lg[0]
    for m in range(1, DEPTH):
        mx = jnp.maximum(mx, lg[m])
    ex = [jnp.exp(lg[m] - mx) for m in range(DEPTH)]
    tot = ex[0]
    for m in range(1, DEPTH):
        tot = tot + ex[m]
    lbs = jnp.zeros_like(tot)
    for m in range(1, DEPTH):
        lbs = lbs + jnp.where(m <= l, ex[m] / tot, 0.0)

    seg = _seg_matrix(256, HG_DK)
    seg_b = seg.astype(BF16)
    bdmask = seg.astype(F32)

    def run(rev, emit):
        lb = lbs[1:2, :] if rev else lbs[0:1, :]
        small = _hg_gates(lb, rev, q_ref, f_ref, qs_sc, lf_sc, kk_sc)
        common = (qs_sc, lf_sc, kk_sc, i_ref, a_sc, v_sc, o_sc, seg_b, bdmask)

        @pl.when(small)
        def _():
            mask = _hg_mask(rev, True)
            st = st_ref[...]
            for ci in range(ncb):
                r0 = (ncb - 1 - ci if rev else ci) * C
                o, st = _hg_chunk(r0, rev, True, st, *common, mask)
                emit(r0, o)
            st_ref[...] = st

        @pl.when(jnp.logical_not(small))
        def _():
            mask = _hg_mask(rev, False)

            def body(ci, carry):
                r0 = pl.multiple_of((ncb - 1 - ci if rev else ci) * C, C)
                o, st = _hg_chunk(r0, rev, False, st_ref[...], *common, mask)
                st_ref[...] = st
                emit(r0, o)
                return carry

            lax.fori_loop(0, ncb, body, 0)

    @pl.when(j < nb)
    def _():
        def emit(r0, o):
            oacc_ref[pl.ds(pl.multiple_of(j * TOK_TILE + r0, C), C), :] = o

        run(False, emit)

    @pl.when(j >= nb)
    def _():
        blk = 2 * nb - 1 - j

        def emit(r0, o):
            o = oacc_ref[pl.ds(pl.multiple_of(blk * TOK_TILE + r0, C), C), :] + o
            ms = _seg_mean(o * o, seg_b, HG_DK)
            on = o * lax.rsqrt(ms + EPS) * g_ref[0]
            on = on * _sigmoid(og_ref[0, pl.ds(r0, C), :])
            y_ref[0, pl.ds(r0, C), :] = (on * _silu(z_ref[0, pl.ds(r0, C), :])).astype(BF16)

        run(True, emit)

    sto_ref[0, 0] = st_ref[...]


def _hgrn(lidx, cols_hg, lb_logits, g3, s0, s0_per_batch):
    B, T, _ = cols_hg.shape
    nb = T // TOK_TILE

    def blk(j):
        return jnp.where(j < nb, j, 2 * nb - 1 - j)

    def dirn(j):
        return jnp.where(j < nb, 0, 1)

    def col(jcol):
        return pl.BlockSpec((1, TOK_TILE, 256), lambda b, j, l, jcol=jcol: (b, blk(j), jcol))

    grid_spec = pltpu.PrefetchScalarGridSpec(
        num_scalar_prefetch=1,
        grid=(B, 2 * nb),
        in_specs=[
            col(0),
            pl.BlockSpec((1, TOK_TILE, 256), lambda b, j, l: (b, blk(j), 1 + dirn(j))),
            col(3), col(4), col(5),
            pl.BlockSpec((DEPTH, 2, 256), lambda b, j, l: (0, 0, 0)),
            pl.BlockSpec((1, 1, 256), lambda b, j, l: (l[0], 0, 0)),
            pl.BlockSpec((1, 1, 256, 256),
                         (lambda b, j, l: (b, dirn(j), 0, 0)) if s0_per_batch else (lambda b, j, l: (0, 0, 0, 0))),
        ],
        out_specs=[
            pl.BlockSpec((1, TOK_TILE, 256), lambda b, j, l: (b, jnp.where(j < nb, nb - 1, 2 * nb - 1 - j), 0)),
            pl.BlockSpec((1, 1, 256, 256), lambda b, j, l: (b, dirn(j), 0, 0)),
        ],
        scratch_shapes=[
            pltpu.VMEM((256, 256), F32),
            pltpu.VMEM((T, 256), F32),
            pltpu.VMEM((TOK_TILE, 256), F32),
            pltpu.VMEM((TOK_TILE, 256), F32),
            pltpu.VMEM((TOK_TILE, 256), F32),
            pltpu.VMEM((HG_CHUNK, 256), F32),
            pltpu.VMEM((HG_CHUNK, 256), F32),
            pltpu.VMEM((HG_CHUNK, 256), F32),
        ],
    )
    return pl.pallas_call(
        functools.partial(_hg_kernel, nb=nb),
        grid_spec=grid_spec,
        out_shape=[
            jax.ShapeDtypeStruct((B, T, 256), BF16),
            jax.ShapeDtypeStruct((B, 2, 256, 256), F32),
        ],
        compiler_params=_cparams(("arbitrary", "arbitrary")),
        name="hgrn",
    )(lidx, cols_hg, cols_hg, cols_hg, cols_hg, cols_hg, lb_logits, g3, s0)


def _rot_cols(w, nheads):
    shp = w.shape
    r = shp[-1] // nheads
    w4 = w.reshape(shp[:-1] + (nheads, 2, r // 2))
    return jnp.concatenate([-w4[..., 1, :], w4[..., 0, :]], axis=-1).reshape(shp)


def _perm_heads(w, nheads, order):
    shp = w.shape
    w3 = w.reshape(shp[:-1] + (nheads, shp[-1] // nheads))
    return w3[..., list(order), :].reshape(shp)


_WG_ORDER = (0, 2, 1, 3)


def _mla_rope_slot(a):
    pad = [(0, 0)] * (a.ndim - 1) + [(MLA_NOPE, 128 - MLA_NOPE - MLA_ROPE)]
    return jnp.pad(a, pad)


def _prep_w_in(w_in):
    def c(name):
        lo, hi = _IN_OFF[name]
        return w_in[:, :, lo:hi]

    kva = c('mla_kva')
    lo = _IN_OFF['na_k'][0]
    hi = _IN_OFF['hg_z'][1]
    parts = [
        c('na_q') * (HEAD_DIM ** -0.5 * LOG2E),
        w_in[:, :, lo:hi],
        c('mla_qa'), c('mla_z'), kva[:, :, :MLA_KV_RANK], _mla_rope_slot(kva[:, :, MLA_KV_RANK:]),
        _perm_heads(c('wg_q'), WG_HEADS, _WG_ORDER) * (HEAD_DIM ** -0.5 * LOG2E),
        _perm_heads(c('wg_z'), WG_HEADS, _WG_ORDER),
        c('wg_k'), c('wg_v'),
    ]
    return jnp.concatenate([p.astype(BF16) for p in parts], axis=-1), c('merge').astype(BF16)


def _rot_matrix(width, head):
    i = np.arange(width)[:, None]
    j = np.arange(width)[None, :]
    half = head // 2
    first = (j % head) < half
    m = np.where(first & (i == j + half), -1.0, 0.0) + np.where(~first & (i == j - half), 1.0, 0.0)
    return jnp.asarray(m, BF16)


def _rope_tables(n_tok, rot_dim, reps):
    t = np.arange(n_tok)
    row = (t // GRID_W).astype(np.float32)
    col = (t % GRID_W).astype(np.float32)
    nf = rot_dim // 4
    inv = jnp.asarray(ROPE_BASE, F32) ** (-jnp.arange(nf, dtype=F32) / nf)
    ang = jnp.concatenate([jnp.asarray(row)[:, None] * inv, jnp.asarray(col)[:, None] * inv], axis=-1)
    cos, sin = jnp.cos(ang), jnp.sin(ang)
    return (jnp.tile(jnp.concatenate([cos, cos], axis=-1), (1, reps)),
            jnp.tile(jnp.concatenate([sin, sin], axis=-1), (1, reps)))


def _state_to_blockdiag(st):
    B = st.shape[0]
    stt = jnp.swapaxes(st, -1, -2)
    eye = jnp.eye(HG_HEADS, dtype=st.dtype)
    bd = stt[:, :, :, :, None, :] * eye[None, None, :, None, :, None]
    return bd.reshape(B, 2, HG_HEADS * HG_DK, HG_HEADS * HG_DK)


def _blockdiag_to_state(bd):
    B = bd.shape[0]
    b6 = bd.reshape(B, 2, HG_HEADS, HG_DK, HG_HEADS, HG_DK)
    diag = jnp.stack([b6[:, :, h, :, h, :] for h in range(HG_HEADS)], axis=2)
    return jnp.swapaxes(diag, -1, -2)


def kernel(x_prompt, x_sample, cache_na_k, cache_na_v, state_hgrn, cache_mla_ckv, cache_mla_krope, cache_wg_k, cache_wg_v, c, c_ctx, norm_g, w_mod, b_mod, w_in, na_rpb, hg_lb_logits, hg_norm_g, mla_q_norm_g, mla_kv_norm_g, mla_w_qb, mla_w_kvb, wg_sink, w_branch, w_out, final_g):
    Bp, Tp, _ = x_prompt.shape
    Bd, Td, _ = x_sample.shape
    L = cache_na_k.shape[2]
    rows = Td // GRID_W
    assert Tp % TOK_TILE == 0 and Td % TOK_TILE == 0 and L % TOK_TILE == 0
    assert rows % NA_QROWS == 0 and rows >= NA_KROWS and Td >= WG_KWIN

    w_proj, w_merge = _prep_w_in(w_in)
    wb = w_branch.astype(BF16)
    wb = wb.at[:, 3].set(_perm_heads(jnp.swapaxes(wb[:, 3], -1, -2), WG_HEADS, _WG_ORDER).swapaxes(-1, -2))
    wo = w_out.astype(BF16)
    qb = mla_w_qb.reshape(DEPTH, MLA_Q_RANK, MLA_HEADS, MLA_NOPE + MLA_ROPE)
    qb_g1 = jnp.pad(qb, ((0, 0), (0, 0), (0, 0), (0, 128 - MLA_NOPE - MLA_ROPE)))
    qb_g2 = _mla_rope_slot(_rot_cols(qb[..., MLA_NOPE:], 1))
    w_qb = jnp.concatenate([qb_g1.reshape(DEPTH, MLA_Q_RANK, -1), qb_g2.reshape(DEPTH, MLA_Q_RANK, -1)],
                           axis=-1)
    w_qb = (w_qb * ((MLA_NOPE + MLA_ROPE) ** -0.5 * LOG2E)).astype(BF16)
    kvb = mla_w_kvb.reshape(DEPTH, MLA_KV_RANK, MLA_HEADS, 2 * MLA_NOPE)
    kvb_k = jnp.pad(kvb[..., :MLA_NOPE], ((0, 0), (0, 0), (0, 0), (0, 128 - MLA_NOPE)))
    w_kvb = jnp.concatenate([kvb_k.reshape(DEPTH, MLA_KV_RANK, -1),
                             kvb[..., MLA_NOPE:].reshape(DEPTH, MLA_KV_RANK, -1)], axis=-1).astype(BF16)
    norm_g3 = norm_g.reshape(DEPTH, 1, D_MODEL)
    g_q3 = mla_q_norm_g.reshape(DEPTH, 1, MLA_Q_RANK)
    g_kv3 = mla_kv_norm_g.reshape(DEPTH, 1, MLA_KV_RANK)
    hg_g3 = jnp.tile(hg_norm_g, (1, HG_HEADS)).reshape(DEPTH, 1, HG_HEADS * HG_DK)
    final_g2 = final_g.reshape(1, D_MODEL)
    cq, sq = _rope_tables(Td, HEAD_DIM, WG_HEADS)
    ck, sk = cq[:, :128], sq[:, :128]
    c1, s1 = _rope_tables(Td, MLA_ROPE, 1)
    c4 = jnp.concatenate([jnp.ones((Td, MLA_NOPE), F32), c1, jnp.zeros((Td, 32), F32)], axis=-1)
    s4 = _mla_rope_slot(s1)
    rm_wg = _rot_matrix(WG_HEADS * HEAD_DIM, HEAD_DIM)
    rm_mla = jnp.zeros((128, 128), BF16).at[MLA_NOPE:MLA_NOPE + MLA_ROPE, MLA_NOPE:MLA_NOPE + MLA_ROPE].set(
        _rot_matrix(MLA_ROPE, MLA_ROPE))

    nrow = 8 * ((1 + Bd + 7) // 8)
    cond = jnp.zeros((nrow, D_MODEL), F32).at[0].set(c_ctx).at[1:1 + Bd].set(c)
    mod4 = _modulation(cond, w_mod, b_mod).reshape(DEPTH, nrow, 1, 3 * D_MODEL)

    kstart, ktype, table = _na_row_types(rows)
    na_bias = _na_bias(na_rpb, table)
    kstart = jnp.asarray(kstart, jnp.int32)
    ktype = jnp.asarray(ktype, jnp.int32)

    cna_k = cache_na_k.reshape(Bd, DEPTH, L, NA_HEADS * HEAD_DIM)
    cna_v = cache_na_v.reshape(Bd, DEPTH, L, NA_HEADS * HEAD_DIM)
    cwg_k = cache_wg_k.reshape(Bd, DEPTH, L, WG_KV_HEADS * HEAD_DIM)
    cwg_v = cache_wg_v.reshape(Bd, DEPTH, L, WG_KV_HEADS * HEAD_DIM)
    ckr4 = _mla_rope_slot(cache_mla_krope)
    st_bd = _state_to_blockdiag(state_hgrn.reshape(Bd * DEPTH, 2, HG_HEADS, HG_DK, HG_DK)).reshape(
        Bd, DEPTH, 2, 256, 256)
    st_zero = jnp.zeros((1, 2, 256, 256), F32)

    hp, hs = x_prompt, x_sample
    states = []
    tok_caches = None
    for layer in range(DEPTH):
        lidx = jnp.full((1,), layer, jnp.int32)
        final = layer == DEPTH - 1

        flat = (1, Bp * Tp)
        res = _in_proj(lidx, hp.reshape(flat + (D_MODEL,)), mod4, 0, False, norm_g3, w_proj, g_kv3, w_kvb, rm_mla,
                       c4, s4, False, caches=tok_caches, cache_shape=(Bp, DEPTH, Tp))
        na, hg, mla, wg, kc_, vc_ = (a.reshape(Bp, Tp, a.shape[-1]) for a in res[:6])
        tok_caches = res[6:]
        y_na = _na_ctx(na)
        y_hg, st_out = _hgrn(lidx, hg, hg_lb_logits, hg_g3, st_zero, False)
        y_mla = _mla_attn(lidx, mla, g_q3, w_qb, c4, s4, [(kc_, vc_)], False)
        y_wg = _wg_ctx(lidx, wg_sink, wg)
        ys = tuple(y.reshape(flat + (BRANCH_W,)) for y in (y_na, y_hg, y_mla, y_wg))
        hp = _post(lidx, hp.reshape(flat + (D_MODEL,)), mod4, 0, False, norm_g3, ys, w_merge, wb, wo, final_g2,
                   final).reshape(Bp, Tp, D_MODEL)
        states.append(st_out)

        na, hg, mla, wg, kl, vl = _in_proj(lidx, hs, mod4, 1, True, norm_g3, w_proj, g_kv3, w_kvb, rm_mla, c4, s4,
                                           True)
        y_na = _na_lat(lidx, kstart, ktype, na, cna_k, cna_v, na_bias)
        y_hg, _ = _hgrn(lidx, hg, hg_lb_logits, hg_g3, st_bd[:, layer], True)
        kx, vx = _mla_kv(lidx, cache_mla_ckv, 0, ckr4, 0, ckr4, 0, g_kv3, w_kvb, c4, s4, False, False, True)
        y_mla = _mla_attn(lidx, mla, g_q3, w_qb, c4, s4, [(kl, vl), (kx, vx)], True)
        y_wg = _wg_lat(lidx, wg_sink, wg, cwg_k, cwg_v, cq, sq, ck, sk, rm_wg)
        hs = _post(lidx, hs, mod4, 1, True, norm_g3, (y_na, y_hg, y_mla, y_wg), w_merge, wb, wo, final_g2, final)

    new_state = _blockdiag_to_state(jnp.stack(states, axis=1).reshape(Bp * DEPTH, 2, 256, 256)).reshape(
        Bp, DEPTH, 2, HG_HEADS, HG_DK, HG_DK)
    na_k, na_v, ckv_c, kr_c, wg_k, wg_v = tok_caches
    return (hp, hs,
            na_k.reshape(Bp, DEPTH, Tp, NA_HEADS, HEAD_DIM), na_v.reshape(Bp, DEPTH, Tp, NA_HEADS, HEAD_DIM),
            new_state, ckv_c, kr_c,
            wg_k.reshape(Bp, DEPTH, Tp, WG_KV_HEADS, HEAD_DIM), wg_v.reshape(Bp, DEPTH, Tp, WG_KV_HEADS, HEAD_DIM))
```

```python
import functools

import jax
import jax.numpy as jnp
import numpy as np
from jax import lax
from jax.experimental import pallas as pl
from jax.experimental.pallas import tpu as pltpu

F32 = jnp.float32
BF16 = jnp.bfloat16

D_MODEL = 1024
DEPTH = 4
GRID_W = 64
HEAD_DIM = 64
BRANCH_W = 256
N_BRANCH = 4
NA_HEADS = 4
NA_WIN_H = 8
NA_WIN_W = 16
HG_HEADS = 4
HG_DK = 64
HG_CHUNK = 64
HG_SUB = 32
HG_FAST_SPAN = 100.0
LOG2E = 1.4426950408889634
F_FLOOR = 1e-30
MLA_HEADS = 4
MLA_Q_RANK = 256
MLA_KV_RANK = 128
MLA_NOPE = 64
MLA_ROPE = 32
WG_HEADS = 4
WG_KV_HEADS = 2
WG_WINDOW = 128
ROPE_BASE = 10000.0
EPS = 1e-6
NEG_INF = -1e30

_IN_LAYOUT = (
    ('na_q', 256), ('na_k', 256), ('na_v', 256), ('na_z', 256),
    ('hg_q', 256), ('hg_ff', 256), ('hg_fb', 256), ('hg_i', 256), ('hg_og', 256), ('hg_z', 256),
    ('mla_qa', 256), ('mla_kva', 160), ('mla_z', 256),
    ('wg_q', 256), ('wg_k', 128), ('wg_v', 128), ('wg_z', 256),
    ('merge', 4096),
)
_IN_OFF = {}
_o = 0
for _n, _w in _IN_LAYOUT:
    _IN_OFF[_n] = (_o, _o + _w)
    _o += _w

NA_W = 1024
HG_W = 1536
MLA_W = 768
WG_W = 768
PROJ_W = NA_W + HG_W + MLA_W + WG_W

TOK_TILE = 256
PROJ_TILE = 512
NA_QROWS = 4
NA_KROWS = 12
WG_TILE = 256
WG_KWIN = WG_TILE + 2 * WG_WINDOW
VMEM_LIMIT = 56 * 1024 * 1024


def _cparams(sem):
    return pltpu.CompilerParams(dimension_semantics=sem, vmem_limit_bytes=VMEM_LIMIT)


def _dot(a, b):
    return jnp.dot(a, b, preferred_element_type=F32)


def _dot_nt(a, b):
    return lax.dot_general(a, b, (((1,), (1,)), ((), ())), preferred_element_type=F32)


def _dot_tn(a, b):
    return lax.dot_general(a, b, (((0,), (0,)), ((), ())), preferred_element_type=F32)


def _sigmoid(x):
    return 1.0 / (1.0 + jnp.exp(-x))


def _silu(x):
    return x * _sigmoid(x)


def _lane_mask(width, lo, hi, dtype):
    lane = lax.broadcasted_iota(jnp.int32, (1, width), 1)
    return ((lane >= lo) & (lane < hi)).astype(dtype)


def _seg_matrix(width, seg):
    r = lax.broadcasted_iota(jnp.int32, (width, width), 0) // seg
    c = lax.broadcasted_iota(jnp.int32, (width, width), 1) // seg
    return r == c


def _seg_mean(x2, seg_bf16, seg):
    hi = x2.astype(BF16)
    lo = (x2 - hi.astype(F32)).astype(BF16)
    return (_dot(hi, seg_bf16) + _dot(lo, seg_bf16)) * (1.0 / seg)


def _mod_kernel(c_ref, w_ref, b_ref, o_ref):
    s = _silu(c_ref[...])
    o_ref[0] = _dot(s.astype(BF16), w_ref[0].astype(BF16)) + b_ref[0]


def _modulation(cond, w_mod, b_mod):
    nrow = cond.shape[0]
    tn = 1024
    return pl.pallas_call(
        _mod_kernel,
        grid=(DEPTH, 3 * D_MODEL // tn),
        in_specs=[
            pl.BlockSpec((nrow, D_MODEL), lambda l, j: (0, 0)),
            pl.BlockSpec((1, D_MODEL, tn), lambda l, j: (l, 0, j)),
            pl.BlockSpec((1, 1, tn), lambda l, j: (l, 0, j)),
        ],
        out_specs=pl.BlockSpec((1, nrow, tn), lambda l, j: (l, 0, j)),
        out_shape=jax.ShapeDtypeStruct((DEPTH, nrow, 3 * D_MODEL), F32),
        compiler_params=_cparams(("arbitrary", "arbitrary")),
        name="modulation",
    )(cond, w_mod, b_mod.reshape(DEPTH, 1, 3 * D_MODEL))


def _mod_norm(x, mod_ref, g_ref):
    ms = jnp.mean(x * x, axis=-1, keepdims=True)
    y = x * lax.rsqrt(ms + EPS) * g_ref[0]
    shift = mod_ref[0, 0, :, 0:D_MODEL]
    scale = mod_ref[0, 0, :, D_MODEL:2 * D_MODEL]
    gate = mod_ref[0, 0, :, 2 * D_MODEL:3 * D_MODEL]
    return y * (1.0 + scale) + shift, gate


_CACHE_WIDTHS = (NA_HEADS * HEAD_DIM, NA_HEADS * HEAD_DIM, MLA_KV_RANK, MLA_ROPE,
                 WG_KV_HEADS * HEAD_DIM, WG_KV_HEADS * HEAD_DIM)


def _in_kernel(l_ref, x_ref, mod_ref, g_ref, w_ref, gkv_ref, wkvb_ref, rm_ref, c_ref, s_ref, *rest,
               rope, n_alias, seq):
    rest = rest[n_alias:]
    na_ref, hg_ref, mla_ref, wg_ref, kall_ref, vall_ref = rest[:6]
    h, _ = _mod_norm(x_ref[0], mod_ref, g_ref)
    hb = h.astype(BF16)
    na = _dot(hb, w_ref[0, :, 0:NA_W])
    na_ref[0] = na.astype(na_ref.dtype)
    hg_ref[0] = _dot(hb, w_ref[0, :, NA_W:NA_W + HG_W])
    wg = _dot(hb, w_ref[0, :, NA_W + HG_W + MLA_W:PROJ_W])
    wg_ref[0] = wg.astype(wg_ref.dtype)
    mla = _dot(hb, w_ref[0, :, NA_W + HG_W:NA_W + HG_W + MLA_W])
    mla_ref[0] = mla[:, 0:512].astype(mla_ref.dtype)
    ckv = mla[:, 512:640]
    ms = jnp.mean(ckv * ckv, axis=-1, keepdims=True)
    ckv = ckv * lax.rsqrt(ms + EPS) * gkv_ref[0]
    kr = mla[:, 640:768]
    if not rope:
        pieces = (na[:, 256:512], na[:, 512:768], ckv, kr[:, MLA_NOPE:MLA_NOPE + MLA_ROPE],
                  wg[:, 512:640], wg[:, 640:768])
        for ref, val in zip(rest[6:], pieces):
            for i in range(val.shape[0] // seq):
                ref[i, 0] = val[i * seq:(i + 1) * seq, :]
    if rope:
        kr = kr * c_ref[...] + _dot(kr.astype(BF16), rm_ref[...]) * s_ref[...]
    kv = _dot(ckv.astype(BF16), wkvb_ref[0])
    for hd in range(MLA_HEADS):
        kall_ref[0, :, 128 * hd:128 * hd + 128] = (kv[:, 128 * hd:128 * hd + 128] + kr).astype(BF16)
    vall_ref[0] = kv[:, 512:768].astype(BF16)


def _in_proj(lidx, x, mod4, mod_row0, per_batch, norm_g3, w_proj, g_kv3, w_kvb, rm_mla, c4, s4, rope,
             caches=None, cache_shape=None):
    B, T, _ = x.shape
    tile = min(PROJ_TILE, T)
    nt = T // tile
    mrow = (lambda b: mod_row0 + b) if per_batch else (lambda b: mod_row0)
    col_dtype = BF16

    def tok(width):
        return pl.BlockSpec((1, tile, width), lambda b, t, l: (b, t, 0))

    out_specs = [tok(NA_W), tok(HG_W), tok(512), tok(WG_W), tok(512), tok(256)]
    out_shape = [
        jax.ShapeDtypeStruct((B, T, NA_W), col_dtype),
        jax.ShapeDtypeStruct((B, T, HG_W), F32),
        jax.ShapeDtypeStruct((B, T, 512), col_dtype),
        jax.ShapeDtypeStruct((B, T, WG_W), col_dtype),
        jax.ShapeDtypeStruct((B, T, 512), BF16),
        jax.ShapeDtypeStruct((B, T, 256), BF16),
    ]
    in_specs = [
        pl.BlockSpec((1, tile, D_MODEL), lambda b, t, l: (b, t, 0)),
        pl.BlockSpec((1, 1, 1, 3 * D_MODEL), lambda b, t, l: (l[0], mrow(b), 0, 0)),
        pl.BlockSpec((1, 1, D_MODEL), lambda b, t, l: (l[0], 0, 0)),
        pl.BlockSpec((1, D_MODEL, PROJ_W), lambda b, t, l: (l[0], 0, 0)),
        pl.BlockSpec((1, 1, 128), lambda b, t, l: (l[0], 0, 0)),
        pl.BlockSpec((1, 128, 768), lambda b, t, l: (l[0], 0, 0)),
        pl.BlockSpec((128, 128), lambda b, t, l: (0, 0)),
        pl.BlockSpec((tile, 128), lambda b, t, l: (t if rope else 0, 0)),
        pl.BlockSpec((tile, 128), lambda b, t, l: (t if rope else 0, 0)),
    ]
    args = [lidx, x, mod4, norm_g3, w_proj, g_kv3, w_kvb, rm_mla, c4, s4]
    aliases = {}
    seq = tile
    if not rope:
        nbatch, _, seq = cache_shape
        assert B == 1 and tile % seq == 0
        for width in _CACHE_WIDTHS:
            out_specs.append(pl.BlockSpec((tile // seq, 1, seq, width), lambda b, t, l: (t, l[0], 0, 0)))
            out_shape.append(jax.ShapeDtypeStruct((nbatch, DEPTH, seq, width), F32))
        if caches is not None:
            for i, arr in enumerate(caches):
                aliases[len(args)] = 6 + i
                in_specs.append(pl.BlockSpec(memory_space=pl.ANY))
                args.append(arr)
    grid_spec = pltpu.PrefetchScalarGridSpec(
        num_scalar_prefetch=1,
        grid=(B, nt),
        in_specs=in_specs,
        out_specs=out_specs,
    )
    return pl.pallas_call(
        functools.partial(_in_kernel, rope=rope, n_alias=len(aliases), seq=seq),
        grid_spec=grid_spec,
        out_shape=out_shape,
        input_output_aliases=aliases,
        compiler_params=_cparams(("arbitrary", "arbitrary")),
        name="in_proj_lat" if rope else ("in_proj_ctx" if caches is None else "in_proj_ctx_inplace"),
    )(*args)


def _post_kernel(l_ref, x_ref, mod_ref, g_ref, y0_ref, y1_ref, y2_ref, y3_ref, wm_ref, wb_ref, wo_ref,
                 fg_ref, o_ref, *, final):
    x = x_ref[0]
    h, gate = _mod_norm(x, mod_ref, g_ref)
    hb = h.astype(BF16)
    acc = jnp.zeros(x.shape, F32)
    for n, y_ref in enumerate((y0_ref, y1_ref, y2_ref, y3_ref)):
        mg = _sigmoid(_dot(hb, wm_ref[0, :, n * D_MODEL:(n + 1) * D_MODEL]))
        acc = acc + mg * _dot(y_ref[0], wb_ref[0, n])
    out = x + gate * _dot(acc.astype(BF16), wo_ref[0])
    if final:
        ms = jnp.mean(out * out, axis=-1, keepdims=True)
        out = out * lax.rsqrt(ms + EPS) * fg_ref[...]
    o_ref[0] = out


def _post(lidx, x, mod4, mod_row0, per_batch, norm_g3, ys, w_merge, w_branch, w_out, final_g2, final):
    B, T, _ = x.shape
    tile = min(PROJ_TILE, T)
    nt = T // tile
    mrow = (lambda b: mod_row0 + b) if per_batch else (lambda b: mod_row0)
    yspec = pl.BlockSpec((1, tile, BRANCH_W), lambda b, t, l: (b, t, 0))
    grid_spec = pltpu.PrefetchScalarGridSpec(
        num_scalar_prefetch=1,
        grid=(B, nt),
        in_specs=[
            pl.BlockSpec((1, tile, D_MODEL), lambda b, t, l: (b, t, 0)),
            pl.BlockSpec((1, 1, 1, 3 * D_MODEL), lambda b, t, l: (l[0], mrow(b), 0, 0)),
            pl.BlockSpec((1, 1, D_MODEL), lambda b, t, l: (l[0], 0, 0)),
            yspec, yspec, yspec, yspec,
            pl.BlockSpec((1, D_MODEL, N_BRANCH * D_MODEL), lambda b, t, l: (l[0], 0, 0)),
            pl.BlockSpec((1, N_BRANCH, BRANCH_W, D_MODEL), lambda b, t, l: (l[0], 0, 0, 0)),
            pl.BlockSpec((1, D_MODEL, D_MODEL), lambda b, t, l: (l[0], 0, 0)),
            pl.BlockSpec((1, D_MODEL), lambda b, t, l: (0, 0)),
        ],
        out_specs=pl.BlockSpec((1, tile, D_MODEL), lambda b, t, l: (b, t, 0)),
    )
    return pl.pallas_call(
        functools.partial(_post_kernel, final=final),
        grid_spec=grid_spec,
        out_shape=jax.ShapeDtypeStruct((B, T, D_MODEL), F32),
        compiler_params=_cparams(("arbitrary", "arbitrary")),
        name="post_final" if final else "post",
    )(lidx, x, mod4, norm_g3, *ys, w_merge, w_branch, w_out, final_g2)


def _attend(q, kparts, vparts, biases, qmasks, omasks, sinks=None):
    out = None
    for h in range(len(omasks)):
        qh = q[h] if qmasks is None else q * qmasks[h]
        ss = []
        for i, kp in enumerate(kparts):
            s = _dot_nt(qh, kp[h] if isinstance(kp, (list, tuple)) else kp)
            if biases is not None and biases[h][i] is not None:
                s = s + biases[h][i]
            ss.append(s)
        m = ss[0].max(axis=-1, keepdims=True)
        for s in ss[1:]:
            m = jnp.maximum(m, s.max(axis=-1, keepdims=True))
        if sinks is not None:
            m = jnp.maximum(m, sinks[h])
        den = None
        oh = None
        for s, vp in zip(ss, vparts):
            p = jnp.exp2(s - m)
            ps = p.sum(axis=-1, keepdims=True)
            den = ps if den is None else den + ps
            pv = _dot(p.astype(BF16), vp)
            oh = pv if oh is None else oh + pv
        if sinks is not None:
            den = den + jnp.exp2(sinks[h] - m)
        oh = (oh / den) * omasks[h]
        out = oh if out is None else out + oh
    return out


def _head_masks(width, seg, nheads, dtype):
    return [_lane_mask(width, h * seg, (h + 1) * seg, dtype) for h in range(nheads)]


def _na_ctx_kernel(q_ref, k_ref, v_ref, z_ref, o_ref):
    q = q_ref[0].astype(BF16)
    k = k_ref[0].astype(BF16)
    v = v_ref[0].astype(BF16)
    o = _attend(q, [k], [v], None, _head_masks(256, 64, NA_HEADS, BF16), _head_masks(256, 64, NA_HEADS, F32))
    o_ref[0] = (o * _silu(z_ref[0].astype(F32))).astype(BF16)


def _na_ctx(cols_na):
    B, T, _ = cols_na.shape

    def spec(j):
        return pl.BlockSpec((1, T, 256), lambda b, j=j: (b, 0, j))

    return pl.pallas_call(
        _na_ctx_kernel,
        grid=(B,),
        in_specs=[spec(0), spec(1), spec(2), spec(3)],
        out_specs=pl.BlockSpec((1, T, 256), lambda b: (b, 0, 0)),
        out_shape=jax.ShapeDtypeStruct((B, T, 256), BF16),
        compiler_params=_cparams(("arbitrary",)),
        name="na_ctx",
    )(cols_na, cols_na, cols_na, cols_na)


def _na_row_types(rows):
    wh = min(NA_WIN_H, rows)
    nblk = rows // NA_QROWS
    starts, types, table = [], [], []
    for blk in range(nblk):
        r0 = blk * NA_QROWS
        k0 = int(np.clip(r0 - wh // 2, 0, rows - NA_KROWS))
        desc = []
        for i in range(NA_QROWS):
            r = r0 + i
            ks = int(np.clip(r - wh // 2, 0, rows - wh))
            for j in range(NA_KROWS):
                kr = k0 + j
                desc.append((kr - r + NA_WIN_H - 1) if ks <= kr < ks + wh else None)
        desc = tuple(desc)
        if desc not in table:
            table.append(desc)
        starts.append(k0)
        types.append(table.index(desc))
    return starts, types, table


def _na_bias_kernel(rpb_ref, o_ref, t2_ref, *, table):
    l = pl.program_id(0)
    h = pl.program_id(1)
    nrow = 2 * NA_WIN_H - 1
    ncol = 2 * NA_WIN_W - 1
    c = lax.broadcasted_iota(jnp.int32, (GRID_W, 128), 0)
    lane = lax.broadcasted_iota(jnp.int32, (GRID_W, 128), 1)
    first = lane < GRID_W
    kc = jnp.where(first, lane, lane - GRID_W)
    qstart = jnp.clip(c - NA_WIN_W // 2, 0, GRID_W - NA_WIN_W)
    ok = (kc >= qstart) & (kc < qstart + NA_WIN_W)
    dcol = kc - c + NA_WIN_W - 1
    for a in range(-1, nrow):
        acc = jnp.full((GRID_W, 128), NEG_INF, F32)
        for bi in range(2 * NA_WIN_W - 1):
            base = (l * NA_HEADS + h) * nrow
            lo = rpb_ref[(base + a) * ncol + bi] if a >= 0 else jnp.float32(NEG_INF)
            hi = rpb_ref[(base + a + 1) * ncol + bi] if a + 1 < nrow else jnp.float32(NEG_INF)
            acc = jnp.where(dcol == bi, jnp.where(first, lo, hi), acc)
        t2_ref[a + 1] = jnp.where(ok, acc * LOG2E, NEG_INF)
    for ti, desc in enumerate(table):
        for i in range(NA_QROWS):
            for p in range(NA_KROWS // 2):
                a0 = desc[i * NA_KROWS + 2 * p]
                a1 = desc[i * NA_KROWS + 2 * p + 1]
                if a0 is None and a1 is None:
                    tile = jnp.full((GRID_W, 128), NEG_INF, F32)
                elif a1 is None:
                    tile = jnp.where(first, t2_ref[a0 + 1], NEG_INF)
                elif a0 is None:
                    tile = jnp.where(first, NEG_INF, t2_ref[a1])
                else:
                    tile = t2_ref[a0 + 1]
                o_ref[0, ti, 0, i * GRID_W:(i + 1) * GRID_W, p * 128:(p + 1) * 128] = tile


def _na_bias(na_rpb, table):
    nt = len(table)
    nq = NA_QROWS * GRID_W
    nk = NA_KROWS * GRID_W
    return pl.pallas_call(
        functools.partial(_na_bias_kernel, table=table),
        grid=(DEPTH, NA_HEADS),
        in_specs=[pl.BlockSpec(memory_space=pltpu.SMEM)],
        out_specs=pl.BlockSpec((1, nt, 1, nq, nk), lambda l, h: (l, 0, h, 0, 0)),
        out_shape=jax.ShapeDtypeStruct((DEPTH, nt, NA_HEADS, nq, nk), F32),
        scratch_shapes=[pltpu.VMEM((2 * NA_WIN_H, GRID_W, 128), F32)],
        compiler_params=_cparams(("arbitrary", "arbitrary")),
        name="na_bias",
    )(na_rpb.reshape(-1))


def _na_lat_kernel(l_ref, ks_ref, ty_ref, q_ref, k_ref, v_ref, z_ref, kc_ref, vc_ref, b_ref, o_ref):
    rb = pl.program_id(1)
    nk = NA_KROWS * GRID_W
    ks = pl.multiple_of(ks_ref[rb] * GRID_W, GRID_W)
    q = q_ref[0].astype(BF16)
    kw = k_ref[0, pl.ds(ks, nk), :].astype(BF16)
    vw = v_ref[0, pl.ds(ks, nk), :].astype(BF16)
    kc = kc_ref[0, 0].astype(BF16)
    vc = vc_ref[0, 0].astype(BF16)
    biases = [[b_ref[0, 0, h], None] for h in range(NA_HEADS)]
    o = _attend(q, [kw, kc], [vw, vc], biases,
                _head_masks(256, 64, NA_HEADS, BF16), _head_masks(256, 64, NA_HEADS, F32))
    o_ref[0] = (o * _silu(z_ref[0].astype(F32))).astype(BF16)


def _na_lat(lidx, kstart, ktype, cols_na, cache_k, cache_v, bias):
    B, T, _ = cols_na.shape
    L = cache_k.shape[2]
    nq = NA_QROWS * GRID_W
    nk = NA_KROWS * GRID_W
    nblk = T // nq
    grid_spec = pltpu.PrefetchScalarGridSpec(
        num_scalar_prefetch=3,
        grid=(B, nblk),
        in_specs=[
            pl.BlockSpec((1, nq, 256), lambda b, r, l, ks, ty: (b, r, 0)),
            pl.BlockSpec((1, T, 256), lambda b, r, l, ks, ty: (b, 0, 1)),
            pl.BlockSpec((1, T, 256), lambda b, r, l, ks, ty: (b, 0, 2)),
            pl.BlockSpec((1, nq, 256), lambda b, r, l, ks, ty: (b, r, 3)),
            pl.BlockSpec((1, 1, L, 256), lambda b, r, l, ks, ty: (b, l[0], 0, 0)),
            pl.BlockSpec((1, 1, L, 256), lambda b, r, l, ks, ty: (b, l[0], 0, 0)),
            pl.BlockSpec((1, 1, NA_HEADS, nq, nk), lambda b, r, l, ks, ty: (l[0], ty[r], 0, 0, 0)),
        ],
        out_specs=pl.BlockSpec((1, nq, 256), lambda b, r, l, ks, ty: (b, r, 0)),
    )
    return pl.pallas_call(
        _na_lat_kernel,
        grid_spec=grid_spec,
        out_shape=jax.ShapeDtypeStruct((B, T, 256), BF16),
        compiler_params=_cparams(("arbitrary", "arbitrary")),
        name="na_lat",
    )(lidx, kstart, ktype, cols_na, cols_na, cols_na, cols_na, cache_k, cache_v, bias)


def _wg_heads(q, kparts, vparts, bias, sink_ref, l):
    qm = _head_masks(128, 64, 2, BF16)
    om = _head_masks(128, 64, 2, F32)
    outs = []
    for grp, heads in enumerate(((0, 2), (1, 3))):
        qg = q[:, grp * 128:(grp + 1) * 128]
        sinks = [sink_ref[l, hh] * LOG2E for hh in heads]
        biases = None if bias is None else [[bias] + [None] * (len(kparts) - 1)] * 2
        outs.append(_attend(qg, kparts, vparts, biases, qm, om, sinks))
    return jnp.concatenate(outs, axis=-1)


def _wg_ctx_kernel(l_ref, sink_ref, q_ref, z_ref, k_ref, v_ref, o_ref):
    q = q_ref[0].astype(BF16)
    o = _wg_heads(q, [k_ref[0].astype(BF16)], [v_ref[0].astype(BF16)], None, sink_ref, l_ref[0])
    o_ref[0] = (o * _silu(z_ref[0].astype(F32))).astype(BF16)


def _wg_ctx(lidx, sink, cols_wg):
    B, T, _ = cols_wg.shape
    grid_spec = pltpu.PrefetchScalarGridSpec(
        num_scalar_prefetch=1,
        grid=(B,),
        in_specs=[
            pl.BlockSpec(memory_space=pltpu.SMEM),
            pl.BlockSpec((1, T, 256), lambda b, l: (b, 0, 0)),
            pl.BlockSpec((1, T, 256), lambda b, l: (b, 0, 1)),
            pl.BlockSpec((1, T, 128), lambda b, l: (b, 0, 4)),
            pl.BlockSpec((1, T, 128), lambda b, l: (b, 0, 5)),
        ],
        out_specs=pl.BlockSpec((1, T, 256), lambda b, l: (b, 0, 0)),
    )
    return pl.pallas_call(
        _wg_ctx_kernel,
        grid_spec=grid_spec,
        out_shape=jax.ShapeDtypeStruct((B, T, 256), BF16),
        compiler_params=_cparams(("arbitrary",)),
        name="wg_ctx",
    )(lidx, sink, cols_wg, cols_wg, cols_wg, cols_wg)


def _wg_lat_kernel(l_ref, sink_ref, q_ref, z_ref, k_ref, v_ref, kc_ref, vc_ref,
                   cq_ref, sq_ref, ck_ref, sk_ref, rm_ref, o_ref, *, seq):
    tb = pl.program_id(1)
    t0 = tb * WG_TILE
    k0 = pl.multiple_of(jnp.clip(t0 - WG_WINDOW, 0, seq - WG_KWIN), 128)
    q = q_ref[0]
    q = q.astype(F32) * cq_ref[...] + _dot(q.astype(BF16), rm_ref[...]) * sq_ref[...]
    kwin = k_ref[0, pl.ds(k0, WG_KWIN), :]
    kwin = kwin.astype(F32) * ck_ref[pl.ds(k0, WG_KWIN), :] \
        + _dot(kwin.astype(BF16), rm_ref[0:128, 0:128]) * sk_ref[pl.ds(k0, WG_KWIN), :]
    vwin = v_ref[0, pl.ds(k0, WG_KWIN), :]
    qi = t0 + lax.broadcasted_iota(jnp.int32, (WG_TILE, WG_KWIN), 0)
    kj = k0 + lax.broadcasted_iota(jnp.int32, (WG_TILE, WG_KWIN), 1)
    bias = jnp.where(jnp.abs(qi - kj) <= WG_WINDOW, 0.0, NEG_INF).astype(F32)
    o = _wg_heads(q.astype(BF16),
                  [kwin.astype(BF16), kc_ref[0, 0].astype(BF16)],
                  [vwin.astype(BF16), vc_ref[0, 0].astype(BF16)],
                  bias, sink_ref, l_ref[0])
    o_ref[0] = (o * _silu(z_ref[0].astype(F32))).astype(BF16)


def _wg_lat(lidx, sink, cols_wg, cache_k, cache_v, cq, sq, ck, sk, rm):
    B, T, _ = cols_wg.shape
    L = cache_k.shape[2]
    nt = T // WG_TILE
    grid_spec = pltpu.PrefetchScalarGridSpec(
        num_scalar_prefetch=1,
        grid=(B, nt),
        in_specs=[
            pl.BlockSpec(memory_space=pltpu.SMEM),
            pl.BlockSpec((1, WG_TILE, 256), lambda b, t, l: (b, t, 0)),
            pl.BlockSpec((1, WG_TILE, 256), lambda b, t, l: (b, t, 1)),
            pl.BlockSpec((1, T, 128), lambda b, t, l: (b, 0, 4)),
            pl.BlockSpec((1, T, 128), lambda b, t, l: (b, 0, 5)),
            pl.BlockSpec((1, 1, L, 128), lambda b, t, l: (b, l[0], 0, 0)),
            pl.BlockSpec((1, 1, L, 128), lambda b, t, l: (b, l[0], 0, 0)),
            pl.BlockSpec((WG_TILE, 256), lambda b, t, l: (t, 0)),
            pl.BlockSpec((WG_TILE, 256), lambda b, t, l: (t, 0)),
            pl.BlockSpec((T, 128), lambda b, t, l: (0, 0)),
            pl.BlockSpec((T, 128), lambda b, t, l: (0, 0)),
            pl.BlockSpec((256, 256), lambda b, t, l: (0, 0)),
        ],
        out_specs=pl.BlockSpec((1, WG_TILE, 256), lambda b, t, l: (b, t, 0)),
    )
    return pl.pallas_call(
        functools.partial(_wg_lat_kernel, seq=T),
        grid_spec=grid_spec,
        out_shape=jax.ShapeDtypeStruct((B, T, 256), BF16),
        compiler_params=_cparams(("arbitrary", "arbitrary")),
        name="wg_lat",
    )(lidx, sink, cols_wg, cols_wg, cols_wg, cols_wg, cache_k, cache_v, cq, sq, ck, sk, rm)


def _mla_kv_kernel(l_ref, ckv_ref, kr_ref, krr_ref, g_ref, w_ref, c_ref, s_ref, k_out, v_out, *rest,
                   norm, rope):
    ckv = ckv_ref[0]
    if norm:
        ms = jnp.mean(ckv * ckv, axis=-1, keepdims=True)
        ckv = ckv * lax.rsqrt(ms + EPS) * g_ref[0]
        rest[0][0] = ckv
    kv = _dot(ckv.astype(BF16), w_ref[0])
    kr = kr_ref[0]
    if rope:
        kr = kr * c_ref[...] + krr_ref[0] * s_ref[...]
    for h in range(MLA_HEADS):
        k_out[0, :, 128 * h:128 * h + 128] = (kv[:, 128 * h:128 * h + 128] + kr).astype(BF16)
    v_out[0] = kv[:, 512:768].astype(BF16)


def _mla_kv(lidx, ckv_arr, ckv_blk, kr_arr, kr_blk, krr_arr, krr_blk, g_kv3, w_kvb, c4, s4, norm, rope,
            layered):
    B = ckv_arr.shape[0]
    T = ckv_arr.shape[-2]
    nt = T // TOK_TILE
    if layered:
        def src(blk):
            return pl.BlockSpec((None, 1, TOK_TILE, 128), lambda b, t, l: (b, l[0], t, 0))
    else:
        def src(blk):
            return pl.BlockSpec((1, TOK_TILE, 128), lambda b, t, l, blk=blk: (b, t, blk))
    out_specs = [
        pl.BlockSpec((1, TOK_TILE, 512), lambda b, t, l: (b, t, 0)),
        pl.BlockSpec((1, TOK_TILE, 256), lambda b, t, l: (b, t, 0)),
    ]
    out_shape = [
        jax.ShapeDtypeStruct((B, T, 512), BF16),
        jax.ShapeDtypeStruct((B, T, 256), BF16),
    ]
    if norm:
        out_specs.append(pl.BlockSpec((1, TOK_TILE, 128), lambda b, t, l: (b, t, 0)))
        out_shape.append(jax.ShapeDtypeStruct((B, T, 128), F32))
    grid_spec = pltpu.PrefetchScalarGridSpec(
        num_scalar_prefetch=1,
        grid=(B, nt),
        in_specs=[
            src(ckv_blk), src(kr_blk), src(krr_blk),
            pl.BlockSpec((1, 1, 128), lambda b, t, l: (l[0], 0, 0)),
            pl.BlockSpec((1, 128, 768), lambda b, t, l: (l[0], 0, 0)),
            pl.BlockSpec((TOK_TILE, 128), lambda b, t, l: (t if rope else 0, 0)),
            pl.BlockSpec((TOK_TILE, 128), lambda b, t, l: (t if rope else 0, 0)),
        ],
        out_specs=out_specs,
    )
    return pl.pallas_call(
        functools.partial(_mla_kv_kernel, norm=norm, rope=rope),
        grid_spec=grid_spec,
        out_shape=out_shape,
        compiler_params=_cparams(("arbitrary", "arbitrary")),
        name="mla_kv" + ("_norm" if norm else "") + ("_rope" if rope else ""),
    )(lidx, ckv_arr, kr_arr, krr_arr, g_kv3, w_kvb, c4, s4)


def _mla_attn_kernel(l_ref, qa_ref, z_ref, g_ref, w_ref, c_ref, s_ref, *refs, nparts, rope):
    kparts = [[refs[2 * i][0, :, 128 * h:128 * h + 128] for h in range(MLA_HEADS)] for i in range(nparts)]
    vparts = [refs[2 * i + 1][0] for i in range(nparts)]
    o_ref = refs[2 * nparts]
    qa = qa_ref[0].astype(F32)
    ms = jnp.mean(qa * qa, axis=-1, keepdims=True)
    qn = qa * lax.rsqrt(ms + EPS) * g_ref[0]
    qq = _dot(qn.astype(BF16), w_ref[0])
    q = []
    for h in range(MLA_HEADS):
        qh = qq[:, 128 * h:128 * h + 128]
        if rope:
            qh = qh * c_ref[...] + qq[:, 512 + 128 * h:512 + 128 * h + 128] * s_ref[...]
        q.append(qh.astype(BF16))
    om = _head_masks(256, 64, MLA_HEADS, F32)
    half = TOK_TILE // 2
    o = jnp.concatenate([_attend([qh[i * half:(i + 1) * half] for qh in q], kparts, vparts, None, None, om)
                         for i in range(2)], axis=0)
    o_ref[0] = (o * _silu(z_ref[0].astype(F32))).astype(BF16)


def _mla_attn(lidx, cols_mla, g_q3, w_qb, c4, s4, kv_parts, rope):
    B, T, _ = cols_mla.shape
    nt = T // TOK_TILE
    in_specs = [
        pl.BlockSpec((1, TOK_TILE, 256), lambda b, t, l: (b, t, 0)),
        pl.BlockSpec((1, TOK_TILE, 256), lambda b, t, l: (b, t, 1)),
        pl.BlockSpec((1, 1, 256), lambda b, t, l: (l[0], 0, 0)),
        pl.BlockSpec((1, 256, 1024), lambda b, t, l: (l[0], 0, 0)),
        pl.BlockSpec((TOK_TILE, 128), lambda b, t, l: (t if rope else 0, 0)),
        pl.BlockSpec((TOK_TILE, 128), lambda b, t, l: (t if rope else 0, 0)),
    ]
    args = [lidx, cols_mla, cols_mla, g_q3, w_qb, c4, s4]
    for kk, vv in kv_parts:
        in_specs.append(pl.BlockSpec((1, kk.shape[1], 512), lambda b, t, l: (b, 0, 0)))
        in_specs.append(pl.BlockSpec((1, vv.shape[1], 256), lambda b, t, l: (b, 0, 0)))
        args += [kk, vv]
    grid_spec = pltpu.PrefetchScalarGridSpec(
        num_scalar_prefetch=1,
        grid=(B, nt),
        in_specs=in_specs,
        out_specs=pl.BlockSpec((1, TOK_TILE, 256), lambda b, t, l: (b, t, 0)),
    )
    return pl.pallas_call(
        functools.partial(_mla_attn_kernel, nparts=len(kv_parts), rope=rope),
        grid_spec=grid_spec,
        out_shape=jax.ShapeDtypeStruct((B, T, 256), BF16),
        compiler_params=_cparams(("arbitrary", "arbitrary")),
        name="mla_attn_lat" if rope else "mla_attn_ctx",
    )(*args)


def _hg_blocks(rev, include_diag):
    C, S = HG_CHUNK, HG_SUB
    out = []
    for g0 in range(0, C, S):
        if rev:
            lo, hi = (g0 if include_diag else g0 + S), C
        else:
            lo, hi = 0, (g0 + S if include_diag else g0)
        if hi > lo:
            out.append((g0, lo, hi))
    return out


def _hg_mask(rev, include_diag):
    blocks = _hg_blocks(rev, include_diag)
    S = HG_SUB
    nrow = len(blocks) * HG_HEADS * S
    ncol = 128 * -(-sum(hi - lo for _, lo, hi in blocks) // 128)
    r = lax.broadcasted_iota(jnp.int32, (nrow, ncol), 0)
    cidx = lax.broadcasted_iota(jnp.int32, (nrow, ncol), 1)
    rblk = r // (HG_HEADS * S)
    cblk = jnp.full_like(cidx, len(blocks))
    s = jnp.zeros_like(cidx)
    t = r % S
    start = 0
    for k_, (g0, lo, hi) in enumerate(blocks):
        inb = (cidx >= start) & (cidx < start + hi - lo)
        cblk = jnp.where(inb, k_, cblk)
        s = jnp.where(inb, cidx - start + lo, s)
        t = jnp.where(rblk == k_, t + g0, t)
        start += hi - lo
    ok = rblk == cblk
    if include_diag:
        ok = ok & ((s >= t) if rev else (s <= t))
    return ok.astype(F32)


def _hg_stacked(qs, kk, v, b2, rev, include_diag, mask, hm):
    C, S = HG_CHUNK, HG_SUB
    blocks = _hg_blocks(rev, include_diag)
    qx, kd_rows, v_rows = [], [], []
    for g0, lo, hi in blocks:
        has_ref = (g0 + S < C) if rev else (g0 > 0)
        eq = b2[g0:g0 + S, :]
        es = -b2[lo:hi, :]
        if has_ref:
            ct = b2[g0 + S:g0 + S + 1, :] if rev else b2[g0 - 1:g0, :]
            eq = eq - ct
            es = es + ct
        qd = qs[g0:g0 + S, :] * jnp.exp2(eq)
        qx += [qd * m for m in hm]
        kd_rows.append(kk[lo:hi, :] * jnp.exp2(es))
        v_rows.append(v[lo:hi, :])
    pad = mask.shape[1] - sum(hi - lo for _, lo, hi in blocks)
    if pad:
        kd_rows.append(jnp.zeros((pad, 256), F32))
        v_rows.append(jnp.zeros((pad, 256), F32))
    kd_all = jnp.concatenate(kd_rows, axis=0).astype(BF16)
    v_all = jnp.concatenate(v_rows, axis=0).astype(BF16)
    att = _dot_nt(jnp.concatenate(qx, axis=0).astype(BF16), kd_all)
    att = jnp.where(mask > 0.5, att, 0.0)
    return _dot(att.astype(BF16), v_all), blocks


def _hg_gates(lb, rev, q_ref, f_ref, qs_sc, lf_sc, kk_sc):
    qs_sc[...] = _silu(q_ref[0])
    fz = f_ref[0]
    f = lb + (1.0 - lb) * _sigmoid(fz)
    lf = jnp.log(jnp.maximum(f, F_FLOOR))
    kk_sc[...] = (1.0 - lb) * _sigmoid(-fz)
    rr = lax.broadcasted_iota(jnp.int32, (TOK_TILE, TOK_TILE), 0)
    cc = lax.broadcasted_iota(jnp.int32, (TOK_TILE, TOK_TILE), 1)
    tri = ((rr // HG_CHUNK == cc // HG_CHUNK) & ((cc >= rr) if rev else (cc <= rr))).astype(BF16)
    hi = lf.astype(BF16)
    r1 = lf - hi.astype(F32)
    mid = r1.astype(BF16)
    lo = (r1 - mid.astype(F32)).astype(BF16)
    lf_sc[...] = _dot(tri, hi) + _dot(tri, mid) + _dot(tri, lo)
    nblk = TOK_TILE // HG_SUB
    r = lax.broadcasted_iota(jnp.int32, (nblk, TOK_TILE), 0)
    cidx = lax.broadcasted_iota(jnp.int32, (nblk, TOK_TILE), 1)
    span = _dot((cidx // HG_SUB == r).astype(BF16), (lf * (-LOG2E)).astype(BF16))
    return jnp.max(span) <= HG_FAST_SPAN


def _hg_chunk(r0, rev, fast, st, qs_sc, lf_sc, kk_sc, i_ref, a_sc, v_sc, o_sc, seg, bdmask, mask):
    C = HG_CHUNK
    S = HG_SUB
    qs = qs_sc[pl.ds(r0, C), :]
    b = lf_sc[pl.ds(r0, C), :]
    kk = kk_sc[pl.ds(r0, C), :]
    v = i_ref[0, pl.ds(r0, C), :]
    b2 = b * LOG2E
    total = b[0:1, :] if rev else b[C - 1:C, :]
    inter = _dot_nt((qs * jnp.exp2(b2)).astype(BF16), st.astype(BF16))
    hm = _head_masks(256, HG_DK, HG_HEADS, F32)
    off, blocks = _hg_stacked(qs, kk, v, b2, rev, fast, mask, hm)

    def head_sum(k_, i):
        acc = None
        for h in range(HG_HEADS):
            r_ = k_ * HG_HEADS * S + h * S + 8 * i
            t_ = off[r_:r_ + 8, :] * hm[h]
            acc = t_ if acc is None else acc + t_
        return acc

    if fast:
        tiles = []
        for k_, (g0, _, _) in enumerate(blocks):
            for i in range(S // 8):
                tiles.append(inter[g0 + 8 * i:g0 + 8 * i + 8, :] + head_sum(k_, i))
        o = jnp.concatenate(tiles, axis=0)
    else:
        a_sc[...] = jnp.log2(kk) - b2
        v_sc[...] = v
        o_sc[...] = inter
        for k_, (g0, _, _) in enumerate(blocks):
            for i in range(S // 8):
                o_sc[g0 + 8 * i:g0 + 8 * i + 8, :] = o_sc[g0 + 8 * i:g0 + 8 * i + 8, :] + head_sum(k_, i)
        sub = lax.broadcasted_iota(jnp.int32, (8, 256), 0)
        for g0 in range(0, C, S):
            parts, spans = [], []
            for s in range(g0, g0 + S):
                t0 = (s // 8) * 8
                arow = a_sc[s:s + 1, :]
                keep = (sub <= s - t0) if rev else (sub >= s - t0)
                rdiag = qs[t0:t0 + 8, :] * jnp.exp2(jnp.where(keep, b2[t0:t0 + 8, :] + arow, NEG_INF))
                rest_lo, rest_hi = (g0, t0) if rev else (t0 + 8, g0 + S)
                if rest_hi > rest_lo:
                    rrest = qs[rest_lo:rest_hi, :] * jnp.exp2(b2[rest_lo:rest_hi, :] + arow)
                    parts += [rrest, rdiag] if rev else [rdiag, rrest]
                else:
                    parts.append(rdiag)
                spans.append((min(rest_lo, t0), max(rest_hi, t0 + 8)) if rest_hi > rest_lo else (t0, t0 + 8))
            res = _dot(jnp.concatenate(parts, axis=0).astype(BF16), seg)
            acc = [o_sc[g0 + 8 * i:g0 + 8 * i + 8, :] for i in range(S // 8)]
            pos = 0
            for s, (lo_r, hi_r) in zip(range(g0, g0 + S), spans):
                vrow = v_sc[s:s + 1, :]
                for i in range((hi_r - lo_r) // 8):
                    ti = (lo_r - g0) // 8 + i
                    acc[ti] = acc[ti] + res[pos + 8 * i:pos + 8 * i + 8, :] * vrow
                pos += hi_r - lo_r
            for i in range(S // 8):
                o_sc[g0 + 8 * i:g0 + 8 * i + 8, :] = acc[i]
        o = o_sc[...]
    kd = (kk * jnp.exp(total - b)).astype(BF16)
    return o, jnp.exp(total) * st + bdmask * _dot_tn(v.astype(BF16), kd)


def _hg_kernel(l_ref, q_ref, f_ref, i_ref, og_ref, z_ref, lbl_ref, g_ref, s0_ref, y_ref, sto_ref,
               st_ref, oacc_ref, qs_sc, lf_sc, kk_sc, a_sc, v_sc, o_sc, *, nb):
    j = pl.program_id(1)
    l = l_ref[0]
    C = HG_CHUNK
    ncb = TOK_TILE // C

    @pl.when((j == 0) | (j == nb))
    def _():
        st_ref[...] = s0_ref[0, 0]

    lg = lbl_ref[...]
    mx = lg[0]
    for m in range(1, DEPTH):
        mx = jnp.maximum(mx, lg[m])
    ex = [jnp.exp(lg[m] - mx) for m in range(DEPTH)]
    tot = ex[0]
    for m in range(1, DEPTH):
        tot = tot + ex[m]
    lbs = jnp.zeros_like(tot)
    for m in range(1, DEPTH):
        lbs = lbs + jnp.where(m <= l, ex[m] / tot, 0.0)

    seg = _seg_matrix(256, HG_DK)
    seg_b = seg.astype(BF16)
    bdmask = seg.astype(F32)

    def run(rev, emit):
        lb = lbs[1:2, :] if rev else lbs[0:1, :]
        small = _hg_gates(lb, rev, q_ref, f_ref, qs_sc, lf_sc, kk_sc)
        common = (qs_sc, lf_sc, kk_sc, i_ref, a_sc, v_sc, o_sc, seg_b, bdmask)

        @pl.when(small)
        def _():
            mask = _hg_mask(rev, True)
            st = st_ref[...]
            for ci in range(ncb):
                r0 = (ncb - 1 - ci if rev else ci) * C
                o, st = _hg_chunk(r0, rev, True, st, *common, mask)
                emit(r0, o)
            st_ref[...] = st

        @pl.when(jnp.logical_not(small))
        def _():
            mask = _hg_mask(rev, False)

            def body(ci, carry):
                r0 = pl.multiple_of((ncb - 1 - ci if rev else ci) * C, C)
                o, st = _hg_chunk(r0, rev, False, st_ref[...], *common, mask)
                st_ref[...] = st
                emit(r0, o)
                return carry

            lax.fori_loop(0, ncb, body, 0)

    @pl.when(j < nb)
    def _():
        def emit(r0, o):
            oacc_ref[pl.ds(pl.multiple_of(j * TOK_TILE + r0, C), C), :] = o

        run(False, emit)

    @pl.when(j >= nb)
    def _():
        blk = 2 * nb - 1 - j

        def emit(r0, o):
            o = oacc_ref[pl.ds(pl.multiple_of(blk * TOK_TILE + r0, C), C), :] + o
            ms = _seg_mean(o * o, seg_b, HG_DK)
            on = o * lax.rsqrt(ms + EPS) * g_ref[0]
            on = on * _sigmoid(og_ref[0, pl.ds(r0, C), :])
            y_ref[0, pl.ds(r0, C), :] = (on * _silu(z_ref[0, pl.ds(r0, C), :])).astype(BF16)

        run(True, emit)

    sto_ref[0, 0] = st_ref[...]


def _hgrn(lidx, cols_hg, lb_logits, g3, s0, s0_per_batch):
    B, T, _ = cols_hg.shape
    nb = T // TOK_TILE

    def blk(j):
        return jnp.where(j < nb, j, 2 * nb - 1 - j)

    def dirn(j):
        return jnp.where(j < nb, 0, 1)

    def col(jcol):
        return pl.BlockSpec((1, TOK_TILE, 256), lambda b, j, l, jcol=jcol: (b, blk(j), jcol))

    grid_spec = pltpu.PrefetchScalarGridSpec(
        num_scalar_prefetch=1,
        grid=(B, 2 * nb),
        in_specs=[
            col(0),
            pl.BlockSpec((1, TOK_TILE, 256), lambda b, j, l: (b, blk(j), 1 + dirn(j))),
            col(3), col(4), col(5),
            pl.BlockSpec((DEPTH, 2, 256), lambda b, j, l: (0, 0, 0)),
            pl.BlockSpec((1, 1, 256), lambda b, j, l: (l[0], 0, 0)),
            pl.BlockSpec((1, 1, 256, 256),
                         (lambda b, j, l: (b, dirn(j), 0, 0)) if s0_per_batch else (lambda b, j, l: (0, 0, 0, 0))),
        ],
        out_specs=[
            pl.BlockSpec((1, TOK_TILE, 256), lambda b, j, l: (b, jnp.where(j < nb, nb - 1, 2 * nb - 1 - j), 0)),
            pl.BlockSpec((1, 1, 256, 256), lambda b, j, l: (b, dirn(j), 0, 0)),
        ],
        scratch_shapes=[
            pltpu.VMEM((256, 256), F32),
            pltpu.VMEM((T, 256), F32),
            pltpu.VMEM((TOK_TILE, 256), F32),
            pltpu.VMEM((TOK_TILE, 256), F32),
            pltpu.VMEM((TOK_TILE, 256), F32),
            pltpu.VMEM((HG_CHUNK, 256), F32),
            pltpu.VMEM((HG_CHUNK, 256), F32),
            pltpu.VMEM((HG_CHUNK, 256), F32),
        ],
    )
    return pl.pallas_call(
        functools.partial(_hg_kernel, nb=nb),
        grid_spec=grid_spec,
        out_shape=[
            jax.ShapeDtypeStruct((B, T, 256), BF16),
            jax.ShapeDtypeStruct((B, 2, 256, 256), F32),
        ],
        compiler_params=_cparams(("arbitrary", "arbitrary")),
        name="hgrn",
    )(lidx, cols_hg, cols_hg, cols_hg, cols_hg, cols_hg, lb_logits, g3, s0)


def _rot_cols(w, nheads):
    shp = w.shape
    r = shp[-1] // nheads
    w4 = w.reshape(shp[:-1] + (nheads, 2, r // 2))
    return jnp.concatenate([-w4[..., 1, :], w4[..., 0, :]], axis=-1).reshape(shp)


def _perm_heads(w, nheads, order):
    shp = w.shape
    w3 = w.reshape(shp[:-1] + (nheads, shp[-1] // nheads))
    return w3[..., list(order), :].reshape(shp)


_WG_ORDER = (0, 2, 1, 3)


def _mla_rope_slot(a):
    pad = [(0, 0)] * (a.ndim - 1) + [(MLA_NOPE, 128 - MLA_NOPE - MLA_ROPE)]
    return jnp.pad(a, pad)


def _prep_w_in(w_in):
    def c(name):
        lo, hi = _IN_OFF[name]
        return w_in[:, :, lo:hi]

    kva = c('mla_kva')
    lo = _IN_OFF['na_k'][0]
    hi = _IN_OFF['hg_z'][1]
    parts = [
        c('na_q') * (HEAD_DIM ** -0.5 * LOG2E),
        w_in[:, :, lo:hi],
        c('mla_qa'), c('mla_z'), kva[:, :, :MLA_KV_RANK], _mla_rope_slot(kva[:, :, MLA_KV_RANK:]),
        _perm_heads(c('wg_q'), WG_HEADS, _WG_ORDER) * (HEAD_DIM ** -0.5 * LOG2E),
        _perm_heads(c('wg_z'), WG_HEADS, _WG_ORDER),
        c('wg_k'), c('wg_v'),
    ]
    return jnp.concatenate([p.astype(BF16) for p in parts], axis=-1), c('merge').astype(BF16)


def _rot_matrix(width, head):
    i = np.arange(width)[:, None]
    j = np.arange(width)[None, :]
    half = head // 2
    first = (j % head) < half
    m = np.where(first & (i == j + half), -1.0, 0.0) + np.where(~first & (i == j - half), 1.0, 0.0)
    return jnp.asarray(m, BF16)


def _rope_tables(n_tok, rot_dim, reps):
    t = np.arange(n_tok)
    row = (t // GRID_W).astype(np.float32)
    col = (t % GRID_W).astype(np.float32)
    nf = rot_dim // 4
    inv = jnp.asarray(ROPE_BASE, F32) ** (-jnp.arange(nf, dtype=F32) / nf)
    ang = jnp.concatenate([jnp.asarray(row)[:, None] * inv, jnp.asarray(col)[:, None] * inv], axis=-1)
    cos, sin = jnp.cos(ang), jnp.sin(ang)
    return (jnp.tile(jnp.concatenate([cos, cos], axis=-1), (1, reps)),
            jnp.tile(jnp.concatenate([sin, sin], axis=-1), (1, reps)))


def _state_to_blockdiag(st):
    B = st.shape[0]
    stt = jnp.swapaxes(st, -1, -2)
    eye = jnp.eye(HG_HEADS, dtype=st.dtype)
    bd = stt[:, :, :, :, None, :] * eye[None, None, :, None, :, None]
    return bd.reshape(B, 2, HG_HEADS * HG_DK, HG_HEADS * HG_DK)


def _blockdiag_to_state(bd):
    B = bd.shape[0]
    b6 = bd.reshape(B, 2, HG_HEADS, HG_DK, HG_HEADS, HG_DK)
    diag = jnp.stack([b6[:, :, h, :, h, :] for h in range(HG_HEADS)], axis=2)
    return jnp.swapaxes(diag, -1, -2)


def kernel(x_prompt, x_sample, cache_na_k, cache_na_v, state_hgrn, cache_mla_ckv, cache_mla_krope, cache_wg_k, cache_wg_v, c, c_ctx, norm_g, w_mod, b_mod, w_in, na_rpb, hg_lb_logits, hg_norm_g, mla_q_norm_g, mla_kv_norm_g, mla_w_qb, mla_w_kvb, wg_sink, w_branch, w_out, final_g):
    Bp, Tp, _ = x_prompt.shape
    Bd, Td, _ = x_sample.shape
    L = cache_na_k.shape[2]
    rows = Td // GRID_W
    assert Tp % TOK_TILE == 0 and Td % TOK_TILE == 0 and L % TOK_TILE == 0
    assert rows % NA_QROWS == 0 and rows >= NA_KROWS and Td >= WG_KWIN

    w_proj, w_merge = _prep_w_in(w_in)
    wb = w_branch.astype(BF16)
    wb = wb.at[:, 3].set(_perm_heads(jnp.swapaxes(wb[:, 3], -1, -2), WG_HEADS, _WG_ORDER).swapaxes(-1, -2))
    wo = w_out.astype(BF16)
    qb = mla_w_qb.reshape(DEPTH, MLA_Q_RANK, MLA_HEADS, MLA_NOPE + MLA_ROPE)
    qb_g1 = jnp.pad(qb, ((0, 0), (0, 0), (0, 0), (0, 128 - MLA_NOPE - MLA_ROPE)))
    qb_g2 = _mla_rope_slot(_rot_cols(qb[..., MLA_NOPE:], 1))
    w_qb = jnp.concatenate([qb_g1.reshape(DEPTH, MLA_Q_RANK, -1), qb_g2.reshape(DEPTH, MLA_Q_RANK, -1)],
                           axis=-1)
    w_qb = (w_qb * ((MLA_NOPE + MLA_ROPE) ** -0.5 * LOG2E)).astype(BF16)
    kvb = mla_w_kvb.reshape(DEPTH, MLA_KV_RANK, MLA_HEADS, 2 * MLA_NOPE)
    kvb_k = jnp.pad(kvb[..., :MLA_NOPE], ((0, 0), (0, 0), (0, 0), (0, 128 - MLA_NOPE)))
    w_kvb = jnp.concatenate([kvb_k.reshape(DEPTH, MLA_KV_RANK, -1),
                             kvb[..., MLA_NOPE:].reshape(DEPTH, MLA_KV_RANK, -1)], axis=-1).astype(BF16)
    norm_g3 = norm_g.reshape(DEPTH, 1, D_MODEL)
    g_q3 = mla_q_norm_g.reshape(DEPTH, 1, MLA_Q_RANK)
    g_kv3 = mla_kv_norm_g.reshape(DEPTH, 1, MLA_KV_RANK)
    hg_g3 = jnp.tile(hg_norm_g, (1, HG_HEADS)).reshape(DEPTH, 1, HG_HEADS * HG_DK)
    final_g2 = final_g.reshape(1, D_MODEL)
    cq, sq = _rope_tables(Td, HEAD_DIM, WG_HEADS)
    ck, sk = cq[:, :128], sq[:, :128]
    c1, s1 = _rope_tables(Td, MLA_ROPE, 1)
    c4 = jnp.concatenate([jnp.ones((Td, MLA_NOPE), F32), c1, jnp.zeros((Td, 32), F32)], axis=-1)
    s4 = _mla_rope_slot(s1)
    rm_wg = _rot_matrix(WG_HEADS * HEAD_DIM, HEAD_DIM)
    rm_mla = jnp.zeros((128, 128), BF16).at[MLA_NOPE:MLA_NOPE + MLA_ROPE, MLA_NOPE:MLA_NOPE + MLA_ROPE].set(
        _rot_matrix(MLA_ROPE, MLA_ROPE))

    nrow = 8 * ((1 + Bd + 7) // 8)
    cond = jnp.zeros((nrow, D_MODEL), F32).at[0].set(c_ctx).at[1:1 + Bd].set(c)
    mod4 = _modulation(cond, w_mod, b_mod).reshape(DEPTH, nrow, 1, 3 * D_MODEL)

    kstart, ktype, table = _na_row_types(rows)
    na_bias = _na_bias(na_rpb, table)
    kstart = jnp.asarray(kstart, jnp.int32)
    ktype = jnp.asarray(ktype, jnp.int32)

    cna_k = cache_na_k.reshape(Bd, DEPTH, L, NA_HEADS * HEAD_DIM)
    cna_v = cache_na_v.reshape(Bd, DEPTH, L, NA_HEADS * HEAD_DIM)
    cwg_k = cache_wg_k.reshape(Bd, DEPTH, L, WG_KV_HEADS * HEAD_DIM)
    cwg_v = cache_wg_v.reshape(Bd, DEPTH, L, WG_KV_HEADS * HEAD_DIM)
    ckr4 = _mla_rope_slot(cache_mla_krope)
    st_bd = _state_to_blockdiag(state_hgrn.reshape(Bd * DEPTH, 2, HG_HEADS, HG_DK, HG_DK)).reshape(
        Bd, DEPTH, 2, 256, 256)
    st_zero = jnp.zeros((1, 2, 256, 256), F32)

    hp, hs = x_prompt, x_sample
    states = []
    tok_caches = None
    for layer in range(DEPTH):
        lidx = jnp.full((1,), layer, jnp.int32)
        final = layer == DEPTH - 1

        flat = (1, Bp * Tp)
        res = _in_proj(lidx, hp.reshape(flat + (D_MODEL,)), mod4, 0, False, norm_g3, w_proj, g_kv3, w_kvb, rm_mla,
                       c4, s4, False, caches=tok_caches, cache_shape=(Bp, DEPTH, Tp))
        na, hg, mla, wg, kc_, vc_ = (a.reshape(Bp, Tp, a.shape[-1]) for a in res[:6])
        tok_caches = res[6:]
        y_na = _na_ctx(na)
        y_hg, st_out = _hgrn(lidx, hg, hg_lb_logits, hg_g3, st_zero, False)
        y_mla = _mla_attn(lidx, mla, g_q3, w_qb, c4, s4, [(kc_, vc_)], False)
        y_wg = _wg_ctx(lidx, wg_sink, wg)
        ys = tuple(y.reshape(flat + (BRANCH_W,)) for y in (y_na, y_hg, y_mla, y_wg))
        hp = _post(lidx, hp.reshape(flat + (D_MODEL,)), mod4, 0, False, norm_g3, ys, w_merge, wb, wo, final_g2,
                   final).reshape(Bp, Tp, D_MODEL)
        states.append(st_out)

        na, hg, mla, wg, kl, vl = _in_proj(lidx, hs, mod4, 1, True, norm_g3, w_proj, g_kv3, w_kvb, rm_mla, c4, s4,
                                           True)
        y_na = _na_lat(lidx, kstart, ktype, na, cna_k, cna_v, na_bias)
        y_hg, _ = _hgrn(lidx, hg, hg_lb_logits, hg_g3, st_bd[:, layer], True)
        kx, vx = _mla_kv(lidx, cache_mla_ckv, 0, ckr4, 0, ckr4, 0, g_kv3, w_kvb, c4, s4, False, False, True)
        y_mla = _mla_attn(lidx, mla, g_q3, w_qb, c4, s4, [(kl, vl), (kx, vx)], True)
        y_wg = _wg_lat(lidx, wg_sink, wg, cwg_k, cwg_v, cq, sq, ck, sk, rm_wg)
        hs = _post(lidx, hs, mod4, 1, True, norm_g3, (y_na, y_hg, y_mla, y_wg), w_merge, wb, wo, final_g2, final)

    new_state = _blockdiag_to_state(jnp.stack(states, axis=1).reshape(Bp * DEPTH, 2, 256, 256)).reshape(
        Bp, DEPTH, 2, HG_HEADS, HG_DK, HG_DK)
    na_k, na_v, ckv_c, kr_c, wg_k, wg_v = tok_caches
    return (hp, hs,
            na_k.reshape(Bp, DEPTH, Tp, NA_HEADS, HEAD_DIM), na_v.reshape(Bp, DEPTH, Tp, NA_HEADS, HEAD_DIM),
            new_state, ckv_c, kr_c,
            wg_k.reshape(Bp, DEPTH, Tp, WG_KV_HEADS, HEAD_DIM), wg_v.reshape(Bp, DEPTH, Tp, WG_KV_HEADS, HEAD_DIM))
```

```python
import functools

import jax
import jax.numpy as jnp
import numpy as np
from jax import lax
from jax.experimental import pallas as pl
from jax.experimental.pallas import tpu as pltpu

F32 = jnp.float32
BF16 = jnp.bfloat16

D_MODEL = 1024
DEPTH = 4
GRID_W = 64
HEAD_DIM = 64
BRANCH_W = 256
N_BRANCH = 4
NA_HEADS = 4
NA_WIN_H = 8
NA_WIN_W = 16
HG_HEADS = 4
HG_DK = 64
HG_CHUNK = 64
HG_SUB = 32
HG_FAST_SPAN = 100.0
LOG2E = 1.4426950408889634
F_FLOOR = 1e-30
MLA_HEADS = 4
MLA_Q_RANK = 256
MLA_KV_RANK = 128
MLA_NOPE = 64
MLA_ROPE = 32
WG_HEADS = 4
WG_KV_HEADS = 2
WG_WINDOW = 128
ROPE_BASE = 10000.0
EPS = 1e-6
NEG_INF = -1e30

_IN_LAYOUT = (
    ('na_q', 256), ('na_k', 256), ('na_v', 256), ('na_z', 256),
    ('hg_q', 256), ('hg_ff', 256), ('hg_fb', 256), ('hg_i', 256), ('hg_og', 256), ('hg_z', 256),
    ('mla_qa', 256), ('mla_kva', 160), ('mla_z', 256),
    ('wg_q', 256), ('wg_k', 128), ('wg_v', 128), ('wg_z', 256),
    ('merge', 4096),
)
_IN_OFF = {}
_o = 0
for _n, _w in _IN_LAYOUT:
    _IN_OFF[_n] = (_o, _o + _w)
    _o += _w

NA_W = 1024
HG_W = 1536
MLA_W = 768
WG_W = 768
PROJ_W = NA_W + HG_W + MLA_W + WG_W

TOK_TILE = 256
PROJ_TILE = 512
NA_QROWS = 4
NA_KROWS = 12
WG_TILE = 256
WG_KWIN = WG_TILE + 2 * WG_WINDOW
VMEM_LIMIT = 56 * 1024 * 1024


def _cparams(sem):
    return pltpu.CompilerParams(dimension_semantics=sem, vmem_limit_bytes=VMEM_LIMIT)


def _dot(a, b):
    return jnp.dot(a, b, preferred_element_type=F32)


def _dot_nt(a, b):
    return lax.dot_general(a, b, (((1,), (1,)), ((), ())), preferred_element_type=F32)


def _dot_tn(a, b):
    return lax.dot_general(a, b, (((0,), (0,)), ((), ())), preferred_element_type=F32)


def _sigmoid(x):
    return 1.0 / (1.0 + jnp.exp(-x))


def _silu(x):
    return x * _sigmoid(x)


def _lane_mask(width, lo, hi, dtype):
    lane = lax.broadcasted_iota(jnp.int32, (1, width), 1)
    return ((lane >= lo) & (lane < hi)).astype(dtype)


def _seg_matrix(width, seg):
    r = lax.broadcasted_iota(jnp.int32, (width, width), 0) // seg
    c = lax.broadcasted_iota(jnp.int32, (width, width), 1) // seg
    return r == c


def _seg_mean(x2, seg_bf16, seg):
    hi = x2.astype(BF16)
    lo = (x2 - hi.astype(F32)).astype(BF16)
    return (_dot(hi, seg_bf16) + _dot(lo, seg_bf16)) * (1.0 / seg)


def _mod_kernel(c_ref, w_ref, b_ref, o_ref):
    s = _silu(c_ref[...])
    o_ref[0] = _dot(s.astype(BF16), w_ref[0].astype(BF16)) + b_ref[0]


def _modulation(cond, w_mod, b_mod):
    nrow = cond.shape[0]
    tn = 1024
    return pl.pallas_call(
        _mod_kernel,
        grid=(DEPTH, 3 * D_MODEL // tn),
        in_specs=[
            pl.BlockSpec((nrow, D_MODEL), lambda l, j: (0, 0)),
            pl.BlockSpec((1, D_MODEL, tn), lambda l, j: (l, 0, j)),
            pl.BlockSpec((1, 1, tn), lambda l, j: (l, 0, j)),
        ],
        out_specs=pl.BlockSpec((1, nrow, tn), lambda l, j: (l, 0, j)),
        out_shape=jax.ShapeDtypeStruct((DEPTH, nrow, 3 * D_MODEL), F32),
        compiler_params=_cparams(("arbitrary", "arbitrary")),
        name="modulation",
    )(cond, w_mod, b_mod.reshape(DEPTH, 1, 3 * D_MODEL))


def _mod_norm(x, mod_ref, g_ref):
    ms = jnp.mean(x * x, axis=-1, keepdims=True)
    y = x * lax.rsqrt(ms + EPS) * g_ref[0]
    shift = mod_ref[0, 0, :, 0:D_MODEL]
    scale = mod_ref[0, 0, :, D_MODEL:2 * D_MODEL]
    gate = mod_ref[0, 0, :, 2 * D_MODEL:3 * D_MODEL]
    return y * (1.0 + scale) + shift, gate


_CACHE_WIDTHS = (NA_HEADS * HEAD_DIM, NA_HEADS * HEAD_DIM, MLA_KV_RANK, MLA_ROPE,
                 WG_KV_HEADS * HEAD_DIM, WG_KV_HEADS * HEAD_DIM)


def _in_kernel(l_ref, x_ref, mod_ref, g_ref, w_ref, gkv_ref, wkvb_ref, rm_ref, c_ref, s_ref, *rest,
               rope, n_alias, seq):
    rest = rest[n_alias:]
    na_ref, hg_ref, mla_ref, wg_ref, kall_ref, vall_ref = rest[:6]
    h, _ = _mod_norm(x_ref[0], mod_ref, g_ref)
    hb = h.astype(BF16)
    na = _dot(hb, w_ref[0, :, 0:NA_W])
    na_ref[0] = na.astype(na_ref.dtype)
    hg_ref[0] = _dot(hb, w_ref[0, :, NA_W:NA_W + HG_W])
    wg = _dot(hb, w_ref[0, :, NA_W + HG_W + MLA_W:PROJ_W])
    wg_ref[0] = wg.astype(wg_ref.dtype)
    mla = _dot(hb, w_ref[0, :, NA_W + HG_W:NA_W + HG_W + MLA_W])
    mla_ref[0] = mla[:, 0:512].astype(mla_ref.dtype)
    ckv = mla[:, 512:640]
    ms = jnp.mean(ckv * ckv, axis=-1, keepdims=True)
    ckv = ckv * lax.rsqrt(ms + EPS) * gkv_ref[0]
    kr = mla[:, 640:768]
    if not rope:
        pieces = (na[:, 256:512], na[:, 512:768], ckv, kr[:, MLA_NOPE:MLA_NOPE + MLA_ROPE],
                  wg[:, 512:640], wg[:, 640:768])
        for ref, val in zip(rest[6:], pieces):
            for i in range(val.shape[0] // seq):
                ref[i, 0] = val[i * seq:(i + 1) * seq, :]
    if rope:
        kr = kr * c_ref[...] + _dot(kr.astype(BF16), rm_ref[...]) * s_ref[...]
    kv = _dot(ckv.astype(BF16), wkvb_ref[0])
    for hd in range(MLA_HEADS):
        kall_ref[0, :, 128 * hd:128 * hd + 128] = (kv[:, 128 * hd:128 * hd + 128] + kr).astype(BF16)
    vall_ref[0] = kv[:, 512:768].astype(BF16)


def _in_proj(lidx, x, mod4, mod_row0, per_batch, norm_g3, w_proj, g_kv3, w_kvb, rm_mla, c4, s4, rope,
             caches=None, cache_shape=None):
    B, T, _ = x.shape
    tile = min(PROJ_TILE, T)
    nt = T // tile
    mrow = (lambda b: mod_row0 + b) if per_batch else (lambda b: mod_row0)
    col_dtype = BF16

    def tok(width):
        return pl.BlockSpec((1, tile, width), lambda b, t, l: (b, t, 0))

    out_specs = [tok(NA_W), tok(HG_W), tok(512), tok(WG_W), tok(512), tok(256)]
    out_shape = [
        jax.ShapeDtypeStruct((B, T, NA_W), col_dtype),
        jax.ShapeDtypeStruct((B, T, HG_W), F32),
        jax.ShapeDtypeStruct((B, T, 512), col_dtype),
        jax.ShapeDtypeStruct((B, T, WG_W), col_dtype),
        jax.ShapeDtypeStruct((B, T, 512), BF16),
        jax.ShapeDtypeStruct((B, T, 256), BF16),
    ]
    in_specs = [
        pl.BlockSpec((1, tile, D_MODEL), lambda b, t, l: (b, t, 0)),
        pl.BlockSpec((1, 1, 1, 3 * D_MODEL), lambda b, t, l: (l[0], mrow(b), 0, 0)),
        pl.BlockSpec((1, 1, D_MODEL), lambda b, t, l: (l[0], 0, 0)),
        pl.BlockSpec((1, D_MODEL, PROJ_W), lambda b, t, l: (l[0], 0, 0)),
        pl.BlockSpec((1, 1, 128), lambda b, t, l: (l[0], 0, 0)),
        pl.BlockSpec((1, 128, 768), lambda b, t, l: (l[0], 0, 0)),
        pl.BlockSpec((128, 128), lambda b, t, l: (0, 0)),
        pl.BlockSpec((tile, 128), lambda b, t, l: (t if rope else 0, 0)),
        pl.BlockSpec((tile, 128), lambda b, t, l: (t if rope else 0, 0)),
    ]
    args = [lidx, x, mod4, norm_g3, w_proj, g_kv3, w_kvb, rm_mla, c4, s4]
    aliases = {}
    seq = tile
    if not rope:
        nbatch, _, seq = cache_shape
        assert B == 1 and tile % seq == 0
        for width in _CACHE_WIDTHS:
            out_specs.append(pl.BlockSpec((tile // seq, 1, seq, width), lambda b, t, l: (t, l[0], 0, 0)))
            out_shape.append(jax.ShapeDtypeStruct((nbatch, DEPTH, seq, width), F32))
        if caches is not None:
            for i, arr in enumerate(caches):
                aliases[len(args)] = 6 + i
                in_specs.append(pl.BlockSpec(memory_space=pl.ANY))
                args.append(arr)
    grid_spec = pltpu.PrefetchScalarGridSpec(
        num_scalar_prefetch=1,
        grid=(B, nt),
        in_specs=in_specs,
        out_specs=out_specs,
    )
    return pl.pallas_call(
        functools.partial(_in_kernel, rope=rope, n_alias=len(aliases), seq=seq),
        grid_spec=grid_spec,
        out_shape=out_shape,
        input_output_aliases=aliases,
        compiler_params=_cparams(("arbitrary", "arbitrary")),
        name="in_proj_lat" if rope else ("in_proj_ctx" if caches is None else "in_proj_ctx_inplace"),
    )(*args)


def _post_kernel(l_ref, x_ref, mod_ref, g_ref, y0_ref, y1_ref, y2_ref, y3_ref, wm_ref, wb_ref, wo_ref,
                 fg_ref, o_ref, *, final):
    x = x_ref[0]
    h, gate = _mod_norm(x, mod_ref, g_ref)
    hb = h.astype(BF16)
    acc = jnp.zeros(x.shape, F32)
    for n, y_ref in enumerate((y0_ref, y1_ref, y2_ref, y3_ref)):
        mg = _sigmoid(_dot(hb, wm_ref[0, :, n * D_MODEL:(n + 1) * D_MODEL]))
        acc = acc + mg * _dot(y_ref[0], wb_ref[0, n])
    out = x + gate * _dot(acc.astype(BF16), wo_ref[0])
    if final:
        ms = jnp.mean(out * out, axis=-1, keepdims=True)
        out = out * lax.rsqrt(ms + EPS) * fg_ref[...]
    o_ref[0] = out


def _post(lidx, x, mod4, mod_row0, per_batch, norm_g3, ys, w_merge, w_branch, w_out, final_g2, final):
    B, T, _ = x.shape
    tile = min(PROJ_TILE, T)
    nt = T // tile
    mrow = (lambda b: mod_row0 + b) if per_batch else (lambda b: mod_row0)
    yspec = pl.BlockSpec((1, tile, BRANCH_W), lambda b, t, l: (b, t, 0))
    grid_spec = pltpu.PrefetchScalarGridSpec(
        num_scalar_prefetch=1,
        grid=(B, nt),
        in_specs=[
            pl.BlockSpec((1, tile, D_MODEL), lambda b, t, l: (b, t, 0)),
            pl.BlockSpec((1, 1, 1, 3 * D_MODEL), lambda b, t, l: (l[0], mrow(b), 0, 0)),
            pl.BlockSpec((1, 1, D_MODEL), lambda b, t, l: (l[0], 0, 0)),
            yspec, yspec, yspec, yspec,
            pl.BlockSpec((1, D_MODEL, N_BRANCH * D_MODEL), lambda b, t, l: (l[0], 0, 0)),
            pl.BlockSpec((1, N_BRANCH, BRANCH_W, D_MODEL), lambda b, t, l: (l[0], 0, 0, 0)),
            pl.BlockSpec((1, D_MODEL, D_MODEL), lambda b, t, l: (l[0], 0, 0)),
            pl.BlockSpec((1, D_MODEL), lambda b, t, l: (0, 0)),
        ],
        out_specs=pl.BlockSpec((1, tile, D_MODEL), lambda b, t, l: (b, t, 0)),
    )
    return pl.pallas_call(
        functools.partial(_post_kernel, final=final),
        grid_spec=grid_spec,
        out_shape=jax.ShapeDtypeStruct((B, T, D_MODEL), F32),
        compiler_params=_cparams(("arbitrary", "arbitrary")),
        name="post_final" if final else "post",
    )(lidx, x, mod4, norm_g3, *ys, w_merge, w_branch, w_out, final_g2)


def _attend(q, kparts, vparts, biases, qmasks, omasks, sinks=None):
    out = None
    for h in range(len(omasks)):
        qh = q[h] if qmasks is None else q * qmasks[h]
        ss = []
        for i, kp in enumerate(kparts):
            s = _dot_nt(qh, kp[h] if isinstance(kp, (list, tuple)) else kp)
            if biases is not None and biases[h][i] is not None:
                s = s + biases[h][i]
            ss.append(s)
        m = ss[0].max(axis=-1, keepdims=True)
        for s in ss[1:]:
            m = jnp.maximum(m, s.max(axis=-1, keepdims=True))
        if sinks is not None:
            m = jnp.maximum(m, sinks[h])
        den = None
        oh = None
        for s, vp in zip(ss, vparts):
            p = jnp.exp2(s - m)
            ps = p.sum(axis=-1, keepdims=True)
            den = ps if den is None else den + ps
            pv = _dot(p.astype(BF16), vp)
            oh = pv if oh is None else oh + pv
        if sinks is not None:
            den = den + jnp.exp2(sinks[h] - m)
        oh = (oh / den) * omasks[h]
        out = oh if out is None else out + oh
    return out


def _head_masks(width, seg, nheads, dtype):
    return [_lane_mask(width, h * seg, (h + 1) * seg, dtype) for h in range(nheads)]


def _na_ctx_kernel(q_ref, k_ref, v_ref, z_ref, o_ref):
    q = q_ref[0].astype(BF16)
    k = k_ref[0].astype(BF16)
    v = v_ref[0].astype(BF16)
    o = _attend(q, [k], [v], None, _head_masks(256, 64, NA_HEADS, BF16), _head_masks(256, 64, NA_HEADS, F32))
    o_ref[0] = (o * _silu(z_ref[0].astype(F32))).astype(BF16)


def _na_ctx(cols_na):
    B, T, _ = cols_na.shape

    def spec(j):
        return pl.BlockSpec((1, T, 256), lambda b, j=j: (b, 0, j))

    return pl.pallas_call(
        _na_ctx_kernel,
        grid=(B,),
        in_specs=[spec(0), spec(1), spec(2), spec(3)],
        out_specs=pl.BlockSpec((1, T, 256), lambda b: (b, 0, 0)),
        out_shape=jax.ShapeDtypeStruct((B, T, 256), BF16),
        compiler_params=_cparams(("arbitrary",)),
        name="na_ctx",
    )(cols_na, cols_na, cols_na, cols_na)


def _na_row_types(rows):
    wh = min(NA_WIN_H, rows)
    nblk = rows // NA_QROWS
    starts, types, table = [], [], []
    for blk in range(nblk):
        r0 = blk * NA_QROWS
        k0 = int(np.clip(r0 - wh // 2, 0, rows - NA_KROWS))
        desc = []
        for i in range(NA_QROWS):
            r = r0 + i
            ks = int(np.clip(r - wh // 2, 0, rows - wh))
            for j in range(NA_KROWS):
                kr = k0 + j
                desc.append((kr - r + NA_WIN_H - 1) if ks <= kr < ks + wh else None)
        desc = tuple(desc)
        if desc not in table:
            table.append(desc)
        starts.append(k0)
        types.append(table.index(desc))
    return starts, types, table


def _na_bias_kernel(rpb_ref, o_ref, t2_ref, *, table):
    l = pl.program_id(0)
    h = pl.program_id(1)
    nrow = 2 * NA_WIN_H - 1
    ncol = 2 * NA_WIN_W - 1
    c = lax.broadcasted_iota(jnp.int32, (GRID_W, 128), 0)
    lane = lax.broadcasted_iota(jnp.int32, (GRID_W, 128), 1)
    first = lane < GRID_W
    kc = jnp.where(first, lane, lane - GRID_W)
    qstart = jnp.clip(c - NA_WIN_W // 2, 0, GRID_W - NA_WIN_W)
    ok = (kc >= qstart) & (kc < qstart + NA_WIN_W)
    dcol = kc - c + NA_WIN_W - 1
    for a in range(-1, nrow):
        acc = jnp.full((GRID_W, 128), NEG_INF, F32)
        for bi in range(2 * NA_WIN_W - 1):
            base = (l * NA_HEADS + h) * nrow
            lo = rpb_ref[(base + a) * ncol + bi] if a >= 0 else jnp.float32(NEG_INF)
            hi = rpb_ref[(base + a + 1) * ncol + bi] if a + 1 < nrow else jnp.float32(NEG_INF)
            acc = jnp.where(dcol == bi, jnp.where(first, lo, hi), acc)
        t2_ref[a + 1] = jnp.where(ok, acc * LOG2E, NEG_INF)
    for ti, desc in enumerate(table):
        for i in range(NA_QROWS):
            for p in range(NA_KROWS // 2):
                a0 = desc[i * NA_KROWS + 2 * p]
                a1 = desc[i * NA_KROWS + 2 * p + 1]
                if a0 is None and a1 is None:
                    tile = jnp.full((GRID_W, 128), NEG_INF, F32)
                elif a1 is None:
                    tile = jnp.where(first, t2_ref[a0 + 1], NEG_INF)
                elif a0 is None:
                    tile = jnp.where(first, NEG_INF, t2_ref[a1])
                else:
                    tile = t2_ref[a0 + 1]
                o_ref[0, ti, 0, i * GRID_W:(i + 1) * GRID_W, p * 128:(p + 1) * 128] = tile


def _na_bias(na_rpb, table):
    nt = len(table)
    nq = NA_QROWS * GRID_W
    nk = NA_KROWS * GRID_W
    return pl.pallas_call(
        functools.partial(_na_bias_kernel, table=table),
        grid=(DEPTH, NA_HEADS),
        in_specs=[pl.BlockSpec(memory_space=pltpu.SMEM)],
        out_specs=pl.BlockSpec((1, nt, 1, nq, nk), lambda l, h: (l, 0, h, 0, 0)),
        out_shape=jax.ShapeDtypeStruct((DEPTH, nt, NA_HEADS, nq, nk), F32),
        scratch_shapes=[pltpu.VMEM((2 * NA_WIN_H, GRID_W, 128), F32)],
        compiler_params=_cparams(("arbitrary", "arbitrary")),
        name="na_bias",
    )(na_rpb.reshape(-1))


def _na_lat_kernel(l_ref, ks_ref, ty_ref, q_ref, k_ref, v_ref, z_ref, kc_ref, vc_ref, b_ref, o_ref):
    rb = pl.program_id(1)
    nk = NA_KROWS * GRID_W
    ks = pl.multiple_of(ks_ref[rb] * GRID_W, GRID_W)
    q = q_ref[0].astype(BF16)
    kw = k_ref[0, pl.ds(ks, nk), :].astype(BF16)
    vw = v_ref[0, pl.ds(ks, nk), :].astype(BF16)
    kc = kc_ref[0, 0].astype(BF16)
    vc = vc_ref[0, 0].astype(BF16)
    biases = [[b_ref[0, 0, h], None] for h in range(NA_HEADS)]
    o = _attend(q, [kw, kc], [vw, vc], biases,
                _head_masks(256, 64, NA_HEADS, BF16), _head_masks(256, 64, NA_HEADS, F32))
    o_ref[0] = (o * _silu(z_ref[0].astype(F32))).astype(BF16)


def _na_lat(lidx, kstart, ktype, cols_na, cache_k, cache_v, bias):
    B, T, _ = cols_na.shape
    L = cache_k.shape[2]
    nq = NA_QROWS * GRID_W
    nk = NA_KROWS * GRID_W
    nblk = T // nq
    grid_spec = pltpu.PrefetchScalarGridSpec(
        num_scalar_prefetch=3,
        grid=(B, nblk),
        in_specs=[
            pl.BlockSpec((1, nq, 256), lambda b, r, l, ks, ty: (b, r, 0)),
            pl.BlockSpec((1, T, 256), lambda b, r, l, ks, ty: (b, 0, 1)),
            pl.BlockSpec((1, T, 256), lambda b, r, l, ks, ty: (b, 0, 2)),
            pl.BlockSpec((1, nq, 256), lambda b, r, l, ks, ty: (b, r, 3)),
            pl.BlockSpec((1, 1, L, 256), lambda b, r, l, ks, ty: (b, l[0], 0, 0)),
            pl.BlockSpec((1, 1, L, 256), lambda b, r, l, ks, ty: (b, l[0], 0, 0)),
            pl.BlockSpec((1, 1, NA_HEADS, nq, nk), lambda b, r, l, ks, ty: (l[0], ty[r], 0, 0, 0)),
        ],
        out_specs=pl.BlockSpec((1, nq, 256), lambda b, r, l, ks, ty: (b, r, 0)),
    )
    return pl.pallas_call(
        _na_lat_kernel,
        grid_spec=grid_spec,
        out_shape=jax.ShapeDtypeStruct((B, T, 256), BF16),
        compiler_params=_cparams(("arbitrary", "arbitrary")),
        name="na_lat",
    )(lidx, kstart, ktype, cols_na, cols_na, cols_na, cols_na, cache_k, cache_v, bias)


def _wg_heads(q, kparts, vparts, bias, sink_ref, l):
    qm = _head_masks(128, 64, 2, BF16)
    om = _head_masks(128, 64, 2, F32)
    outs = []
    for grp, heads in enumerate(((0, 2), (1, 3))):
        qg = q[:, grp * 128:(grp + 1) * 128]
        sinks = [sink_ref[l, hh] * LOG2E for hh in heads]
        biases = None if bias is None else [[bias] + [None] * (len(kparts) - 1)] * 2
        outs.append(_attend(qg, kparts, vparts, biases, qm, om, sinks))
    return jnp.concatenate(outs, axis=-1)


def _wg_ctx_kernel(l_ref, sink_ref, q_ref, z_ref, k_ref, v_ref, o_ref):
    q = q_ref[0].astype(BF16)
    o = _wg_heads(q, [k_ref[0].astype(BF16)], [v_ref[0].astype(BF16)], None, sink_ref, l_ref[0])
    o_ref[0] = (o * _silu(z_ref[0].astype(F32))).astype(BF16)


def _wg_ctx(lidx, sink, cols_wg):
    B, T, _ = cols_wg.shape
    grid_spec = pltpu.PrefetchScalarGridSpec(
        num_scalar_prefetch=1,
        grid=(B,),
        in_specs=[
            pl.BlockSpec(memory_space=pltpu.SMEM),
            pl.BlockSpec((1, T, 256), lambda b, l: (b, 0, 0)),
            pl.BlockSpec((1, T, 256), lambda b, l: (b, 0, 1)),
            pl.BlockSpec((1, T, 128), lambda b, l: (b, 0, 4)),
            pl.BlockSpec((1, T, 128), lambda b, l: (b, 0, 5)),
        ],
        out_specs=pl.BlockSpec((1, T, 256), lambda b, l: (b, 0, 0)),
    )
    return pl.pallas_call(
        _wg_ctx_kernel,
        grid_spec=grid_spec,
        out_shape=jax.ShapeDtypeStruct((B, T, 256), BF16),
        compiler_params=_cparams(("arbitrary",)),
        name="wg_ctx",
    )(lidx, sink, cols_wg, cols_wg, cols_wg, cols_wg)


def _wg_lat_kernel(l_ref, sink_ref, q_ref, z_ref, k_ref, v_ref, kc_ref, vc_ref,
                   cq_ref, sq_ref, ck_ref, sk_ref, rm_ref, o_ref, *, seq):
    tb = pl.program_id(1)
    t0 = tb * WG_TILE
    k0 = pl.multiple_of(jnp.clip(t0 - WG_WINDOW, 0, seq - WG_KWIN), 128)
    q = q_ref[0]
    q = q.astype(F32) * cq_ref[...] + _dot(q.astype(BF16), rm_ref[...]) * sq_ref[...]
    kwin = k_ref[0, pl.ds(k0, WG_KWIN), :]
    kwin = kwin.astype(F32) * ck_ref[pl.ds(k0, WG_KWIN), :] \
        + _dot(kwin.astype(BF16), rm_ref[0:128, 0:128]) * sk_ref[pl.ds(k0, WG_KWIN), :]
    vwin = v_ref[0, pl.ds(k0, WG_KWIN), :]
    qi = t0 + lax.broadcasted_iota(jnp.int32, (WG_TILE, WG_KWIN), 0)
    kj = k0 + lax.broadcasted_iota(jnp.int32, (WG_TILE, WG_KWIN), 1)
    bias = jnp.where(jnp.abs(qi - kj) <= WG_WINDOW, 0.0, NEG_INF).astype(F32)
    o = _wg_heads(q.astype(BF16),
                  [kwin.astype(BF16), kc_ref[0, 0].astype(BF16)],
                  [vwin.astype(BF16), vc_ref[0, 0].astype(BF16)],
                  bias, sink_ref, l_ref[0])
    o_ref[0] = (o * _silu(z_ref[0].astype(F32))).astype(BF16)


def _wg_lat(lidx, sink, cols_wg, cache_k, cache_v, cq, sq, ck, sk, rm):
    B, T, _ = cols_wg.shape
    L = cache_k.shape[2]
    nt = T // WG_TILE
    grid_spec = pltpu.PrefetchScalarGridSpec(
        num_scalar_prefetch=1,
        grid=(B, nt),
        in_specs=[
            pl.BlockSpec(memory_space=pltpu.SMEM),
            pl.BlockSpec((1, WG_TILE, 256), lambda b, t, l: (b, t, 0)),
            pl.BlockSpec((1, WG_TILE, 256), lambda b, t, l: (b, t, 1)),
            pl.BlockSpec((1, T, 128), lambda b, t, l: (b, 0, 4)),
            pl.BlockSpec((1, T, 128), lambda b, t, l: (b, 0, 5)),
            pl.BlockSpec((1, 1, L, 128), lambda b, t, l: (b, l[0], 0, 0)),
            pl.BlockSpec((1, 1, L, 128), lambda b, t, l: (b, l[0], 0, 0)),
            pl.BlockSpec((WG_TILE, 256), lambda b, t, l: (t, 0)),
            pl.BlockSpec((WG_TILE, 256), lambda b, t, l: (t, 0)),
            pl.BlockSpec((T, 128), lambda b, t, l: (0, 0)),
            pl.BlockSpec((T, 128), lambda b, t, l: (0, 0)),
            pl.BlockSpec((256, 256), lambda b, t, l: (0, 0)),
        ],
        out_specs=pl.BlockSpec((1, WG_TILE, 256), lambda b, t, l: (b, t, 0)),
    )
    return pl.pallas_call(
        functools.partial(_wg_lat_kernel, seq=T),
        grid_spec=grid_spec,
        out_shape=jax.ShapeDtypeStruct((B, T, 256), BF16),
        compiler_params=_cparams(("arbitrary", "arbitrary")),
        name="wg_lat",
    )(lidx, sink, cols_wg, cols_wg, cols_wg, cols_wg, cache_k, cache_v, cq, sq, ck, sk, rm)


def _mla_kv_kernel(l_ref, ckv_ref, kr_ref, krr_ref, g_ref, w_ref, c_ref, s_ref, k_out, v_out, *rest,
                   norm, rope):
    ckv = ckv_ref[0]
    if norm:
        ms = jnp.mean(ckv * ckv, axis=-1, keepdims=True)
        ckv = ckv * lax.rsqrt(ms + EPS) * g_ref[0]
        rest[0][0] = ckv
    kv = _dot(ckv.astype(BF16), w_ref[0])
    kr = kr_ref[0]
    if rope:
        kr = kr * c_ref[...] + krr_ref[0] * s_ref[...]
    for h in range(MLA_HEADS):
        k_out[0, :, 128 * h:128 * h + 128] = (kv[:, 128 * h:128 * h + 128] + kr).astype(BF16)
    v_out[0] = kv[:, 512:768].astype(BF16)


def _mla_kv(lidx, ckv_arr, ckv_blk, kr_arr, kr_blk, krr_arr, krr_blk, g_kv3, w_kvb, c4, s4, norm, rope,
            layered):
    B = ckv_arr.shape[0]
    T = ckv_arr.shape[-2]
    nt = T // TOK_TILE
    if layered:
        def src(blk):
            return pl.BlockSpec((None, 1, TOK_TILE, 128), lambda b, t, l: (b, l[0], t, 0))
    else:
        def src(blk):
            return pl.BlockSpec((1, TOK_TILE, 128), lambda b, t, l, blk=blk: (b, t, blk))
    out_specs = [
        pl.BlockSpec((1, TOK_TILE, 512), lambda b, t, l: (b, t, 0)),
        pl.BlockSpec((1, TOK_TILE, 256), lambda b, t, l: (b, t, 0)),
    ]
    out_shape = [
        jax.ShapeDtypeStruct((B, T, 512), BF16),
        jax.ShapeDtypeStruct((B, T, 256), BF16),
    ]
    if norm:
        out_specs.append(pl.BlockSpec((1, TOK_TILE, 128), lambda b, t, l: (b, t, 0)))
        out_shape.append(jax.ShapeDtypeStruct((B, T, 128), F32))
    grid_spec = pltpu.PrefetchScalarGridSpec(
        num_scalar_prefetch=1,
        grid=(B, nt),
        in_specs=[
            src(ckv_blk), src(kr_blk), src(krr_blk),
            pl.BlockSpec((1, 1, 128), lambda b, t, l: (l[0], 0, 0)),
            pl.BlockSpec((1, 128, 768), lambda b, t, l: (l[0], 0, 0)),
            pl.BlockSpec((TOK_TILE, 128), lambda b, t, l: (t if rope else 0, 0)),
            pl.BlockSpec((TOK_TILE, 128), lambda b, t, l: (t if rope else 0, 0)),
        ],
        out_specs=out_specs,
    )
    return pl.pallas_call(
        functools.partial(_mla_kv_kernel, norm=norm, rope=rope),
        grid_spec=grid_spec,
        out_shape=out_shape,
        compiler_params=_cparams(("arbitrary", "arbitrary")),
        name="mla_kv" + ("_norm" if norm else "") + ("_rope" if rope else ""),
    )(lidx, ckv_arr, kr_arr, krr_arr, g_kv3, w_kvb, c4, s4)


def _mla_attn_kernel(l_ref, qa_ref, z_ref, g_ref, w_ref, c_ref, s_ref, *refs, nparts, rope):
    kparts = [[refs[2 * i][0, :, 128 * h:128 * h + 128] for h in range(MLA_HEADS)] for i in range(nparts)]
    vparts = [refs[2 * i + 1][0] for i in range(nparts)]
    o_ref = refs[2 * nparts]
    qa = qa_ref[0].astype(F32)
    ms = jnp.mean(qa * qa, axis=-1, keepdims=True)
    qn = qa * lax.rsqrt(ms + EPS) * g_ref[0]
    qq = _dot(qn.astype(BF16), w_ref[0])
    q = []
    for h in range(MLA_HEADS):
        qh = qq[:, 128 * h:128 * h + 128]
        if rope:
            qh = qh * c_ref[...] + qq[:, 512 + 128 * h:512 + 128 * h + 128] * s_ref[...]
        q.append(qh.astype(BF16))
    om = _head_masks(256, 64, MLA_HEADS, F32)
    half = TOK_TILE // 2
    o = jnp.concatenate([_attend([qh[i * half:(i + 1) * half] for qh in q], kparts, vparts, None, None, om)
                         for i in range(2)], axis=0)
    o_ref[0] = (o * _silu(z_ref[0].astype(F32))).astype(BF16)


def _attn_ctx_kernel(l_ref, *refs):
    na_in, refs = refs[:4], refs[4:]
    wg_in, refs = refs[:5], refs[5:]
    mla_in, refs = refs[:8], refs[8:]
    o_na, o_wg, o_mla = refs
    _na_ctx_kernel(*na_in, o_na)
    _wg_ctx_kernel(l_ref, *wg_in, o_wg)
    _mla_attn_kernel(l_ref, *mla_in, o_mla, nparts=1, rope=False)


def _attn_ctx(lidx, cols_na, sink, cols_wg, cols_mla, g_q3, w_qb, c4, s4, kc, vc):
    B, T, _ = cols_na.shape

    def col(width, j):
        return pl.BlockSpec((1, T, width), lambda b, l, j=j: (b, 0, j))

    in_specs = [
        col(256, 0), col(256, 1), col(256, 2), col(256, 3),
        pl.BlockSpec(memory_space=pltpu.SMEM), col(256, 0), col(256, 1), col(128, 4), col(128, 5),
        col(256, 0), col(256, 1),
        pl.BlockSpec((1, 1, 256), lambda b, l: (l[0], 0, 0)),
        pl.BlockSpec((1, 256, 1024), lambda b, l: (l[0], 0, 0)),
        pl.BlockSpec((TOK_TILE, 128), lambda b, l: (0, 0)),
        pl.BlockSpec((TOK_TILE, 128), lambda b, l: (0, 0)),
        col(512, 0), col(256, 0),
    ]
    out = pl.BlockSpec((1, T, 256), lambda b, l: (b, 0, 0))
    grid_spec = pltpu.PrefetchScalarGridSpec(num_scalar_prefetch=1, grid=(B,), in_specs=in_specs,
                                             out_specs=[out, out, out])
    return pl.pallas_call(
        _attn_ctx_kernel,
        grid_spec=grid_spec,
        out_shape=[jax.ShapeDtypeStruct((B, T, 256), BF16)] * 3,
        compiler_params=_cparams(("arbitrary",)),
        name="attn_ctx",
    )(lidx, cols_na, cols_na, cols_na, cols_na, sink, cols_wg, cols_wg, cols_wg, cols_wg,
      cols_mla, cols_mla, g_q3, w_qb, c4, s4, kc, vc)


def _attn_lat_kernel(l_ref, ks_ref, ty_ref, *refs, seq):
    na_in, refs = refs[:7], refs[7:]
    wg_in, refs = refs[:12], refs[12:]
    mla_in, refs = refs[:10], refs[10:]
    o_na, o_wg, o_mla = refs
    _na_lat_kernel(l_ref, ks_ref, ty_ref, *na_in, o_na)
    _wg_lat_kernel(l_ref, *wg_in, o_wg, seq=seq)
    _mla_attn_kernel(l_ref, *mla_in, o_mla, nparts=2, rope=True)


def _attn_lat(lidx, kstart, ktype, cols_na, cna_k, cna_v, bias, sink, cols_wg, cwg_k, cwg_v, cq, sq, ck, sk, rm,
              cols_mla, g_q3, w_qb, c4, s4, kl, vl, kx, vx):
    B, T, _ = cols_na.shape
    L = cna_k.shape[2]
    nq = NA_QROWS * GRID_W
    nk = NA_KROWS * GRID_W
    assert nq == TOK_TILE == WG_TILE
    nt = T // TOK_TILE

    def tok(width, col):
        return pl.BlockSpec((1, TOK_TILE, width), lambda b, t, l, ks, ty, col=col: (b, t, col))

    def full(rows, width, col):
        return pl.BlockSpec((1, rows, width), lambda b, t, l, ks, ty, col=col: (b, 0, col))

    def layer(rows, width):
        return pl.BlockSpec((1, 1, rows, width), lambda b, t, l, ks, ty: (b, l[0], 0, 0))

    def table(width):
        return pl.BlockSpec((TOK_TILE, width), lambda b, t, l, ks, ty: (t, 0))

    in_specs = [
        tok(256, 0), full(T, 256, 1), full(T, 256, 2), tok(256, 3), layer(L, 256), layer(L, 256),
        pl.BlockSpec((1, 1, NA_HEADS, nq, nk), lambda b, t, l, ks, ty: (l[0], ty[t], 0, 0, 0)),
        pl.BlockSpec(memory_space=pltpu.SMEM),
        tok(256, 0), tok(256, 1), full(T, 128, 4), full(T, 128, 5), layer(L, 128), layer(L, 128),
        table(256), table(256),
        pl.BlockSpec((T, 128), lambda b, t, l, ks, ty: (0, 0)),
        pl.BlockSpec((T, 128), lambda b, t, l, ks, ty: (0, 0)),
        pl.BlockSpec((256, 256), lambda b, t, l, ks, ty: (0, 0)),
        tok(256, 0), tok(256, 1),
        pl.BlockSpec((1, 1, 256), lambda b, t, l, ks, ty: (l[0], 0, 0)),
        pl.BlockSpec((1, 256, 1024), lambda b, t, l, ks, ty: (l[0], 0, 0)),
        table(128), table(128),
        full(kl.shape[1], 512, 0), full(vl.shape[1], 256, 0), full(kx.shape[1], 512, 0), full(vx.shape[1], 256, 0),
    ]
    out = pl.BlockSpec((1, TOK_TILE, 256), lambda b, t, l, ks, ty: (b, t, 0))
    grid_spec = pltpu.PrefetchScalarGridSpec(
        num_scalar_prefetch=3,
        grid=(B, nt),
        in_specs=in_specs,
        out_specs=[out, out, out],
    )
    return pl.pallas_call(
        functools.partial(_attn_lat_kernel, seq=T),
        grid_spec=grid_spec,
        out_shape=[jax.ShapeDtypeStruct((B, T, 256), BF16)] * 3,
        compiler_params=_cparams(("arbitrary", "arbitrary")),
        name="attn_lat",
    )(lidx, kstart, ktype,
      cols_na, cols_na, cols_na, cols_na, cna_k, cna_v, bias,
      sink, cols_wg, cols_wg, cols_wg, cols_wg, cwg_k, cwg_v, cq, sq, ck, sk, rm,
      cols_mla, cols_mla, g_q3, w_qb, c4, s4, kl, vl, kx, vx)


def _mla_attn(lidx, cols_mla, g_q3, w_qb, c4, s4, kv_parts, rope):
    B, T, _ = cols_mla.shape
    nt = T // TOK_TILE
    in_specs = [
        pl.BlockSpec((1, TOK_TILE, 256), lambda b, t, l: (b, t, 0)),
        pl.BlockSpec((1, TOK_TILE, 256), lambda b, t, l: (b, t, 1)),
        pl.BlockSpec((1, 1, 256), lambda b, t, l: (l[0], 0, 0)),
        pl.BlockSpec((1, 256, 1024), lambda b, t, l: (l[0], 0, 0)),
        pl.BlockSpec((TOK_TILE, 128), lambda b, t, l: (t if rope else 0, 0)),
        pl.BlockSpec((TOK_TILE, 128), lambda b, t, l: (t if rope else 0, 0)),
    ]
    args = [lidx, cols_mla, cols_mla, g_q3, w_qb, c4, s4]
    for kk, vv in kv_parts:
        in_specs.append(pl.BlockSpec((1, kk.shape[1], 512), lambda b, t, l: (b, 0, 0)))
        in_specs.append(pl.BlockSpec((1, vv.shape[1], 256), lambda b, t, l: (b, 0, 0)))
        args += [kk, vv]
    grid_spec = pltpu.PrefetchScalarGridSpec(
        num_scalar_prefetch=1,
        grid=(B, nt),
        in_specs=in_specs,
        out_specs=pl.BlockSpec((1, TOK_TILE, 256), lambda b, t, l: (b, t, 0)),
    )
    return pl.pallas_call(
        functools.partial(_mla_attn_kernel, nparts=len(kv_parts), rope=rope),
        grid_spec=grid_spec,
        out_shape=jax.ShapeDtypeStruct((B, T, 256), BF16),
        compiler_params=_cparams(("arbitrary", "arbitrary")),
        name="mla_attn_lat" if rope else "mla_attn_ctx",
    )(*args)


def _hg_blocks(rev, include_diag):
    C, S = HG_CHUNK, HG_SUB
    out = []
    for g0 in range(0, C, S):
        if rev:
            lo, hi = (g0 if include_diag else g0 + S), C
        else:
            lo, hi = 0, (g0 + S if include_diag else g0)
        if hi > lo:
            out.append((g0, lo, hi))
    return out


def _hg_mask(rev, include_diag):
    blocks = _hg_blocks(rev, include_diag)
    S = HG_SUB
    nrow = len(blocks) * HG_HEADS * S
    ncol = 128 * -(-sum(hi - lo for _, lo, hi in blocks) // 128)
    r = lax.broadcasted_iota(jnp.int32, (nrow, ncol), 0)
    cidx = lax.broadcasted_iota(jnp.int32, (nrow, ncol), 1)
    rblk = r // (HG_HEADS * S)
    cblk = jnp.full_like(cidx, len(blocks))
    s = jnp.zeros_like(cidx)
    t = r % S
    start = 0
    for k_, (g0, lo, hi) in enumerate(blocks):
        inb = (cidx >= start) & (cidx < start + hi - lo)
        cblk = jnp.where(inb, k_, cblk)
        s = jnp.where(inb, cidx - start + lo, s)
        t = jnp.where(rblk == k_, t + g0, t)
        start += hi - lo
    ok = rblk == cblk
    if include_diag:
        ok = ok & ((s >= t) if rev else (s <= t))
    return ok.astype(F32)


def _hg_stacked(qs, kk, v, b2, rev, include_diag, mask, hm):
    C, S = HG_CHUNK, HG_SUB
    blocks = _hg_blocks(rev, include_diag)
    qx, kd_rows, v_rows = [], [], []
    for g0, lo, hi in blocks:
        has_ref = (g0 + S < C) if rev else (g0 > 0)
        eq = b2[g0:g0 + S, :]
        es = -b2[lo:hi, :]
        if has_ref:
            ct = b2[g0 + S:g0 + S + 1, :] if rev else b2[g0 - 1:g0, :]
            eq = eq - ct
            es = es + ct
        qd = qs[g0:g0 + S, :] * jnp.exp2(eq)
        qx += [qd * m for m in hm]
        kd_rows.append(kk[lo:hi, :] * jnp.exp2(es))
        v_rows.append(v[lo:hi, :])
    pad = mask.shape[1] - sum(hi - lo for _, lo, hi in blocks)
    if pad:
        kd_rows.append(jnp.zeros((pad, 256), F32))
        v_rows.append(jnp.zeros((pad, 256), F32))
    kd_all = jnp.concatenate(kd_rows, axis=0).astype(BF16)
    v_all = jnp.concatenate(v_rows, axis=0).astype(BF16)
    att = _dot_nt(jnp.concatenate(qx, axis=0).astype(BF16), kd_all)
    att = jnp.where(mask > 0.5, att, 0.0)
    return _dot(att.astype(BF16), v_all), blocks


def _hg_gates(lb, rev, q_ref, f_ref, qs_sc, lf_sc, kk_sc):
    qs_sc[...] = _silu(q_ref[0])
    fz = f_ref[0]
    e = jnp.exp(-jnp.abs(fz))
    big = 1.0 / (1.0 + e)
    small_ = e * big
    pos = fz >= 0.0
    f = lb + (1.0 - lb) * jnp.where(pos, big, small_)
    lf = jnp.log(jnp.maximum(f, F_FLOOR))
    kk_sc[...] = (1.0 - lb) * jnp.where(pos, small_, big)
    rr = lax.broadcasted_iota(jnp.int32, (TOK_TILE, TOK_TILE), 0)
    cc = lax.broadcasted_iota(jnp.int32, (TOK_TILE, TOK_TILE), 1)
    tri = ((rr // HG_CHUNK == cc // HG_CHUNK) & ((cc >= rr) if rev else (cc <= rr))).astype(BF16)
    hi = lf.astype(BF16)
    r1 = lf - hi.astype(F32)
    mid = r1.astype(BF16)
    lo = (r1 - mid.astype(F32)).astype(BF16)
    lf_sc[...] = _dot(tri, hi) + _dot(tri, mid) + _dot(tri, lo)
    nblk = TOK_TILE // HG_SUB
    r = lax.broadcasted_iota(jnp.int32, (nblk, TOK_TILE), 0)
    cidx = lax.broadcasted_iota(jnp.int32, (nblk, TOK_TILE), 1)
    span = _dot((cidx // HG_SUB == r).astype(BF16), (lf * (-LOG2E)).astype(BF16))
    return jnp.max(span) <= HG_FAST_SPAN


def _hg_chunk(r0, rev, fast, st, qs_sc, lf_sc, kk_sc, i_ref, a_sc, v_sc, o_sc, seg, bdmask, mask):
    C = HG_CHUNK
    S = HG_SUB
    qs = qs_sc[pl.ds(r0, C), :]
    b = lf_sc[pl.ds(r0, C), :]
    kk = kk_sc[pl.ds(r0, C), :]
    v = i_ref[0, pl.ds(r0, C), :]
    b2 = b * LOG2E
    total = b[0:1, :] if rev else b[C - 1:C, :]
    inter = _dot_nt((qs * jnp.exp2(b2)).astype(BF16), st.astype(BF16))
    hm = _head_masks(256, HG_DK, HG_HEADS, F32)
    off, blocks = _hg_stacked(qs, kk, v, b2, rev, fast, mask, hm)

    def head_sum(k_, i):
        acc = None
        for h in range(HG_HEADS):
            r_ = k_ * HG_HEADS * S + h * S + 8 * i
            t_ = off[r_:r_ + 8, :] * hm[h]
            acc = t_ if acc is None else acc + t_
        return acc

    if fast:
        tiles = []
        for k_, (g0, _, _) in enumerate(blocks):
            for i in range(S // 8):
                tiles.append(inter[g0 + 8 * i:g0 + 8 * i + 8, :] + head_sum(k_, i))
        o = jnp.concatenate(tiles, axis=0)
    else:
        a_sc[...] = jnp.log2(kk) - b2
        v_sc[...] = v
        o_sc[...] = inter
        for k_, (g0, _, _) in enumerate(blocks):
            for i in range(S // 8):
                o_sc[g0 + 8 * i:g0 + 8 * i + 8, :] = o_sc[g0 + 8 * i:g0 + 8 * i + 8, :] + head_sum(k_, i)
        sub = lax.broadcasted_iota(jnp.int32, (8, 256), 0)
        for g0 in range(0, C, S):
            parts, spans = [], []
            for s in range(g0, g0 + S):
                t0 = (s // 8) * 8
                arow = a_sc[s:s + 1, :]
                keep = (sub <= s - t0) if rev else (sub >= s - t0)
                rdiag = qs[t0:t0 + 8, :] * jnp.exp2(jnp.where(keep, b2[t0:t0 + 8, :] + arow, NEG_INF))
                rest_lo, rest_hi = (g0, t0) if rev else (t0 + 8, g0 + S)
                if rest_hi > rest_lo:
                    rrest = qs[rest_lo:rest_hi, :] * jnp.exp2(b2[rest_lo:rest_hi, :] + arow)
                    parts += [rrest, rdiag] if rev else [rdiag, rrest]
                else:
                    parts.append(rdiag)
                spans.append((min(rest_lo, t0), max(rest_hi, t0 + 8)) if rest_hi > rest_lo else (t0, t0 + 8))
            res = _dot(jnp.concatenate(parts, axis=0).astype(BF16), seg)
            acc = [o_sc[g0 + 8 * i:g0 + 8 * i + 8, :] for i in range(S // 8)]
            pos = 0
            for s, (lo_r, hi_r) in zip(range(g0, g0 + S), spans):
                vrow = v_sc[s:s + 1, :]
                for i in range((hi_r - lo_r) // 8):
                    ti = (lo_r - g0) // 8 + i
                    acc[ti] = acc[ti] + res[pos + 8 * i:pos + 8 * i + 8, :] * vrow
                pos += hi_r - lo_r
            for i in range(S // 8):
                o_sc[g0 + 8 * i:g0 + 8 * i + 8, :] = acc[i]
        o = o_sc[...]
    kd = (kk * jnp.exp(total - b)).astype(BF16)
    return o, jnp.exp(total) * st + bdmask * _dot_tn(v.astype(BF16), kd)


def _hg_kernel(l_ref, q_ref, f_ref, i_ref, og_ref, z_ref, lbl_ref, g_ref, s0_ref, y_ref, sto_ref,
               st_ref, oacc_ref, qs_sc, lf_sc, kk_sc, a_sc, v_sc, o_sc, *, nb):
    j = pl.program_id(1)
    l = l_ref[0]
    C = HG_CHUNK
    ncb = TOK_TILE // C

    @pl.when((j == 0) | (j == nb))
    def _():
        st_ref[...] = s0_ref[0, 0]

    lg = lbl_ref[...]
    mx = lg[0]
    for m in range(1, DEPTH):
        mx = jnp.maximum(mx, lg[m])
    ex = [jnp.exp(lg[m] - mx) for m in range(DEPTH)]
    tot = ex[0]
    for m in range(1, DEPTH):
        tot = tot + ex[m]
    lbs = jnp.zeros_like(tot)
    for m in range(1, DEPTH):
        lbs = lbs + jnp.where(m <= l, ex[m] / tot, 0.0)

    seg = _seg_matrix(256, HG_DK)
    seg_b = seg.astype(BF16)
    bdmask = seg.astype(F32)

    def run(rev, emit):
        lb = lbs[1:2, :] if rev else lbs[0:1, :]
        small = _hg_gates(lb, rev, q_ref, f_ref, qs_sc, lf_sc, kk_sc)
        common = (qs_sc, lf_sc, kk_sc, i_ref, a_sc, v_sc, o_sc, seg_b, bdmask)

        @pl.when(small)
        def _():
            mask = _hg_mask(rev, True)
            st = st_ref[...]
            for ci in range(ncb):
                r0 = (ncb - 1 - ci if rev else ci) * C
                o, st = _hg_chunk(r0, rev, True, st, *common, mask)
                emit(r0, o)
            st_ref[...] = st

        @pl.when(jnp.logical_not(small))
        def _():
            mask = _hg_mask(rev, False)

            def body(ci, carry):
                r0 = pl.multiple_of((ncb - 1 - ci if rev else ci) * C, C)
                o, st = _hg_chunk(r0, rev, False, st_ref[...], *common, mask)
                st_ref[...] = st
                emit(r0, o)
                return carry

            lax.fori_loop(0, ncb, body, 0)

    @pl.when(j < nb)
    def _():
        def emit(r0, o):
            oacc_ref[pl.ds(pl.multiple_of(j * TOK_TILE + r0, C), C), :] = o

        run(False, emit)

    @pl.when(j >= nb)
    def _():
        blk = 2 * nb - 1 - j

        def emit(r0, o):
            o = oacc_ref[pl.ds(pl.multiple_of(blk * TOK_TILE + r0, C), C), :] + o
            ms = _seg_mean(o * o, seg_b, HG_DK)
            on = o * lax.rsqrt(ms + EPS) * g_ref[0]
            on = on * _sigmoid(og_ref[0, pl.ds(r0, C), :])
            y_ref[0, pl.ds(r0, C), :] = (on * _silu(z_ref[0, pl.ds(r0, C), :])).astype(BF16)

        run(True, emit)

    sto_ref[0, 0] = st_ref[...]


def _hgrn(lidx, cols_hg, lb_logits, g3, s0, s0_per_batch):
    B, T, _ = cols_hg.shape
    nb = T // TOK_TILE

    def blk(j):
        return jnp.where(j < nb, j, 2 * nb - 1 - j)

    def dirn(j):
        return jnp.where(j < nb, 0, 1)

    def col(jcol):
        return pl.BlockSpec((1, TOK_TILE, 256), lambda b, j, l, jcol=jcol: (b, blk(j), jcol))

    grid_spec = pltpu.PrefetchScalarGridSpec(
        num_scalar_prefetch=1,
        grid=(B, 2 * nb),
        in_specs=[
            col(0),
            pl.BlockSpec((1, TOK_TILE, 256), lambda b, j, l: (b, blk(j), 1 + dirn(j))),
            col(3), col(4), col(5),
            pl.BlockSpec((DEPTH, 2, 256), lambda b, j, l: (0, 0, 0)),
            pl.BlockSpec((1, 1, 256), lambda b, j, l: (l[0], 0, 0)),
            pl.BlockSpec((1, 1, 256, 256),
                         (lambda b, j, l: (b, dirn(j), 0, 0)) if s0_per_batch else (lambda b, j, l: (0, 0, 0, 0))),
        ],
        out_specs=[
            pl.BlockSpec((1, TOK_TILE, 256), lambda b, j, l: (b, jnp.where(j < nb, nb - 1, 2 * nb - 1 - j), 0)),
            pl.BlockSpec((1, 1, 256, 256), lambda b, j, l: (b, dirn(j), 0, 0)),
        ],
        scratch_shapes=[
            pltpu.VMEM((256, 256), F32),
            pltpu.VMEM((T, 256), F32),
            pltpu.VMEM((TOK_TILE, 256), F32),
            pltpu.VMEM((TOK_TILE, 256), F32),
            pltpu.VMEM((TOK_TILE, 256), F32),
            pltpu.VMEM((HG_CHUNK, 256), F32),
            pltpu.VMEM((HG_CHUNK, 256), F32),
            pltpu.VMEM((HG_CHUNK, 256), F32),
        ],
    )
    return pl.pallas_call(
        functools.partial(_hg_kernel, nb=nb),
        grid_spec=grid_spec,
        out_shape=[
            jax.ShapeDtypeStruct((B, T, 256), BF16),
            jax.ShapeDtypeStruct((B, 2, 256, 256), F32),
        ],
        compiler_params=_cparams(("arbitrary", "arbitrary")),
        name="hgrn",
    )(lidx, cols_hg, cols_hg, cols_hg, cols_hg, cols_hg, lb_logits, g3, s0)


def _rot_cols(w, nheads):
    shp = w.shape
    r = shp[-1] // nheads
    w4 = w.reshape(shp[:-1] + (nheads, 2, r // 2))
    return jnp.concatenate([-w4[..., 1, :], w4[..., 0, :]], axis=-1).reshape(shp)


def _perm_heads(w, nheads, order):
    shp = w.shape
    w3 = w.reshape(shp[:-1] + (nheads, shp[-1] // nheads))
    return w3[..., list(order), :].reshape(shp)


_WG_ORDER = (0, 2, 1, 3)


def _mla_rope_slot(a):
    pad = [(0, 0)] * (a.ndim - 1) + [(MLA_NOPE, 128 - MLA_NOPE - MLA_ROPE)]
    return jnp.pad(a, pad)


def _prep_w_in(w_in):
    def c(name):
        lo, hi = _IN_OFF[name]
        return w_in[:, :, lo:hi]

    kva = c('mla_kva')
    lo = _IN_OFF['na_k'][0]
    hi = _IN_OFF['hg_z'][1]
    parts = [
        c('na_q') * (HEAD_DIM ** -0.5 * LOG2E),
        w_in[:, :, lo:hi],
        c('mla_qa'), c('mla_z'), kva[:, :, :MLA_KV_RANK], _mla_rope_slot(kva[:, :, MLA_KV_RANK:]),
        _perm_heads(c('wg_q'), WG_HEADS, _WG_ORDER) * (HEAD_DIM ** -0.5 * LOG2E),
        _perm_heads(c('wg_z'), WG_HEADS, _WG_ORDER),
        c('wg_k'), c('wg_v'),
    ]
    return jnp.concatenate([p.astype(BF16) for p in parts], axis=-1), c('merge').astype(BF16)


def _rot_matrix(width, head):
    i = np.arange(width)[:, None]
    j = np.arange(width)[None, :]
    half = head // 2
    first = (j % head) < half
    m = np.where(first & (i == j + half), -1.0, 0.0) + np.where(~first & (i == j - half), 1.0, 0.0)
    return jnp.asarray(m, BF16)


def _rope_tables(n_tok, rot_dim, reps):
    t = np.arange(n_tok)
    row = (t // GRID_W).astype(np.float32)
    col = (t % GRID_W).astype(np.float32)
    nf = rot_dim // 4
    inv = jnp.asarray(ROPE_BASE, F32) ** (-jnp.arange(nf, dtype=F32) / nf)
    ang = jnp.concatenate([jnp.asarray(row)[:, None] * inv, jnp.asarray(col)[:, None] * inv], axis=-1)
    cos, sin = jnp.cos(ang), jnp.sin(ang)
    return (jnp.tile(jnp.concatenate([cos, cos], axis=-1), (1, reps)),
            jnp.tile(jnp.concatenate([sin, sin], axis=-1), (1, reps)))


def _state_to_blockdiag(st):
    B = st.shape[0]
    stt = jnp.swapaxes(st, -1, -2)
    eye = jnp.eye(HG_HEADS, dtype=st.dtype)
    bd = stt[:, :, :, :, None, :] * eye[None, None, :, None, :, None]
    return bd.reshape(B, 2, HG_HEADS * HG_DK, HG_HEADS * HG_DK)


def _blockdiag_to_state(bd):
    B = bd.shape[0]
    b6 = bd.reshape(B, 2, HG_HEADS, HG_DK, HG_HEADS, HG_DK)
    diag = jnp.stack([b6[:, :, h, :, h, :] for h in range(HG_HEADS)], axis=2)
    return jnp.swapaxes(diag, -1, -2)


def kernel(x_prompt, x_sample, cache_na_k, cache_na_v, state_hgrn, cache_mla_ckv, cache_mla_krope, cache_wg_k, cache_wg_v, c, c_ctx, norm_g, w_mod, b_mod, w_in, na_rpb, hg_lb_logits, hg_norm_g, mla_q_norm_g, mla_kv_norm_g, mla_w_qb, mla_w_kvb, wg_sink, w_branch, w_out, final_g):
    Bp, Tp, _ = x_prompt.shape
    Bd, Td, _ = x_sample.shape
    L = cache_na_k.shape[2]
    rows = Td // GRID_W
    assert Tp % TOK_TILE == 0 and Td % TOK_TILE == 0 and L % TOK_TILE == 0
    assert rows % NA_QROWS == 0 and rows >= NA_KROWS and Td >= WG_KWIN

    w_proj, w_merge = _prep_w_in(w_in)
    wb = w_branch.astype(BF16)
    wb = wb.at[:, 3].set(_perm_heads(jnp.swapaxes(wb[:, 3], -1, -2), WG_HEADS, _WG_ORDER).swapaxes(-1, -2))
    wo = w_out.astype(BF16)
    qb = mla_w_qb.reshape(DEPTH, MLA_Q_RANK, MLA_HEADS, MLA_NOPE + MLA_ROPE)
    qb_g1 = jnp.pad(qb, ((0, 0), (0, 0), (0, 0), (0, 128 - MLA_NOPE - MLA_ROPE)))
    qb_g2 = _mla_rope_slot(_rot_cols(qb[..., MLA_NOPE:], 1))
    w_qb = jnp.concatenate([qb_g1.reshape(DEPTH, MLA_Q_RANK, -1), qb_g2.reshape(DEPTH, MLA_Q_RANK, -1)],
                           axis=-1)
    w_qb = (w_qb * ((MLA_NOPE + MLA_ROPE) ** -0.5 * LOG2E)).astype(BF16)
    kvb = mla_w_kvb.reshape(DEPTH, MLA_KV_RANK, MLA_HEADS, 2 * MLA_NOPE)
    kvb_k = jnp.pad(kvb[..., :MLA_NOPE], ((0, 0), (0, 0), (0, 0), (0, 128 - MLA_NOPE)))
    w_kvb = jnp.concatenate([kvb_k.reshape(DEPTH, MLA_KV_RANK, -1),
                             kvb[..., MLA_NOPE:].reshape(DEPTH, MLA_KV_RANK, -1)], axis=-1).astype(BF16)
    norm_g3 = norm_g.reshape(DEPTH, 1, D_MODEL)
    g_q3 = mla_q_norm_g.reshape(DEPTH, 1, MLA_Q_RANK)
    g_kv3 = mla_kv_norm_g.reshape(DEPTH, 1, MLA_KV_RANK)
    hg_g3 = jnp.tile(hg_norm_g, (1, HG_HEADS)).reshape(DEPTH, 1, HG_HEADS * HG_DK)
    final_g2 = final_g.reshape(1, D_MODEL)
    cq, sq = _rope_tables(Td, HEAD_DIM, WG_HEADS)
    ck, sk = cq[:, :128], sq[:, :128]
    c1, s1 = _rope_tables(Td, MLA_ROPE, 1)
    c4 = jnp.concatenate([jnp.ones((Td, MLA_NOPE), F32), c1, jnp.zeros((Td, 32), F32)], axis=-1)
    s4 = _mla_rope_slot(s1)
    rm_wg = _rot_matrix(WG_HEADS * HEAD_DIM, HEAD_DIM)
    rm_mla = jnp.zeros((128, 128), BF16).at[MLA_NOPE:MLA_NOPE + MLA_ROPE, MLA_NOPE:MLA_NOPE + MLA_ROPE].set(
        _rot_matrix(MLA_ROPE, MLA_ROPE))

    nrow = 8 * ((1 + Bd + 7) // 8)
    cond = jnp.zeros((nrow, D_MODEL), F32).at[0].set(c_ctx).at[1:1 + Bd].set(c)
    mod4 = _modulation(cond, w_mod, b_mod).reshape(DEPTH, nrow, 1, 3 * D_MODEL)

    kstart, ktype, table = _na_row_types(rows)
    na_bias = _na_bias(na_rpb, table)
    kstart = jnp.asarray(kstart, jnp.int32)
    ktype = jnp.asarray(ktype, jnp.int32)

    cna_k = cache_na_k.reshape(Bd, DEPTH, L, NA_HEADS * HEAD_DIM)
    cna_v = cache_na_v.reshape(Bd, DEPTH, L, NA_HEADS * HEAD_DIM)
    cwg_k = cache_wg_k.reshape(Bd, DEPTH, L, WG_KV_HEADS * HEAD_DIM)
    cwg_v = cache_wg_v.reshape(Bd, DEPTH, L, WG_KV_HEADS * HEAD_DIM)
    ckr4 = _mla_rope_slot(cache_mla_krope)
    st_bd = _state_to_blockdiag(state_hgrn.reshape(Bd * DEPTH, 2, HG_HEADS, HG_DK, HG_DK)).reshape(
        Bd, DEPTH, 2, 256, 256)
    st_zero = jnp.zeros((1, 2, 256, 256), F32)

    hp, hs = x_prompt, x_sample
    states = []
    tok_caches = None
    for layer in range(DEPTH):
        lidx = jnp.full((1,), layer, jnp.int32)
        final = layer == DEPTH - 1

        flat = (1, Bp * Tp)
        res = _in_proj(lidx, hp.reshape(flat + (D_MODEL,)), mod4, 0, False, norm_g3, w_proj, g_kv3, w_kvb, rm_mla,
                       c4, s4, False, caches=tok_caches, cache_shape=(Bp, DEPTH, Tp))
        na, hg, mla, wg, kc_, vc_ = (a.reshape(Bp, Tp, a.shape[-1]) for a in res[:6])
        tok_caches = res[6:]
        y_hg, st_out = _hgrn(lidx, hg, hg_lb_logits, hg_g3, st_zero, False)
        y_na, y_wg, y_mla = _attn_ctx(lidx, na, wg_sink, wg, mla, g_q3, w_qb, c4, s4, kc_, vc_)
        ys = tuple(y.reshape(flat + (BRANCH_W,)) for y in (y_na, y_hg, y_mla, y_wg))
        hp = _post(lidx, hp.reshape(flat + (D_MODEL,)), mod4, 0, False, norm_g3, ys, w_merge, wb, wo, final_g2,
                   final).reshape(Bp, Tp, D_MODEL)
        states.append(st_out)

        na, hg, mla, wg, kl, vl = _in_proj(lidx, hs, mod4, 1, True, norm_g3, w_proj, g_kv3, w_kvb, rm_mla, c4, s4,
                                           True)
        y_hg, _ = _hgrn(lidx, hg, hg_lb_logits, hg_g3, st_bd[:, layer], True)
        kx, vx = _mla_kv(lidx, cache_mla_ckv, 0, ckr4, 0, ckr4, 0, g_kv3, w_kvb, c4, s4, False, False, True)
        y_na, y_wg, y_mla = _attn_lat(lidx, kstart, ktype, na, cna_k, cna_v, na_bias, wg_sink, wg, cwg_k, cwg_v,
                                      cq, sq, ck, sk, rm_wg, mla, g_q3, w_qb, c4, s4, kl, vl, kx, vx)
        hs = _post(lidx, hs, mod4, 1, True, norm_g3, (y_na, y_hg, y_mla, y_wg), w_merge, wb, wo, final_g2, final)

    new_state = _blockdiag_to_state(jnp.stack(states, axis=1).reshape(Bp * DEPTH, 2, 256, 256)).reshape(
        Bp, DEPTH, 2, HG_HEADS, HG_DK, HG_DK)
    na_k, na_v, ckv_c, kr_c, wg_k, wg_v = tok_caches
    return (hp, hs,
            na_k.reshape(Bp, DEPTH, Tp, NA_HEADS, HEAD_DIM), na_v.reshape(Bp, DEPTH, Tp, NA_HEADS, HEAD_DIM),
            new_state, ckv_c, kr_c,
            wg_k.reshape(Bp, DEPTH, Tp, WG_KV_HEADS, HEAD_DIM), wg_v.reshape(Bp, DEPTH, Tp, WG_KV_HEADS, HEAD_DIM))
```

```python
import functools

import jax
import jax.numpy as jnp
import numpy as np
from jax import lax
from jax.experimental import pallas as pl
from jax.experimental.pallas import tpu as pltpu

F32 = jnp.float32
BF16 = jnp.bfloat16

D_MODEL = 1024
DEPTH = 4
GRID_W = 64
HEAD_DIM = 64
BRANCH_W = 256
N_BRANCH = 4
NA_HEADS = 4
NA_WIN_H = 8
NA_WIN_W = 16
HG_HEADS = 4
HG_DK = 64
HG_CHUNK = 64
HG_SUB = 32
HG_PAIR = 2
HG_FAST_SPAN = 100.0
LOG2E = 1.4426950408889634
F_FLOOR = 1e-30
MLA_HEADS = 4
MLA_Q_RANK = 256
MLA_KV_RANK = 128
MLA_NOPE = 64
MLA_ROPE = 32
WG_HEADS = 4
WG_KV_HEADS = 2
WG_WINDOW = 128
ROPE_BASE = 10000.0
EPS = 1e-6
NEG_INF = -1e30

_IN_LAYOUT = (
    ('na_q', 256), ('na_k', 256), ('na_v', 256), ('na_z', 256),
    ('hg_q', 256), ('hg_ff', 256), ('hg_fb', 256), ('hg_i', 256), ('hg_og', 256), ('hg_z', 256),
    ('mla_qa', 256), ('mla_kva', 160), ('mla_z', 256),
    ('wg_q', 256), ('wg_k', 128), ('wg_v', 128), ('wg_z', 256),
    ('merge', 4096),
)
_IN_OFF = {}
_o = 0
for _n, _w in _IN_LAYOUT:
    _IN_OFF[_n] = (_o, _o + _w)
    _o += _w

NA_W = 1024
HG_W = 1536
MLA_W = 768
WG_W = 768
PROJ_W = NA_W + HG_W + MLA_W + WG_W

TOK_TILE = 256
PROJ_TILE = 512
NA_QROWS = 4
NA_KROWS = 12
WG_TILE = 256
WG_KWIN = WG_TILE + 2 * WG_WINDOW
VMEM_LIMIT = 56 * 1024 * 1024


def _cparams(sem):
    return pltpu.CompilerParams(dimension_semantics=sem, vmem_limit_bytes=VMEM_LIMIT)


def _dot(a, b):
    return jnp.dot(a, b, preferred_element_type=F32)


def _dot_nt(a, b):
    return lax.dot_general(a, b, (((1,), (1,)), ((), ())), preferred_element_type=F32)


def _dot_tn(a, b):
    return lax.dot_general(a, b, (((0,), (0,)), ((), ())), preferred_element_type=F32)


def _sigmoid(x):
    return 1.0 / (1.0 + jnp.exp(-x))


def _silu(x):
    return x * _sigmoid(x)


def _lane_mask(width, lo, hi, dtype):
    lane = lax.broadcasted_iota(jnp.int32, (1, width), 1)
    return ((lane >= lo) & (lane < hi)).astype(dtype)


def _seg_matrix(width, seg):
    r = lax.broadcasted_iota(jnp.int32, (width, width), 0) // seg
    c = lax.broadcasted_iota(jnp.int32, (width, width), 1) // seg
    return r == c


def _seg_mean(x2, seg_bf16, seg):
    hi = x2.astype(BF16)
    lo = (x2 - hi.astype(F32)).astype(BF16)
    return (_dot(hi, seg_bf16) + _dot(lo, seg_bf16)) * (1.0 / seg)


def _mod_kernel(c_ref, w_ref, b_ref, o_ref):
    s = _silu(c_ref[...])
    o_ref[0] = _dot(s.astype(BF16), w_ref[0].astype(BF16)) + b_ref[0]


def _modulation(cond, w_mod, b_mod):
    nrow = cond.shape[0]
    tn = 1024
    return pl.pallas_call(
        _mod_kernel,
        grid=(DEPTH, 3 * D_MODEL // tn),
        in_specs=[
            pl.BlockSpec((nrow, D_MODEL), lambda l, j: (0, 0)),
            pl.BlockSpec((1, D_MODEL, tn), lambda l, j: (l, 0, j)),
            pl.BlockSpec((1, 1, tn), lambda l, j: (l, 0, j)),
        ],
        out_specs=pl.BlockSpec((1, nrow, tn), lambda l, j: (l, 0, j)),
        out_shape=jax.ShapeDtypeStruct((DEPTH, nrow, 3 * D_MODEL), F32),
        compiler_params=_cparams(("arbitrary", "arbitrary")),
        name="modulation",
    )(cond, w_mod, b_mod.reshape(DEPTH, 1, 3 * D_MODEL))


def _mod_norm(x, mod_ref, g_ref):
    ms = jnp.mean(x * x, axis=-1, keepdims=True)
    y = x * lax.rsqrt(ms + EPS) * g_ref[0]
    shift = mod_ref[0, 0, :, 0:D_MODEL]
    scale = mod_ref[0, 0, :, D_MODEL:2 * D_MODEL]
    gate = mod_ref[0, 0, :, 2 * D_MODEL:3 * D_MODEL]
    return y * (1.0 + scale) + shift, gate


_CACHE_WIDTHS = (NA_HEADS * HEAD_DIM, NA_HEADS * HEAD_DIM, MLA_KV_RANK, MLA_ROPE,
                 WG_KV_HEADS * HEAD_DIM, WG_KV_HEADS * HEAD_DIM)


def _in_kernel(l_ref, x_ref, mod_ref, g_ref, w_ref, gkv_ref, wkvb_ref, rm_ref, c_ref, s_ref, *rest,
               rope, n_alias, seq):
    rest = rest[n_alias:]
    na_ref, hg_ref, mla_ref, wg_ref, kall_ref, vall_ref = rest[:6]
    h, _ = _mod_norm(x_ref[0], mod_ref, g_ref)
    hb = h.astype(BF16)
    na = _dot(hb, w_ref[0, :, 0:NA_W])
    na_ref[0] = na.astype(na_ref.dtype)
    hg_ref[0] = _dot(hb, w_ref[0, :, NA_W:NA_W + HG_W])
    wg = _dot(hb, w_ref[0, :, NA_W + HG_W + MLA_W:PROJ_W])
    wg_ref[0] = wg.astype(wg_ref.dtype)
    mla = _dot(hb, w_ref[0, :, NA_W + HG_W:NA_W + HG_W + MLA_W])
    mla_ref[0] = mla[:, 0:512].astype(mla_ref.dtype)
    ckv = mla[:, 512:640]
    ms = jnp.mean(ckv * ckv, axis=-1, keepdims=True)
    ckv = ckv * lax.rsqrt(ms + EPS) * gkv_ref[0]
    kr = mla[:, 640:768]
    if not rope:
        pieces = (na[:, 256:512], na[:, 512:768], ckv, kr[:, MLA_NOPE:MLA_NOPE + MLA_ROPE],
                  wg[:, 512:640], wg[:, 640:768])
        for ref, val in zip(rest[6:], pieces):
            for i in range(val.shape[0] // seq):
                ref[i, 0] = val[i * seq:(i + 1) * seq, :]
    if rope:
        kr = kr * c_ref[...] + _dot(kr.astype(BF16), rm_ref[...]) * s_ref[...]
    kv = _dot(ckv.astype(BF16), wkvb_ref[0])
    for hd in range(MLA_HEADS):
        kall_ref[0, :, 128 * hd:128 * hd + 128] = (kv[:, 128 * hd:128 * hd + 128] + kr).astype(BF16)
    vall_ref[0] = kv[:, 512:768].astype(BF16)


def _in_proj(lidx, x, mod4, mod_row0, per_batch, norm_g3, w_proj, g_kv3, w_kvb, rm_mla, c4, s4, rope,
             caches=None, cache_shape=None):
    B, T, _ = x.shape
    tile = min(PROJ_TILE, T)
    nt = T // tile
    mrow = (lambda b: mod_row0 + b) if per_batch else (lambda b: mod_row0)
    col_dtype = BF16

    def tok(width):
        return pl.BlockSpec((1, tile, width), lambda b, t, l: (b, t, 0))

    out_specs = [tok(NA_W), tok(HG_W), tok(512), tok(WG_W), tok(512), tok(256)]
    out_shape = [
        jax.ShapeDtypeStruct((B, T, NA_W), col_dtype),
        jax.ShapeDtypeStruct((B, T, HG_W), F32),
        jax.ShapeDtypeStruct((B, T, 512), col_dtype),
        jax.ShapeDtypeStruct((B, T, WG_W), col_dtype),
        jax.ShapeDtypeStruct((B, T, 512), BF16),
        jax.ShapeDtypeStruct((B, T, 256), BF16),
    ]
    in_specs = [
        pl.BlockSpec((1, tile, D_MODEL), lambda b, t, l: (b, t, 0)),
        pl.BlockSpec((1, 1, 1, 3 * D_MODEL), lambda b, t, l: (l[0], mrow(b), 0, 0)),
        pl.BlockSpec((1, 1, D_MODEL), lambda b, t, l: (l[0], 0, 0)),
        pl.BlockSpec((1, D_MODEL, PROJ_W), lambda b, t, l: (l[0], 0, 0)),
        pl.BlockSpec((1, 1, 128), lambda b, t, l: (l[0], 0, 0)),
        pl.BlockSpec((1, 128, 768), lambda b, t, l: (l[0], 0, 0)),
        pl.BlockSpec((128, 128), lambda b, t, l: (0, 0)),
        pl.BlockSpec((tile, 128), lambda b, t, l: (t if rope else 0, 0)),
        pl.BlockSpec((tile, 128), lambda b, t, l: (t if rope else 0, 0)),
    ]
    args = [lidx, x, mod4, norm_g3, w_proj, g_kv3, w_kvb, rm_mla, c4, s4]
    aliases = {}
    seq = tile
    if not rope:
        nbatch, _, seq = cache_shape
        assert B == 1 and tile % seq == 0
        for width in _CACHE_WIDTHS:
            out_specs.append(pl.BlockSpec((tile // seq, 1, seq, width), lambda b, t, l: (t, l[0], 0, 0)))
            out_shape.append(jax.ShapeDtypeStruct((nbatch, DEPTH, seq, width), F32))
        if caches is not None:
            for i, arr in enumerate(caches):
                aliases[len(args)] = 6 + i
                in_specs.append(pl.BlockSpec(memory_space=pl.ANY))
                args.append(arr)
    grid_spec = pltpu.PrefetchScalarGridSpec(
        num_scalar_prefetch=1,
        grid=(B, nt),
        in_specs=in_specs,
        out_specs=out_specs,
    )
    return pl.pallas_call(
        functools.partial(_in_kernel, rope=rope, n_alias=len(aliases), seq=seq),
        grid_spec=grid_spec,
        out_shape=out_shape,
        input_output_aliases=aliases,
        compiler_params=_cparams(("arbitrary", "arbitrary")),
        name="in_proj_lat" if rope else ("in_proj_ctx" if caches is None else "in_proj_ctx_inplace"),
    )(*args)


def _post_kernel(l_ref, x_ref, mod_ref, g_ref, y0_ref, y1_ref, y2_ref, y3_ref, wm_ref, wb_ref, wo_ref,
                 fg_ref, o_ref, *, final):
    x = x_ref[0]
    h, gate = _mod_norm(x, mod_ref, g_ref)
    hb = h.astype(BF16)
    acc = jnp.zeros(x.shape, F32)
    for n, y_ref in enumerate((y0_ref, y1_ref, y2_ref, y3_ref)):
        mg = _sigmoid(_dot(hb, wm_ref[0, :, n * D_MODEL:(n + 1) * D_MODEL]))
        acc = acc + mg * _dot(y_ref[0], wb_ref[0, n])
    out = x + gate * _dot(acc.astype(BF16), wo_ref[0])
    if final:
        ms = jnp.mean(out * out, axis=-1, keepdims=True)
        out = out * lax.rsqrt(ms + EPS) * fg_ref[...]
    o_ref[0] = out


def _post(lidx, x, mod4, mod_row0, per_batch, norm_g3, ys, w_merge, w_branch, w_out, final_g2, final):
    B, T, _ = x.shape
    tile = min(PROJ_TILE, T)
    nt = T // tile
    mrow = (lambda b: mod_row0 + b) if per_batch else (lambda b: mod_row0)
    yspec = pl.BlockSpec((1, tile, BRANCH_W), lambda b, t, l: (b, t, 0))
    grid_spec = pltpu.PrefetchScalarGridSpec(
        num_scalar_prefetch=1,
        grid=(B, nt),
        in_specs=[
            pl.BlockSpec((1, tile, D_MODEL), lambda b, t, l: (b, t, 0)),
            pl.BlockSpec((1, 1, 1, 3 * D_MODEL), lambda b, t, l: (l[0], mrow(b), 0, 0)),
            pl.BlockSpec((1, 1, D_MODEL), lambda b, t, l: (l[0], 0, 0)),
            yspec, yspec, yspec, yspec,
            pl.BlockSpec((1, D_MODEL, N_BRANCH * D_MODEL), lambda b, t, l: (l[0], 0, 0)),
            pl.BlockSpec((1, N_BRANCH, BRANCH_W, D_MODEL), lambda b, t, l: (l[0], 0, 0, 0)),
            pl.BlockSpec((1, D_MODEL, D_MODEL), lambda b, t, l: (l[0], 0, 0)),
            pl.BlockSpec((1, D_MODEL), lambda b, t, l: (0, 0)),
        ],
        out_specs=pl.BlockSpec((1, tile, D_MODEL), lambda b, t, l: (b, t, 0)),
    )
    return pl.pallas_call(
        functools.partial(_post_kernel, final=final),
        grid_spec=grid_spec,
        out_shape=jax.ShapeDtypeStruct((B, T, D_MODEL), F32),
        compiler_params=_cparams(("arbitrary", "arbitrary")),
        name="post_final" if final else "post",
    )(lidx, x, mod4, norm_g3, *ys, w_merge, w_branch, w_out, final_g2)


def _attend(q, kparts, vparts, biases, qmasks, omasks, sinks=None):
    out = None
    for h in range(len(omasks)):
        qh = q[h] if qmasks is None else q * qmasks[h]
        ss = []
        for i, kp in enumerate(kparts):
            s = _dot_nt(qh, kp[h] if isinstance(kp, (list, tuple)) else kp)
            if biases is not None and biases[h][i] is not None:
                s = s + biases[h][i]
            ss.append(s)
        m = ss[0].max(axis=-1, keepdims=True)
        for s in ss[1:]:
            m = jnp.maximum(m, s.max(axis=-1, keepdims=True))
        if sinks is not None:
            m = jnp.maximum(m, sinks[h])
        den = None
        oh = None
        for s, vp in zip(ss, vparts):
            p = jnp.exp2(s - m)
            ps = p.sum(axis=-1, keepdims=True)
            den = ps if den is None else den + ps
            pv = _dot(p.astype(BF16), vp)
            oh = pv if oh is None else oh + pv
        if sinks is not None:
            den = den + jnp.exp2(sinks[h] - m)
        oh = (oh / den) * omasks[h]
        out = oh if out is None else out + oh
    return out


def _head_masks(width, seg, nheads, dtype):
    return [_lane_mask(width, h * seg, (h + 1) * seg, dtype) for h in range(nheads)]


def _na_ctx_kernel(q_ref, k_ref, v_ref, z_ref, o_ref):
    q = q_ref[0].astype(BF16)
    k = k_ref[0].astype(BF16)
    v = v_ref[0].astype(BF16)
    o = _attend(q, [k], [v], None, _head_masks(256, 64, NA_HEADS, BF16), _head_masks(256, 64, NA_HEADS, F32))
    o_ref[0] = (o * _silu(z_ref[0].astype(F32))).astype(BF16)


def _na_ctx(cols_na):
    B, T, _ = cols_na.shape

    def spec(j):
        return pl.BlockSpec((1, T, 256), lambda b, j=j: (b, 0, j))

    return pl.pallas_call(
        _na_ctx_kernel,
        grid=(B,),
        in_specs=[spec(0), spec(1), spec(2), spec(3)],
        out_specs=pl.BlockSpec((1, T, 256), lambda b: (b, 0, 0)),
        out_shape=jax.ShapeDtypeStruct((B, T, 256), BF16),
        compiler_params=_cparams(("arbitrary",)),
        name="na_ctx",
    )(cols_na, cols_na, cols_na, cols_na)


def _na_row_types(rows):
    wh = min(NA_WIN_H, rows)
    nblk = rows // NA_QROWS
    starts, types, table = [], [], []
    for blk in range(nblk):
        r0 = blk * NA_QROWS
        k0 = int(np.clip(r0 - wh // 2, 0, rows - NA_KROWS))
        desc = []
        for i in range(NA_QROWS):
            r = r0 + i
            ks = int(np.clip(r - wh // 2, 0, rows - wh))
            for j in range(NA_KROWS):
                kr = k0 + j
                desc.append((kr - r + NA_WIN_H - 1) if ks <= kr < ks + wh else None)
        desc = tuple(desc)
        if desc not in table:
            table.append(desc)
        starts.append(k0)
        types.append(table.index(desc))
    return starts, types, table


def _na_bias_kernel(rpb_ref, o_ref, t2_ref, *, table):
    l = pl.program_id(0)
    h = pl.program_id(1)
    nrow = 2 * NA_WIN_H - 1
    ncol = 2 * NA_WIN_W - 1
    c = lax.broadcasted_iota(jnp.int32, (GRID_W, 128), 0)
    lane = lax.broadcasted_iota(jnp.int32, (GRID_W, 128), 1)
    first = lane < GRID_W
    kc = jnp.where(first, lane, lane - GRID_W)
    qstart = jnp.clip(c - NA_WIN_W // 2, 0, GRID_W - NA_WIN_W)
    ok = (kc >= qstart) & (kc < qstart + NA_WIN_W)
    dcol = kc - c + NA_WIN_W - 1
    for a in range(-1, nrow):
        acc = jnp.full((GRID_W, 128), NEG_INF, F32)
        for bi in range(2 * NA_WIN_W - 1):
            base = (l * NA_HEADS + h) * nrow
            lo = rpb_ref[(base + a) * ncol + bi] if a >= 0 else jnp.float32(NEG_INF)
            hi = rpb_ref[(base + a + 1) * ncol + bi] if a + 1 < nrow else jnp.float32(NEG_INF)
            acc = jnp.where(dcol == bi, jnp.where(first, lo, hi), acc)
        t2_ref[a + 1] = jnp.where(ok, acc * LOG2E, NEG_INF)
    for ti, desc in enumerate(table):
        for i in range(NA_QROWS):
            for p in range(NA_KROWS // 2):
                a0 = desc[i * NA_KROWS + 2 * p]
                a1 = desc[i * NA_KROWS + 2 * p + 1]
                if a0 is None and a1 is None:
                    tile = jnp.full((GRID_W, 128), NEG_INF, F32)
                elif a1 is None:
                    tile = jnp.where(first, t2_ref[a0 + 1], NEG_INF)
                elif a0 is None:
                    tile = jnp.where(first, NEG_INF, t2_ref[a1])
                else:
                    tile = t2_ref[a0 + 1]
                o_ref[0, ti, 0, i * GRID_W:(i + 1) * GRID_W, p * 128:(p + 1) * 128] = tile


def _na_bias(na_rpb, table):
    nt = len(table)
    nq = NA_QROWS * GRID_W
    nk = NA_KROWS * GRID_W
    return pl.pallas_call(
        functools.partial(_na_bias_kernel, table=table),
        grid=(DEPTH, NA_HEADS),
        in_specs=[pl.BlockSpec(memory_space=pltpu.SMEM)],
        out_specs=pl.BlockSpec((1, nt, 1, nq, nk), lambda l, h: (l, 0, h, 0, 0)),
        out_shape=jax.ShapeDtypeStruct((DEPTH, nt, NA_HEADS, nq, nk), F32),
        scratch_shapes=[pltpu.VMEM((2 * NA_WIN_H, GRID_W, 128), F32)],
        compiler_params=_cparams(("arbitrary", "arbitrary")),
        name="na_bias",
    )(na_rpb.reshape(-1))


def _na_lat_kernel(l_ref, ks_ref, ty_ref, q_ref, k_ref, v_ref, z_ref, kc_ref, vc_ref, b_ref, o_ref):
    rb = pl.program_id(1)
    nk = NA_KROWS * GRID_W
    ks = pl.multiple_of(ks_ref[rb] * GRID_W, GRID_W)
    q = q_ref[0].astype(BF16)
    kw = k_ref[0, pl.ds(ks, nk), :].astype(BF16)
    vw = v_ref[0, pl.ds(ks, nk), :].astype(BF16)
    kc = kc_ref[0, 0].astype(BF16)
    vc = vc_ref[0, 0].astype(BF16)
    biases = [[b_ref[0, 0, h], None] for h in range(NA_HEADS)]
    o = _attend(q, [kw, kc], [vw, vc], biases,
                _head_masks(256, 64, NA_HEADS, BF16), _head_masks(256, 64, NA_HEADS, F32))
    o_ref[0] = (o * _silu(z_ref[0].astype(F32))).astype(BF16)


def _na_lat(lidx, kstart, ktype, cols_na, cache_k, cache_v, bias):
    B, T, _ = cols_na.shape
    L = cache_k.shape[2]
    nq = NA_QROWS * GRID_W
    nk = NA_KROWS * GRID_W
    nblk = T // nq
    grid_spec = pltpu.PrefetchScalarGridSpec(
        num_scalar_prefetch=3,
        grid=(B, nblk),
        in_specs=[
            pl.BlockSpec((1, nq, 256), lambda b, r, l, ks, ty: (b, r, 0)),
            pl.BlockSpec((1, T, 256), lambda b, r, l, ks, ty: (b, 0, 1)),
            pl.BlockSpec((1, T, 256), lambda b, r, l, ks, ty: (b, 0, 2)),
            pl.BlockSpec((1, nq, 256), lambda b, r, l, ks, ty: (b, r, 3)),
            pl.BlockSpec((1, 1, L, 256), lambda b, r, l, ks, ty: (b, l[0], 0, 0)),
            pl.BlockSpec((1, 1, L, 256), lambda b, r, l, ks, ty: (b, l[0], 0, 0)),
            pl.BlockSpec((1, 1, NA_HEADS, nq, nk), lambda b, r, l, ks, ty: (l[0], ty[r], 0, 0, 0)),
        ],
        out_specs=pl.BlockSpec((1, nq, 256), lambda b, r, l, ks, ty: (b, r, 0)),
    )
    return pl.pallas_call(
        _na_lat_kernel,
        grid_spec=grid_spec,
        out_shape=jax.ShapeDtypeStruct((B, T, 256), BF16),
        compiler_params=_cparams(("arbitrary", "arbitrary")),
        name="na_lat",
    )(lidx, kstart, ktype, cols_na, cols_na, cols_na, cols_na, cache_k, cache_v, bias)


def _wg_heads(q, kparts, vparts, bias, sink_ref, l):
    qm = _head_masks(128, 64, 2, BF16)
    om = _head_masks(128, 64, 2, F32)
    outs = []
    for grp, heads in enumerate(((0, 2), (1, 3))):
        qg = q[:, grp * 128:(grp + 1) * 128]
        sinks = [sink_ref[l, hh] * LOG2E for hh in heads]
        biases = None if bias is None else [[bias] + [None] * (len(kparts) - 1)] * 2
        outs.append(_attend(qg, kparts, vparts, biases, qm, om, sinks))
    return jnp.concatenate(outs, axis=-1)


def _wg_ctx_kernel(l_ref, sink_ref, q_ref, z_ref, k_ref, v_ref, o_ref):
    q = q_ref[0].astype(BF16)
    o = _wg_heads(q, [k_ref[0].astype(BF16)], [v_ref[0].astype(BF16)], None, sink_ref, l_ref[0])
    o_ref[0] = (o * _silu(z_ref[0].astype(F32))).astype(BF16)


def _wg_ctx(lidx, sink, cols_wg):
    B, T, _ = cols_wg.shape
    grid_spec = pltpu.PrefetchScalarGridSpec(
        num_scalar_prefetch=1,
        grid=(B,),
        in_specs=[
            pl.BlockSpec(memory_space=pltpu.SMEM),
            pl.BlockSpec((1, T, 256), lambda b, l: (b, 0, 0)),
            pl.BlockSpec((1, T, 256), lambda b, l: (b, 0, 1)),
            pl.BlockSpec((1, T, 128), lambda b, l: (b, 0, 4)),
            pl.BlockSpec((1, T, 128), lambda b, l: (b, 0, 5)),
        ],
        out_specs=pl.BlockSpec((1, T, 256), lambda b, l: (b, 0, 0)),
    )
    return pl.pallas_call(
        _wg_ctx_kernel,
        grid_spec=grid_spec,
        out_shape=jax.ShapeDtypeStruct((B, T, 256), BF16),
        compiler_params=_cparams(("arbitrary",)),
        name="wg_ctx",
    )(lidx, sink, cols_wg, cols_wg, cols_wg, cols_wg)


def _wg_lat_kernel(l_ref, sink_ref, q_ref, z_ref, k_ref, v_ref, kc_ref, vc_ref,
                   cq_ref, sq_ref, ck_ref, sk_ref, rm_ref, o_ref, *, seq):
    tb = pl.program_id(1)
    t0 = tb * WG_TILE
    k0 = pl.multiple_of(jnp.clip(t0 - WG_WINDOW, 0, seq - WG_KWIN), 128)
    q = q_ref[0]
    q = q.astype(F32) * cq_ref[...] + _dot(q.astype(BF16), rm_ref[...]) * sq_ref[...]
    kwin = k_ref[0, pl.ds(k0, WG_KWIN), :]
    kwin = kwin.astype(F32) * ck_ref[pl.ds(k0, WG_KWIN), :] \
        + _dot(kwin.astype(BF16), rm_ref[0:128, 0:128]) * sk_ref[pl.ds(k0, WG_KWIN), :]
    vwin = v_ref[0, pl.ds(k0, WG_KWIN), :]
    qi = t0 + lax.broadcasted_iota(jnp.int32, (WG_TILE, WG_KWIN), 0)
    kj = k0 + lax.broadcasted_iota(jnp.int32, (WG_TILE, WG_KWIN), 1)
    bias = jnp.where(jnp.abs(qi - kj) <= WG_WINDOW, 0.0, NEG_INF).astype(F32)
    o = _wg_heads(q.astype(BF16),
                  [kwin.astype(BF16), kc_ref[0, 0].astype(BF16)],
                  [vwin.astype(BF16), vc_ref[0, 0].astype(BF16)],
                  bias, sink_ref, l_ref[0])
    o_ref[0] = (o * _silu(z_ref[0].astype(F32))).astype(BF16)


def _wg_lat(lidx, sink, cols_wg, cache_k, cache_v, cq, sq, ck, sk, rm):
    B, T, _ = cols_wg.shape
    L = cache_k.shape[2]
    nt = T // WG_TILE
    grid_spec = pltpu.PrefetchScalarGridSpec(
        num_scalar_prefetch=1,
        grid=(B, nt),
        in_specs=[
            pl.BlockSpec(memory_space=pltpu.SMEM),
            pl.BlockSpec((1, WG_TILE, 256), lambda b, t, l: (b, t, 0)),
            pl.BlockSpec((1, WG_TILE, 256), lambda b, t, l: (b, t, 1)),
            pl.BlockSpec((1, T, 128), lambda b, t, l: (b, 0, 4)),
            pl.BlockSpec((1, T, 128), lambda b, t, l: (b, 0, 5)),
            pl.BlockSpec((1, 1, L, 128), lambda b, t, l: (b, l[0], 0, 0)),
            pl.BlockSpec((1, 1, L, 128), lambda b, t, l: (b, l[0], 0, 0)),
            pl.BlockSpec((WG_TILE, 256), lambda b, t, l: (t, 0)),
            pl.BlockSpec((WG_TILE, 256), lambda b, t, l: (t, 0)),
            pl.BlockSpec((T, 128), lambda b, t, l: (0, 0)),
            pl.BlockSpec((T, 128), lambda b, t, l: (0, 0)),
            pl.BlockSpec((256, 256), lambda b, t, l: (0, 0)),
        ],
        out_specs=pl.BlockSpec((1, WG_TILE, 256), lambda b, t, l: (b, t, 0)),
    )
    return pl.pallas_call(
        functools.partial(_wg_lat_kernel, seq=T),
        grid_spec=grid_spec,
        out_shape=jax.ShapeDtypeStruct((B, T, 256), BF16),
        compiler_params=_cparams(("arbitrary", "arbitrary")),
        name="wg_lat",
    )(lidx, sink, cols_wg, cols_wg, cols_wg, cols_wg, cache_k, cache_v, cq, sq, ck, sk, rm)


def _mla_kv_kernel(l_ref, ckv_ref, kr_ref, krr_ref, g_ref, w_ref, c_ref, s_ref, k_out, v_out, *rest,
                   norm, rope):
    ckv = ckv_ref[0]
    if norm:
        ms = jnp.mean(ckv * ckv, axis=-1, keepdims=True)
        ckv = ckv * lax.rsqrt(ms + EPS) * g_ref[0]
        rest[0][0] = ckv
    kv = _dot(ckv.astype(BF16), w_ref[0])
    kr = kr_ref[0]
    if rope:
        kr = kr * c_ref[...] + krr_ref[0] * s_ref[...]
    for h in range(MLA_HEADS):
        k_out[0, :, 128 * h:128 * h + 128] = (kv[:, 128 * h:128 * h + 128] + kr).astype(BF16)
    v_out[0] = kv[:, 512:768].astype(BF16)


def _mla_kv(lidx, ckv_arr, ckv_blk, kr_arr, kr_blk, krr_arr, krr_blk, g_kv3, w_kvb, c4, s4, norm, rope,
            layered):
    B = ckv_arr.shape[0]
    T = ckv_arr.shape[-2]
    nt = T // TOK_TILE
    if layered:
        def src(blk):
            return pl.BlockSpec((None, 1, TOK_TILE, 128), lambda b, t, l: (b, l[0], t, 0))
    else:
        def src(blk):
            return pl.BlockSpec((1, TOK_TILE, 128), lambda b, t, l, blk=blk: (b, t, blk))
    out_specs = [
        pl.BlockSpec((1, TOK_TILE, 512), lambda b, t, l: (b, t, 0)),
        pl.BlockSpec((1, TOK_TILE, 256), lambda b, t, l: (b, t, 0)),
    ]
    out_shape = [
        jax.ShapeDtypeStruct((B, T, 512), BF16),
        jax.ShapeDtypeStruct((B, T, 256), BF16),
    ]
    if norm:
        out_specs.append(pl.BlockSpec((1, TOK_TILE, 128), lambda b, t, l: (b, t, 0)))
        out_shape.append(jax.ShapeDtypeStruct((B, T, 128), F32))
    grid_spec = pltpu.PrefetchScalarGridSpec(
        num_scalar_prefetch=1,
        grid=(B, nt),
        in_specs=[
            src(ckv_blk), src(kr_blk), src(krr_blk),
            pl.BlockSpec((1, 1, 128), lambda b, t, l: (l[0], 0, 0)),
            pl.BlockSpec((1, 128, 768), lambda b, t, l: (l[0], 0, 0)),
            pl.BlockSpec((TOK_TILE, 128), lambda b, t, l: (t if rope else 0, 0)),
            pl.BlockSpec((TOK_TILE, 128), lambda b, t, l: (t if rope else 0, 0)),
        ],
        out_specs=out_specs,
    )
    return pl.pallas_call(
        functools.partial(_mla_kv_kernel, norm=norm, rope=rope),
        grid_spec=grid_spec,
        out_shape=out_shape,
        compiler_params=_cparams(("arbitrary", "arbitrary")),
        name="mla_kv" + ("_norm" if norm else "") + ("_rope" if rope else ""),
    )(lidx, ckv_arr, kr_arr, krr_arr, g_kv3, w_kvb, c4, s4)


def _mla_attn_kernel(l_ref, qa_ref, z_ref, g_ref, w_ref, c_ref, s_ref, *refs, nparts, rope):
    kparts = [[refs[2 * i][0, :, 128 * h:128 * h + 128] for h in range(MLA_HEADS)] for i in range(nparts)]
    vparts = [refs[2 * i + 1][0] for i in range(nparts)]
    o_ref = refs[2 * nparts]
    qa = qa_ref[0].astype(F32)
    ms = jnp.mean(qa * qa, axis=-1, keepdims=True)
    qn = qa * lax.rsqrt(ms + EPS) * g_ref[0]
    qq = _dot(qn.astype(BF16), w_ref[0])
    q = []
    for h in range(MLA_HEADS):
        qh = qq[:, 128 * h:128 * h + 128]
        if rope:
            qh = qh * c_ref[...] + qq[:, 512 + 128 * h:512 + 128 * h + 128] * s_ref[...]
        q.append(qh.astype(BF16))
    om = _head_masks(256, 64, MLA_HEADS, F32)
    half = TOK_TILE // 2
    o = jnp.concatenate([_attend([qh[i * half:(i + 1) * half] for qh in q], kparts, vparts, None, None, om)
                         for i in range(2)], axis=0)
    o_ref[0] = (o * _silu(z_ref[0].astype(F32))).astype(BF16)


def _attn_ctx_kernel(l_ref, *refs):
    na_in, refs = refs[:4], refs[4:]
    wg_in, refs = refs[:5], refs[5:]
    mla_in, refs = refs[:8], refs[8:]
    o_na, o_wg, o_mla = refs
    _na_ctx_kernel(*na_in, o_na)
    _wg_ctx_kernel(l_ref, *wg_in, o_wg)
    _mla_attn_kernel(l_ref, *mla_in, o_mla, nparts=1, rope=False)


def _attn_ctx(lidx, cols_na, sink, cols_wg, cols_mla, g_q3, w_qb, c4, s4, kc, vc):
    B, T, _ = cols_na.shape

    def col(width, j):
        return pl.BlockSpec((1, T, width), lambda b, l, j=j: (b, 0, j))

    in_specs = [
        col(256, 0), col(256, 1), col(256, 2), col(256, 3),
        pl.BlockSpec(memory_space=pltpu.SMEM), col(256, 0), col(256, 1), col(128, 4), col(128, 5),
        col(256, 0), col(256, 1),
        pl.BlockSpec((1, 1, 256), lambda b, l: (l[0], 0, 0)),
        pl.BlockSpec((1, 256, 1024), lambda b, l: (l[0], 0, 0)),
        pl.BlockSpec((TOK_TILE, 128), lambda b, l: (0, 0)),
        pl.BlockSpec((TOK_TILE, 128), lambda b, l: (0, 0)),
        col(512, 0), col(256, 0),
    ]
    out = pl.BlockSpec((1, T, 256), lambda b, l: (b, 0, 0))
    grid_spec = pltpu.PrefetchScalarGridSpec(num_scalar_prefetch=1, grid=(B,), in_specs=in_specs,
                                             out_specs=[out, out, out])
    return pl.pallas_call(
        _attn_ctx_kernel,
        grid_spec=grid_spec,
        out_shape=[jax.ShapeDtypeStruct((B, T, 256), BF16)] * 3,
        compiler_params=_cparams(("arbitrary",)),
        name="attn_ctx",
    )(lidx, cols_na, cols_na, cols_na, cols_na, sink, cols_wg, cols_wg, cols_wg, cols_wg,
      cols_mla, cols_mla, g_q3, w_qb, c4, s4, kc, vc)


def _attn_lat_kernel(l_ref, ks_ref, ty_ref, *refs, seq):
    na_in, refs = refs[:7], refs[7:]
    wg_in, refs = refs[:12], refs[12:]
    mla_in, refs = refs[:10], refs[10:]
    o_na, o_wg, o_mla = refs
    _na_lat_kernel(l_ref, ks_ref, ty_ref, *na_in, o_na)
    _wg_lat_kernel(l_ref, *wg_in, o_wg, seq=seq)
    _mla_attn_kernel(l_ref, *mla_in, o_mla, nparts=2, rope=True)


def _attn_lat(lidx, kstart, ktype, cols_na, cna_k, cna_v, bias, sink, cols_wg, cwg_k, cwg_v, cq, sq, ck, sk, rm,
              cols_mla, g_q3, w_qb, c4, s4, kl, vl, kx, vx):
    B, T, _ = cols_na.shape
    L = cna_k.shape[2]
    nq = NA_QROWS * GRID_W
    nk = NA_KROWS * GRID_W
    assert nq == TOK_TILE == WG_TILE
    nt = T // TOK_TILE

    def tok(width, col):
        return pl.BlockSpec((1, TOK_TILE, width), lambda b, t, l, ks, ty, col=col: (b, t, col))

    def full(rows, width, col):
        return pl.BlockSpec((1, rows, width), lambda b, t, l, ks, ty, col=col: (b, 0, col))

    def layer(rows, width):
        return pl.BlockSpec((1, 1, rows, width), lambda b, t, l, ks, ty: (b, l[0], 0, 0))

    def table(width):
        return pl.BlockSpec((TOK_TILE, width), lambda b, t, l, ks, ty: (t, 0))

    in_specs = [
        tok(256, 0), full(T, 256, 1), full(T, 256, 2), tok(256, 3), layer(L, 256), layer(L, 256),
        pl.BlockSpec((1, 1, NA_HEADS, nq, nk), lambda b, t, l, ks, ty: (l[0], ty[t], 0, 0, 0)),
        pl.BlockSpec(memory_space=pltpu.SMEM),
        tok(256, 0), tok(256, 1), full(T, 128, 4), full(T, 128, 5), layer(L, 128), layer(L, 128),
        table(256), table(256),
        pl.BlockSpec((T, 128), lambda b, t, l, ks, ty: (0, 0)),
        pl.BlockSpec((T, 128), lambda b, t, l, ks, ty: (0, 0)),
        pl.BlockSpec((256, 256), lambda b, t, l, ks, ty: (0, 0)),
        tok(256, 0), tok(256, 1),
        pl.BlockSpec((1, 1, 256), lambda b, t, l, ks, ty: (l[0], 0, 0)),
        pl.BlockSpec((1, 256, 1024), lambda b, t, l, ks, ty: (l[0], 0, 0)),
        table(128), table(128),
        full(kl.shape[1], 512, 0), full(vl.shape[1], 256, 0), full(kx.shape[1], 512, 0), full(vx.shape[1], 256, 0),
    ]
    out = pl.BlockSpec((1, TOK_TILE, 256), lambda b, t, l, ks, ty: (b, t, 0))
    grid_spec = pltpu.PrefetchScalarGridSpec(
        num_scalar_prefetch=3,
        grid=(B, nt),
        in_specs=in_specs,
        out_specs=[out, out, out],
    )
    return pl.pallas_call(
        functools.partial(_attn_lat_kernel, seq=T),
        grid_spec=grid_spec,
        out_shape=[jax.ShapeDtypeStruct((B, T, 256), BF16)] * 3,
        compiler_params=_cparams(("arbitrary", "arbitrary")),
        name="attn_lat",
    )(lidx, kstart, ktype,
      cols_na, cols_na, cols_na, cols_na, cna_k, cna_v, bias,
      sink, cols_wg, cols_wg, cols_wg, cols_wg, cwg_k, cwg_v, cq, sq, ck, sk, rm,
      cols_mla, cols_mla, g_q3, w_qb, c4, s4, kl, vl, kx, vx)


def _mla_attn(lidx, cols_mla, g_q3, w_qb, c4, s4, kv_parts, rope):
    B, T, _ = cols_mla.shape
    nt = T // TOK_TILE
    in_specs = [
        pl.BlockSpec((1, TOK_TILE, 256), lambda b, t, l: (b, t, 0)),
        pl.BlockSpec((1, TOK_TILE, 256), lambda b, t, l: (b, t, 1)),
        pl.BlockSpec((1, 1, 256), lambda b, t, l: (l[0], 0, 0)),
        pl.BlockSpec((1, 256, 1024), lambda b, t, l: (l[0], 0, 0)),
        pl.BlockSpec((TOK_TILE, 128), lambda b, t, l: (t if rope else 0, 0)),
        pl.BlockSpec((TOK_TILE, 128), lambda b, t, l: (t if rope else 0, 0)),
    ]
    args = [lidx, cols_mla, cols_mla, g_q3, w_qb, c4, s4]
    for kk, vv in kv_parts:
        in_specs.append(pl.BlockSpec((1, kk.shape[1], 512), lambda b, t, l: (b, 0, 0)))
        in_specs.append(pl.BlockSpec((1, vv.shape[1], 256), lambda b, t, l: (b, 0, 0)))
        args += [kk, vv]
    grid_spec = pltpu.PrefetchScalarGridSpec(
        num_scalar_prefetch=1,
        grid=(B, nt),
        in_specs=in_specs,
        out_specs=pl.BlockSpec((1, TOK_TILE, 256), lambda b, t, l: (b, t, 0)),
    )
    return pl.pallas_call(
        functools.partial(_mla_attn_kernel, nparts=len(kv_parts), rope=rope),
        grid_spec=grid_spec,
        out_shape=jax.ShapeDtypeStruct((B, T, 256), BF16),
        compiler_params=_cparams(("arbitrary", "arbitrary")),
        name="mla_attn_lat" if rope else "mla_attn_ctx",
    )(*args)


def _hg_blocks(rev, include_diag):
    C, S = HG_CHUNK, HG_SUB
    out = []
    for g0 in range(0, C, S):
        if rev:
            lo, hi = (g0 if include_diag else g0 + S), C
        else:
            lo, hi = 0, (g0 + S if include_diag else g0)
        if hi > lo:
            out.append((g0, lo, hi))
    return out


def _hg_mask(rev, include_diag):
    blocks = _hg_blocks(rev, include_diag)
    S = HG_SUB
    nrow = len(blocks) * HG_HEADS * S
    ncol = 128 * -(-sum(hi - lo for _, lo, hi in blocks) // 128)
    r = lax.broadcasted_iota(jnp.int32, (nrow, ncol), 0)
    cidx = lax.broadcasted_iota(jnp.int32, (nrow, ncol), 1)
    rblk = r // (HG_HEADS * S)
    cblk = jnp.full_like(cidx, len(blocks))
    s = jnp.zeros_like(cidx)
    t = r % S
    start = 0
    for k_, (g0, lo, hi) in enumerate(blocks):
        inb = (cidx >= start) & (cidx < start + hi - lo)
        cblk = jnp.where(inb, k_, cblk)
        s = jnp.where(inb, cidx - start + lo, s)
        t = jnp.where(rblk == k_, t + g0, t)
        start += hi - lo
    ok = rblk == cblk
    if include_diag:
        ok = ok & ((s >= t) if rev else (s <= t))
    return ok.astype(F32)


def _hg_stacked(qs, kk, v, b2, rev, include_diag, mask, hm):
    C, S = HG_CHUNK, HG_SUB
    blocks = _hg_blocks(rev, include_diag)
    qx, kd_rows, v_rows = [], [], []
    for g0, lo, hi in blocks:
        has_ref = (g0 + S < C) if rev else (g0 > 0)
        eq = b2[g0:g0 + S, :]
        es = -b2[lo:hi, :]
        if has_ref:
            ct = b2[g0 + S:g0 + S + 1, :] if rev else b2[g0 - 1:g0, :]
            eq = eq - ct
            es = es + ct
        qd = qs[g0:g0 + S, :] * jnp.exp2(eq)
        qx += [qd * m for m in hm]
        kd_rows.append(kk[lo:hi, :] * jnp.exp2(es))
        v_rows.append(v[lo:hi, :])
    pad = mask.shape[1] - sum(hi - lo for _, lo, hi in blocks)
    if pad:
        kd_rows.append(jnp.zeros((pad, 256), F32))
        v_rows.append(jnp.zeros((pad, 256), F32))
    kd_all = jnp.concatenate(kd_rows, axis=0).astype(BF16)
    v_all = jnp.concatenate(v_rows, axis=0).astype(BF16)
    att = _dot_nt(jnp.concatenate(qx, axis=0).astype(BF16), kd_all)
    att = jnp.where(mask > 0.5, att, 0.0)
    return _dot(att.astype(BF16), v_all), blocks


def _hg_gates(lb, rev, q_ref, f_ref, qs_sc, lf_sc, kk_sc):
    qs_sc[...] = _silu(q_ref[0])
    fz = f_ref[0]
    e = jnp.exp(-jnp.abs(fz))
    big = 1.0 / (1.0 + e)
    small_ = e * big
    pos = fz >= 0.0
    f = lb + (1.0 - lb) * jnp.where(pos, big, small_)
    lf = jnp.log(jnp.maximum(f, F_FLOOR))
    kk_sc[...] = (1.0 - lb) * jnp.where(pos, small_, big)
    rr = lax.broadcasted_iota(jnp.int32, (TOK_TILE, TOK_TILE), 0)
    cc = lax.broadcasted_iota(jnp.int32, (TOK_TILE, TOK_TILE), 1)
    tri = ((rr // HG_CHUNK == cc // HG_CHUNK) & ((cc >= rr) if rev else (cc <= rr))).astype(BF16)
    hi = lf.astype(BF16)
    r1 = lf - hi.astype(F32)
    mid = r1.astype(BF16)
    lo = (r1 - mid.astype(F32)).astype(BF16)
    lf_sc[...] = _dot(tri, hi) + _dot(tri, mid) + _dot(tri, lo)
    nblk = TOK_TILE // HG_SUB
    r = lax.broadcasted_iota(jnp.int32, (nblk, TOK_TILE), 0)
    cidx = lax.broadcasted_iota(jnp.int32, (nblk, TOK_TILE), 1)
    span = _dot((cidx // HG_SUB == r).astype(BF16), (lf * (-LOG2E)).astype(BF16))
    return jnp.max(span) <= HG_FAST_SPAN


def _hg_chunk(r0, rev, fast, st, qs_sc, lf_sc, kk_sc, i_ref, a_sc, v_sc, o_sc, seg, bdmask, mask):
    C = HG_CHUNK
    S = HG_SUB
    qs = qs_sc[pl.ds(r0, C), :]
    b = lf_sc[pl.ds(r0, C), :]
    kk = kk_sc[pl.ds(r0, C), :]
    v = i_ref[0, pl.ds(r0, C), :]
    b2 = b * LOG2E
    total = b[0:1, :] if rev else b[C - 1:C, :]
    inter = _dot_nt((qs * jnp.exp2(b2)).astype(BF16), st.astype(BF16))
    hm = _head_masks(256, HG_DK, HG_HEADS, F32)
    off, blocks = _hg_stacked(qs, kk, v, b2, rev, fast, mask, hm)

    def head_sum(k_, i):
        acc = None
        for h in range(HG_HEADS):
            r_ = k_ * HG_HEADS * S + h * S + 8 * i
            t_ = off[r_:r_ + 8, :] * hm[h]
            acc = t_ if acc is None else acc + t_
        return acc

    if fast:
        tiles = []
        for k_, (g0, _, _) in enumerate(blocks):
            for i in range(S // 8):
                tiles.append(inter[g0 + 8 * i:g0 + 8 * i + 8, :] + head_sum(k_, i))
        o = jnp.concatenate(tiles, axis=0)
    else:
        a_sc[...] = jnp.log2(kk) - b2
        v_sc[...] = v
        o_sc[...] = inter
        for k_, (g0, _, _) in enumerate(blocks):
            for i in range(S // 8):
                o_sc[g0 + 8 * i:g0 + 8 * i + 8, :] = o_sc[g0 + 8 * i:g0 + 8 * i + 8, :] + head_sum(k_, i)
        sub = lax.broadcasted_iota(jnp.int32, (8, 256), 0)
        for g0 in range(0, C, S):
            parts, spans = [], []
            for s in range(g0, g0 + S):
                t0 = (s // 8) * 8
                arow = a_sc[s:s + 1, :]
                keep = (sub <= s - t0) if rev else (sub >= s - t0)
                rdiag = qs[t0:t0 + 8, :] * jnp.exp2(jnp.where(keep, b2[t0:t0 + 8, :] + arow, NEG_INF))
                rest_lo, rest_hi = (g0, t0) if rev else (t0 + 8, g0 + S)
                if rest_hi > rest_lo:
                    rrest = qs[rest_lo:rest_hi, :] * jnp.exp2(b2[rest_lo:rest_hi, :] + arow)
                    parts += [rrest, rdiag] if rev else [rdiag, rrest]
                else:
                    parts.append(rdiag)
                spans.append((min(rest_lo, t0), max(rest_hi, t0 + 8)) if rest_hi > rest_lo else (t0, t0 + 8))
            res = _dot(jnp.concatenate(parts, axis=0).astype(BF16), seg)
            acc = [o_sc[g0 + 8 * i:g0 + 8 * i + 8, :] for i in range(S // 8)]
            pos = 0
            for s, (lo_r, hi_r) in zip(range(g0, g0 + S), spans):
                vrow = v_sc[s:s + 1, :]
                for i in range((hi_r - lo_r) // 8):
                    ti = (lo_r - g0) // 8 + i
                    acc[ti] = acc[ti] + res[pos + 8 * i:pos + 8 * i + 8, :] * vrow
                pos += hi_r - lo_r
            for i in range(S // 8):
                o_sc[g0 + 8 * i:g0 + 8 * i + 8, :] = acc[i]
        o = o_sc[...]
    kd = (kk * jnp.exp(total - b)).astype(BF16)
    return o, jnp.exp(total) * st + bdmask * _dot_tn(v.astype(BF16), kd)


def _hg_kernel(l_ref, q_ref, f_ref, i_ref, og_ref, z_ref, lbl_ref, g_ref, s0_ref, y_ref, sto_ref,
               st_ref, oacc_ref, qs_sc, lf_sc, kk_sc, a_sc, v_sc, o_sc, *, nb, s0_per_batch):
    j = pl.program_id(1)
    l = l_ref[0]
    C = HG_CHUNK
    ncb = TOK_TILE // C
    P = q_ref.shape[0]

    @pl.when((j == 0) | (j == nb))
    def _():
        for e in range(P):
            st_ref[e] = s0_ref[e if s0_per_batch else 0, 0]

    lg = lbl_ref[...]
    mx = lg[0]
    for m in range(1, DEPTH):
        mx = jnp.maximum(mx, lg[m])
    ex = [jnp.exp(lg[m] - mx) for m in range(DEPTH)]
    tot = ex[0]
    for m in range(1, DEPTH):
        tot = tot + ex[m]
    lbs = jnp.zeros_like(tot)
    for m in range(1, DEPTH):
        lbs = lbs + jnp.where(m <= l, ex[m] / tot, 0.0)

    seg = _seg_matrix(256, HG_DK)
    seg_b = seg.astype(BF16)
    bdmask = seg.astype(F32)

    def run(rev, emit):
        lb = lbs[1:2, :] if rev else lbs[0:1, :]
        small = None
        for e in range(P):
            s_e = _hg_gates(lb, rev, q_ref.at[pl.ds(e, 1)], f_ref.at[pl.ds(e, 1)], qs_sc.at[e], lf_sc.at[e],
                            kk_sc.at[e])
            small = s_e if small is None else small & s_e

        def common(e):
            return (qs_sc.at[e], lf_sc.at[e], kk_sc.at[e], i_ref.at[pl.ds(e, 1)], a_sc, v_sc, o_sc, seg_b, bdmask)

        @pl.when(small)
        def _():
            mask = _hg_mask(rev, True)
            sts = [st_ref[e] for e in range(P)]
            for ci in range(ncb):
                r0 = (ncb - 1 - ci if rev else ci) * C
                for e in range(P):
                    o, sts[e] = _hg_chunk(r0, rev, True, sts[e], *common(e), mask)
                    emit(e, r0, o)
            for e in range(P):
                st_ref[e] = sts[e]

        @pl.when(jnp.logical_not(small))
        def _():
            mask = _hg_mask(rev, False)
            for e in range(P):
                def body(ci, carry, e=e):
                    r0 = pl.multiple_of((ncb - 1 - ci if rev else ci) * C, C)
                    o, st = _hg_chunk(r0, rev, False, st_ref[e], *common(e), mask)
                    st_ref[e] = st
                    emit(e, r0, o)
                    return carry

                lax.fori_loop(0, ncb, body, 0)

    @pl.when(j < nb)
    def _():
        def emit(e, r0, o):
            oacc_ref[e, pl.ds(pl.multiple_of(j * TOK_TILE + r0, C), C), :] = o

        run(False, emit)

    @pl.when(j >= nb)
    def _():
        blk = 2 * nb - 1 - j

        def emit(e, r0, o):
            o = oacc_ref[e, pl.ds(pl.multiple_of(blk * TOK_TILE + r0, C), C), :] + o
            ms = _seg_mean(o * o, seg_b, HG_DK)
            on = o * lax.rsqrt(ms + EPS) * g_ref[0]
            on = on * _sigmoid(og_ref[e, pl.ds(r0, C), :])
            y_ref[e, pl.ds(r0, C), :] = (on * _silu(z_ref[e, pl.ds(r0, C), :])).astype(BF16)

        run(True, emit)

    for e in range(P):
        sto_ref[e, 0] = st_ref[e]


def _hgrn(lidx, cols_hg, lb_logits, g3, s0, s0_per_batch):
    B, T, _ = cols_hg.shape
    nb = T // TOK_TILE
    P = HG_PAIR
    assert B % P == 0

    def blk(j):
        return jnp.where(j < nb, j, 2 * nb - 1 - j)

    def dirn(j):
        return jnp.where(j < nb, 0, 1)

    def col(jcol):
        return pl.BlockSpec((P, TOK_TILE, 256), lambda b, j, l, jcol=jcol: (b, blk(j), jcol))

    grid_spec = pltpu.PrefetchScalarGridSpec(
        num_scalar_prefetch=1,
        grid=(B // P, 2 * nb),
        in_specs=[
            col(0),
            pl.BlockSpec((P, TOK_TILE, 256), lambda b, j, l: (b, blk(j), 1 + dirn(j))),
            col(3), col(4), col(5),
            pl.BlockSpec((DEPTH, 2, 256), lambda b, j, l: (0, 0, 0)),
            pl.BlockSpec((1, 1, 256), lambda b, j, l: (l[0], 0, 0)),
            pl.BlockSpec((P, 1, 256, 256), lambda b, j, l: (b, dirn(j), 0, 0)) if s0_per_batch
            else pl.BlockSpec((1, 1, 256, 256), lambda b, j, l: (0, 0, 0, 0)),
        ],
        out_specs=[
            pl.BlockSpec((P, TOK_TILE, 256), lambda b, j, l: (b, jnp.where(j < nb, nb - 1, 2 * nb - 1 - j), 0)),
            pl.BlockSpec((P, 1, 256, 256), lambda b, j, l: (b, dirn(j), 0, 0)),
        ],
        scratch_shapes=[
            pltpu.VMEM((P, 256, 256), F32),
            pltpu.VMEM((P, T, 256), F32),
            pltpu.VMEM((P, TOK_TILE, 256), F32),
            pltpu.VMEM((P, TOK_TILE, 256), F32),
            pltpu.VMEM((P, TOK_TILE, 256), F32),
            pltpu.VMEM((HG_CHUNK, 256), F32),
            pltpu.VMEM((HG_CHUNK, 256), F32),
            pltpu.VMEM((HG_CHUNK, 256), F32),
        ],
    )
    return pl.pallas_call(
        functools.partial(_hg_kernel, nb=nb, s0_per_batch=s0_per_batch),
        grid_spec=grid_spec,
        out_shape=[
            jax.ShapeDtypeStruct((B, T, 256), BF16),
            jax.ShapeDtypeStruct((B, 2, 256, 256), F32),
        ],
        compiler_params=_cparams(("arbitrary", "arbitrary")),
        name="hgrn",
    )(lidx, cols_hg, cols_hg, cols_hg, cols_hg, cols_hg, lb_logits, g3, s0)


def _rot_cols(w, nheads):
    shp = w.shape
    r = shp[-1] // nheads
    w4 = w.reshape(shp[:-1] + (nheads, 2, r // 2))
    return jnp.concatenate([-w4[..., 1, :], w4[..., 0, :]], axis=-1).reshape(shp)


def _perm_heads(w, nheads, order):
    shp = w.shape
    w3 = w.reshape(shp[:-1] + (nheads, shp[-1] // nheads))
    return w3[..., list(order), :].reshape(shp)


_WG_ORDER = (0, 2, 1, 3)


def _mla_rope_slot(a):
    pad = [(0, 0)] * (a.ndim - 1) + [(MLA_NOPE, 128 - MLA_NOPE - MLA_ROPE)]
    return jnp.pad(a, pad)


def _prep_w_in(w_in):
    def c(name):
        lo, hi = _IN_OFF[name]
        return w_in[:, :, lo:hi]

    kva = c('mla_kva')
    lo = _IN_OFF['na_k'][0]
    hi = _IN_OFF['hg_z'][1]
    parts = [
        c('na_q') * (HEAD_DIM ** -0.5 * LOG2E),
        w_in[:, :, lo:hi],
        c('mla_qa'), c('mla_z'), kva[:, :, :MLA_KV_RANK], _mla_rope_slot(kva[:, :, MLA_KV_RANK:]),
        _perm_heads(c('wg_q'), WG_HEADS, _WG_ORDER) * (HEAD_DIM ** -0.5 * LOG2E),
        _perm_heads(c('wg_z'), WG_HEADS, _WG_ORDER),
        c('wg_k'), c('wg_v'),
    ]
    return jnp.concatenate([p.astype(BF16) for p in parts], axis=-1), c('merge').astype(BF16)


def _rot_matrix(width, head):
    i = np.arange(width)[:, None]
    j = np.arange(width)[None, :]
    half = head // 2
    first = (j % head) < half
    m = np.where(first & (i == j + half), -1.0, 0.0) + np.where(~first & (i == j - half), 1.0, 0.0)
    return jnp.asarray(m, BF16)


def _rope_tables(n_tok, rot_dim, reps):
    t = np.arange(n_tok)
    row = (t // GRID_W).astype(np.float32)
    col = (t % GRID_W).astype(np.float32)
    nf = rot_dim // 4
    inv = jnp.asarray(ROPE_BASE, F32) ** (-jnp.arange(nf, dtype=F32) / nf)
    ang = jnp.concatenate([jnp.asarray(row)[:, None] * inv, jnp.asarray(col)[:, None] * inv], axis=-1)
    cos, sin = jnp.cos(ang), jnp.sin(ang)
    return (jnp.tile(jnp.concatenate([cos, cos], axis=-1), (1, reps)),
            jnp.tile(jnp.concatenate([sin, sin], axis=-1), (1, reps)))


def _state_to_blockdiag(st):
    B = st.shape[0]
    stt = jnp.swapaxes(st, -1, -2)
    eye = jnp.eye(HG_HEADS, dtype=st.dtype)
    bd = stt[:, :, :, :, None, :] * eye[None, None, :, None, :, None]
    return bd.reshape(B, 2, HG_HEADS * HG_DK, HG_HEADS * HG_DK)


def _blockdiag_to_state(bd):
    B = bd.shape[0]
    b6 = bd.reshape(B, 2, HG_HEADS, HG_DK, HG_HEADS, HG_DK)
    diag = jnp.stack([b6[:, :, h, :, h, :] for h in range(HG_HEADS)], axis=2)
    return jnp.swapaxes(diag, -1, -2)


def kernel(x_prompt, x_sample, cache_na_k, cache_na_v, state_hgrn, cache_mla_ckv, cache_mla_krope, cache_wg_k, cache_wg_v, c, c_ctx, norm_g, w_mod, b_mod, w_in, na_rpb, hg_lb_logits, hg_norm_g, mla_q_norm_g, mla_kv_norm_g, mla_w_qb, mla_w_kvb, wg_sink, w_branch, w_out, final_g):
    Bp, Tp, _ = x_prompt.shape
    Bd, Td, _ = x_sample.shape
    L = cache_na_k.shape[2]
    rows = Td // GRID_W
    assert Tp % TOK_TILE == 0 and Td % TOK_TILE == 0 and L % TOK_TILE == 0
    assert rows % NA_QROWS == 0 and rows >= NA_KROWS and Td >= WG_KWIN

    w_proj, w_merge = _prep_w_in(w_in)
    wb = w_branch.astype(BF16)
    wb = wb.at[:, 3].set(_perm_heads(jnp.swapaxes(wb[:, 3], -1, -2), WG_HEADS, _WG_ORDER).swapaxes(-1, -2))
    wo = w_out.astype(BF16)
    qb = mla_w_qb.reshape(DEPTH, MLA_Q_RANK, MLA_HEADS, MLA_NOPE + MLA_ROPE)
    qb_g1 = jnp.pad(qb, ((0, 0), (0, 0), (0, 0), (0, 128 - MLA_NOPE - MLA_ROPE)))
    qb_g2 = _mla_rope_slot(_rot_cols(qb[..., MLA_NOPE:], 1))
    w_qb = jnp.concatenate([qb_g1.reshape(DEPTH, MLA_Q_RANK, -1), qb_g2.reshape(DEPTH, MLA_Q_RANK, -1)],
                           axis=-1)
    w_qb = (w_qb * ((MLA_NOPE + MLA_ROPE) ** -0.5 * LOG2E)).astype(BF16)
    kvb = mla_w_kvb.reshape(DEPTH, MLA_KV_RANK, MLA_HEADS, 2 * MLA_NOPE)
    kvb_k = jnp.pad(kvb[..., :MLA_NOPE], ((0, 0), (0, 0), (0, 0), (0, 128 - MLA_NOPE)))
    w_kvb = jnp.concatenate([kvb_k.reshape(DEPTH, MLA_KV_RANK, -1),
                             kvb[..., MLA_NOPE:].reshape(DEPTH, MLA_KV_RANK, -1)], axis=-1).astype(BF16)
    norm_g3 = norm_g.reshape(DEPTH, 1, D_MODEL)
    g_q3 = mla_q_norm_g.reshape(DEPTH, 1, MLA_Q_RANK)
    g_kv3 = mla_kv_norm_g.reshape(DEPTH, 1, MLA_KV_RANK)
    hg_g3 = jnp.tile(hg_norm_g, (1, HG_HEADS)).reshape(DEPTH, 1, HG_HEADS * HG_DK)
    final_g2 = final_g.reshape(1, D_MODEL)
    cq, sq = _rope_tables(Td, HEAD_DIM, WG_HEADS)
    ck, sk = cq[:, :128], sq[:, :128]
    c1, s1 = _rope_tables(Td, MLA_ROPE, 1)
    c4 = jnp.concatenate([jnp.ones((Td, MLA_NOPE), F32), c1, jnp.zeros((Td, 32), F32)], axis=-1)
    s4 = _mla_rope_slot(s1)
    rm_wg = _rot_matrix(WG_HEADS * HEAD_DIM, HEAD_DIM)
    rm_mla = jnp.zeros((128, 128), BF16).at[MLA_NOPE:MLA_NOPE + MLA_ROPE, MLA_NOPE:MLA_NOPE + MLA_ROPE].set(
        _rot_matrix(MLA_ROPE, MLA_ROPE))

    nrow = 8 * ((1 + Bd + 7) // 8)
    cond = jnp.zeros((nrow, D_MODEL), F32).at[0].set(c_ctx).at[1:1 + Bd].set(c)
    mod4 = _modulation(cond, w_mod, b_mod).reshape(DEPTH, nrow, 1, 3 * D_MODEL)

    kstart, ktype, table = _na_row_types(rows)
    na_bias = _na_bias(na_rpb, table)
    kstart = jnp.asarray(kstart, jnp.int32)
    ktype = jnp.asarray(ktype, jnp.int32)

    cna_k = cache_na_k.reshape(Bd, DEPTH, L, NA_HEADS * HEAD_DIM)
    cna_v = cache_na_v.reshape(Bd, DEPTH, L, NA_HEADS * HEAD_DIM)
    cwg_k = cache_wg_k.reshape(Bd, DEPTH, L, WG_KV_HEADS * HEAD_DIM)
    cwg_v = cache_wg_v.reshape(Bd, DEPTH, L, WG_KV_HEADS * HEAD_DIM)
    ckr4 = _mla_rope_slot(cache_mla_krope)
    st_bd = _state_to_blockdiag(state_hgrn.reshape(Bd * DEPTH, 2, HG_HEADS, HG_DK, HG_DK)).reshape(
        Bd, DEPTH, 2, 256, 256)
    st_zero = jnp.zeros((1, 2, 256, 256), F32)

    hp, hs = x_prompt, x_sample
    states = []
    tok_caches = None
    for layer in range(DEPTH):
        lidx = jnp.full((1,), layer, jnp.int32)
        final = layer == DEPTH - 1

        flat = (1, Bp * Tp)
        res = _in_proj(lidx, hp.reshape(flat + (D_MODEL,)), mod4, 0, False, norm_g3, w_proj, g_kv3, w_kvb, rm_mla,
                       c4, s4, False, caches=tok_caches, cache_shape=(Bp, DEPTH, Tp))
        na, hg, mla, wg, kc_, vc_ = (a.reshape(Bp, Tp, a.shape[-1]) for a in res[:6])
        tok_caches = res[6:]
        y_hg, st_out = _hgrn(lidx, hg, hg_lb_logits, hg_g3, st_zero, False)
        y_na, y_wg, y_mla = _attn_ctx(lidx, na, wg_sink, wg, mla, g_q3, w_qb, c4, s4, kc_, vc_)
        ys = tuple(y.reshape(flat + (BRANCH_W,)) for y in (y_na, y_hg, y_mla, y_wg))
        hp = _post(lidx, hp.reshape(flat + (D_MODEL,)), mod4, 0, False, norm_g3, ys, w_merge, wb, wo, final_g2,
                   final).reshape(Bp, Tp, D_MODEL)
        states.append(st_out)

        na, hg, mla, wg, kl, vl = _in_proj(lidx, hs, mod4, 1, True, norm_g3, w_proj, g_kv3, w_kvb, rm_mla, c4, s4,
                                           True)
        y_hg, _ = _hgrn(lidx, hg, hg_lb_logits, hg_g3, st_bd[:, layer], True)
        kx, vx = _mla_kv(lidx, cache_mla_ckv, 0, ckr4, 0, ckr4, 0, g_kv3, w_kvb, c4, s4, False, False, True)
        y_na, y_wg, y_mla = _attn_lat(lidx, kstart, ktype, na, cna_k, cna_v, na_bias, wg_sink, wg, cwg_k, cwg_v,
                                      cq, sq, ck, sk, rm_wg, mla, g_q3, w_qb, c4, s4, kl, vl, kx, vx)
        hs = _post(lidx, hs, mod4, 1, True, norm_g3, (y_na, y_hg, y_mla, y_wg), w_merge, wb, wo, final_g2, final)

    new_state = _blockdiag_to_state(jnp.stack(states, axis=1).reshape(Bp * DEPTH, 2, 256, 256)).reshape(
        Bp, DEPTH, 2, HG_HEADS, HG_DK, HG_DK)
    na_k, na_v, ckv_c, kr_c, wg_k, wg_v = tok_caches
    return (hp, hs,
            na_k.reshape(Bp, DEPTH, Tp, NA_HEADS, HEAD_DIM), na_v.reshape(Bp, DEPTH, Tp, NA_HEADS, HEAD_DIM),
            new_state, ckv_c, kr_c,
            wg_k.reshape(Bp, DEPTH, Tp, WG_KV_HEADS, HEAD_DIM), wg_v.reshape(Bp, DEPTH, Tp, WG_KV_HEADS, HEAD_DIM))
```

```python
import functools

import jax
import jax.numpy as jnp
import numpy as np
from jax import lax
from jax.experimental import pallas as pl
from jax.experimental.pallas import tpu as pltpu

F32 = jnp.float32
BF16 = jnp.bfloat16

D_MODEL = 1024
DEPTH = 4
GRID_W = 64
HEAD_DIM = 64
BRANCH_W = 256
N_BRANCH = 4
NA_HEADS = 4
NA_WIN_H = 8
NA_WIN_W = 16
HG_HEADS = 4
HG_DK = 64
HG_CHUNK = 64
HG_SUB = 32
HG_PAIR = 2
HG_FAST_SPAN = 100.0
LOG2E = 1.4426950408889634
F_FLOOR = 1e-30
MLA_HEADS = 4
MLA_Q_RANK = 256
MLA_KV_RANK = 128
MLA_NOPE = 64
MLA_ROPE = 32
WG_HEADS = 4
WG_KV_HEADS = 2
WG_WINDOW = 128
ROPE_BASE = 10000.0
EPS = 1e-6
NEG_INF = -1e30

_IN_LAYOUT = (
    ('na_q', 256), ('na_k', 256), ('na_v', 256), ('na_z', 256),
    ('hg_q', 256), ('hg_ff', 256), ('hg_fb', 256), ('hg_i', 256), ('hg_og', 256), ('hg_z', 256),
    ('mla_qa', 256), ('mla_kva', 160), ('mla_z', 256),
    ('wg_q', 256), ('wg_k', 128), ('wg_v', 128), ('wg_z', 256),
    ('merge', 4096),
)
_IN_OFF = {}
_o = 0
for _n, _w in _IN_LAYOUT:
    _IN_OFF[_n] = (_o, _o + _w)
    _o += _w

NA_W = 1024
HG_W = 1536
MLA_W = 768
WG_W = 768
PROJ_W = NA_W + HG_W + MLA_W + WG_W

TOK_TILE = 256
PROJ_TILE = 512
NA_QROWS = 4
NA_KROWS = 12
WG_TILE = 256
WG_KWIN = WG_TILE + 2 * WG_WINDOW
VMEM_LIMIT = 56 * 1024 * 1024


def _cparams(sem):
    return pltpu.CompilerParams(dimension_semantics=sem, vmem_limit_bytes=VMEM_LIMIT)


def _dot(a, b):
    return jnp.dot(a, b, preferred_element_type=F32)


def _dot_nt(a, b):
    return lax.dot_general(a, b, (((1,), (1,)), ((), ())), preferred_element_type=F32)


def _dot_tn(a, b):
    return lax.dot_general(a, b, (((0,), (0,)), ((), ())), preferred_element_type=F32)


def _sigmoid(x):
    return 1.0 / (1.0 + jnp.exp(-x))


def _silu(x):
    return x * _sigmoid(x)


def _lane_mask(width, lo, hi, dtype):
    lane = lax.broadcasted_iota(jnp.int32, (1, width), 1)
    return ((lane >= lo) & (lane < hi)).astype(dtype)


def _seg_matrix(width, seg):
    r = lax.broadcasted_iota(jnp.int32, (width, width), 0) // seg
    c = lax.broadcasted_iota(jnp.int32, (width, width), 1) // seg
    return r == c


def _seg_mean(x2, seg_bf16, seg):
    hi = x2.astype(BF16)
    lo = (x2 - hi.astype(F32)).astype(BF16)
    return (_dot(hi, seg_bf16) + _dot(lo, seg_bf16)) * (1.0 / seg)


def _mod_kernel(c_ref, w_ref, b_ref, o_ref):
    s = _silu(c_ref[...])
    o_ref[0] = _dot(s.astype(BF16), w_ref[0].astype(BF16)) + b_ref[0]


def _modulation(cond, w_mod, b_mod):
    nrow = cond.shape[0]
    tn = 1024
    return pl.pallas_call(
        _mod_kernel,
        grid=(DEPTH, 3 * D_MODEL // tn),
        in_specs=[
            pl.BlockSpec((nrow, D_MODEL), lambda l, j: (0, 0)),
            pl.BlockSpec((1, D_MODEL, tn), lambda l, j: (l, 0, j)),
            pl.BlockSpec((1, 1, tn), lambda l, j: (l, 0, j)),
        ],
        out_specs=pl.BlockSpec((1, nrow, tn), lambda l, j: (l, 0, j)),
        out_shape=jax.ShapeDtypeStruct((DEPTH, nrow, 3 * D_MODEL), F32),
        compiler_params=_cparams(("arbitrary", "arbitrary")),
        name="modulation",
    )(cond, w_mod, b_mod.reshape(DEPTH, 1, 3 * D_MODEL))


def _mod_norm(x, mod_ref, g_ref):
    ms = jnp.mean(x * x, axis=-1, keepdims=True)
    y = x * lax.rsqrt(ms + EPS) * g_ref[0]
    shift = mod_ref[0, 0, :, 0:D_MODEL]
    scale = mod_ref[0, 0, :, D_MODEL:2 * D_MODEL]
    gate = mod_ref[0, 0, :, 2 * D_MODEL:3 * D_MODEL]
    return y * (1.0 + scale) + shift, gate


_CACHE_WIDTHS = (NA_HEADS * HEAD_DIM, NA_HEADS * HEAD_DIM, MLA_KV_RANK, MLA_ROPE,
                 WG_KV_HEADS * HEAD_DIM, WG_KV_HEADS * HEAD_DIM)


def _in_kernel(l_ref, x_ref, mod_ref, g_ref, w_ref, gkv_ref, wkvb_ref, rm_ref, c_ref, s_ref, *rest,
               rope, n_alias, seq):
    rest = rest[n_alias:]
    na_ref, hg_ref, mla_ref, wg_ref, kall_ref, vall_ref = rest[:6]
    h, _ = _mod_norm(x_ref[0], mod_ref, g_ref)
    hb = h.astype(BF16)
    na = _dot(hb, w_ref[0, :, 0:NA_W])
    na_ref[0] = na.astype(na_ref.dtype)
    hg_ref[0] = _dot(hb, w_ref[0, :, NA_W:NA_W + HG_W])
    wg = _dot(hb, w_ref[0, :, NA_W + HG_W + MLA_W:PROJ_W])
    wg_ref[0] = wg.astype(wg_ref.dtype)
    mla = _dot(hb, w_ref[0, :, NA_W + HG_W:NA_W + HG_W + MLA_W])
    mla_ref[0] = mla[:, 0:512].astype(mla_ref.dtype)
    ckv = mla[:, 512:640]
    ms = jnp.mean(ckv * ckv, axis=-1, keepdims=True)
    ckv = ckv * lax.rsqrt(ms + EPS) * gkv_ref[0]
    kr = mla[:, 640:768]
    if not rope:
        pieces = (na[:, 256:512], na[:, 512:768], ckv, kr[:, MLA_NOPE:MLA_NOPE + MLA_ROPE],
                  wg[:, 512:640], wg[:, 640:768])
        for ref, val in zip(rest[6:], pieces):
            for i in range(val.shape[0] // seq):
                ref[i, 0] = val[i * seq:(i + 1) * seq, :]
    if rope:
        kr = kr * c_ref[...] + _dot(kr.astype(BF16), rm_ref[...]) * s_ref[...]
    kv = _dot(ckv.astype(BF16), wkvb_ref[0])
    for hd in range(MLA_HEADS):
        kall_ref[0, :, 128 * hd:128 * hd + 128] = (kv[:, 128 * hd:128 * hd + 128] + kr).astype(BF16)
    vall_ref[0] = kv[:, 512:768].astype(BF16)


def _in_proj(lidx, x, mod4, mod_row0, per_batch, norm_g3, w_proj, g_kv3, w_kvb, rm_mla, c4, s4, rope,
             caches=None, cache_shape=None):
    B, T, _ = x.shape
    tile = min(PROJ_TILE, T)
    nt = T // tile
    mrow = (lambda b: mod_row0 + b) if per_batch else (lambda b: mod_row0)
    col_dtype = BF16

    def tok(width):
        return pl.BlockSpec((1, tile, width), lambda b, t, l: (b, t, 0))

    out_specs = [tok(NA_W), tok(HG_W), tok(512), tok(WG_W), tok(512), tok(256)]
    out_shape = [
        jax.ShapeDtypeStruct((B, T, NA_W), col_dtype),
        jax.ShapeDtypeStruct((B, T, HG_W), F32),
        jax.ShapeDtypeStruct((B, T, 512), col_dtype),
        jax.ShapeDtypeStruct((B, T, WG_W), col_dtype),
        jax.ShapeDtypeStruct((B, T, 512), BF16),
        jax.ShapeDtypeStruct((B, T, 256), BF16),
    ]
    in_specs = [
        pl.BlockSpec((1, tile, D_MODEL), lambda b, t, l: (b, t, 0)),
        pl.BlockSpec((1, 1, 1, 3 * D_MODEL), lambda b, t, l: (l[0], mrow(b), 0, 0)),
        pl.BlockSpec((1, 1, D_MODEL), lambda b, t, l: (l[0], 0, 0)),
        pl.BlockSpec((1, D_MODEL, PROJ_W), lambda b, t, l: (l[0], 0, 0)),
        pl.BlockSpec((1, 1, 128), lambda b, t, l: (l[0], 0, 0)),
        pl.BlockSpec((1, 128, 768), lambda b, t, l: (l[0], 0, 0)),
        pl.BlockSpec((128, 128), lambda b, t, l: (0, 0)),
        pl.BlockSpec((tile, 128), lambda b, t, l: (t if rope else 0, 0)),
        pl.BlockSpec((tile, 128), lambda b, t, l: (t if rope else 0, 0)),
    ]
    args = [lidx, x, mod4, norm_g3, w_proj, g_kv3, w_kvb, rm_mla, c4, s4]
    aliases = {}
    seq = tile
    if not rope:
        nbatch, _, seq = cache_shape
        assert B == 1 and tile % seq == 0
        for width in _CACHE_WIDTHS:
            out_specs.append(pl.BlockSpec((tile // seq, 1, seq, width), lambda b, t, l: (t, l[0], 0, 0)))
            out_shape.append(jax.ShapeDtypeStruct((nbatch, DEPTH, seq, width), F32))
        if caches is not None:
            for i, arr in enumerate(caches):
                aliases[len(args)] = 6 + i
                in_specs.append(pl.BlockSpec(memory_space=pl.ANY))
                args.append(arr)
    grid_spec = pltpu.PrefetchScalarGridSpec(
        num_scalar_prefetch=1,
        grid=(B, nt),
        in_specs=in_specs,
        out_specs=out_specs,
    )
    return pl.pallas_call(
        functools.partial(_in_kernel, rope=rope, n_alias=len(aliases), seq=seq),
        grid_spec=grid_spec,
        out_shape=out_shape,
        input_output_aliases=aliases,
        compiler_params=_cparams(("arbitrary", "arbitrary")),
        name="in_proj_lat" if rope else ("in_proj_ctx" if caches is None else "in_proj_ctx_inplace"),
    )(*args)


def _post_kernel(l_ref, x_ref, mod_ref, g_ref, y0_ref, y1_ref, y2_ref, y3_ref, wm_ref, wb_ref, wo_ref,
                 fg_ref, o_ref, *, final):
    x = x_ref[0]
    h, gate = _mod_norm(x, mod_ref, g_ref)
    hb = h.astype(BF16)
    acc = jnp.zeros(x.shape, F32)
    for n, y_ref in enumerate((y0_ref, y1_ref, y2_ref, y3_ref)):
        mg = _sigmoid(_dot(hb, wm_ref[0, :, n * D_MODEL:(n + 1) * D_MODEL]))
        acc = acc + mg * _dot(y_ref[0], wb_ref[0, n])
    out = x + gate * _dot(acc.astype(BF16), wo_ref[0])
    if final:
        ms = jnp.mean(out * out, axis=-1, keepdims=True)
        out = out * lax.rsqrt(ms + EPS) * fg_ref[...]
    o_ref[0] = out


def _post(lidx, x, mod4, mod_row0, per_batch, norm_g3, ys, w_merge, w_branch, w_out, final_g2, final):
    B, T, _ = x.shape
    tile = min(PROJ_TILE, T)
    nt = T // tile
    mrow = (lambda b: mod_row0 + b) if per_batch else (lambda b: mod_row0)
    yspec = pl.BlockSpec((1, tile, BRANCH_W), lambda b, t, l: (b, t, 0))
    grid_spec = pltpu.PrefetchScalarGridSpec(
        num_scalar_prefetch=1,
        grid=(B, nt),
        in_specs=[
            pl.BlockSpec((1, tile, D_MODEL), lambda b, t, l: (b, t, 0)),
            pl.BlockSpec((1, 1, 1, 3 * D_MODEL), lambda b, t, l: (l[0], mrow(b), 0, 0)),
            pl.BlockSpec((1, 1, D_MODEL), lambda b, t, l: (l[0], 0, 0)),
            yspec, yspec, yspec, yspec,
            pl.BlockSpec((1, D_MODEL, N_BRANCH * D_MODEL), lambda b, t, l: (l[0], 0, 0)),
            pl.BlockSpec((1, N_BRANCH, BRANCH_W, D_MODEL), lambda b, t, l: (l[0], 0, 0, 0)),
            pl.BlockSpec((1, D_MODEL, D_MODEL), lambda b, t, l: (l[0], 0, 0)),
            pl.BlockSpec((1, D_MODEL), lambda b, t, l: (0, 0)),
        ],
        out_specs=pl.BlockSpec((1, tile, D_MODEL), lambda b, t, l: (b, t, 0)),
    )
    return pl.pallas_call(
        functools.partial(_post_kernel, final=final),
        grid_spec=grid_spec,
        out_shape=jax.ShapeDtypeStruct((B, T, D_MODEL), F32),
        compiler_params=_cparams(("arbitrary", "arbitrary")),
        name="post_final" if final else "post",
    )(lidx, x, mod4, norm_g3, *ys, w_merge, w_branch, w_out, final_g2)


def _attend(q, kparts, vparts, biases, qmasks, omasks, sinks=None):
    out = None
    for h in range(len(omasks)):
        qh = q[h] if qmasks is None else q * qmasks[h]
        ss = []
        for i, kp in enumerate(kparts):
            s = _dot_nt(qh, kp[h] if isinstance(kp, (list, tuple)) else kp)
            if biases is not None and biases[h][i] is not None:
                s = s + biases[h][i]
            ss.append(s)
        m = ss[0].max(axis=-1, keepdims=True)
        for s in ss[1:]:
            m = jnp.maximum(m, s.max(axis=-1, keepdims=True))
        if sinks is not None:
            m = jnp.maximum(m, sinks[h])
        den = None
        oh = None
        for s, vp in zip(ss, vparts):
            p = jnp.exp2(s - m)
            ps = p.sum(axis=-1, keepdims=True)
            den = ps if den is None else den + ps
            pv = _dot(p.astype(BF16), vp)
            oh = pv if oh is None else oh + pv
        if sinks is not None:
            den = den + jnp.exp2(sinks[h] - m)
        oh = (oh / den) * omasks[h]
        out = oh if out is None else out + oh
    return out


def _head_masks(width, seg, nheads, dtype):
    return [_lane_mask(width, h * seg, (h + 1) * seg, dtype) for h in range(nheads)]


def _na_ctx_kernel(q_ref, k_ref, v_ref, z_ref, o_ref):
    q = q_ref[0].astype(BF16)
    k = k_ref[0].astype(BF16)
    v = v_ref[0].astype(BF16)
    o = _attend(q, [k], [v], None, _head_masks(256, 64, NA_HEADS, BF16), _head_masks(256, 64, NA_HEADS, F32))
    o_ref[0] = (o * _silu(z_ref[0].astype(F32))).astype(BF16)


def _na_row_types(rows):
    wh = min(NA_WIN_H, rows)
    nblk = rows // NA_QROWS
    starts, types, table = [], [], []
    for blk in range(nblk):
        r0 = blk * NA_QROWS
        k0 = int(np.clip(r0 - wh // 2, 0, rows - NA_KROWS))
        desc = []
        for i in range(NA_QROWS):
            r = r0 + i
            ks = int(np.clip(r - wh // 2, 0, rows - wh))
            for j in range(NA_KROWS):
                kr = k0 + j
                desc.append((kr - r + NA_WIN_H - 1) if ks <= kr < ks + wh else None)
        desc = tuple(desc)
        if desc not in table:
            table.append(desc)
        starts.append(k0)
        types.append(table.index(desc))
    return starts, types, table


def _na_bias_kernel(rpb_ref, o_ref, t2_ref, *, table):
    l = pl.program_id(0)
    h = pl.program_id(1)
    nrow = 2 * NA_WIN_H - 1
    ncol = 2 * NA_WIN_W - 1
    c = lax.broadcasted_iota(jnp.int32, (GRID_W, 128), 0)
    lane = lax.broadcasted_iota(jnp.int32, (GRID_W, 128), 1)
    first = lane < GRID_W
    kc = jnp.where(first, lane, lane - GRID_W)
    qstart = jnp.clip(c - NA_WIN_W // 2, 0, GRID_W - NA_WIN_W)
    ok = (kc >= qstart) & (kc < qstart + NA_WIN_W)
    dcol = kc - c + NA_WIN_W - 1
    for a in range(-1, nrow):
        acc = jnp.full((GRID_W, 128), NEG_INF, F32)
        for bi in range(2 * NA_WIN_W - 1):
            base = (l * NA_HEADS + h) * nrow
            lo = rpb_ref[(base + a) * ncol + bi] if a >= 0 else jnp.float32(NEG_INF)
            hi = rpb_ref[(base + a + 1) * ncol + bi] if a + 1 < nrow else jnp.float32(NEG_INF)
            acc = jnp.where(dcol == bi, jnp.where(first, lo, hi), acc)
        t2_ref[a + 1] = jnp.where(ok, acc * LOG2E, NEG_INF)
    for ti, desc in enumerate(table):
        for i in range(NA_QROWS):
            for p in range(NA_KROWS // 2):
                a0 = desc[i * NA_KROWS + 2 * p]
                a1 = desc[i * NA_KROWS + 2 * p + 1]
                if a0 is None and a1 is None:
                    tile = jnp.full((GRID_W, 128), NEG_INF, F32)
                elif a1 is None:
                    tile = jnp.where(first, t2_ref[a0 + 1], NEG_INF)
                elif a0 is None:
                    tile = jnp.where(first, NEG_INF, t2_ref[a1])
                else:
                    tile = t2_ref[a0 + 1]
                o_ref[0, ti, 0, i * GRID_W:(i + 1) * GRID_W, p * 128:(p + 1) * 128] = tile


def _na_bias(na_rpb, table):
    nt = len(table)
    nq = NA_QROWS * GRID_W
    nk = NA_KROWS * GRID_W
    return pl.pallas_call(
        functools.partial(_na_bias_kernel, table=table),
        grid=(DEPTH, NA_HEADS),
        in_specs=[pl.BlockSpec(memory_space=pltpu.SMEM)],
        out_specs=pl.BlockSpec((1, nt, 1, nq, nk), lambda l, h: (l, 0, h, 0, 0)),
        out_shape=jax.ShapeDtypeStruct((DEPTH, nt, NA_HEADS, nq, nk), F32),
        scratch_shapes=[pltpu.VMEM((2 * NA_WIN_H, GRID_W, 128), F32)],
        compiler_params=_cparams(("arbitrary", "arbitrary")),
        name="na_bias",
    )(na_rpb.reshape(-1))


def _na_lat_kernel(l_ref, ks_ref, ty_ref, q_ref, k_ref, v_ref, z_ref, kc_ref, vc_ref, b_ref, o_ref):
    rb = pl.program_id(1)
    nk = NA_KROWS * GRID_W
    ks = pl.multiple_of(ks_ref[rb] * GRID_W, GRID_W)
    q = q_ref[0].astype(BF16)
    kw = k_ref[0, pl.ds(ks, nk), :].astype(BF16)
    vw = v_ref[0, pl.ds(ks, nk), :].astype(BF16)
    kc = kc_ref[0, 0].astype(BF16)
    vc = vc_ref[0, 0].astype(BF16)
    biases = [[b_ref[0, 0, h], None] for h in range(NA_HEADS)]
    o = _attend(q, [kw, kc], [vw, vc], biases,
                _head_masks(256, 64, NA_HEADS, BF16), _head_masks(256, 64, NA_HEADS, F32))
    o_ref[0] = (o * _silu(z_ref[0].astype(F32))).astype(BF16)


def _wg_heads(q, kparts, vparts, bias, sink_ref, l):
    qm = _head_masks(128, 64, 2, BF16)
    om = _head_masks(128, 64, 2, F32)
    outs = []
    for grp, heads in enumerate(((0, 2), (1, 3))):
        qg = q[:, grp * 128:(grp + 1) * 128]
        sinks = [sink_ref[l, hh] * LOG2E for hh in heads]
        biases = None if bias is None else [[bias] + [None] * (len(kparts) - 1)] * 2
        outs.append(_attend(qg, kparts, vparts, biases, qm, om, sinks))
    return jnp.concatenate(outs, axis=-1)


def _wg_ctx_kernel(l_ref, sink_ref, q_ref, z_ref, k_ref, v_ref, o_ref):
    q = q_ref[0].astype(BF16)
    o = _wg_heads(q, [k_ref[0].astype(BF16)], [v_ref[0].astype(BF16)], None, sink_ref, l_ref[0])
    o_ref[0] = (o * _silu(z_ref[0].astype(F32))).astype(BF16)


def _wg_lat_kernel(l_ref, sink_ref, q_ref, z_ref, k_ref, v_ref, kc_ref, vc_ref,
                   cq_ref, sq_ref, ck_ref, sk_ref, rm_ref, o_ref, *, seq):
    tb = pl.program_id(1)
    t0 = tb * WG_TILE
    k0 = pl.multiple_of(jnp.clip(t0 - WG_WINDOW, 0, seq - WG_KWIN), 128)
    q = q_ref[0]
    q = q.astype(F32) * cq_ref[...] + _dot(q.astype(BF16), rm_ref[...]) * sq_ref[...]
    kwin = k_ref[0, pl.ds(k0, WG_KWIN), :]
    kwin = kwin.astype(F32) * ck_ref[pl.ds(k0, WG_KWIN), :] \
        + _dot(kwin.astype(BF16), rm_ref[0:128, 0:128]) * sk_ref[pl.ds(k0, WG_KWIN), :]
    vwin = v_ref[0, pl.ds(k0, WG_KWIN), :]
    qi = t0 + lax.broadcasted_iota(jnp.int32, (WG_TILE, WG_KWIN), 0)
    kj = k0 + lax.broadcasted_iota(jnp.int32, (WG_TILE, WG_KWIN), 1)
    bias = jnp.where(jnp.abs(qi - kj) <= WG_WINDOW, 0.0, NEG_INF).astype(F32)
    o = _wg_heads(q.astype(BF16),
                  [kwin.astype(BF16), kc_ref[0, 0].astype(BF16)],
                  [vwin.astype(BF16), vc_ref[0, 0].astype(BF16)],
                  bias, sink_ref, l_ref[0])
    o_ref[0] = (o * _silu(z_ref[0].astype(F32))).astype(BF16)


def _mla_kv_kernel(l_ref, ckv_ref, kr_ref, krr_ref, g_ref, w_ref, c_ref, s_ref, k_out, v_out, *rest,
                   norm, rope):
    ckv = ckv_ref[0]
    if norm:
        ms = jnp.mean(ckv * ckv, axis=-1, keepdims=True)
        ckv = ckv * lax.rsqrt(ms + EPS) * g_ref[0]
        rest[0][0] = ckv
    kv = _dot(ckv.astype(BF16), w_ref[0])
    kr = kr_ref[0]
    if rope:
        kr = kr * c_ref[...] + krr_ref[0] * s_ref[...]
    for h in range(MLA_HEADS):
        k_out[0, :, 128 * h:128 * h + 128] = (kv[:, 128 * h:128 * h + 128] + kr).astype(BF16)
    v_out[0] = kv[:, 512:768].astype(BF16)


def _mla_kv(lidx, ckv_arr, ckv_blk, kr_arr, kr_blk, krr_arr, krr_blk, g_kv3, w_kvb, c4, s4, norm, rope,
            layered):
    B = ckv_arr.shape[0]
    T = ckv_arr.shape[-2]
    nt = T // TOK_TILE
    if layered:
        def src(blk):
            return pl.BlockSpec((None, 1, TOK_TILE, 128), lambda b, t, l: (b, l[0], t, 0))
    else:
        def src(blk):
            return pl.BlockSpec((1, TOK_TILE, 128), lambda b, t, l, blk=blk: (b, t, blk))
    out_specs = [
        pl.BlockSpec((1, TOK_TILE, 512), lambda b, t, l: (b, t, 0)),
        pl.BlockSpec((1, TOK_TILE, 256), lambda b, t, l: (b, t, 0)),
    ]
    out_shape = [
        jax.ShapeDtypeStruct((B, T, 512), BF16),
        jax.ShapeDtypeStruct((B, T, 256), BF16),
    ]
    if norm:
        out_specs.append(pl.BlockSpec((1, TOK_TILE, 128), lambda b, t, l: (b, t, 0)))
        out_shape.append(jax.ShapeDtypeStruct((B, T, 128), F32))
    grid_spec = pltpu.PrefetchScalarGridSpec(
        num_scalar_prefetch=1,
        grid=(B, nt),
        in_specs=[
            src(ckv_blk), src(kr_blk), src(krr_blk),
            pl.BlockSpec((1, 1, 128), lambda b, t, l: (l[0], 0, 0)),
            pl.BlockSpec((1, 128, 768), lambda b, t, l: (l[0], 0, 0)),
            pl.BlockSpec((TOK_TILE, 128), lambda b, t, l: (t if rope else 0, 0)),
            pl.BlockSpec((TOK_TILE, 128), lambda b, t, l: (t if rope else 0, 0)),
        ],
        out_specs=out_specs,
    )
    return pl.pallas_call(
        functools.partial(_mla_kv_kernel, norm=norm, rope=rope),
        grid_spec=grid_spec,
        out_shape=out_shape,
        compiler_params=_cparams(("arbitrary", "arbitrary")),
        name="mla_kv" + ("_norm" if norm else "") + ("_rope" if rope else ""),
    )(lidx, ckv_arr, kr_arr, krr_arr, g_kv3, w_kvb, c4, s4)


def _mla_attn_kernel(l_ref, qa_ref, z_ref, g_ref, w_ref, c_ref, s_ref, *refs, nparts, rope):
    kparts = [[refs[2 * i][0, :, 128 * h:128 * h + 128] for h in range(MLA_HEADS)] for i in range(nparts)]
    vparts = [refs[2 * i + 1][0] for i in range(nparts)]
    o_ref = refs[2 * nparts]
    qa = qa_ref[0].astype(F32)
    ms = jnp.mean(qa * qa, axis=-1, keepdims=True)
    qn = qa * lax.rsqrt(ms + EPS) * g_ref[0]
    qq = _dot(qn.astype(BF16), w_ref[0])
    q = []
    for h in range(MLA_HEADS):
        qh = qq[:, 128 * h:128 * h + 128]
        if rope:
            qh = qh * c_ref[...] + qq[:, 512 + 128 * h:512 + 128 * h + 128] * s_ref[...]
        q.append(qh.astype(BF16))
    om = _head_masks(256, 64, MLA_HEADS, F32)
    half = TOK_TILE // 2
    o = jnp.concatenate([_attend([qh[i * half:(i + 1) * half] for qh in q], kparts, vparts, None, None, om)
                         for i in range(2)], axis=0)
    o_ref[0] = (o * _silu(z_ref[0].astype(F32))).astype(BF16)


def _attn_ctx_kernel(l_ref, *refs):
    na_in, refs = refs[:4], refs[4:]
    wg_in, refs = refs[:5], refs[5:]
    mla_in, refs = refs[:8], refs[8:]
    o_na, o_wg, o_mla = refs
    _na_ctx_kernel(*na_in, o_na)
    _wg_ctx_kernel(l_ref, *wg_in, o_wg)
    _mla_attn_kernel(l_ref, *mla_in, o_mla, nparts=1, rope=False)


def _attn_ctx(lidx, cols_na, sink, cols_wg, cols_mla, g_q3, w_qb, c4, s4, kc, vc):
    B, T, _ = cols_na.shape

    def col(width, j):
        return pl.BlockSpec((1, T, width), lambda b, l, j=j: (b, 0, j))

    in_specs = [
        col(256, 0), col(256, 1), col(256, 2), col(256, 3),
        pl.BlockSpec(memory_space=pltpu.SMEM), col(256, 0), col(256, 1), col(128, 4), col(128, 5),
        col(256, 0), col(256, 1),
        pl.BlockSpec((1, 1, 256), lambda b, l: (l[0], 0, 0)),
        pl.BlockSpec((1, 256, 1024), lambda b, l: (l[0], 0, 0)),
        pl.BlockSpec((TOK_TILE, 128), lambda b, l: (0, 0)),
        pl.BlockSpec((TOK_TILE, 128), lambda b, l: (0, 0)),
        col(512, 0), col(256, 0),
    ]
    out = pl.BlockSpec((1, T, 256), lambda b, l: (b, 0, 0))
    grid_spec = pltpu.PrefetchScalarGridSpec(num_scalar_prefetch=1, grid=(B,), in_specs=in_specs,
                                             out_specs=[out, out, out])
    return pl.pallas_call(
        _attn_ctx_kernel,
        grid_spec=grid_spec,
        out_shape=[jax.ShapeDtypeStruct((B, T, 256), BF16)] * 3,
        compiler_params=_cparams(("arbitrary",)),
        name="attn_ctx",
    )(lidx, cols_na, cols_na, cols_na, cols_na, sink, cols_wg, cols_wg, cols_wg, cols_wg,
      cols_mla, cols_mla, g_q3, w_qb, c4, s4, kc, vc)


def _attn_lat_kernel(l_ref, ks_ref, ty_ref, *refs, seq):
    na_in, refs = refs[:7], refs[7:]
    wg_in, refs = refs[:12], refs[12:]
    mla_in, refs = refs[:10], refs[10:]
    o_na, o_wg, o_mla = refs
    _na_lat_kernel(l_ref, ks_ref, ty_ref, *na_in, o_na)
    _wg_lat_kernel(l_ref, *wg_in, o_wg, seq=seq)
    _mla_attn_kernel(l_ref, *mla_in, o_mla, nparts=2, rope=True)


def _attn_lat(lidx, kstart, ktype, cols_na, cna_k, cna_v, bias, sink, cols_wg, cwg_k, cwg_v, cq, sq, ck, sk, rm,
              cols_mla, g_q3, w_qb, c4, s4, kl, vl, kx, vx):
    B, T, _ = cols_na.shape
    L = cna_k.shape[2]
    nq = NA_QROWS * GRID_W
    nk = NA_KROWS * GRID_W
    assert nq == TOK_TILE == WG_TILE
    nt = T // TOK_TILE

    def tok(width, col):
        return pl.BlockSpec((1, TOK_TILE, width), lambda b, t, l, ks, ty, col=col: (b, t, col))

    def full(rows, width, col):
        return pl.BlockSpec((1, rows, width), lambda b, t, l, ks, ty, col=col: (b, 0, col))

    def layer(rows, width):
        return pl.BlockSpec((1, 1, rows, width), lambda b, t, l, ks, ty: (b, l[0], 0, 0))

    def table(width):
        return pl.BlockSpec((TOK_TILE, width), lambda b, t, l, ks, ty: (t, 0))

    in_specs = [
        tok(256, 0), full(T, 256, 1), full(T, 256, 2), tok(256, 3), layer(L, 256), layer(L, 256),
        pl.BlockSpec((1, 1, NA_HEADS, nq, nk), lambda b, t, l, ks, ty: (l[0], ty[t], 0, 0, 0)),
        pl.BlockSpec(memory_space=pltpu.SMEM),
        tok(256, 0), tok(256, 1), full(T, 128, 4), full(T, 128, 5), layer(L, 128), layer(L, 128),
        table(256), table(256),
        pl.BlockSpec((T, 128), lambda b, t, l, ks, ty: (0, 0)),
        pl.BlockSpec((T, 128), lambda b, t, l, ks, ty: (0, 0)),
        pl.BlockSpec((256, 256), lambda b, t, l, ks, ty: (0, 0)),
        tok(256, 0), tok(256, 1),
        pl.BlockSpec((1, 1, 256), lambda b, t, l, ks, ty: (l[0], 0, 0)),
        pl.BlockSpec((1, 256, 1024), lambda b, t, l, ks, ty: (l[0], 0, 0)),
        table(128), table(128),
        full(kl.shape[1], 512, 0), full(vl.shape[1], 256, 0), full(kx.shape[1], 512, 0), full(vx.shape[1], 256, 0),
    ]
    out = pl.BlockSpec((1, TOK_TILE, 256), lambda b, t, l, ks, ty: (b, t, 0))
    grid_spec = pltpu.PrefetchScalarGridSpec(
        num_scalar_prefetch=3,
        grid=(B, nt),
        in_specs=in_specs,
        out_specs=[out, out, out],
    )
    return pl.pallas_call(
        functools.partial(_attn_lat_kernel, seq=T),
        grid_spec=grid_spec,
        out_shape=[jax.ShapeDtypeStruct((B, T, 256), BF16)] * 3,
        compiler_params=_cparams(("arbitrary", "arbitrary")),
        name="attn_lat",
    )(lidx, kstart, ktype,
      cols_na, cols_na, cols_na, cols_na, cna_k, cna_v, bias,
      sink, cols_wg, cols_wg, cols_wg, cols_wg, cwg_k, cwg_v, cq, sq, ck, sk, rm,
      cols_mla, cols_mla, g_q3, w_qb, c4, s4, kl, vl, kx, vx)


def _hg_blocks(rev, include_diag):
    C, S = HG_CHUNK, HG_SUB
    out = []
    for g0 in range(0, C, S):
        if rev:
            lo, hi = (g0 if include_diag else g0 + S), C
        else:
            lo, hi = 0, (g0 + S if include_diag else g0)
        if hi > lo:
            out.append((g0, lo, hi))
    return out


def _hg_mask(rev, include_diag):
    blocks = _hg_blocks(rev, include_diag)
    S = HG_SUB
    nrow = len(blocks) * HG_HEADS * S
    ncol = 128 * -(-sum(hi - lo for _, lo, hi in blocks) // 128)
    r = lax.broadcasted_iota(jnp.int32, (nrow, ncol), 0)
    cidx = lax.broadcasted_iota(jnp.int32, (nrow, ncol), 1)
    rblk = r // (HG_HEADS * S)
    cblk = jnp.full_like(cidx, len(blocks))
    s = jnp.zeros_like(cidx)
    t = r % S
    start = 0
    for k_, (g0, lo, hi) in enumerate(blocks):
        inb = (cidx >= start) & (cidx < start + hi - lo)
        cblk = jnp.where(inb, k_, cblk)
        s = jnp.where(inb, cidx - start + lo, s)
        t = jnp.where(rblk == k_, t + g0, t)
        start += hi - lo
    ok = rblk == cblk
    if include_diag:
        ok = ok & ((s >= t) if rev else (s <= t))
    return ok.astype(F32)


def _hg_stacked(qs, kk, v, b2, rev, include_diag, mask, hm):
    C, S = HG_CHUNK, HG_SUB
    blocks = _hg_blocks(rev, include_diag)
    qx, kd_rows, v_rows = [], [], []
    for g0, lo, hi in blocks:
        has_ref = (g0 + S < C) if rev else (g0 > 0)
        eq = b2[g0:g0 + S, :]
        es = -b2[lo:hi, :]
        if has_ref:
            ct = b2[g0 + S:g0 + S + 1, :] if rev else b2[g0 - 1:g0, :]
            eq = eq - ct
            es = es + ct
        qd = qs[g0:g0 + S, :] * jnp.exp2(eq)
        qx += [qd * m for m in hm]
        kd_rows.append(kk[lo:hi, :] * jnp.exp2(es))
        v_rows.append(v[lo:hi, :])
    pad = mask.shape[1] - sum(hi - lo for _, lo, hi in blocks)
    if pad:
        kd_rows.append(jnp.zeros((pad, 256), F32))
        v_rows.append(jnp.zeros((pad, 256), F32))
    kd_all = jnp.concatenate(kd_rows, axis=0).astype(BF16)
    v_all = jnp.concatenate(v_rows, axis=0).astype(BF16)
    att = _dot_nt(jnp.concatenate(qx, axis=0).astype(BF16), kd_all)
    att = jnp.where(mask > 0.5, att, 0.0)
    return _dot(att.astype(BF16), v_all), blocks


def _hg_gates(lb, rev, q_ref, f_ref, qs_sc, lf_sc, kk_sc):
    qs_sc[...] = _silu(q_ref[0])
    fz = f_ref[0]
    e = jnp.exp(-jnp.abs(fz))
    big = 1.0 / (1.0 + e)
    small_ = e * big
    pos = fz >= 0.0
    f = lb + (1.0 - lb) * jnp.where(pos, big, small_)
    lf = jnp.log(jnp.maximum(f, F_FLOOR))
    kk_sc[...] = (1.0 - lb) * jnp.where(pos, small_, big)
    rr = lax.broadcasted_iota(jnp.int32, (TOK_TILE, TOK_TILE), 0)
    cc = lax.broadcasted_iota(jnp.int32, (TOK_TILE, TOK_TILE), 1)
    tri = ((rr // HG_CHUNK == cc // HG_CHUNK) & ((cc >= rr) if rev else (cc <= rr))).astype(BF16)
    hi = lf.astype(BF16)
    r1 = lf - hi.astype(F32)
    mid = r1.astype(BF16)
    lo = (r1 - mid.astype(F32)).astype(BF16)
    lf_sc[...] = _dot(tri, hi) + _dot(tri, mid) + _dot(tri, lo)
    nblk = TOK_TILE // HG_SUB
    r = lax.broadcasted_iota(jnp.int32, (nblk, TOK_TILE), 0)
    cidx = lax.broadcasted_iota(jnp.int32, (nblk, TOK_TILE), 1)
    span = _dot((cidx // HG_SUB == r).astype(BF16), (lf * (-LOG2E)).astype(BF16))
    return jnp.max(span) <= HG_FAST_SPAN


def _hg_chunk(r0, rev, fast, st, qs_sc, lf_sc, kk_sc, i_ref, a_sc, v_sc, o_sc, seg, bdmask, mask):
    C = HG_CHUNK
    S = HG_SUB
    qs = qs_sc[pl.ds(r0, C), :]
    b = lf_sc[pl.ds(r0, C), :]
    kk = kk_sc[pl.ds(r0, C), :]
    v = i_ref[0, pl.ds(r0, C), :]
    b2 = b * LOG2E
    total = b[0:1, :] if rev else b[C - 1:C, :]
    inter = _dot_nt((qs * jnp.exp2(b2)).astype(BF16), st.astype(BF16))
    hm = _head_masks(256, HG_DK, HG_HEADS, F32)
    off, blocks = _hg_stacked(qs, kk, v, b2, rev, fast, mask, hm)

    def head_sum(k_, i):
        acc = None
        for h in range(HG_HEADS):
            r_ = k_ * HG_HEADS * S + h * S + 8 * i
            t_ = off[r_:r_ + 8, :] * hm[h]
            acc = t_ if acc is None else acc + t_
        return acc

    if fast:
        tiles = []
        for k_, (g0, _, _) in enumerate(blocks):
            for i in range(S // 8):
                tiles.append(inter[g0 + 8 * i:g0 + 8 * i + 8, :] + head_sum(k_, i))
        o = jnp.concatenate(tiles, axis=0)
    else:
        a_sc[...] = jnp.log2(kk) - b2
        v_sc[...] = v
        o_sc[...] = inter
        for k_, (g0, _, _) in enumerate(blocks):
            for i in range(S // 8):
                o_sc[g0 + 8 * i:g0 + 8 * i + 8, :] = o_sc[g0 + 8 * i:g0 + 8 * i + 8, :] + head_sum(k_, i)
        sub = lax.broadcasted_iota(jnp.int32, (8, 256), 0)
        for g0 in range(0, C, S):
            parts, spans = [], []
            for s in range(g0, g0 + S):
                t0 = (s // 8) * 8
                arow = a_sc[s:s + 1, :]
                keep = (sub <= s - t0) if rev else (sub >= s - t0)
                rdiag = qs[t0:t0 + 8, :] * jnp.exp2(jnp.where(keep, b2[t0:t0 + 8, :] + arow, NEG_INF))
                rest_lo, rest_hi = (g0, t0) if rev else (t0 + 8, g0 + S)
                if rest_hi > rest_lo:
                    rrest = qs[rest_lo:rest_hi, :] * jnp.exp2(b2[rest_lo:rest_hi, :] + arow)
                    parts += [rrest, rdiag] if rev else [rdiag, rrest]
                else:
                    parts.append(rdiag)
                spans.append((min(rest_lo, t0), max(rest_hi, t0 + 8)) if rest_hi > rest_lo else (t0, t0 + 8))
            res = _dot(jnp.concatenate(parts, axis=0).astype(BF16), seg)
            acc = [o_sc[g0 + 8 * i:g0 + 8 * i + 8, :] for i in range(S // 8)]
            pos = 0
            for s, (lo_r, hi_r) in zip(range(g0, g0 + S), spans):
                vrow = v_sc[s:s + 1, :]
                for i in range((hi_r - lo_r) // 8):
                    ti = (lo_r - g0) // 8 + i
                    acc[ti] = acc[ti] + res[pos + 8 * i:pos + 8 * i + 8, :] * vrow
                pos += hi_r - lo_r
            for i in range(S // 8):
                o_sc[g0 + 8 * i:g0 + 8 * i + 8, :] = acc[i]
        o = o_sc[...]
    kd = (kk * jnp.exp(total - b)).astype(BF16)
    return o, jnp.exp(total) * st + bdmask * _dot_tn(v.astype(BF16), kd)


def _hg_kernel(l_ref, q_ref, f_ref, i_ref, og_ref, z_ref, lbl_ref, g_ref, s0_ref, y_ref, sto_ref,
               st_ref, oacc_ref, qs_sc, lf_sc, kk_sc, a_sc, v_sc, o_sc, *, nb, s0_per_batch):
    j = pl.program_id(1)
    l = l_ref[0]
    C = HG_CHUNK
    ncb = TOK_TILE // C
    P = q_ref.shape[0]

    @pl.when((j == 0) | (j == nb))
    def _():
        for e in range(P):
            st_ref[e] = s0_ref[e if s0_per_batch else 0, 0]

    lg = lbl_ref[...]
    mx = lg[0]
    for m in range(1, DEPTH):
        mx = jnp.maximum(mx, lg[m])
    ex = [jnp.exp(lg[m] - mx) for m in range(DEPTH)]
    tot = ex[0]
    for m in range(1, DEPTH):
        tot = tot + ex[m]
    lbs = jnp.zeros_like(tot)
    for m in range(1, DEPTH):
        lbs = lbs + jnp.where(m <= l, ex[m] / tot, 0.0)

    seg = _seg_matrix(256, HG_DK)
    seg_b = seg.astype(BF16)
    bdmask = seg.astype(F32)

    def run(rev, emit):
        lb = lbs[1:2, :] if rev else lbs[0:1, :]
        small = None
        for e in range(P):
            s_e = _hg_gates(lb, rev, q_ref.at[pl.ds(e, 1)], f_ref.at[pl.ds(e, 1)], qs_sc.at[e], lf_sc.at[e],
                            kk_sc.at[e])
            small = s_e if small is None else small & s_e

        def common(e):
            return (qs_sc.at[e], lf_sc.at[e], kk_sc.at[e], i_ref.at[pl.ds(e, 1)], a_sc, v_sc, o_sc, seg_b, bdmask)

        @pl.when(small)
        def _():
            mask = _hg_mask(rev, True)
            sts = [st_ref[e] for e in range(P)]
            for ci in range(ncb):
                r0 = (ncb - 1 - ci if rev else ci) * C
                for e in range(P):
                    o, sts[e] = _hg_chunk(r0, rev, True, sts[e], *common(e), mask)
                    emit(e, r0, o)
            for e in range(P):
                st_ref[e] = sts[e]

        @pl.when(jnp.logical_not(small))
        def _():
            mask = _hg_mask(rev, False)
            for e in range(P):
                def body(ci, carry, e=e):
                    r0 = pl.multiple_of((ncb - 1 - ci if rev else ci) * C, C)
                    o, st = _hg_chunk(r0, rev, False, st_ref[e], *common(e), mask)
                    st_ref[e] = st
                    emit(e, r0, o)
                    return carry

                lax.fori_loop(0, ncb, body, 0)

    @pl.when(j < nb)
    def _():
        def emit(e, r0, o):
            oacc_ref[e, pl.ds(pl.multiple_of(j * TOK_TILE + r0, C), C), :] = o

        run(False, emit)

    @pl.when(j >= nb)
    def _():
        blk = 2 * nb - 1 - j

        def emit(e, r0, o):
            o = oacc_ref[e, pl.ds(pl.multiple_of(blk * TOK_TILE + r0, C), C), :] + o
            ms = _seg_mean(o * o, seg_b, HG_DK)
            on = o * lax.rsqrt(ms + EPS) * g_ref[0]
            on = on * _sigmoid(og_ref[e, pl.ds(r0, C), :])
            y_ref[e, pl.ds(r0, C), :] = (on * _silu(z_ref[e, pl.ds(r0, C), :])).astype(BF16)

        run(True, emit)

    for e in range(P):
        sto_ref[e, 0] = st_ref[e]


def _hgrn(lidx, cols_hg, lb_logits, g3, s0, s0_per_batch):
    B, T, _ = cols_hg.shape
    nb = T // TOK_TILE
    P = HG_PAIR
    assert B % P == 0

    def blk(j):
        return jnp.where(j < nb, j, 2 * nb - 1 - j)

    def dirn(j):
        return jnp.where(j < nb, 0, 1)

    def col(jcol):
        return pl.BlockSpec((P, TOK_TILE, 256), lambda b, j, l, jcol=jcol: (b, blk(j), jcol))

    grid_spec = pltpu.PrefetchScalarGridSpec(
        num_scalar_prefetch=1,
        grid=(B // P, 2 * nb),
        in_specs=[
            col(0),
            pl.BlockSpec((P, TOK_TILE, 256), lambda b, j, l: (b, blk(j), 1 + dirn(j))),
            col(3), col(4), col(5),
            pl.BlockSpec((DEPTH, 2, 256), lambda b, j, l: (0, 0, 0)),
            pl.BlockSpec((1, 1, 256), lambda b, j, l: (l[0], 0, 0)),
            pl.BlockSpec((P, 1, 256, 256), lambda b, j, l: (b, dirn(j), 0, 0)) if s0_per_batch
            else pl.BlockSpec((1, 1, 256, 256), lambda b, j, l: (0, 0, 0, 0)),
        ],
        out_specs=[
            pl.BlockSpec((P, TOK_TILE, 256), lambda b, j, l: (b, jnp.where(j < nb, nb - 1, 2 * nb - 1 - j), 0)),
            pl.BlockSpec((P, 1, 256, 256), lambda b, j, l: (b, dirn(j), 0, 0)),
        ],
        scratch_shapes=[
            pltpu.VMEM((P, 256, 256), F32),
            pltpu.VMEM((P, T, 256), F32),
            pltpu.VMEM((P, TOK_TILE, 256), F32),
            pltpu.VMEM((P, TOK_TILE, 256), F32),
            pltpu.VMEM((P, TOK_TILE, 256), F32),
            pltpu.VMEM((HG_CHUNK, 256), F32),
            pltpu.VMEM((HG_CHUNK, 256), F32),
            pltpu.VMEM((HG_CHUNK, 256), F32),
        ],
    )
    return pl.pallas_call(
        functools.partial(_hg_kernel, nb=nb, s0_per_batch=s0_per_batch),
        grid_spec=grid_spec,
        out_shape=[
            jax.ShapeDtypeStruct((B, T, 256), BF16),
            jax.ShapeDtypeStruct((B, 2, 256, 256), F32),
        ],
        compiler_params=_cparams(("arbitrary", "arbitrary")),
        name="hgrn",
    )(lidx, cols_hg, cols_hg, cols_hg, cols_hg, cols_hg, lb_logits, g3, s0)


def _rot_cols(w, nheads):
    shp = w.shape
    r = shp[-1] // nheads
    w4 = w.reshape(shp[:-1] + (nheads, 2, r // 2))
    return jnp.concatenate([-w4[..., 1, :], w4[..., 0, :]], axis=-1).reshape(shp)


def _perm_heads(w, nheads, order):
    shp = w.shape
    w3 = w.reshape(shp[:-1] + (nheads, shp[-1] // nheads))
    return w3[..., list(order), :].reshape(shp)


_WG_ORDER = (0, 2, 1, 3)


def _mla_rope_slot(a):
    pad = [(0, 0)] * (a.ndim - 1) + [(MLA_NOPE, 128 - MLA_NOPE - MLA_ROPE)]
    return jnp.pad(a, pad)


def _prep_w_in(w_in):
    def c(name):
        lo, hi = _IN_OFF[name]
        return w_in[:, :, lo:hi]

    kva = c('mla_kva')
    lo = _IN_OFF['na_k'][0]
    hi = _IN_OFF['hg_z'][1]
    parts = [
        c('na_q') * (HEAD_DIM ** -0.5 * LOG2E),
        w_in[:, :, lo:hi],
        c('mla_qa'), c('mla_z'), kva[:, :, :MLA_KV_RANK], _mla_rope_slot(kva[:, :, MLA_KV_RANK:]),
        _perm_heads(c('wg_q'), WG_HEADS, _WG_ORDER) * (HEAD_DIM ** -0.5 * LOG2E),
        _perm_heads(c('wg_z'), WG_HEADS, _WG_ORDER),
        c('wg_k'), c('wg_v'),
    ]
    return jnp.concatenate([p.astype(BF16) for p in parts], axis=-1), c('merge').astype(BF16)


def _rot_matrix(width, head):
    i = np.arange(width)[:, None]
    j = np.arange(width)[None, :]
    half = head // 2
    first = (j % head) < half
    m = np.where(first & (i == j + half), -1.0, 0.0) + np.where(~first & (i == j - half), 1.0, 0.0)
    return jnp.asarray(m, BF16)


def _rope_tables(n_tok, rot_dim, reps):
    t = np.arange(n_tok)
    row = (t // GRID_W).astype(np.float32)
    col = (t % GRID_W).astype(np.float32)
    nf = rot_dim // 4
    inv = jnp.asarray(ROPE_BASE, F32) ** (-jnp.arange(nf, dtype=F32) / nf)
    ang = jnp.concatenate([jnp.asarray(row)[:, None] * inv, jnp.asarray(col)[:, None] * inv], axis=-1)
    cos, sin = jnp.cos(ang), jnp.sin(ang)
    return (jnp.tile(jnp.concatenate([cos, cos], axis=-1), (1, reps)),
            jnp.tile(jnp.concatenate([sin, sin], axis=-1), (1, reps)))


def _state_to_blockdiag(st):
    B = st.shape[0]
    stt = jnp.swapaxes(st, -1, -2)
    eye = jnp.eye(HG_HEADS, dtype=st.dtype)
    bd = stt[:, :, :, :, None, :] * eye[None, None, :, None, :, None]
    return bd.reshape(B, 2, HG_HEADS * HG_DK, HG_HEADS * HG_DK)


def _blockdiag_to_state(bd):
    B = bd.shape[0]
    b6 = bd.reshape(B, 2, HG_HEADS, HG_DK, HG_HEADS, HG_DK)
    diag = jnp.stack([b6[:, :, h, :, h, :] for h in range(HG_HEADS)], axis=2)
    return jnp.swapaxes(diag, -1, -2)


def kernel(x_prompt, x_sample, cache_na_k, cache_na_v, state_hgrn, cache_mla_ckv, cache_mla_krope, cache_wg_k, cache_wg_v, c, c_ctx, norm_g, w_mod, b_mod, w_in, na_rpb, hg_lb_logits, hg_norm_g, mla_q_norm_g, mla_kv_norm_g, mla_w_qb, mla_w_kvb, wg_sink, w_branch, w_out, final_g):
    Bp, Tp, _ = x_prompt.shape
    Bd, Td, _ = x_sample.shape
    L = cache_na_k.shape[2]
    rows = Td // GRID_W
    assert Tp % TOK_TILE == 0 and Td % TOK_TILE == 0 and L % TOK_TILE == 0
    assert rows % NA_QROWS == 0 and rows >= NA_KROWS and Td >= WG_KWIN

    w_proj, w_merge = _prep_w_in(w_in)
    wb = w_branch.astype(BF16)
    wb = wb.at[:, 3].set(_perm_heads(jnp.swapaxes(wb[:, 3], -1, -2), WG_HEADS, _WG_ORDER).swapaxes(-1, -2))
    wo = w_out.astype(BF16)
    qb = mla_w_qb.reshape(DEPTH, MLA_Q_RANK, MLA_HEADS, MLA_NOPE + MLA_ROPE)
    qb_g1 = jnp.pad(qb, ((0, 0), (0, 0), (0, 0), (0, 128 - MLA_NOPE - MLA_ROPE)))
    qb_g2 = _mla_rope_slot(_rot_cols(qb[..., MLA_NOPE:], 1))
    w_qb = jnp.concatenate([qb_g1.reshape(DEPTH, MLA_Q_RANK, -1), qb_g2.reshape(DEPTH, MLA_Q_RANK, -1)],
                           axis=-1)
    w_qb = (w_qb * ((MLA_NOPE + MLA_ROPE) ** -0.5 * LOG2E)).astype(BF16)
    kvb = mla_w_kvb.reshape(DEPTH, MLA_KV_RANK, MLA_HEADS, 2 * MLA_NOPE)
    kvb_k = jnp.pad(kvb[..., :MLA_NOPE], ((0, 0), (0, 0), (0, 0), (0, 128 - MLA_NOPE)))
    w_kvb = jnp.concatenate([kvb_k.reshape(DEPTH, MLA_KV_RANK, -1),
                             kvb[..., MLA_NOPE:].reshape(DEPTH, MLA_KV_RANK, -1)], axis=-1).astype(BF16)
    norm_g3 = norm_g.reshape(DEPTH, 1, D_MODEL)
    g_q3 = mla_q_norm_g.reshape(DEPTH, 1, MLA_Q_RANK)
    g_kv3 = mla_kv_norm_g.reshape(DEPTH, 1, MLA_KV_RANK)
    hg_g3 = jnp.tile(hg_norm_g, (1, HG_HEADS)).reshape(DEPTH, 1, HG_HEADS * HG_DK)
    final_g2 = final_g.reshape(1, D_MODEL)
    cq, sq = _rope_tables(Td, HEAD_DIM, WG_HEADS)
    ck, sk = cq[:, :128], sq[:, :128]
    c1, s1 = _rope_tables(Td, MLA_ROPE, 1)
    c4 = jnp.concatenate([jnp.ones((Td, MLA_NOPE), F32), c1, jnp.zeros((Td, 32), F32)], axis=-1)
    s4 = _mla_rope_slot(s1)
    rm_wg = _rot_matrix(WG_HEADS * HEAD_DIM, HEAD_DIM)
    rm_mla = jnp.zeros((128, 128), BF16).at[MLA_NOPE:MLA_NOPE + MLA_ROPE, MLA_NOPE:MLA_NOPE + MLA_ROPE].set(
        _rot_matrix(MLA_ROPE, MLA_ROPE))

    nrow = 8 * ((1 + Bd + 7) // 8)
    cond = jnp.zeros((nrow, D_MODEL), F32).at[0].set(c_ctx).at[1:1 + Bd].set(c)
    mod4 = _modulation(cond, w_mod, b_mod).reshape(DEPTH, nrow, 1, 3 * D_MODEL)

    kstart, ktype, table = _na_row_types(rows)
    na_bias = _na_bias(na_rpb, table)
    kstart = jnp.asarray(kstart, jnp.int32)
    ktype = jnp.asarray(ktype, jnp.int32)

    cna_k = cache_na_k.reshape(Bd, DEPTH, L, NA_HEADS * HEAD_DIM)
    cna_v = cache_na_v.reshape(Bd, DEPTH, L, NA_HEADS * HEAD_DIM)
    cwg_k = cache_wg_k.reshape(Bd, DEPTH, L, WG_KV_HEADS * HEAD_DIM)
    cwg_v = cache_wg_v.reshape(Bd, DEPTH, L, WG_KV_HEADS * HEAD_DIM)
    ckr4 = _mla_rope_slot(cache_mla_krope)
    st_bd = _state_to_blockdiag(state_hgrn.reshape(Bd * DEPTH, 2, HG_HEADS, HG_DK, HG_DK)).reshape(
        Bd, DEPTH, 2, 256, 256)
    st_zero = jnp.zeros((1, 2, 256, 256), F32)

    hp, hs = x_prompt, x_sample
    states = []
    tok_caches = None
    for layer in range(DEPTH):
        lidx = jnp.full((1,), layer, jnp.int32)
        final = layer == DEPTH - 1

        flat = (1, Bp * Tp)
        res = _in_proj(lidx, hp.reshape(flat + (D_MODEL,)), mod4, 0, False, norm_g3, w_proj, g_kv3, w_kvb, rm_mla,
                       c4, s4, False, caches=tok_caches, cache_shape=(Bp, DEPTH, Tp))
        na, hg, mla, wg, kc_, vc_ = (a.reshape(Bp, Tp, a.shape[-1]) for a in res[:6])
        tok_caches = res[6:]
        y_hg, st_out = _hgrn(lidx, hg, hg_lb_logits, hg_g3, st_zero, False)
        y_na, y_wg, y_mla = _attn_ctx(lidx, na, wg_sink, wg, mla, g_q3, w_qb, c4, s4, kc_, vc_)
        ys = tuple(y.reshape(flat + (BRANCH_W,)) for y in (y_na, y_hg, y_mla, y_wg))
        hp = _post(lidx, hp.reshape(flat + (D_MODEL,)), mod4, 0, False, norm_g3, ys, w_merge, wb, wo, final_g2,
                   final).reshape(Bp, Tp, D_MODEL)
        states.append(st_out)

        na, hg, mla, wg, kl, vl = _in_proj(lidx, hs, mod4, 1, True, norm_g3, w_proj, g_kv3, w_kvb, rm_mla, c4, s4,
                                           True)
        y_hg, _ = _hgrn(lidx, hg, hg_lb_logits, hg_g3, st_bd[:, layer], True)
        kx, vx = _mla_kv(lidx, cache_mla_ckv, 0, ckr4, 0, ckr4, 0, g_kv3, w_kvb, c4, s4, False, False, True)
        y_na, y_wg, y_mla = _attn_lat(lidx, kstart, ktype, na, cna_k, cna_v, na_bias, wg_sink, wg, cwg_k, cwg_v,
                                      cq, sq, ck, sk, rm_wg, mla, g_q3, w_qb, c4, s4, kl, vl, kx, vx)
        hs = _post(lidx, hs, mod4, 1, True, norm_g3, (y_na, y_hg, y_mla, y_wg), w_merge, wb, wo, final_g2, final)

    new_state = _blockdiag_to_state(jnp.stack(states, axis=1).reshape(Bp * DEPTH, 2, 256, 256)).reshape(
        Bp, DEPTH, 2, HG_HEADS, HG_DK, HG_DK)
    na_k, na_v, ckv_c, kr_c, wg_k, wg_v = tok_caches
    return (hp, hs,
            na_k.reshape(Bp, DEPTH, Tp, NA_HEADS, HEAD_DIM), na_v.reshape(Bp, DEPTH, Tp, NA_HEADS, HEAD_DIM),
            new_state, ckv_c, kr_c,
            wg_k.reshape(Bp, DEPTH, Tp, WG_KV_HEADS, HEAD_DIM), wg_v.reshape(Bp, DEPTH, Tp, WG_KV_HEADS, HEAD_DIM))
```

```python
import functools

import jax
import jax.numpy as jnp
import numpy as np
from jax import lax
from jax.experimental import pallas as pl
from jax.experimental.pallas import tpu as pltpu

F32 = jnp.float32
BF16 = jnp.bfloat16

D_MODEL = 1024
DEPTH = 4
GRID_W = 64
HEAD_DIM = 64
BRANCH_W = 256
N_BRANCH = 4
NA_HEADS = 4
NA_WIN_H = 8
NA_WIN_W = 16
HG_HEADS = 4
HG_DK = 64
HG_CHUNK = 64
HG_SUB = 32
HG_PAIR = 4
HG_FAST_SPAN = 100.0
LOG2E = 1.4426950408889634
F_FLOOR = 1e-30
MLA_HEADS = 4
MLA_Q_RANK = 256
MLA_KV_RANK = 128
MLA_NOPE = 64
MLA_ROPE = 32
WG_HEADS = 4
WG_KV_HEADS = 2
WG_WINDOW = 128
ROPE_BASE = 10000.0
EPS = 1e-6
NEG_INF = -1e30

_IN_LAYOUT = (
    ('na_q', 256), ('na_k', 256), ('na_v', 256), ('na_z', 256),
    ('hg_q', 256), ('hg_ff', 256), ('hg_fb', 256), ('hg_i', 256), ('hg_og', 256), ('hg_z', 256),
    ('mla_qa', 256), ('mla_kva', 160), ('mla_z', 256),
    ('wg_q', 256), ('wg_k', 128), ('wg_v', 128), ('wg_z', 256),
    ('merge', 4096),
)
_IN_OFF = {}
_o = 0
for _n, _w in _IN_LAYOUT:
    _IN_OFF[_n] = (_o, _o + _w)
    _o += _w

NA_W = 1024
HG_W = 1536
MLA_W = 768
WG_W = 768
PROJ_W = NA_W + HG_W + MLA_W + WG_W

TOK_TILE = 256
PROJ_TILE = 512
NA_QROWS = 4
NA_KROWS = 12
WG_TILE = 256
WG_KWIN = WG_TILE + 2 * WG_WINDOW
VMEM_LIMIT = 56 * 1024 * 1024


def _cparams(sem):
    return pltpu.CompilerParams(dimension_semantics=sem, vmem_limit_bytes=VMEM_LIMIT)


def _dot(a, b):
    return jnp.dot(a, b, preferred_element_type=F32)


def _dot_nt(a, b):
    return lax.dot_general(a, b, (((1,), (1,)), ((), ())), preferred_element_type=F32)


def _dot_tn(a, b):
    return lax.dot_general(a, b, (((0,), (0,)), ((), ())), preferred_element_type=F32)


def _sigmoid(x):
    return 1.0 / (1.0 + jnp.exp(-x))


def _silu(x):
    return x * _sigmoid(x)


def _lane_mask(width, lo, hi, dtype):
    lane = lax.broadcasted_iota(jnp.int32, (1, width), 1)
    return ((lane >= lo) & (lane < hi)).astype(dtype)


def _seg_matrix(width, seg):
    r = lax.broadcasted_iota(jnp.int32, (width, width), 0) // seg
    c = lax.broadcasted_iota(jnp.int32, (width, width), 1) // seg
    return r == c


def _seg_mean(x2, seg_bf16, seg):
    hi = x2.astype(BF16)
    lo = (x2 - hi.astype(F32)).astype(BF16)
    return (_dot(hi, seg_bf16) + _dot(lo, seg_bf16)) * (1.0 / seg)


def _mod_kernel(c_ref, w_ref, b_ref, o_ref):
    s = _silu(c_ref[...])
    o_ref[0] = _dot(s.astype(BF16), w_ref[0].astype(BF16)) + b_ref[0]


def _modulation(cond, w_mod, b_mod):
    nrow = cond.shape[0]
    tn = 1024
    return pl.pallas_call(
        _mod_kernel,
        grid=(DEPTH, 3 * D_MODEL // tn),
        in_specs=[
            pl.BlockSpec((nrow, D_MODEL), lambda l, j: (0, 0)),
            pl.BlockSpec((1, D_MODEL, tn), lambda l, j: (l, 0, j)),
            pl.BlockSpec((1, 1, tn), lambda l, j: (l, 0, j)),
        ],
        out_specs=pl.BlockSpec((1, nrow, tn), lambda l, j: (l, 0, j)),
        out_shape=jax.ShapeDtypeStruct((DEPTH, nrow, 3 * D_MODEL), F32),
        compiler_params=_cparams(("arbitrary", "arbitrary")),
        name="modulation",
    )(cond, w_mod, b_mod.reshape(DEPTH, 1, 3 * D_MODEL))


def _mod_norm(x, mod_ref, g_ref):
    ms = jnp.mean(x * x, axis=-1, keepdims=True)
    y = x * lax.rsqrt(ms + EPS) * g_ref[0]
    shift = mod_ref[0, 0, :, 0:D_MODEL]
    scale = mod_ref[0, 0, :, D_MODEL:2 * D_MODEL]
    gate = mod_ref[0, 0, :, 2 * D_MODEL:3 * D_MODEL]
    return y * (1.0 + scale) + shift, gate


_CACHE_WIDTHS = (NA_HEADS * HEAD_DIM, NA_HEADS * HEAD_DIM, MLA_KV_RANK, MLA_ROPE,
                 WG_KV_HEADS * HEAD_DIM, WG_KV_HEADS * HEAD_DIM)


def _in_kernel(l_ref, x_ref, mod_ref, g_ref, w_ref, gkv_ref, wkvb_ref, rm_ref, c_ref, s_ref, *rest,
               rope, n_alias, seq):
    rest = rest[n_alias:]
    na_ref, hg_ref, mla_ref, wg_ref, kall_ref, vall_ref = rest[:6]
    h, _ = _mod_norm(x_ref[0], mod_ref, g_ref)
    hb = h.astype(BF16)
    na = _dot(hb, w_ref[0, :, 0:NA_W])
    na_ref[0] = na.astype(na_ref.dtype)
    hg_ref[0] = _dot(hb, w_ref[0, :, NA_W:NA_W + HG_W])
    wg = _dot(hb, w_ref[0, :, NA_W + HG_W + MLA_W:PROJ_W])
    wg_ref[0] = wg.astype(wg_ref.dtype)
    mla = _dot(hb, w_ref[0, :, NA_W + HG_W:NA_W + HG_W + MLA_W])
    mla_ref[0] = mla[:, 0:512].astype(mla_ref.dtype)
    ckv = mla[:, 512:640]
    ms = jnp.mean(ckv * ckv, axis=-1, keepdims=True)
    ckv = ckv * lax.rsqrt(ms + EPS) * gkv_ref[0]
    kr = mla[:, 640:768]
    if not rope:
        pieces = (na[:, 256:512], na[:, 512:768], ckv, kr[:, MLA_NOPE:MLA_NOPE + MLA_ROPE],
                  wg[:, 512:640], wg[:, 640:768])
        for ref, val in zip(rest[6:], pieces):
            for i in range(val.shape[0] // seq):
                ref[i, 0] = val[i * seq:(i + 1) * seq, :]
    if rope:
        kr = kr * c_ref[...] + _dot(kr.astype(BF16), rm_ref[...]) * s_ref[...]
    kv = _dot(ckv.astype(BF16), wkvb_ref[0])
    for hd in range(MLA_HEADS):
        kall_ref[0, :, 128 * hd:128 * hd + 128] = (kv[:, 128 * hd:128 * hd + 128] + kr).astype(BF16)
    vall_ref[0] = kv[:, 512:768].astype(BF16)


def _in_proj(lidx, x, mod4, mod_row0, per_batch, norm_g3, w_proj, g_kv3, w_kvb, rm_mla, c4, s4, rope,
             caches=None, cache_shape=None):
    B, T, _ = x.shape
    tile = min(PROJ_TILE, T)
    nt = T // tile
    mrow = (lambda b: mod_row0 + b) if per_batch else (lambda b: mod_row0)
    col_dtype = BF16

    def tok(width):
        return pl.BlockSpec((1, tile, width), lambda b, t, l: (b, t, 0))

    out_specs = [tok(NA_W), tok(HG_W), tok(512), tok(WG_W), tok(512), tok(256)]
    out_shape = [
        jax.ShapeDtypeStruct((B, T, NA_W), col_dtype),
        jax.ShapeDtypeStruct((B, T, HG_W), F32),
        jax.ShapeDtypeStruct((B, T, 512), col_dtype),
        jax.ShapeDtypeStruct((B, T, WG_W), col_dtype),
        jax.ShapeDtypeStruct((B, T, 512), BF16),
        jax.ShapeDtypeStruct((B, T, 256), BF16),
    ]
    in_specs = [
        pl.BlockSpec((1, tile, D_MODEL), lambda b, t, l: (b, t, 0)),
        pl.BlockSpec((1, 1, 1, 3 * D_MODEL), lambda b, t, l: (l[0], mrow(b), 0, 0)),
        pl.BlockSpec((1, 1, D_MODEL), lambda b, t, l: (l[0], 0, 0)),
        pl.BlockSpec((1, D_MODEL, PROJ_W), lambda b, t, l: (l[0], 0, 0)),
        pl.BlockSpec((1, 1, 128), lambda b, t, l: (l[0], 0, 0)),
        pl.BlockSpec((1, 128, 768), lambda b, t, l: (l[0], 0, 0)),
        pl.BlockSpec((128, 128), lambda b, t, l: (0, 0)),
        pl.BlockSpec((tile, 128), lambda b, t, l: (t if rope else 0, 0)),
        pl.BlockSpec((tile, 128), lambda b, t, l: (t if rope else 0, 0)),
    ]
    args = [lidx, x, mod4, norm_g3, w_proj, g_kv3, w_kvb, rm_mla, c4, s4]
    aliases = {}
    seq = tile
    if not rope:
        nbatch, _, seq = cache_shape
        assert B == 1 and tile % seq == 0
        for width in _CACHE_WIDTHS:
            out_specs.append(pl.BlockSpec((tile // seq, 1, seq, width), lambda b, t, l: (t, l[0], 0, 0)))
            out_shape.append(jax.ShapeDtypeStruct((nbatch, DEPTH, seq, width), F32))
        if caches is not None:
            for i, arr in enumerate(caches):
                aliases[len(args)] = 6 + i
                in_specs.append(pl.BlockSpec(memory_space=pl.ANY))
                args.append(arr)
    grid_spec = pltpu.PrefetchScalarGridSpec(
        num_scalar_prefetch=1,
        grid=(B, nt),
        in_specs=in_specs,
        out_specs=out_specs,
    )
    return pl.pallas_call(
        functools.partial(_in_kernel, rope=rope, n_alias=len(aliases), seq=seq),
        grid_spec=grid_spec,
        out_shape=out_shape,
        input_output_aliases=aliases,
        compiler_params=_cparams(("arbitrary", "arbitrary")),
        name="in_proj_lat" if rope else ("in_proj_ctx" if caches is None else "in_proj_ctx_inplace"),
    )(*args)


def _post_kernel(l_ref, x_ref, mod_ref, g_ref, y0_ref, y1_ref, y2_ref, y3_ref, wm_ref, wb_ref, wo_ref,
                 fg_ref, o_ref, *, final):
    x = x_ref[0]
    h, gate = _mod_norm(x, mod_ref, g_ref)
    hb = h.astype(BF16)
    acc = jnp.zeros(x.shape, F32)
    for n, y_ref in enumerate((y0_ref, y1_ref, y2_ref, y3_ref)):
        mg = _sigmoid(_dot(hb, wm_ref[0, :, n * D_MODEL:(n + 1) * D_MODEL]))
        acc = acc + mg * _dot(y_ref[0], wb_ref[0, n])
    out = x + gate * _dot(acc.astype(BF16), wo_ref[0])
    if final:
        ms = jnp.mean(out * out, axis=-1, keepdims=True)
        out = out * lax.rsqrt(ms + EPS) * fg_ref[...]
    o_ref[0] = out


def _post(lidx, x, mod4, mod_row0, per_batch, norm_g3, ys, w_merge, w_branch, w_out, final_g2, final):
    B, T, _ = x.shape
    tile = min(PROJ_TILE, T)
    nt = T // tile
    mrow = (lambda b: mod_row0 + b) if per_batch else (lambda b: mod_row0)
    yspec = pl.BlockSpec((1, tile, BRANCH_W), lambda b, t, l: (b, t, 0))
    grid_spec = pltpu.PrefetchScalarGridSpec(
        num_scalar_prefetch=1,
        grid=(B, nt),
        in_specs=[
            pl.BlockSpec((1, tile, D_MODEL), lambda b, t, l: (b, t, 0)),
            pl.BlockSpec((1, 1, 1, 3 * D_MODEL), lambda b, t, l: (l[0], mrow(b), 0, 0)),
            pl.BlockSpec((1, 1, D_MODEL), lambda b, t, l: (l[0], 0, 0)),
            yspec, yspec, yspec, yspec,
            pl.BlockSpec((1, D_MODEL, N_BRANCH * D_MODEL), lambda b, t, l: (l[0], 0, 0)),
            pl.BlockSpec((1, N_BRANCH, BRANCH_W, D_MODEL), lambda b, t, l: (l[0], 0, 0, 0)),
            pl.BlockSpec((1, D_MODEL, D_MODEL), lambda b, t, l: (l[0], 0, 0)),
            pl.BlockSpec((1, D_MODEL), lambda b, t, l: (0, 0)),
        ],
        out_specs=pl.BlockSpec((1, tile, D_MODEL), lambda b, t, l: (b, t, 0)),
    )
    return pl.pallas_call(
        functools.partial(_post_kernel, final=final),
        grid_spec=grid_spec,
        out_shape=jax.ShapeDtypeStruct((B, T, D_MODEL), F32),
        compiler_params=_cparams(("arbitrary", "arbitrary")),
        name="post_final" if final else "post",
    )(lidx, x, mod4, norm_g3, *ys, w_merge, w_branch, w_out, final_g2)


def _attend(q, kparts, vparts, biases, qmasks, omasks, sinks=None):
    out = None
    for h in range(len(omasks)):
        qh = q[h] if qmasks is None else q * qmasks[h]
        ss = []
        for i, kp in enumerate(kparts):
            s = _dot_nt(qh, kp[h] if isinstance(kp, (list, tuple)) else kp)
            if biases is not None and biases[h][i] is not None:
                s = s + biases[h][i]
            ss.append(s)
        m = ss[0].max(axis=-1, keepdims=True)
        for s in ss[1:]:
            m = jnp.maximum(m, s.max(axis=-1, keepdims=True))
        if sinks is not None:
            m = jnp.maximum(m, sinks[h])
        den = None
        oh = None
        for s, vp in zip(ss, vparts):
            p = jnp.exp2(s - m)
            ps = p.sum(axis=-1, keepdims=True)
            den = ps if den is None else den + ps
            pv = _dot(p.astype(BF16), vp)
            oh = pv if oh is None else oh + pv
        if sinks is not None:
            den = den + jnp.exp2(sinks[h] - m)
        oh = (oh / den) * omasks[h]
        out = oh if out is None else out + oh
    return out


def _head_masks(width, seg, nheads, dtype):
    return [_lane_mask(width, h * seg, (h + 1) * seg, dtype) for h in range(nheads)]


def _na_ctx_kernel(q_ref, k_ref, v_ref, z_ref, o_ref):
    q = q_ref[0].astype(BF16)
    k = k_ref[0].astype(BF16)
    v = v_ref[0].astype(BF16)
    o = _attend(q, [k], [v], None, _head_masks(256, 64, NA_HEADS, BF16), _head_masks(256, 64, NA_HEADS, F32))
    o_ref[0] = (o * _silu(z_ref[0].astype(F32))).astype(BF16)


def _na_row_types(rows):
    wh = min(NA_WIN_H, rows)
    nblk = rows // NA_QROWS
    starts, types, table = [], [], []
    for blk in range(nblk):
        r0 = blk * NA_QROWS
        k0 = int(np.clip(r0 - wh // 2, 0, rows - NA_KROWS))
        desc = []
        for i in range(NA_QROWS):
            r = r0 + i
            ks = int(np.clip(r - wh // 2, 0, rows - wh))
            for j in range(NA_KROWS):
                kr = k0 + j
                desc.append((kr - r + NA_WIN_H - 1) if ks <= kr < ks + wh else None)
        desc = tuple(desc)
        if desc not in table:
            table.append(desc)
        starts.append(k0)
        types.append(table.index(desc))
    return starts, types, table


def _na_bias_kernel(rpb_ref, o_ref, t2_ref, *, table):
    l = pl.program_id(0)
    h = pl.program_id(1)
    nrow = 2 * NA_WIN_H - 1
    ncol = 2 * NA_WIN_W - 1
    c = lax.broadcasted_iota(jnp.int32, (GRID_W, 128), 0)
    lane = lax.broadcasted_iota(jnp.int32, (GRID_W, 128), 1)
    first = lane < GRID_W
    kc = jnp.where(first, lane, lane - GRID_W)
    qstart = jnp.clip(c - NA_WIN_W // 2, 0, GRID_W - NA_WIN_W)
    ok = (kc >= qstart) & (kc < qstart + NA_WIN_W)
    dcol = kc - c + NA_WIN_W - 1
    for a in range(-1, nrow):
        acc = jnp.full((GRID_W, 128), NEG_INF, F32)
        for bi in range(2 * NA_WIN_W - 1):
            base = (l * NA_HEADS + h) * nrow
            lo = rpb_ref[(base + a) * ncol + bi] if a >= 0 else jnp.float32(NEG_INF)
            hi = rpb_ref[(base + a + 1) * ncol + bi] if a + 1 < nrow else jnp.float32(NEG_INF)
            acc = jnp.where(dcol == bi, jnp.where(first, lo, hi), acc)
        t2_ref[a + 1] = jnp.where(ok, acc * LOG2E, NEG_INF)
    for ti, desc in enumerate(table):
        for i in range(NA_QROWS):
            for p in range(NA_KROWS // 2):
                a0 = desc[i * NA_KROWS + 2 * p]
                a1 = desc[i * NA_KROWS + 2 * p + 1]
                if a0 is None and a1 is None:
                    tile = jnp.full((GRID_W, 128), NEG_INF, F32)
                elif a1 is None:
                    tile = jnp.where(first, t2_ref[a0 + 1], NEG_INF)
                elif a0 is None:
                    tile = jnp.where(first, NEG_INF, t2_ref[a1])
                else:
                    tile = t2_ref[a0 + 1]
                o_ref[0, ti, 0, i * GRID_W:(i + 1) * GRID_W, p * 128:(p + 1) * 128] = tile


def _na_bias(na_rpb, table):
    nt = len(table)
    nq = NA_QROWS * GRID_W
    nk = NA_KROWS * GRID_W
    return pl.pallas_call(
        functools.partial(_na_bias_kernel, table=table),
        grid=(DEPTH, NA_HEADS),
        in_specs=[pl.BlockSpec(memory_space=pltpu.SMEM)],
        out_specs=pl.BlockSpec((1, nt, 1, nq, nk), lambda l, h: (l, 0, h, 0, 0)),
        out_shape=jax.ShapeDtypeStruct((DEPTH, nt, NA_HEADS, nq, nk), F32),
        scratch_shapes=[pltpu.VMEM((2 * NA_WIN_H, GRID_W, 128), F32)],
        compiler_params=_cparams(("arbitrary", "arbitrary")),
        name="na_bias",
    )(na_rpb.reshape(-1))


def _na_lat_kernel(l_ref, ks_ref, ty_ref, q_ref, k_ref, v_ref, z_ref, kc_ref, vc_ref, b_ref, o_ref):
    rb = pl.program_id(1)
    nk = NA_KROWS * GRID_W
    ks = pl.multiple_of(ks_ref[rb] * GRID_W, GRID_W)
    q = q_ref[0].astype(BF16)
    kw = k_ref[0, pl.ds(ks, nk), :].astype(BF16)
    vw = v_ref[0, pl.ds(ks, nk), :].astype(BF16)
    kc = kc_ref[0, 0].astype(BF16)
    vc = vc_ref[0, 0].astype(BF16)
    biases = [[b_ref[0, 0, h], None] for h in range(NA_HEADS)]
    o = _attend(q, [kw, kc], [vw, vc], biases,
                _head_masks(256, 64, NA_HEADS, BF16), _head_masks(256, 64, NA_HEADS, F32))
    o_ref[0] = (o * _silu(z_ref[0].astype(F32))).astype(BF16)


def _wg_heads(q, kparts, vparts, bias, sink_ref, l):
    qm = _head_masks(128, 64, 2, BF16)
    om = _head_masks(128, 64, 2, F32)
    outs = []
    for grp, heads in enumerate(((0, 2), (1, 3))):
        qg = q[:, grp * 128:(grp + 1) * 128]
        sinks = [sink_ref[l, hh] * LOG2E for hh in heads]
        biases = None if bias is None else [[bias] + [None] * (len(kparts) - 1)] * 2
        outs.append(_attend(qg, kparts, vparts, biases, qm, om, sinks))
    return jnp.concatenate(outs, axis=-1)


def _wg_ctx_kernel(l_ref, sink_ref, q_ref, z_ref, k_ref, v_ref, o_ref):
    q = q_ref[0].astype(BF16)
    o = _wg_heads(q, [k_ref[0].astype(BF16)], [v_ref[0].astype(BF16)], None, sink_ref, l_ref[0])
    o_ref[0] = (o * _silu(z_ref[0].astype(F32))).astype(BF16)


def _wg_lat_kernel(l_ref, sink_ref, q_ref, z_ref, k_ref, v_ref, kc_ref, vc_ref,
                   cq_ref, sq_ref, ck_ref, sk_ref, rm_ref, o_ref, *, seq):
    tb = pl.program_id(1)
    t0 = tb * WG_TILE
    k0 = pl.multiple_of(jnp.clip(t0 - WG_WINDOW, 0, seq - WG_KWIN), 128)
    q = q_ref[0]
    q = q.astype(F32) * cq_ref[...] + _dot(q.astype(BF16), rm_ref[...]) * sq_ref[...]
    kwin = k_ref[0, pl.ds(k0, WG_KWIN), :]
    kwin = kwin.astype(F32) * ck_ref[pl.ds(k0, WG_KWIN), :] \
        + _dot(kwin.astype(BF16), rm_ref[0:128, 0:128]) * sk_ref[pl.ds(k0, WG_KWIN), :]
    vwin = v_ref[0, pl.ds(k0, WG_KWIN), :]
    qi = t0 + lax.broadcasted_iota(jnp.int32, (WG_TILE, WG_KWIN), 0)
    kj = k0 + lax.broadcasted_iota(jnp.int32, (WG_TILE, WG_KWIN), 1)
    bias = jnp.where(jnp.abs(qi - kj) <= WG_WINDOW, 0.0, NEG_INF).astype(F32)
    o = _wg_heads(q.astype(BF16),
                  [kwin.astype(BF16), kc_ref[0, 0].astype(BF16)],
                  [vwin.astype(BF16), vc_ref[0, 0].astype(BF16)],
                  bias, sink_ref, l_ref[0])
    o_ref[0] = (o * _silu(z_ref[0].astype(F32))).astype(BF16)


def _mla_kv_kernel(l_ref, ckv_ref, kr_ref, krr_ref, g_ref, w_ref, c_ref, s_ref, k_out, v_out, *rest,
                   norm, rope):
    ckv = ckv_ref[0]
    if norm:
        ms = jnp.mean(ckv * ckv, axis=-1, keepdims=True)
        ckv = ckv * lax.rsqrt(ms + EPS) * g_ref[0]
        rest[0][0] = ckv
    kv = _dot(ckv.astype(BF16), w_ref[0])
    kr = kr_ref[0]
    if rope:
        kr = kr * c_ref[...] + krr_ref[0] * s_ref[...]
    for h in range(MLA_HEADS):
        k_out[0, :, 128 * h:128 * h + 128] = (kv[:, 128 * h:128 * h + 128] + kr).astype(BF16)
    v_out[0] = kv[:, 512:768].astype(BF16)


def _mla_kv(lidx, ckv_arr, ckv_blk, kr_arr, kr_blk, krr_arr, krr_blk, g_kv3, w_kvb, c4, s4, norm, rope,
            layered):
    B = ckv_arr.shape[0]
    T = ckv_arr.shape[-2]
    nt = T // TOK_TILE
    if layered:
        def src(blk):
            return pl.BlockSpec((None, 1, TOK_TILE, 128), lambda b, t, l: (b, l[0], t, 0))
    else:
        def src(blk):
            return pl.BlockSpec((1, TOK_TILE, 128), lambda b, t, l, blk=blk: (b, t, blk))
    out_specs = [
        pl.BlockSpec((1, TOK_TILE, 512), lambda b, t, l: (b, t, 0)),
        pl.BlockSpec((1, TOK_TILE, 256), lambda b, t, l: (b, t, 0)),
    ]
    out_shape = [
        jax.ShapeDtypeStruct((B, T, 512), BF16),
        jax.ShapeDtypeStruct((B, T, 256), BF16),
    ]
    if norm:
        out_specs.append(pl.BlockSpec((1, TOK_TILE, 128), lambda b, t, l: (b, t, 0)))
        out_shape.append(jax.ShapeDtypeStruct((B, T, 128), F32))
    grid_spec = pltpu.PrefetchScalarGridSpec(
        num_scalar_prefetch=1,
        grid=(B, nt),
        in_specs=[
            src(ckv_blk), src(kr_blk), src(krr_blk),
            pl.BlockSpec((1, 1, 128), lambda b, t, l: (l[0], 0, 0)),
            pl.BlockSpec((1, 128, 768), lambda b, t, l: (l[0], 0, 0)),
            pl.BlockSpec((TOK_TILE, 128), lambda b, t, l: (t if rope else 0, 0)),
            pl.BlockSpec((TOK_TILE, 128), lambda b, t, l: (t if rope else 0, 0)),
        ],
        out_specs=out_specs,
    )
    return pl.pallas_call(
        functools.partial(_mla_kv_kernel, norm=norm, rope=rope),
        grid_spec=grid_spec,
        out_shape=out_shape,
        compiler_params=_cparams(("arbitrary", "arbitrary")),
        name="mla_kv" + ("_norm" if norm else "") + ("_rope" if rope else ""),
    )(lidx, ckv_arr, kr_arr, krr_arr, g_kv3, w_kvb, c4, s4)


def _mla_attn_kernel(l_ref, qa_ref, z_ref, g_ref, w_ref, c_ref, s_ref, *refs, nparts, rope):
    kparts = [[refs[2 * i][0, :, 128 * h:128 * h + 128] for h in range(MLA_HEADS)] for i in range(nparts)]
    vparts = [refs[2 * i + 1][0] for i in range(nparts)]
    o_ref = refs[2 * nparts]
    qa = qa_ref[0].astype(F32)
    ms = jnp.mean(qa * qa, axis=-1, keepdims=True)
    qn = qa * lax.rsqrt(ms + EPS) * g_ref[0]
    qq = _dot(qn.astype(BF16), w_ref[0])
    q = []
    for h in range(MLA_HEADS):
        qh = qq[:, 128 * h:128 * h + 128]
        if rope:
            qh = qh * c_ref[...] + qq[:, 512 + 128 * h:512 + 128 * h + 128] * s_ref[...]
        q.append(qh.astype(BF16))
    om = _head_masks(256, 64, MLA_HEADS, F32)
    half = TOK_TILE // 2
    o = jnp.concatenate([_attend([qh[i * half:(i + 1) * half] for qh in q], kparts, vparts, None, None, om)
                         for i in range(2)], axis=0)
    o_ref[0] = (o * _silu(z_ref[0].astype(F32))).astype(BF16)


def _attn_ctx_kernel(l_ref, *refs):
    na_in, refs = refs[:4], refs[4:]
    wg_in, refs = refs[:5], refs[5:]
    mla_in, refs = refs[:8], refs[8:]
    o_na, o_wg, o_mla = refs
    _na_ctx_kernel(*na_in, o_na)
    _wg_ctx_kernel(l_ref, *wg_in, o_wg)
    _mla_attn_kernel(l_ref, *mla_in, o_mla, nparts=1, rope=False)


def _attn_ctx(lidx, cols_na, sink, cols_wg, cols_mla, g_q3, w_qb, c4, s4, kc, vc):
    B, T, _ = cols_na.shape

    def col(width, j):
        return pl.BlockSpec((1, T, width), lambda b, l, j=j: (b, 0, j))

    in_specs = [
        col(256, 0), col(256, 1), col(256, 2), col(256, 3),
        pl.BlockSpec(memory_space=pltpu.SMEM), col(256, 0), col(256, 1), col(128, 4), col(128, 5),
        col(256, 0), col(256, 1),
        pl.BlockSpec((1, 1, 256), lambda b, l: (l[0], 0, 0)),
        pl.BlockSpec((1, 256, 1024), lambda b, l: (l[0], 0, 0)),
        pl.BlockSpec((TOK_TILE, 128), lambda b, l: (0, 0)),
        pl.BlockSpec((TOK_TILE, 128), lambda b, l: (0, 0)),
        col(512, 0), col(256, 0),
    ]
    out = pl.BlockSpec((1, T, 256), lambda b, l: (b, 0, 0))
    grid_spec = pltpu.PrefetchScalarGridSpec(num_scalar_prefetch=1, grid=(B,), in_specs=in_specs,
                                             out_specs=[out, out, out])
    return pl.pallas_call(
        _attn_ctx_kernel,
        grid_spec=grid_spec,
        out_shape=[jax.ShapeDtypeStruct((B, T, 256), BF16)] * 3,
        compiler_params=_cparams(("arbitrary",)),
        name="attn_ctx",
    )(lidx, cols_na, cols_na, cols_na, cols_na, sink, cols_wg, cols_wg, cols_wg, cols_wg,
      cols_mla, cols_mla, g_q3, w_qb, c4, s4, kc, vc)


def _attn_lat_kernel(l_ref, ks_ref, ty_ref, *refs, seq):
    na_in, refs = refs[:7], refs[7:]
    wg_in, refs = refs[:12], refs[12:]
    mla_in, refs = refs[:10], refs[10:]
    o_na, o_wg, o_mla = refs
    _na_lat_kernel(l_ref, ks_ref, ty_ref, *na_in, o_na)
    _wg_lat_kernel(l_ref, *wg_in, o_wg, seq=seq)
    _mla_attn_kernel(l_ref, *mla_in, o_mla, nparts=2, rope=True)


def _attn_lat(lidx, kstart, ktype, cols_na, cna_k, cna_v, bias, sink, cols_wg, cwg_k, cwg_v, cq, sq, ck, sk, rm,
              cols_mla, g_q3, w_qb, c4, s4, kl, vl, kx, vx):
    B, T, _ = cols_na.shape
    L = cna_k.shape[2]
    nq = NA_QROWS * GRID_W
    nk = NA_KROWS * GRID_W
    assert nq == TOK_TILE == WG_TILE
    nt = T // TOK_TILE

    def tok(width, col):
        return pl.BlockSpec((1, TOK_TILE, width), lambda b, t, l, ks, ty, col=col: (b, t, col))

    def full(rows, width, col):
        return pl.BlockSpec((1, rows, width), lambda b, t, l, ks, ty, col=col: (b, 0, col))

    def layer(rows, width):
        return pl.BlockSpec((1, 1, rows, width), lambda b, t, l, ks, ty: (b, l[0], 0, 0))

    def table(width):
        return pl.BlockSpec((TOK_TILE, width), lambda b, t, l, ks, ty: (t, 0))

    in_specs = [
        tok(256, 0), full(T, 256, 1), full(T, 256, 2), tok(256, 3), layer(L, 256), layer(L, 256),
        pl.BlockSpec((1, 1, NA_HEADS, nq, nk), lambda b, t, l, ks, ty: (l[0], ty[t], 0, 0, 0)),
        pl.BlockSpec(memory_space=pltpu.SMEM),
        tok(256, 0), tok(256, 1), full(T, 128, 4), full(T, 128, 5), layer(L, 128), layer(L, 128),
        table(256), table(256),
        pl.BlockSpec((T, 128), lambda b, t, l, ks, ty: (0, 0)),
        pl.BlockSpec((T, 128), lambda b, t, l, ks, ty: (0, 0)),
        pl.BlockSpec((256, 256), lambda b, t, l, ks, ty: (0, 0)),
        tok(256, 0), tok(256, 1),
        pl.BlockSpec((1, 1, 256), lambda b, t, l, ks, ty: (l[0], 0, 0)),
        pl.BlockSpec((1, 256, 1024), lambda b, t, l, ks, ty: (l[0], 0, 0)),
        table(128), table(128),
        full(kl.shape[1], 512, 0), full(vl.shape[1], 256, 0), full(kx.shape[1], 512, 0), full(vx.shape[1], 256, 0),
    ]
    out = pl.BlockSpec((1, TOK_TILE, 256), lambda b, t, l, ks, ty: (b, t, 0))
    grid_spec = pltpu.PrefetchScalarGridSpec(
        num_scalar_prefetch=3,
        grid=(B, nt),
        in_specs=in_specs,
        out_specs=[out, out, out],
    )
    return pl.pallas_call(
        functools.partial(_attn_lat_kernel, seq=T),
        grid_spec=grid_spec,
        out_shape=[jax.ShapeDtypeStruct((B, T, 256), BF16)] * 3,
        compiler_params=_cparams(("arbitrary", "arbitrary")),
        name="attn_lat",
    )(lidx, kstart, ktype,
      cols_na, cols_na, cols_na, cols_na, cna_k, cna_v, bias,
      sink, cols_wg, cols_wg, cols_wg, cols_wg, cwg_k, cwg_v, cq, sq, ck, sk, rm,
      cols_mla, cols_mla, g_q3, w_qb, c4, s4, kl, vl, kx, vx)


def _hg_blocks(rev, include_diag):
    C, S = HG_CHUNK, HG_SUB
    out = []
    for g0 in range(0, C, S):
        if rev:
            lo, hi = (g0 if include_diag else g0 + S), C
        else:
            lo, hi = 0, (g0 + S if include_diag else g0)
        if hi > lo:
            out.append((g0, lo, hi))
    return out


def _hg_mask(rev, include_diag):
    blocks = _hg_blocks(rev, include_diag)
    S = HG_SUB
    nrow = len(blocks) * HG_HEADS * S
    ncol = 128 * -(-sum(hi - lo for _, lo, hi in blocks) // 128)
    r = lax.broadcasted_iota(jnp.int32, (nrow, ncol), 0)
    cidx = lax.broadcasted_iota(jnp.int32, (nrow, ncol), 1)
    rblk = r // (HG_HEADS * S)
    cblk = jnp.full_like(cidx, len(blocks))
    s = jnp.zeros_like(cidx)
    t = r % S
    start = 0
    for k_, (g0, lo, hi) in enumerate(blocks):
        inb = (cidx >= start) & (cidx < start + hi - lo)
        cblk = jnp.where(inb, k_, cblk)
        s = jnp.where(inb, cidx - start + lo, s)
        t = jnp.where(rblk == k_, t + g0, t)
        start += hi - lo
    ok = rblk == cblk
    if include_diag:
        ok = ok & ((s >= t) if rev else (s <= t))
    return ok.astype(F32)


def _hg_stacked(qs, kk, v, b2, rev, include_diag, mask, hm):
    C, S = HG_CHUNK, HG_SUB
    blocks = _hg_blocks(rev, include_diag)
    qx, kd_rows, v_rows = [], [], []
    for g0, lo, hi in blocks:
        has_ref = (g0 + S < C) if rev else (g0 > 0)
        eq = b2[g0:g0 + S, :]
        es = -b2[lo:hi, :]
        if has_ref:
            ct = b2[g0 + S:g0 + S + 1, :] if rev else b2[g0 - 1:g0, :]
            eq = eq - ct
            es = es + ct
        qd = qs[g0:g0 + S, :] * jnp.exp2(eq)
        qx += [qd * m for m in hm]
        kd_rows.append(kk[lo:hi, :] * jnp.exp2(es))
        v_rows.append(v[lo:hi, :])
    pad = mask.shape[1] - sum(hi - lo for _, lo, hi in blocks)
    if pad:
        kd_rows.append(jnp.zeros((pad, 256), F32))
        v_rows.append(jnp.zeros((pad, 256), F32))
    kd_all = jnp.concatenate(kd_rows, axis=0).astype(BF16)
    v_all = jnp.concatenate(v_rows, axis=0).astype(BF16)
    att = _dot_nt(jnp.concatenate(qx, axis=0).astype(BF16), kd_all)
    att = jnp.where(mask > 0.5, att, 0.0)
    return _dot(att.astype(BF16), v_all), blocks


def _hg_gates(lb, rev, q_ref, f_ref, qs_sc, lf_sc, kk_sc):
    qs_sc[...] = _silu(q_ref[0])
    fz = f_ref[0]
    e = jnp.exp(-jnp.abs(fz))
    big = 1.0 / (1.0 + e)
    small_ = e * big
    pos = fz >= 0.0
    f = lb + (1.0 - lb) * jnp.where(pos, big, small_)
    lf = jnp.log(jnp.maximum(f, F_FLOOR))
    kk_sc[...] = (1.0 - lb) * jnp.where(pos, small_, big)
    rr = lax.broadcasted_iota(jnp.int32, (TOK_TILE, TOK_TILE), 0)
    cc = lax.broadcasted_iota(jnp.int32, (TOK_TILE, TOK_TILE), 1)
    tri = ((rr // HG_CHUNK == cc // HG_CHUNK) & ((cc >= rr) if rev else (cc <= rr))).astype(BF16)
    hi = lf.astype(BF16)
    r1 = lf - hi.astype(F32)
    mid = r1.astype(BF16)
    lo = (r1 - mid.astype(F32)).astype(BF16)
    lf_sc[...] = _dot(tri, hi) + _dot(tri, mid) + _dot(tri, lo)
    nblk = TOK_TILE // HG_SUB
    r = lax.broadcasted_iota(jnp.int32, (nblk, TOK_TILE), 0)
    cidx = lax.broadcasted_iota(jnp.int32, (nblk, TOK_TILE), 1)
    span = _dot((cidx // HG_SUB == r).astype(BF16), (lf * (-LOG2E)).astype(BF16))
    return jnp.max(span) <= HG_FAST_SPAN


def _hg_chunk(r0, rev, fast, st, qs_sc, lf_sc, kk_sc, i_ref, a_sc, v_sc, o_sc, seg, bdmask, mask):
    C = HG_CHUNK
    S = HG_SUB
    qs = qs_sc[pl.ds(r0, C), :]
    b = lf_sc[pl.ds(r0, C), :]
    kk = kk_sc[pl.ds(r0, C), :]
    v = i_ref[0, pl.ds(r0, C), :]
    b2 = b * LOG2E
    total = b[0:1, :] if rev else b[C - 1:C, :]
    inter = _dot_nt((qs * jnp.exp2(b2)).astype(BF16), st.astype(BF16))
    hm = _head_masks(256, HG_DK, HG_HEADS, F32)
    off, blocks = _hg_stacked(qs, kk, v, b2, rev, fast, mask, hm)

    def head_sum(k_, i):
        acc = None
        for h in range(HG_HEADS):
            r_ = k_ * HG_HEADS * S + h * S + 8 * i
            t_ = off[r_:r_ + 8, :] * hm[h]
            acc = t_ if acc is None else acc + t_
        return acc

    if fast:
        tiles = []
        for k_, (g0, _, _) in enumerate(blocks):
            for i in range(S // 8):
                tiles.append(inter[g0 + 8 * i:g0 + 8 * i + 8, :] + head_sum(k_, i))
        o = jnp.concatenate(tiles, axis=0)
    else:
        a_sc[...] = jnp.log2(kk) - b2
        v_sc[...] = v
        o_sc[...] = inter
        for k_, (g0, _, _) in enumerate(blocks):
            for i in range(S // 8):
                o_sc[g0 + 8 * i:g0 + 8 * i + 8, :] = o_sc[g0 + 8 * i:g0 + 8 * i + 8, :] + head_sum(k_, i)
        sub = lax.broadcasted_iota(jnp.int32, (8, 256), 0)
        for g0 in range(0, C, S):
            parts, spans = [], []
            for s in range(g0, g0 + S):
                t0 = (s // 8) * 8
                arow = a_sc[s:s + 1, :]
                keep = (sub <= s - t0) if rev else (sub >= s - t0)
                rdiag = qs[t0:t0 + 8, :] * jnp.exp2(jnp.where(keep, b2[t0:t0 + 8, :] + arow, NEG_INF))
                rest_lo, rest_hi = (g0, t0) if rev else (t0 + 8, g0 + S)
                if rest_hi > rest_lo:
                    rrest = qs[rest_lo:rest_hi, :] * jnp.exp2(b2[rest_lo:rest_hi, :] + arow)
                    parts += [rrest, rdiag] if rev else [rdiag, rrest]
                else:
                    parts.append(rdiag)
                spans.append((min(rest_lo, t0), max(rest_hi, t0 + 8)) if rest_hi > rest_lo else (t0, t0 + 8))
            res = _dot(jnp.concatenate(parts, axis=0).astype(BF16), seg)
            acc = [o_sc[g0 + 8 * i:g0 + 8 * i + 8, :] for i in range(S // 8)]
            pos = 0
            for s, (lo_r, hi_r) in zip(range(g0, g0 + S), spans):
                vrow = v_sc[s:s + 1, :]
                for i in range((hi_r - lo_r) // 8):
                    ti = (lo_r - g0) // 8 + i
                    acc[ti] = acc[ti] + res[pos + 8 * i:pos + 8 * i + 8, :] * vrow
                pos += hi_r - lo_r
            for i in range(S // 8):
                o_sc[g0 + 8 * i:g0 + 8 * i + 8, :] = acc[i]
        o = o_sc[...]
    kd = (kk * jnp.exp(total - b)).astype(BF16)
    return o, jnp.exp(total) * st + bdmask * _dot_tn(v.astype(BF16), kd)


def _hg_kernel(l_ref, q_ref, f_ref, i_ref, og_ref, z_ref, lbl_ref, g_ref, s0_ref, y_ref, sto_ref,
               st_ref, oacc_ref, qs_sc, lf_sc, kk_sc, a_sc, v_sc, o_sc, *, nb, s0_per_batch):
    j = pl.program_id(1)
    l = l_ref[0]
    C = HG_CHUNK
    ncb = TOK_TILE // C
    P = q_ref.shape[0]

    @pl.when((j == 0) | (j == nb))
    def _():
        for e in range(P):
            st_ref[e] = s0_ref[e if s0_per_batch else 0, 0]

    lg = lbl_ref[...]
    mx = lg[0]
    for m in range(1, DEPTH):
        mx = jnp.maximum(mx, lg[m])
    ex = [jnp.exp(lg[m] - mx) for m in range(DEPTH)]
    tot = ex[0]
    for m in range(1, DEPTH):
        tot = tot + ex[m]
    lbs = jnp.zeros_like(tot)
    for m in range(1, DEPTH):
        lbs = lbs + jnp.where(m <= l, ex[m] / tot, 0.0)

    seg = _seg_matrix(256, HG_DK)
    seg_b = seg.astype(BF16)
    bdmask = seg.astype(F32)

    def run(rev, emit):
        lb = lbs[1:2, :] if rev else lbs[0:1, :]
        small = None
        for e in range(P):
            s_e = _hg_gates(lb, rev, q_ref.at[pl.ds(e, 1)], f_ref.at[pl.ds(e, 1)], qs_sc.at[e], lf_sc.at[e],
                            kk_sc.at[e])
            small = s_e if small is None else small & s_e

        def common(e):
            return (qs_sc.at[e], lf_sc.at[e], kk_sc.at[e], i_ref.at[pl.ds(e, 1)], a_sc, v_sc, o_sc, seg_b, bdmask)

        @pl.when(small)
        def _():
            mask = _hg_mask(rev, True)
            sts = [st_ref[e] for e in range(P)]
            for ci in range(ncb):
                r0 = (ncb - 1 - ci if rev else ci) * C
                for e in range(P):
                    o, sts[e] = _hg_chunk(r0, rev, True, sts[e], *common(e), mask)
                    emit(e, r0, o)
            for e in range(P):
                st_ref[e] = sts[e]

        @pl.when(jnp.logical_not(small))
        def _():
            mask = _hg_mask(rev, False)
            for e in range(P):
                def body(ci, carry, e=e):
                    r0 = pl.multiple_of((ncb - 1 - ci if rev else ci) * C, C)
                    o, st = _hg_chunk(r0, rev, False, st_ref[e], *common(e), mask)
                    st_ref[e] = st
                    emit(e, r0, o)
                    return carry

                lax.fori_loop(0, ncb, body, 0)

    @pl.when(j < nb)
    def _():
        def emit(e, r0, o):
            oacc_ref[e, pl.ds(pl.multiple_of(j * TOK_TILE + r0, C), C), :] = o

        run(False, emit)

    @pl.when(j >= nb)
    def _():
        blk = 2 * nb - 1 - j

        def emit(e, r0, o):
            o = oacc_ref[e, pl.ds(pl.multiple_of(blk * TOK_TILE + r0, C), C), :] + o
            ms = _seg_mean(o * o, seg_b, HG_DK)
            on = o * lax.rsqrt(ms + EPS) * g_ref[0]
            on = on * _sigmoid(og_ref[e, pl.ds(r0, C), :])
            y_ref[e, pl.ds(r0, C), :] = (on * _silu(z_ref[e, pl.ds(r0, C), :])).astype(BF16)

        run(True, emit)

    for e in range(P):
        sto_ref[e, 0] = st_ref[e]


def _hgrn(lidx, cols_hg, lb_logits, g3, s0, s0_per_batch):
    B, T, _ = cols_hg.shape
    nb = T // TOK_TILE
    P = HG_PAIR
    assert B % P == 0

    def blk(j):
        return jnp.where(j < nb, j, 2 * nb - 1 - j)

    def dirn(j):
        return jnp.where(j < nb, 0, 1)

    def col(jcol):
        return pl.BlockSpec((P, TOK_TILE, 256), lambda b, j, l, jcol=jcol: (b, blk(j), jcol))

    grid_spec = pltpu.PrefetchScalarGridSpec(
        num_scalar_prefetch=1,
        grid=(B // P, 2 * nb),
        in_specs=[
            col(0),
            pl.BlockSpec((P, TOK_TILE, 256), lambda b, j, l: (b, blk(j), 1 + dirn(j))),
            col(3), col(4), col(5),
            pl.BlockSpec((DEPTH, 2, 256), lambda b, j, l: (0, 0, 0)),
            pl.BlockSpec((1, 1, 256), lambda b, j, l: (l[0], 0, 0)),
            pl.BlockSpec((P, 1, 256, 256), lambda b, j, l: (b, dirn(j), 0, 0)) if s0_per_batch
            else pl.BlockSpec((1, 1, 256, 256), lambda b, j, l: (0, 0, 0, 0)),
        ],
        out_specs=[
            pl.BlockSpec((P, TOK_TILE, 256), lambda b, j, l: (b, jnp.where(j < nb, nb - 1, 2 * nb - 1 - j), 0)),
            pl.BlockSpec((P, 1, 256, 256), lambda b, j, l: (b, dirn(j), 0, 0)),
        ],
        scratch_shapes=[
            pltpu.VMEM((P, 256, 256), F32),
            pltpu.VMEM((P, T, 256), F32),
            pltpu.VMEM((P, TOK_TILE, 256), F32),
            pltpu.VMEM((P, TOK_TILE, 256), F32),
            pltpu.VMEM((P, TOK_TILE, 256), F32),
            pltpu.VMEM((HG_CHUNK, 256), F32),
            pltpu.VMEM((HG_CHUNK, 256), F32),
            pltpu.VMEM((HG_CHUNK, 256), F32),
        ],
    )
    return pl.pallas_call(
        functools.partial(_hg_kernel, nb=nb, s0_per_batch=s0_per_batch),
        grid_spec=grid_spec,
        out_shape=[
            jax.ShapeDtypeStruct((B, T, 256), BF16),
            jax.ShapeDtypeStruct((B, 2, 256, 256), F32),
        ],
        compiler_params=_cparams(("arbitrary", "arbitrary")),
        name="hgrn",
    )(lidx, cols_hg, cols_hg, cols_hg, cols_hg, cols_hg, lb_logits, g3, s0)


def _rot_cols(w, nheads):
    shp = w.shape
    r = shp[-1] // nheads
    w4 = w.reshape(shp[:-1] + (nheads, 2, r // 2))
    return jnp.concatenate([-w4[..., 1, :], w4[..., 0, :]], axis=-1).reshape(shp)


def _perm_heads(w, nheads, order):
    shp = w.shape
    w3 = w.reshape(shp[:-1] + (nheads, shp[-1] // nheads))
    return w3[..., list(order), :].reshape(shp)


_WG_ORDER = (0, 2, 1, 3)


def _mla_rope_slot(a):
    pad = [(0, 0)] * (a.ndim - 1) + [(MLA_NOPE, 128 - MLA_NOPE - MLA_ROPE)]
    return jnp.pad(a, pad)


def _prep_w_in(w_in):
    def c(name):
        lo, hi = _IN_OFF[name]
        return w_in[:, :, lo:hi]

    kva = c('mla_kva')
    lo = _IN_OFF['na_k'][0]
    hi = _IN_OFF['hg_z'][1]
    parts = [
        c('na_q') * (HEAD_DIM ** -0.5 * LOG2E),
        w_in[:, :, lo:hi],
        c('mla_qa'), c('mla_z'), kva[:, :, :MLA_KV_RANK], _mla_rope_slot(kva[:, :, MLA_KV_RANK:]),
        _perm_heads(c('wg_q'), WG_HEADS, _WG_ORDER) * (HEAD_DIM ** -0.5 * LOG2E),
        _perm_heads(c('wg_z'), WG_HEADS, _WG_ORDER),
        c('wg_k'), c('wg_v'),
    ]
    return jnp.concatenate([p.astype(BF16) for p in parts], axis=-1), c('merge').astype(BF16)


def _rot_matrix(width, head):
    i = np.arange(width)[:, None]
    j = np.arange(width)[None, :]
    half = head // 2
    first = (j % head) < half
    m = np.where(first & (i == j + half), -1.0, 0.0) + np.where(~first & (i == j - half), 1.0, 0.0)
    return jnp.asarray(m, BF16)


def _rope_tables(n_tok, rot_dim, reps):
    t = np.arange(n_tok)
    row = (t // GRID_W).astype(np.float32)
    col = (t % GRID_W).astype(np.float32)
    nf = rot_dim // 4
    inv = jnp.asarray(ROPE_BASE, F32) ** (-jnp.arange(nf, dtype=F32) / nf)
    ang = jnp.concatenate([jnp.asarray(row)[:, None] * inv, jnp.asarray(col)[:, None] * inv], axis=-1)
    cos, sin = jnp.cos(ang), jnp.sin(ang)
    return (jnp.tile(jnp.concatenate([cos, cos], axis=-1), (1, reps)),
            jnp.tile(jnp.concatenate([sin, sin], axis=-1), (1, reps)))


def _state_to_blockdiag(st):
    B = st.shape[0]
    stt = jnp.swapaxes(st, -1, -2)
    eye = jnp.eye(HG_HEADS, dtype=st.dtype)
    bd = stt[:, :, :, :, None, :] * eye[None, None, :, None, :, None]
    return bd.reshape(B, 2, HG_HEADS * HG_DK, HG_HEADS * HG_DK)


def _blockdiag_to_state(bd):
    B = bd.shape[0]
    b6 = bd.reshape(B, 2, HG_HEADS, HG_DK, HG_HEADS, HG_DK)
    diag = jnp.stack([b6[:, :, h, :, h, :] for h in range(HG_HEADS)], axis=2)
    return jnp.swapaxes(diag, -1, -2)


def kernel(x_prompt, x_sample, cache_na_k, cache_na_v, state_hgrn, cache_mla_ckv, cache_mla_krope, cache_wg_k, cache_wg_v, c, c_ctx, norm_g, w_mod, b_mod, w_in, na_rpb, hg_lb_logits, hg_norm_g, mla_q_norm_g, mla_kv_norm_g, mla_w_qb, mla_w_kvb, wg_sink, w_branch, w_out, final_g):
    Bp, Tp, _ = x_prompt.shape
    Bd, Td, _ = x_sample.shape
    L = cache_na_k.shape[2]
    rows = Td // GRID_W
    assert Tp % TOK_TILE == 0 and Td % TOK_TILE == 0 and L % TOK_TILE == 0
    assert rows % NA_QROWS == 0 and rows >= NA_KROWS and Td >= WG_KWIN

    w_proj, w_merge = _prep_w_in(w_in)
    wb = w_branch.astype(BF16)
    wb = wb.at[:, 3].set(_perm_heads(jnp.swapaxes(wb[:, 3], -1, -2), WG_HEADS, _WG_ORDER).swapaxes(-1, -2))
    wo = w_out.astype(BF16)
    qb = mla_w_qb.reshape(DEPTH, MLA_Q_RANK, MLA_HEADS, MLA_NOPE + MLA_ROPE)
    qb_g1 = jnp.pad(qb, ((0, 0), (0, 0), (0, 0), (0, 128 - MLA_NOPE - MLA_ROPE)))
    qb_g2 = _mla_rope_slot(_rot_cols(qb[..., MLA_NOPE:], 1))
    w_qb = jnp.concatenate([qb_g1.reshape(DEPTH, MLA_Q_RANK, -1), qb_g2.reshape(DEPTH, MLA_Q_RANK, -1)],
                           axis=-1)
    w_qb = (w_qb * ((MLA_NOPE + MLA_ROPE) ** -0.5 * LOG2E)).astype(BF16)
    kvb = mla_w_kvb.reshape(DEPTH, MLA_KV_RANK, MLA_HEADS, 2 * MLA_NOPE)
    kvb_k = jnp.pad(kvb[..., :MLA_NOPE], ((0, 0), (0, 0), (0, 0), (0, 128 - MLA_NOPE)))
    w_kvb = jnp.concatenate([kvb_k.reshape(DEPTH, MLA_KV_RANK, -1),
                             kvb[..., MLA_NOPE:].reshape(DEPTH, MLA_KV_RANK, -1)], axis=-1).astype(BF16)
    norm_g3 = norm_g.reshape(DEPTH, 1, D_MODEL)
    g_q3 = mla_q_norm_g.reshape(DEPTH, 1, MLA_Q_RANK)
    g_kv3 = mla_kv_norm_g.reshape(DEPTH, 1, MLA_KV_RANK)
    hg_g3 = jnp.tile(hg_norm_g, (1, HG_HEADS)).reshape(DEPTH, 1, HG_HEADS * HG_DK)
    final_g2 = final_g.reshape(1, D_MODEL)
    cq, sq = _rope_tables(Td, HEAD_DIM, WG_HEADS)
    ck, sk = cq[:, :128], sq[:, :128]
    c1, s1 = _rope_tables(Td, MLA_ROPE, 1)
    c4 = jnp.concatenate([jnp.ones((Td, MLA_NOPE), F32), c1, jnp.zeros((Td, 32), F32)], axis=-1)
    s4 = _mla_rope_slot(s1)
    rm_wg = _rot_matrix(WG_HEADS * HEAD_DIM, HEAD_DIM)
    rm_mla = jnp.zeros((128, 128), BF16).at[MLA_NOPE:MLA_NOPE + MLA_ROPE, MLA_NOPE:MLA_NOPE + MLA_ROPE].set(
        _rot_matrix(MLA_ROPE, MLA_ROPE))

    nrow = 8 * ((1 + Bd + 7) // 8)
    cond = jnp.zeros((nrow, D_MODEL), F32).at[0].set(c_ctx).at[1:1 + Bd].set(c)
    mod4 = _modulation(cond, w_mod, b_mod).reshape(DEPTH, nrow, 1, 3 * D_MODEL)

    kstart, ktype, table = _na_row_types(rows)
    na_bias = _na_bias(na_rpb, table)
    kstart = jnp.asarray(kstart, jnp.int32)
    ktype = jnp.asarray(ktype, jnp.int32)

    cna_k = cache_na_k.reshape(Bd, DEPTH, L, NA_HEADS * HEAD_DIM)
    cna_v = cache_na_v.reshape(Bd, DEPTH, L, NA_HEADS * HEAD_DIM)
    cwg_k = cache_wg_k.reshape(Bd, DEPTH, L, WG_KV_HEADS * HEAD_DIM)
    cwg_v = cache_wg_v.reshape(Bd, DEPTH, L, WG_KV_HEADS * HEAD_DIM)
    ckr4 = _mla_rope_slot(cache_mla_krope)
    st_bd = _state_to_blockdiag(state_hgrn.reshape(Bd * DEPTH, 2, HG_HEADS, HG_DK, HG_DK)).reshape(
        Bd, DEPTH, 2, 256, 256)
    st_zero = jnp.zeros((1, 2, 256, 256), F32)

    hp, hs = x_prompt, x_sample
    states = []
    tok_caches = None
    for layer in range(DEPTH):
        lidx = jnp.full((1,), layer, jnp.int32)
        final = layer == DEPTH - 1

        flat = (1, Bp * Tp)
        res = _in_proj(lidx, hp.reshape(flat + (D_MODEL,)), mod4, 0, False, norm_g3, w_proj, g_kv3, w_kvb, rm_mla,
                       c4, s4, False, caches=tok_caches, cache_shape=(Bp, DEPTH, Tp))
        na, hg, mla, wg, kc_, vc_ = (a.reshape(Bp, Tp, a.shape[-1]) for a in res[:6])
        tok_caches = res[6:]
        y_hg, st_out = _hgrn(lidx, hg, hg_lb_logits, hg_g3, st_zero, False)
        y_na, y_wg, y_mla = _attn_ctx(lidx, na, wg_sink, wg, mla, g_q3, w_qb, c4, s4, kc_, vc_)
        ys = tuple(y.reshape(flat + (BRANCH_W,)) for y in (y_na, y_hg, y_mla, y_wg))
        hp = _post(lidx, hp.reshape(flat + (D_MODEL,)), mod4, 0, False, norm_g3, ys, w_merge, wb, wo, final_g2,
                   final).reshape(Bp, Tp, D_MODEL)
        states.append(st_out)

        na, hg, mla, wg, kl, vl = _in_proj(lidx, hs, mod4, 1, True, norm_g3, w_proj, g_kv3, w_kvb, rm_mla, c4, s4,
                                           True)
        y_hg, _ = _hgrn(lidx, hg, hg_lb_logits, hg_g3, st_bd[:, layer], True)
        kx, vx = _mla_kv(lidx, cache_mla_ckv, 0, ckr4, 0, ckr4, 0, g_kv3, w_kvb, c4, s4, False, False, True)
        y_na, y_wg, y_mla = _attn_lat(lidx, kstart, ktype, na, cna_k, cna_v, na_bias, wg_sink, wg, cwg_k, cwg_v,
                                      cq, sq, ck, sk, rm_wg, mla, g_q3, w_qb, c4, s4, kl, vl, kx, vx)
        hs = _post(lidx, hs, mod4, 1, True, norm_g3, (y_na, y_hg, y_mla, y_wg), w_merge, wb, wo, final_g2, final)

    new_state = _blockdiag_to_state(jnp.stack(states, axis=1).reshape(Bp * DEPTH, 2, 256, 256)).reshape(
        Bp, DEPTH, 2, HG_HEADS, HG_DK, HG_DK)
    na_k, na_v, ckv_c, kr_c, wg_k, wg_v = tok_caches
    return (hp, hs,
            na_k.reshape(Bp, DEPTH, Tp, NA_HEADS, HEAD_DIM), na_v.reshape(Bp, DEPTH, Tp, NA_HEADS, HEAD_DIM),
            new_state, ckv_c, kr_c,
            wg_k.reshape(Bp, DEPTH, Tp, WG_KV_HEADS, HEAD_DIM), wg_v.reshape(Bp, DEPTH, Tp, WG_KV_HEADS, HEAD_DIM))
```
